```python
import math
import jax, jax.numpy as jnp
from jax import lax
import numpy as np

D_MODEL = 1024
BATCH = 4
SEQ = 8192
DEPTH = 2

CTX_LEN = 256
GRID_W = 64
RET_HEADS = D_MODEL // 256
RET_DK = 256
RET_DV = 256
RET_CHUNK = 128
DIFF_HEADS = D_MODEL // 128
DIFF_HD = 64
DIFF_DV = 2 * DIFF_HD
NA_HEADS = D_MODEL // 64
NA_HD = 64
NA_KH = 8
NA_KW = 16
D_FF = 4 * D_MODEL
Q_BLOCK = 128
ROPE_BASE = 10000.0
LN_EPS = 1e-6
SUBLN_EPS = 1e-5

RET_QK_W = RET_HEADS * RET_DK
RET_V_W = RET_HEADS * RET_DV
DIFF_QK_W = 2 * DIFF_HEADS * DIFF_HD
DIFF_V_W = DIFF_HEADS * DIFF_DV
NA_W = NA_HEADS * NA_HD
IN_SIZES = (RET_QK_W, RET_QK_W, RET_V_W, RET_V_W, DIFF_QK_W, DIFF_QK_W, DIFF_V_W, NA_W, NA_W, NA_W, D_MODEL, D_MODEL, D_MODEL)
IN_WIDTH = 2 * RET_QK_W + 2 * RET_V_W + 2 * DIFF_QK_W + DIFF_V_W + 3 * NA_W + 3 * D_MODEL

kernel_name = 'hybrid_retention_diffattn_natten_dit_block'


def layer_norm(x, g=None, b=None, eps=LN_EPS):
    xf = x.astype(jnp.float32)
    mu = jnp.mean(xf, axis=-1, keepdims=True)
    var = jnp.mean(jnp.square(xf - mu), axis=-1, keepdims=True)
    y = (xf - mu) * lax.rsqrt(var + eps)
    if g is not None:
        y = y * g.astype(jnp.float32) + b.astype(jnp.float32)
    return y.astype(x.dtype)


def rms_norm(x, g, eps=SUBLN_EPS):
    xf = x.astype(jnp.float32)
    y = xf * lax.rsqrt(jnp.mean(jnp.square(xf), axis=-1, keepdims=True) + eps) * g.astype(jnp.float32)
    return y.astype(x.dtype)


def modulate(x, shift, scale):
    return layer_norm(x) * (1.0 + scale) + shift


def split_cols(z):
    idx = []
    acc = 0
    for s in IN_SIZES[:-1]:
        acc += s
        idx.append(acc)
    return jnp.split(z, idx, axis=-1)


def heads(a, n, d):
    b, L, _ = a.shape
    return a.reshape(b, L, n, d).transpose(0, 2, 1, 3)


def merge_heads(a):
    b, h, L, d = a.shape
    return a.transpose(0, 2, 1, 3).reshape(b, L, h * d)


def diff_qk_heads(a):
    b, L, _ = a.shape
    return a.reshape(b, L, DIFF_HEADS, 2, DIFF_HD).transpose(0, 2, 3, 1, 4)


def rope(x, ang):
    cos = jnp.cos(ang).astype(x.dtype)
    sin = jnp.sin(ang).astype(x.dtype)
    x1, x2 = jnp.split(x, 2, axis=-1)
    return jnp.concatenate([x1 * cos - x2 * sin, x1 * sin + x2 * cos], axis=-1)


def retention_angles(L):
    t = jnp.arange(L, dtype=jnp.float32)
    inv = ROPE_BASE ** (-jnp.linspace(0.0, 1.0, RET_DK // 2, dtype=jnp.float32))
    return t[:, None] * inv[None, :]


def axial_angles(L, hd):
    t = jnp.arange(L)
    row = (t // GRID_W).astype(jnp.float32)
    col = (t % GRID_W).astype(jnp.float32)
    half = hd // 2
    inv = ROPE_BASE ** (-jnp.arange(0, half, 2, dtype=jnp.float32) / half)
    return row[:, None] * inv[None, :], col[:, None] * inv[None, :]


def axial_rope(x, ang_r, ang_c):
    xr, xc = jnp.split(x, 2, axis=-1)
    return jnp.concatenate([rope(xr, ang_r), rope(xc, ang_c)], axis=-1)


def retention_scan(q, k, v, log_g, s0):
    b, h, L, dk = q.shape
    dv = v.shape[-1]
    C = RET_CHUNK
    n = L // C

    def chunks(a):
        return jnp.moveaxis(a.reshape(b, h, n, C, a.shape[-1]), 2, 0)

    i = jnp.arange(C, dtype=jnp.float32)
    lg = log_g[:, None]
    dist = i[:, None] - i[None, :]
    intra = jnp.where(dist >= 0, jnp.exp(lg[..., None] * jnp.maximum(dist, 0.0)), 0.0).astype(q.dtype)
    q_dec = jnp.exp(lg * (i + 1.0)).astype(q.dtype)
    k_dec = jnp.exp(lg * (C - 1.0 - i)).astype(q.dtype)
    c_dec = jnp.exp(log_g * C).astype(q.dtype)

    def step(s, xs):
        qc, kc, vc = xs
        att = jnp.einsum('bhid,bhjd->bhij', qc, kc) * intra
        o = jnp.einsum('bhij,bhjv->bhiv', att, vc) + jnp.einsum('bhid,bhdv->bhiv', qc * q_dec[:, :, None], s)
        s = s * c_dec[:, None, None] + jnp.einsum('bhjd,bhjv->bhdv', kc * k_dec[:, :, None], vc)
        return s, o

    s_fin, o = lax.scan(step, s0, (chunks(q), chunks(k), chunks(v)))
    o = jnp.moveaxis(o, 0, 2).reshape(b, h, L, dv)
    return s_fin, o


def retention_mixer(q_l, k_l, v_l, q_c, k_c, v_c, dec_f, dec_b, with_ctx_out):
    lg_f = jax.nn.log_sigmoid(dec_f.astype(jnp.float32))
    lg_b = jax.nn.log_sigmoid(dec_b.astype(jnp.float32))
    b, h, _, dk = q_c.shape
    s0 = jnp.zeros((b, h, dk, v_c.shape[-1]), v_c.dtype)
    rev = lambda a: jnp.flip(a, axis=2)
    s_cf, o_cf = retention_scan(q_c, k_c, v_c, lg_f, s0)
    s_cb, o_cb = retention_scan(rev(q_c), rev(k_c), rev(v_c), lg_b, s0)
    _, o_lf = retention_scan(q_l, k_l, v_l, lg_f, s_cf)
    _, o_lb = retention_scan(rev(q_l), rev(k_l), rev(v_l), lg_b, s_cb)
    o_l = o_lf + rev(o_lb)
    o_c = o_cf + rev(o_cb) if with_ctx_out else None
    return o_l, o_c


def diff_probs_out(q, k, v, lam):
    s = jnp.einsum('bhmqd,bhmkd->bhmqk', q, k).astype(jnp.float32)
    p = jax.nn.softmax(s, axis=-1)
    a = (p[:, :, 0] - lam * p[:, :, 1]).astype(v.dtype)
    return jnp.einsum('bhqk,bhkv->bhqv', a, v)


def sweep_query_blocks(fn, q, *rest):
    L = q.shape[-2]
    nb = L // Q_BLOCK
    qb = jnp.moveaxis(q.reshape(q.shape[:-2] + (nb, Q_BLOCK, q.shape[-1])), -3, 0)
    ob = lax.map(lambda blk: fn(blk, *rest), qb)
    ob = jnp.moveaxis(ob, 0, -3)
    return ob.reshape(ob.shape[:-3] + (L, ob.shape[-1]))


def diff_mixer(q_l, k_l, v_l, q_c, k_c, v_c, lq1, lk1, lq2, lk2, subln_g, layer_idx, with_ctx_out):
    lam_init = 0.8 - 0.6 * math.exp(-0.3 * layer_idx)
    f = lambda a: a.astype(jnp.float32)
    lam = jnp.exp(jnp.sum(f(lq1) * f(lk1))) - jnp.exp(jnp.sum(f(lq2) * f(lk2))) + lam_init
    k_all = jnp.concatenate([k_l, k_c], axis=3)
    v_all = jnp.concatenate([v_l, v_c], axis=2)
    post = lambda o: rms_norm(o, subln_g) * (1.0 - lam_init)
    o_l = post(sweep_query_blocks(diff_probs_out, q_l, k_all, v_all, lam))
    o_c = post(diff_probs_out(q_c, k_c, v_c, lam)) if with_ctx_out else None
    return o_l, o_c


def ctx_attention(q, k, v):
    p = jax.nn.softmax(jnp.einsum('bhqd,bhkd->bhqk', q, k).astype(jnp.float32), axis=-1).astype(v.dtype)
    return jnp.einsum('bhqk,bhkv->bhqv', p, v)


def na_mixer(q_l, k_l, v_l, q_c, k_c, v_c, rpb, with_ctx_out):
    b, h, L, d = q_l.shape
    rows = L // GRID_W
    kh = min(NA_KH, rows)
    kw = NA_KW
    grid = lambda a: a.reshape(b, h, rows, GRID_W, a.shape[-1])
    kg, vg = grid(k_l), grid(v_l)
    qg = jnp.moveaxis(grid(q_l), 2, 0)
    j = jnp.arange(GRID_W)
    cs = jnp.clip(j - kw // 2, 0, GRID_W - kw)
    col_mask = (j[None, :] >= cs[:, None]) & (j[None, :] < cs[:, None] + kw)
    col_idx = jnp.clip(j[None, :] - j[:, None] + NA_KW - 1, 0, 2 * NA_KW - 2)
    a = jnp.arange(kh)
    n_nb = kh * GRID_W

    def row_block(args):
        r, qr = args
        rs = jnp.clip(r - kh // 2, 0, rows - kh)
        kb = lax.dynamic_slice_in_dim(kg, rs, kh, axis=2)
        vb = lax.dynamic_slice_in_dim(vg, rs, kh, axis=2)
        row_idx = rs + a - r + NA_KH - 1
        bias = rpb[:, row_idx[None, :, None], col_idx[:, None, :]].astype(jnp.float32)
        s_nb = jnp.einsum('bhjd,bhakd->bhjak', qr, kb).astype(jnp.float32) + bias
        s_nb = jnp.where(col_mask[:, None, :], s_nb, -jnp.inf).reshape(b, h, GRID_W, n_nb)
        s_cx = jnp.einsum('bhjd,bhcd->bhjc', qr, k_c).astype(jnp.float32)
        p = jax.nn.softmax(jnp.concatenate([s_nb, s_cx], axis=-1), axis=-1).astype(vb.dtype)
        p_nb = p[..., :n_nb].reshape(b, h, GRID_W, kh, GRID_W)
        p_cx = p[..., n_nb:]
        return jnp.einsum('bhjak,bhakv->bhjv', p_nb, vb) + jnp.einsum('bhjc,bhcv->bhjv', p_cx, v_c)

    o = lax.map(row_block, (jnp.arange(rows), qg))
    o_l = jnp.moveaxis(o, 0, 2).reshape(b, h, L, d)
    o_c = ctx_attention(q_c, k_c, v_c) if with_ctx_out else None
    return o_l, o_c


def mixer_sublayer(h_l, h_c, w_in, dec_f, dec_b, lq1, lk1, lq2, lk2, subln_g, rpb, w_pa, w_pb, w_pc, w_o, layer_idx, with_ctx_out):
    L = h_l.shape[1]
    (rq_l, rk_l, rv_l, rg_l, dq_l, dk_l, dv_l, nq_l, nk_l, nv_l, ga_l, gb_l, gc_l) = split_cols(h_l @ w_in)
    (rq_c, rk_c, rv_c, rg_c, dq_c, dk_c, dv_c, nq_c, nk_c, nv_c, ga_c, gb_c, gc_c) = split_cols(h_c @ w_in)

    ang = retention_angles(L)
    rsc = RET_DK ** -0.5
    o_rl, o_rc = retention_mixer(
        rope(heads(rq_l, RET_HEADS, RET_DK), ang), rope(heads(rk_l, RET_HEADS, RET_DK), ang) * rsc, heads(rv_l, RET_HEADS, RET_DV),
        heads(rq_c, RET_HEADS, RET_DK), heads(rk_c, RET_HEADS, RET_DK) * rsc, heads(rv_c, RET_HEADS, RET_DV),
        dec_f, dec_b, with_ctx_out)

    ang_r, ang_c = axial_angles(L, DIFF_HD)
    dsc = DIFF_HD ** -0.5
    o_dl, o_dc = diff_mixer(
        axial_rope(diff_qk_heads(dq_l), ang_r, ang_c) * dsc, axial_rope(diff_qk_heads(dk_l), ang_r, ang_c), heads(dv_l, DIFF_HEADS, DIFF_DV),
        diff_qk_heads(dq_c) * dsc, diff_qk_heads(dk_c), heads(dv_c, DIFF_HEADS, DIFF_DV),
        lq1, lk1, lq2, lk2, subln_g, layer_idx, with_ctx_out)

    nsc = NA_HD ** -0.5
    o_nl, o_nc = na_mixer(
        heads(nq_l, NA_HEADS, NA_HD) * nsc, heads(nk_l, NA_HEADS, NA_HD), heads(nv_l, NA_HEADS, NA_HD),
        heads(nq_c, NA_HEADS, NA_HD) * nsc, heads(nk_c, NA_HEADS, NA_HD), heads(nv_c, NA_HEADS, NA_HD),
        rpb, with_ctx_out)

    def merge_branches(o_r, g_r, o_d, o_n, ga, gb, gc):
        y_a = (jax.nn.silu(g_r) * merge_heads(layer_norm(o_r))) @ w_pa
        y_b = merge_heads(o_d) @ w_pb
        y_c = merge_heads(o_n) @ w_pc
        m = jax.nn.sigmoid(ga) * y_a + jax.nn.sigmoid(gb) * y_b + jax.nn.sigmoid(gc) * y_c
        return m @ w_o

    y_l = merge_branches(o_rl, rg_l, o_dl, o_nl, ga_l, gb_l, gc_l)
    y_c = merge_branches(o_rc, rg_c, o_dc, o_nc, ga_c, gb_c, gc_c) if with_ctx_out else None
    return y_l, y_c


def sq_relu_ffn(h, w1, w2):
    return jnp.square(jax.nn.relu(h @ w1)) @ w2


def setup_inputs(seed: int = 0) -> dict:
    key = jax.random.key(seed)
    ks = jax.random.split(key, 26)
    f32 = jnp.float32
    nrm = lambda k, shape, s: s * jax.random.normal(k, shape, f32)
    beta = (8.0 * DEPTH) ** -0.25
    ret_init = jnp.asarray(np.log(2.0 ** (5.0 + np.arange(RET_HEADS)) - 1.0).astype(np.float32))
    return {
        'x': nrm(ks[0], (BATCH, SEQ, D_MODEL), 1.0),
        'c': nrm(ks[1], (BATCH, D_MODEL), 1.0),
        'ctx': nrm(ks[2], (BATCH, CTX_LEN, D_MODEL), 1.0),
        'c_ctx': nrm(ks[3], (D_MODEL,), 1.0),
        'w_mod': nrm(ks[4], (DEPTH, D_MODEL, 6 * D_MODEL), 0.5 * D_MODEL ** -0.5),
        'b_mod': nrm(ks[5], (DEPTH, 6 * D_MODEL), 0.01),
        'w_in': nrm(ks[6], (DEPTH, D_MODEL, IN_WIDTH), D_MODEL ** -0.5),
        'ret_decay_f': ret_init[None, :] + nrm(ks[7], (DEPTH, RET_HEADS), 0.01),
        'ret_decay_b': ret_init[None, :] + nrm(ks[8], (DEPTH, RET_HEADS), 0.01),
        'diff_lq1': nrm(ks[9], (DEPTH, DIFF_HD), 0.1),
        'diff_lk1': nrm(ks[10], (DEPTH, DIFF_HD), 0.1),
        'diff_lq2': nrm(ks[11], (DEPTH, DIFF_HD), 0.1),
        'diff_lk2': nrm(ks[12], (DEPTH, DIFF_HD), 0.1),
        'diff_subln_g': 1.0 + nrm(ks[13], (DEPTH, DIFF_DV), 0.01),
        'na_rpb': nrm(ks[14], (DEPTH, NA_HEADS, 2 * NA_KH - 1, 2 * NA_KW - 1), 0.05),
        'w_pa': nrm(ks[15], (DEPTH, RET_V_W, D_MODEL), RET_V_W ** -0.5),
        'w_pb': nrm(ks[16], (DEPTH, DIFF_V_W, D_MODEL), DIFF_V_W ** -0.5),
        'w_pc': nrm(ks[17], (DEPTH, NA_W, D_MODEL), NA_W ** -0.5),
        'w_o': nrm(ks[18], (DEPTH, D_MODEL, D_MODEL), beta * D_MODEL ** -0.5),
        'ln1_g': 1.0 + nrm(ks[19], (DEPTH, D_MODEL), 0.01),
        'ln1_b': nrm(ks[20], (DEPTH, D_MODEL), 0.01),
        'w_ff1': nrm(ks[21], (DEPTH, D_MODEL, D_FF), D_MODEL ** -0.5),
        'w_ff2': nrm(ks[22], (DEPTH, D_FF, D_MODEL), beta * D_FF ** -0.5),
        'ln2_g': 1.0 + nrm(ks[23], (DEPTH, D_MODEL), 0.01),
        'ln2_b': nrm(ks[24], (DEPTH, D_MODEL), 0.01),
    }


def reference(x, c, ctx, c_ctx, w_mod, b_mod, w_in, ret_decay_f, ret_decay_b, diff_lq1, diff_lk1, diff_lq2, diff_lk2, diff_subln_g, na_rpb, w_pa, w_pb, w_pc, w_o, ln1_g, ln1_b, w_ff1, w_ff2, ln2_g, ln2_b):
    alpha = (2.0 * DEPTH) ** 0.25
    x_l, x_c = x, ctx
    for l in range(DEPTH):
        with_ctx_out = l < DEPTH - 1
        mod_l = jax.nn.silu(c) @ w_mod[l] + b_mod[l]
        mod_c = jax.nn.silu(c_ctx) @ w_mod[l] + b_mod[l]
        sh1, sc1, g1, sh2, sc2, g2 = jnp.split(mod_l[:, None, :], 6, axis=-1)
        csh1, csc1, cg1, csh2, csc2, cg2 = jnp.split(mod_c, 6, axis=-1)

        h_l = modulate(x_l, sh1, sc1)
        h_c = modulate(x_c, csh1, csc1)
        y_l, y_c = mixer_sublayer(h_l, h_c, w_in[l], ret_decay_f[l], ret_decay_b[l], diff_lq1[l], diff_lk1[l], diff_lq2[l], diff_lk2[l],
                                  diff_subln_g[l], na_rpb[l], w_pa[l], w_pb[l], w_pc[l], w_o[l], l, with_ctx_out)
        x_l = layer_norm(alpha * x_l + g1 * y_l, ln1_g[l], ln1_b[l])
        x_l = layer_norm(alpha * x_l + g2 * sq_relu_ffn(modulate(x_l, sh2, sc2), w_ff1[l], w_ff2[l]), ln2_g[l], ln2_b[l])
        if with_ctx_out:
            x_c = layer_norm(alpha * x_c + cg1 * y_c, ln1_g[l], ln1_b[l])
            x_c = layer_norm(alpha * x_c + cg2 * sq_relu_ffn(modulate(x_c, csh2, csc2), w_ff1[l], w_ff2[l]), ln2_g[l], ln2_b[l])
    return x_l
```

```python
import functools
import math

import jax
import jax.numpy as jnp
from jax import lax
from jax.experimental import pallas as pl
from jax.experimental.pallas import tpu as pltpu

F32 = jnp.float32
BF16 = jnp.bfloat16

D_MODEL = 1024
GRID_W = 64
RET_HEADS, RET_DK = 4, 256
DIFF_HEADS, DIFF_HD, DIFF_DV = 8, 64, 128
NA_HEADS, NA_HD, NA_KH, NA_KW = 16, 64, 8, 16
D_FF = 4 * D_MODEL
ROPE_BASE = 10000.0
LN_EPS = 1e-6
SUBLN_EPS = 1e-5
IN_WIDTH = 13 * D_MODEL
NEG_BIG = -1e30

LANES = 128
VMEM_LIMIT_CAP = 56 << 20

ROW_TILE = 1024
LN_TILE = 512
MERGE_TILE = 256
FFN_TILE = 512
RET_CHUNK = 256
DIFF_TQ = 256
DIFF_TK = 512
NA_ROWS = 4

NT_DIMS = (((1,), (1,)), ((), ()))
TN_DIMS = (((0,), (0,)), ((), ()))


def _params(n_axes, vmem_mb):
    return pltpu.CompilerParams(dimension_semantics=("arbitrary",) * n_axes,
                                vmem_limit_bytes=min(vmem_mb << 20, VMEM_LIMIT_CAP))


def _sigmoid(x):
    return 1.0 / (1.0 + jnp.exp(-x))


def _layer_norm_rows(x):
    mu = jnp.mean(x, axis=-1, keepdims=True)
    xc = x - mu
    var = jnp.mean(xc * xc, axis=-1, keepdims=True)
    return xc * lax.rsqrt(var + LN_EPS)


def _mod_kernel(c_ref, w_ref, b_ref, o_ref):
    c = c_ref[...]
    a = (c * _sigmoid(c)).astype(BF16)
    o_ref[0] = jnp.dot(a, w_ref[0].astype(BF16), preferred_element_type=F32) + b_ref[0]


def _modulation(cvec, w_mod, b_mod):
    depth = w_mod.shape[0]
    d = D_MODEL
    return pl.pallas_call(
        _mod_kernel,
        grid=(depth, 6),
        in_specs=[pl.BlockSpec((8, d), lambda l, j: (0, 0)),
                  pl.BlockSpec((1, d, d), lambda l, j: (l, 0, j)),
                  pl.BlockSpec((1, 1, d), lambda l, j: (l, 0, j))],
        out_specs=pl.BlockSpec((1, 8, d), lambda l, j: (l, 0, j)),
        out_shape=jax.ShapeDtypeStruct((depth, 8, 6 * d), F32),
        compiler_params=_params(2, 32),
        name="modulation",
    )(cvec, w_mod, b_mod.reshape(depth, 1, 6 * d))


def _ln_mod_kernel(x_ref, sh_ref, sc_ref, o_ref):
    y = _layer_norm_rows(x_ref[...])
    o_ref[...] = (y * (1.0 + sc_ref[0]) + sh_ref[0]).astype(o_ref.dtype)


def _ln_mod(x, shift, scale, mod_row):
    r, d = x.shape
    tm = min(LN_TILE, r)
    return pl.pallas_call(
        _ln_mod_kernel,
        grid=(r // tm,),
        in_specs=[pl.BlockSpec((tm, d), lambda i: (i, 0)),
                  pl.BlockSpec((1, 1, d), lambda i: (mod_row(i, tm), 0, 0)),
                  pl.BlockSpec((1, 1, d), lambda i: (mod_row(i, tm), 0, 0))],
        out_specs=pl.BlockSpec((tm, d), lambda i: (i, 0)),
        out_shape=jax.ShapeDtypeStruct((r, d), BF16),
        compiler_params=_params(1, 32),
        name="ln_modulate",
    )(x, shift, scale)


def _proj_kernel(h_ref, w_ref, cs_ref, *rest, mode, transposed, tn, tk_out):
    o_ref = rest[-1]
    acc = jnp.dot(h_ref[...], w_ref[...], preferred_element_type=F32) * cs_ref[...]
    if mode == "rope_ret":
        cos = rest[0][...]
        sin = rest[1][...]
        half = RET_DK // 2
        for hh in range(tn // RET_DK):
            a = hh * RET_DK
            x1 = acc[:, a:a + half]
            x2 = acc[:, a + half:a + RET_DK]
            o_ref[:, a:a + half] = (x1 * cos - x2 * sin).astype(o_ref.dtype)
            o_ref[:, a + half:a + RET_DK] = (x1 * sin + x2 * cos).astype(o_ref.dtype)
    elif mode == "rope_diff":
        cos = rest[0][...]
        sin_lo = rest[1][...]
        sin_hi = rest[2][...]
        q16 = DIFF_HD // 4
        for g in range(tn // LANES):
            x = acc[:, g * LANES:(g + 1) * LANES]
            y = x * cos + pltpu.roll(x, LANES - q16, 1) * sin_lo + pltpu.roll(x, q16, 1) * sin_hi
            o_ref[:, g * LANES:(g + 1) * LANES] = y.astype(o_ref.dtype)
    elif transposed:
        acc_t = acc.T
        for a in range(o_ref.shape[0]):
            o_ref[a] = acc_t[:, a * tk_out:(a + 1) * tk_out].astype(o_ref.dtype)
    else:
        o_ref[...] = acc.astype(o_ref.dtype)


def _proj(h, w, colscale, col0, ncols, out_dtype, mode="plain", tables=(), tiles_per_batch=1, tk_out=None):
    r, d = h.shape
    tm = min(ROW_TILE, r)
    tn = 512
    assert ncols % tn == 0 and col0 % tn == 0 and r % tm == 0
    jb = col0 // tn
    transposed = tk_out is not None
    in_specs = [pl.BlockSpec((tm, d), lambda i, j: (i, 0)),
                pl.BlockSpec((d, tn), lambda i, j: (0, jb + j)),
                pl.BlockSpec((1, tn), lambda i, j: (0, jb + j))]
    for _ in tables:
        in_specs.append(pl.BlockSpec((tm, LANES), lambda i, j: (i % tiles_per_batch, 0)))
    if transposed:
        assert tm % tk_out == 0
        na = tm // tk_out
        out_specs = pl.BlockSpec((na, tn, tk_out), lambda i, j: (i, j, 0))
        out_shape = jax.ShapeDtypeStruct((r // tk_out, ncols, tk_out), out_dtype)
    else:
        out_specs = pl.BlockSpec((tm, tn), lambda i, j: (i, j))
        out_shape = jax.ShapeDtypeStruct((r, ncols), out_dtype)
    return pl.pallas_call(
        functools.partial(_proj_kernel, mode=mode, transposed=transposed, tn=tn, tk_out=tk_out),
        grid=(r // tm, ncols // tn),
        in_specs=in_specs,
        out_specs=out_specs,
        out_shape=out_shape,
        compiler_params=_params(2, 40),
        name="in_proj_" + mode + ("_t" if transposed else ""),
    )(h, w, colscale, *tables)


def _ret_kernel(dec_ref, q_ref, k_ref, v_ref, s0_ref, *rest, reverse, has_prev, chunk):
    if has_prev:
        prev_ref, o_ref, sfin_ref, s_scr, intra_scr, qd_scr, kd_scr, cd_scr = rest
    else:
        o_ref, sfin_ref, s_scr, intra_scr, qd_scr, kd_scr, cd_scr = rest
    head = pl.program_id(1)
    c = pl.program_id(2)
    cf = float(chunk)

    @pl.when(c == 0)
    def _init():
        s_scr[...] = s0_ref[0, 0]
        dv = jnp.full((1, 1), dec_ref[head], F32)
        lg = jnp.minimum(dv, 0.0) - jnp.log(1.0 + jnp.exp(-jnp.abs(dv)))
        ii = lax.broadcasted_iota(jnp.int32, (chunk, chunk), 0).astype(F32)
        jj = lax.broadcasted_iota(jnp.int32, (chunk, chunk), 1).astype(F32)
        dist = (jj - ii) if reverse else (ii - jj)
        intra_scr[...] = jnp.where(dist >= 0.0, jnp.exp(lg * jnp.maximum(dist, 0.0)), 0.0)
        i1 = lax.broadcasted_iota(jnp.int32, (chunk, 1), 0).astype(F32)
        qd_scr[...] = jnp.exp(lg * ((cf - i1) if reverse else (i1 + 1.0)))
        kd_scr[...] = jnp.exp(lg * (i1 if reverse else (cf - 1.0 - i1)))
        cd_scr[...] = jnp.exp(lg * cf)

    q = q_ref[0]
    k = k_ref[0]
    v = v_ref[0]
    s = s_scr[...]
    att = lax.dot_general(q, k, NT_DIMS, preferred_element_type=F32) * intra_scr[...]
    qd = (q.astype(F32) * qd_scr[...]).astype(BF16)
    o = (jnp.dot(att.astype(BF16), v, preferred_element_type=F32)
         + jnp.dot(qd, s.astype(BF16), preferred_element_type=F32))
    kd = (k.astype(F32) * kd_scr[...]).astype(BF16)
    s_new = s * cd_scr[...] + lax.dot_general(kd, v, TN_DIMS, preferred_element_type=F32)
    s_scr[...] = s_new
    if has_prev:
        o = o + prev_ref[0]
    o_ref[0] = o

    @pl.when(c == pl.num_programs(2) - 1)
    def _fin():
        sfin_ref[0, 0] = s_new


def _retention(dec, qk, v, s0, prev, reverse):
    b, length, _ = qk.shape
    hh, dk = RET_HEADS, RET_DK
    c = min(RET_CHUNK, length)
    n = length // c
    pos = (lambda i: n - 1 - i) if reverse else (lambda i: i)
    in_specs = [pl.BlockSpec(memory_space=pltpu.SMEM),
                pl.BlockSpec((1, c, dk), lambda bi, h, i: (bi, pos(i), h)),
                pl.BlockSpec((1, c, dk), lambda bi, h, i: (bi, pos(i), hh + h)),
                pl.BlockSpec((1, c, dk), lambda bi, h, i: (bi, pos(i), h)),
                pl.BlockSpec((1, 1, dk, dk), lambda bi, h, i: (bi, h, 0, 0))]
    args = [dec, qk, qk, v, s0]
    if prev is not None:
        in_specs.append(pl.BlockSpec((1, c, dk), lambda bi, h, i: (bi, pos(i), h)))
        args.append(prev)
    return pl.pallas_call(
        functools.partial(_ret_kernel, reverse=reverse, has_prev=prev is not None, chunk=c),
        grid=(b, hh, n),
        in_specs=in_specs,
        out_specs=[pl.BlockSpec((1, c, dk), lambda bi, h, i: (bi, pos(i), h)),
                   pl.BlockSpec((1, 1, dk, dk), lambda bi, h, i: (bi, h, 0, 0))],
        out_shape=[jax.ShapeDtypeStruct((b, length, hh * dk), F32),
                   jax.ShapeDtypeStruct((b, hh, dk, dk), F32)],
        scratch_shapes=[pltpu.VMEM((dk, dk), F32), pltpu.VMEM((c, c), F32),
                        pltpu.VMEM((c, 1), F32), pltpu.VMEM((c, 1), F32), pltpu.VMEM((1, 1), F32)],
        compiler_params=_params(3, 32),
        name="retention_bwd" if reverse else "retention_fwd",
    )(*args)


def _diff_kernel(lam_ref, g_ref, q_ref, *rest, tq, n_lat, tk, lam_init):
    if n_lat:
        kl_ref, vtl_ref, kc_ref, vtc_ref, o_ref, m_scr, l_scr, acc_scr = rest
    else:
        kc_ref, vtc_ref, o_ref, m_scr, l_scr, acc_scr = rest
    q = q_ref[0].astype(F32)
    lane = lax.broadcasted_iota(jnp.int32, q.shape, 1)
    q1 = jnp.where(lane < DIFF_HD, q, 0.0).astype(BF16)
    q2 = jnp.where(lane >= DIFF_HD, q, 0.0).astype(BF16)
    qq = jnp.concatenate([q1, q2], axis=0)
    m_scr[...] = jnp.full(m_scr.shape, NEG_BIG, F32)
    l_scr[...] = jnp.zeros(l_scr.shape, F32)
    acc_scr[...] = jnp.zeros(acc_scr.shape, F32)

    def step(k, vt):
        s = lax.dot_general(k, qq, NT_DIMS, preferred_element_type=F32)
        m_old = m_scr[...]
        m_new = jnp.maximum(m_old, jnp.max(s, axis=0, keepdims=True))
        alpha = jnp.exp(m_old - m_new)
        p = jnp.exp(s - m_new)
        l_scr[...] = alpha * l_scr[...] + jnp.sum(p, axis=0, keepdims=True)
        acc_scr[...] = acc_scr[...] * alpha + jnp.dot(vt, p.astype(BF16), preferred_element_type=F32)
        m_scr[...] = m_new

    if n_lat:
        def body(j, carry):
            off = pl.multiple_of(j * tk, tk)
            step(kl_ref[0, pl.ds(off, tk), :], vtl_ref[j])
            return carry
        lax.fori_loop(0, n_lat, body, 0)
    step(kc_ref[0], vtc_ref[0])

    o = acc_scr[...] * (1.0 / l_scr[...])
    lam4 = lam_ref[...]
    s1 = jnp.sum(lam4[0:1] * lam4[1:2], axis=-1, keepdims=True)
    s2 = jnp.sum(lam4[2:3] * lam4[3:4], axis=-1, keepdims=True)
    lam = jnp.exp(s1) - jnp.exp(s2) + lam_init
    y = (o[:, :tq] - lam * o[:, tq:]).T
    ms = jnp.mean(y * y, axis=-1, keepdims=True)
    y = y * lax.rsqrt(ms + SUBLN_EPS) * g_ref[...] * (1.0 - lam_init)
    o_ref[0] = y.astype(o_ref.dtype)


def _diff_attention(lam4, subln_g, q_arr, lat, ctx, lam_init):
    b, lq, _ = q_arr.shape
    hh = DIFF_HEADS
    tq = min(DIFF_TQ, lq)
    qk_c, vt_c = ctx
    lc = qk_c.shape[1]
    in_specs = [pl.BlockSpec((4, DIFF_HD), lambda bi, h, i: (0, 0)),
                pl.BlockSpec((1, DIFF_DV), lambda bi, h, i: (0, 0)),
                pl.BlockSpec((1, tq, LANES), lambda bi, h, i: (bi, i, h))]
    args = [lam4, subln_g, q_arr]
    n_lat, tk = 0, 0
    if lat is not None:
        qk_l, vt_l = lat
        length = qk_l.shape[1]
        tk = vt_l.shape[2]
        n_lat = length // tk
        in_specs += [pl.BlockSpec((1, length, LANES), lambda bi, h, i: (bi, 0, hh + h)),
                     pl.BlockSpec((n_lat, DIFF_DV, tk), lambda bi, h, i: (bi, h, 0))]
        args += [qk_l, vt_l]
    in_specs += [pl.BlockSpec((1, lc, LANES), lambda bi, h, i: (bi, 0, hh + h)),
                 pl.BlockSpec((1, DIFF_DV, lc), lambda bi, h, i: (bi, h, 0))]
    args += [qk_c, vt_c]
    return pl.pallas_call(
        functools.partial(_diff_kernel, tq=tq, n_lat=n_lat, tk=tk, lam_init=lam_init),
        grid=(b, hh, lq // tq),
        in_specs=in_specs,
        out_specs=pl.BlockSpec((1, tq, LANES), lambda bi, h, i: (bi, i, h)),
        out_shape=jax.ShapeDtypeStruct((b, lq, hh * DIFF_DV), BF16),
        scratch_shapes=[pltpu.VMEM((1, 2 * tq), F32), pltpu.VMEM((1, 2 * tq), F32),
                        pltpu.VMEM((DIFF_DV, 2 * tq), F32)],
        compiler_params=_params(3, 40),
        name="diff_attention" if lat is not None else "diff_attention_ctx",
    )(*args)


def _softmax_pv(s_list, v_list):
    m = s_list[0].max(axis=-1, keepdims=True)
    for s in s_list[1:]:
        m = jnp.maximum(m, s.max(axis=-1, keepdims=True))
    den = 0.0
    o = 0.0
    for s, v in zip(s_list, v_list):
        p = jnp.exp(s - m)
        den = den + jnp.sum(p, axis=-1, keepdims=True)
        o = o + jnp.dot(p.astype(BF16), v, preferred_element_type=F32)
    return o * (1.0 / den)


def _na_kernel(q_ref, k_ref, v_ref, kc_ref, vc_ref, bias_ref, o_ref, *, g_rows, n_rows):
    r0 = pl.program_id(2) * g_rows
    kc = kc_ref[0]
    vc = vc_ref[0]
    win = NA_KH * GRID_W
    lane = lax.broadcasted_iota(jnp.int32, (GRID_W, LANES), 1)
    for g in range(g_rows):
        r = r0 + g
        rs = jnp.clip(r - NA_KH // 2, 0, n_rows - NA_KH)
        d = r - rs
        off = pl.multiple_of(rs * GRID_W, GRID_W)
        kw = k_ref[0, pl.ds(off, win), :]
        vw = v_ref[0, pl.ds(off, win), :]
        q = q_ref[0, g * GRID_W:(g + 1) * GRID_W, :].astype(F32)
        outs = []
        for hh in range(2):
            sel = (lane < NA_HD) if hh == 0 else (lane >= NA_HD)
            qm = jnp.where(sel, q, 0.0).astype(BF16)
            s_nb = lax.dot_general(qm, kw, NT_DIMS, preferred_element_type=F32) + bias_ref[0, hh, d]
            s_cx = lax.dot_general(qm, kc, NT_DIMS, preferred_element_type=F32)
            outs.append(_softmax_pv([s_nb, s_cx], [vw, vc]))
        o = jnp.where(lane < NA_HD, outs[0], outs[1])
        o_ref[0, g * GRID_W:(g + 1) * GRID_W, :] = o.astype(o_ref.dtype)


def _na_attention(qkv_l, qkv_c, bias):
    b, length, _ = qkv_l.shape
    lc = qkv_c.shape[1]
    n_rows = length // GRID_W
    assert n_rows >= NA_KH
    g_rows = NA_ROWS
    npair = NA_HEADS // 2
    tq = g_rows * GRID_W
    return pl.pallas_call(
        functools.partial(_na_kernel, g_rows=g_rows, n_rows=n_rows),
        grid=(b, npair, n_rows // g_rows),
        in_specs=[pl.BlockSpec((1, tq, LANES), lambda bi, p, i: (bi, i, p)),
                  pl.BlockSpec((1, length, LANES), lambda bi, p, i: (bi, 0, npair + p)),
                  pl.BlockSpec((1, length, LANES), lambda bi, p, i: (bi, 0, 2 * npair + p)),
                  pl.BlockSpec((1, lc, LANES), lambda bi, p, i: (bi, 0, npair + p)),
                  pl.BlockSpec((1, lc, LANES), lambda bi, p, i: (bi, 0, 2 * npair + p)),
                  pl.BlockSpec((1, 2, NA_KH, GRID_W, NA_KH * GRID_W), lambda bi, p, i: (p, 0, 0, 0, 0))],
        out_specs=pl.BlockSpec((1, tq, LANES), lambda bi, p, i: (bi, i, p)),
        out_shape=jax.ShapeDtypeStruct((b, length, NA_HEADS * NA_HD), BF16),
        compiler_params=_params(3, 40),
        name="neighbourhood_attention",
    )(qkv_l, qkv_l, qkv_l, qkv_c, qkv_c, bias)


def _ctx_attn_kernel(q_ref, k_ref, v_ref, o_ref):
    q = q_ref[0].astype(F32)
    k = k_ref[0]
    v = v_ref[0]
    lane = lax.broadcasted_iota(jnp.int32, q.shape, 1)
    outs = []
    for hh in range(2):
        sel = (lane < NA_HD) if hh == 0 else (lane >= NA_HD)
        qm = jnp.where(sel, q, 0.0).astype(BF16)
        s = lax.dot_general(qm, k, NT_DIMS, preferred_element_type=F32)
        outs.append(_softmax_pv([s], [v]))
    o_ref[0] = jnp.where(lane < NA_HD, outs[0], outs[1]).astype(o_ref.dtype)


def _ctx_attention(qkv_c):
    b, lc, _ = qkv_c.shape
    npair = NA_HEADS // 2
    return pl.pallas_call(
        _ctx_attn_kernel,
        grid=(b, npair),
        in_specs=[pl.BlockSpec((1, lc, LANES), lambda bi, p: (bi, 0, p)),
                  pl.BlockSpec((1, lc, LANES), lambda bi, p: (bi, 0, npair + p)),
                  pl.BlockSpec((1, lc, LANES), lambda bi, p: (bi, 0, 2 * npair + p))],
        out_specs=pl.BlockSpec((1, lc, LANES), lambda bi, p: (bi, 0, p)),
        out_shape=jax.ShapeDtypeStruct((b, lc, NA_HEADS * NA_HD), BF16),
        compiler_params=_params(2, 32),
        name="ctx_attention",
    )(qkv_c, qkv_c, qkv_c)


def _merge_kernel(oret_ref, rg_ref, odiff_ref, ona_ref, gates_ref, x_ref, g1_ref, lng_ref, lnb_ref,
                  wpa_ref, wpb_ref, wpc_ref, wo_ref, o_ref, *, alpha):
    d = D_MODEL
    rg = rg_ref[...]
    ya_parts = []
    for hh in range(RET_HEADS):
        a = hh * RET_DK
        seg = _layer_norm_rows(oret_ref[:, a:a + RET_DK])
        g = rg[:, a:a + RET_DK]
        ya_parts.append((g * _sigmoid(g) * seg).astype(BF16))
    ya_in = jnp.concatenate(ya_parts, axis=1)
    y_a = jnp.dot(ya_in, wpa_ref[...], preferred_element_type=F32)
    y_b = jnp.dot(odiff_ref[...], wpb_ref[...], preferred_element_type=F32)
    y_c = jnp.dot(ona_ref[...], wpc_ref[...], preferred_element_type=F32)
    m = (_sigmoid(gates_ref[:, 0:d]) * y_a + _sigmoid(gates_ref[:, d:2 * d]) * y_b
         + _sigmoid(gates_ref[:, 2 * d:3 * d]) * y_c)
    y = jnp.dot(m.astype(BF16), wo_ref[...], preferred_element_type=F32)
    z = alpha * x_ref[...] + g1_ref[0] * y
    o_ref[...] = _layer_norm_rows(z) * lng_ref[...] + lnb_ref[...]


def _merge(o_ret, rg, o_diff, o_na, gates, x, g1, ln_g, ln_b, w_pa, w_pb, w_pc, w_o, mod_row, alpha):
    r, d = x.shape
    tm = min(MERGE_TILE, r)
    row = lambda i: (i, 0)
    const = lambda i: (0, 0)
    wspec = pl.BlockSpec((d, d), const)
    return pl.pallas_call(
        functools.partial(_merge_kernel, alpha=alpha),
        grid=(r // tm,),
        in_specs=[pl.BlockSpec((tm, d), row), pl.BlockSpec((tm, d), row), pl.BlockSpec((tm, d), row),
                  pl.BlockSpec((tm, d), row), pl.BlockSpec((tm, 3 * d), row), pl.BlockSpec((tm, d), row),
                  pl.BlockSpec((1, 1, d), lambda i: (mod_row(i, tm), 0, 0)),
                  pl.BlockSpec((1, d), const), pl.BlockSpec((1, d), const),
                  wspec, wspec, wspec, wspec],
        out_specs=pl.BlockSpec((tm, d), row),
        out_shape=jax.ShapeDtypeStruct((r, d), F32),
        compiler_params=_params(1, 52),
        name="merge_out_proj",
    )(o_ret, rg, o_diff, o_na, gates, x, g1, ln_g, ln_b, w_pa, w_pb, w_pc, w_o)


def _ffn_kernel(x_ref, sh_ref, sc_ref, g2_ref, lng_ref, lnb_ref, w1_ref, w2_ref, o_ref, *, alpha, ff_chunk):
    x = x_ref[...]
    h = (_layer_norm_rows(x) * (1.0 + sc_ref[0]) + sh_ref[0]).astype(BF16)
    acc = jnp.zeros(x.shape, F32)
    for cidx in range(D_FF // ff_chunk):
        a = cidx * ff_chunk
        u = jnp.maximum(jnp.dot(h, w1_ref[:, a:a + ff_chunk], preferred_element_type=F32), 0.0)
        acc = acc + jnp.dot((u * u).astype(BF16), w2_ref[a:a + ff_chunk, :], preferred_element_type=F32)
    z = alpha * x + g2_ref[0] * acc
    o_ref[...] = _layer_norm_rows(z) * lng_ref[...] + lnb_ref[...]


def _ffn(x, shift, scale, g2, ln_g, ln_b, w1, w2, mod_row, alpha):
    r, d = x.shape
    tm = min(FFN_TILE, r)
    row = lambda i: (i, 0)
    const = lambda i: (0, 0)
    mspec = pl.BlockSpec((1, 1, d), lambda i: (mod_row(i, tm), 0, 0))
    return pl.pallas_call(
        functools.partial(_ffn_kernel, alpha=alpha, ff_chunk=1024),
        grid=(r // tm,),
        in_specs=[pl.BlockSpec((tm, d), row), mspec, mspec, mspec,
                  pl.BlockSpec((1, d), const), pl.BlockSpec((1, d), const),
                  pl.BlockSpec((d, D_FF), const), pl.BlockSpec((D_FF, d), const)],
        out_specs=pl.BlockSpec((tm, d), row),
        out_shape=jax.ShapeDtypeStruct((r, d), F32),
        compiler_params=_params(1, 56),
        name="ffn",
    )(x, shift, scale, g2, ln_g, ln_b, w1, w2)


def _retention_tables(length):
    t = jnp.arange(length, dtype=F32)
    inv = ROPE_BASE ** (-jnp.linspace(0.0, 1.0, RET_DK // 2, dtype=F32))
    ang = t[:, None] * inv[None, :]
    return jnp.cos(ang), jnp.sin(ang)


def _axial_tables(length):
    t = jnp.arange(length)
    row = (t // GRID_W).astype(F32)
    col = (t % GRID_W).astype(F32)
    half = DIFF_HD // 2
    inv = ROPE_BASE ** (-jnp.arange(0, half, 2, dtype=F32) / half)
    ang_r = row[:, None] * inv[None, :]
    ang_c = col[:, None] * inv[None, :]
    zero = jnp.zeros_like(ang_r)
    unit_ang = jnp.concatenate([ang_r, ang_r, ang_c, ang_c], axis=-1)
    cos = jnp.cos(unit_ang)
    sin_r, sin_c = jnp.sin(ang_r), jnp.sin(ang_c)
    sin_lo = jnp.concatenate([-sin_r, zero, -sin_c, zero], axis=-1)
    sin_hi = jnp.concatenate([zero, sin_r, zero, sin_c], axis=-1)
    rep = LANES // DIFF_HD
    return tuple(jnp.tile(a, (1, rep)) for a in (cos, sin_lo, sin_hi))


def _na_bias_table(rpb):
    j = jnp.arange(GRID_W)
    cs = jnp.clip(j - NA_KW // 2, 0, GRID_W - NA_KW)
    col_mask = (j[None, :] >= cs[:, None]) & (j[None, :] < cs[:, None] + NA_KW)
    col_idx = jnp.clip(j[None, :] - j[:, None] + NA_KW - 1, 0, 2 * NA_KW - 2)
    dd = jnp.arange(NA_KH)
    row_idx = dd[None, :] - dd[:, None] + NA_KH - 1
    tbl = rpb[:, row_idx[:, None, :, None], col_idx[None, :, None, :]]
    tbl = jnp.where(col_mask[None, None, :, None, :], tbl.astype(F32), NEG_BIG)
    return tbl.reshape(NA_HEADS // 2, 2, NA_KH, GRID_W, NA_KH * GRID_W)


def _column_scale():
    d = D_MODEL
    cs = jnp.ones((IN_WIDTH,), F32)
    cs = cs.at[d:2 * d].set(RET_DK ** -0.5)
    cs = cs.at[4 * d:5 * d].set(DIFF_HD ** -0.5)
    cs = cs.at[7 * d:8 * d].set(NA_HD ** -0.5)
    return cs.reshape(1, IN_WIDTH)


def _project_all(h, w, colscale, rope, ret_tables, ax_tables, tiles_per_batch, tk_out):
    d = D_MODEL
    if rope:
        ret_qk = _proj(h, w, colscale, 0, 2 * d, BF16, "rope_ret", ret_tables, tiles_per_batch)
        diff_qk = _proj(h, w, colscale, 4 * d, 2 * d, BF16, "rope_diff", ax_tables, tiles_per_batch)
    else:
        ret_qk = _proj(h, w, colscale, 0, 2 * d, BF16)
        diff_qk = _proj(h, w, colscale, 4 * d, 2 * d, BF16)
    ret_v = _proj(h, w, colscale, 2 * d, d, BF16)
    ret_g = _proj(h, w, colscale, 3 * d, d, F32)
    diff_vt = _proj(h, w, colscale, 6 * d, d, BF16, tk_out=tk_out)
    na_qkv = _proj(h, w, colscale, 7 * d, 3 * d, BF16)
    gates = _proj(h, w, colscale, 10 * d, 3 * d, F32)
    return ret_qk, ret_v, ret_g, diff_qk, diff_vt, na_qkv, gates


def kernel(x, c, ctx, c_ctx, w_mod, b_mod, w_in, ret_decay_f, ret_decay_b, diff_lq1, diff_lk1, diff_lq2, diff_lk2,
           diff_subln_g, na_rpb, w_pa, w_pb, w_pc, w_o, ln1_g, ln1_b, w_ff1, w_ff2, ln2_g, ln2_b):
    b, length, d = x.shape
    lc = ctx.shape[1]
    depth = w_mod.shape[0]
    assert d == D_MODEL and b < 8 and length % ROW_TILE == 0 and length % GRID_W == 0
    alpha = (2.0 * depth) ** 0.25

    cvec = jnp.zeros((8, d), F32).at[:b].set(c).at[b].set(c_ctx)
    mods = _modulation(cvec, w_mod, b_mod)

    lat_row = lambda i, tm: (i * tm) // length
    ctx_row = lambda i, tm: b
    colscale = _column_scale()
    ret_tables = _retention_tables(length)
    ax_tables = _axial_tables(length)
    tk = min(DIFF_TK, length)

    x_l = x.reshape(b * length, d)
    x_c = ctx.reshape(b * lc, d)
    zero_state = jnp.zeros((b, RET_HEADS, RET_DK, RET_DK), F32)
    for l in range(depth):
        with_ctx_out = l < depth - 1
        mod = [mods[l, :, k * d:(k + 1) * d].reshape(8, 1, d) for k in range(6)]
        sh1, sc1, g1, sh2, sc2, g2 = mod
        w = w_in[l].astype(BF16)
        wpa, wpb, wpc, wo = (a[l].astype(BF16) for a in (w_pa, w_pb, w_pc, w_o))
        w1, w2 = w_ff1[l].astype(BF16), w_ff2[l].astype(BF16)
        lng1, lnb1 = ln1_g[l].reshape(1, d), ln1_b[l].reshape(1, d)
        lng2, lnb2 = ln2_g[l].reshape(1, d), ln2_b[l].reshape(1, d)
        lam_init = 0.8 - 0.6 * math.exp(-0.3 * l)
        lam4 = jnp.stack([diff_lq1[l], diff_lk1[l], diff_lq2[l], diff_lk2[l]]).astype(F32)
        subln = diff_subln_g[l].reshape(1, DIFF_DV).astype(F32)
        bias = _na_bias_table(na_rpb[l])

        h_l = _ln_mod(x_l, sh1, sc1, lat_row)
        h_c = _ln_mod(x_c, sh1, sc1, ctx_row)
        rqk_l, rv_l, rg_l, dqk_l, dvt_l, nqkv_l, gates_l = _project_all(
            h_l, w, colscale, True, ret_tables, ax_tables, length // ROW_TILE, tk)
        rqk_c, rv_c, rg_c, dqk_c, dvt_c, nqkv_c, gates_c = _project_all(
            h_c, w, colscale, False, (), (), 1, lc)
        sh3 = lambda a, n: a.reshape(b, n, a.shape[-1])

        rqk_l3, rv_l3, rqk_c3, rv_c3 = sh3(rqk_l, length), sh3(rv_l, length), sh3(rqk_c, lc), sh3(rv_c, lc)
        o_cf, s_cf = _retention(ret_decay_f[l], rqk_c3, rv_c3, zero_state, None, False)
        o_rc, s_cb = _retention(ret_decay_b[l], rqk_c3, rv_c3, zero_state, o_cf, True)
        o_lf, _ = _retention(ret_decay_f[l], rqk_l3, rv_l3, s_cf, None, False)
        o_rl, _ = _retention(ret_decay_b[l], rqk_l3, rv_l3, s_cb, o_lf, True)

        dqk_l3, dqk_c3 = sh3(dqk_l, length), sh3(dqk_c, lc)
        o_dl = _diff_attention(lam4, subln, dqk_l3, (dqk_l3, dvt_l), (dqk_c3, dvt_c), lam_init)
        nqkv_l3, nqkv_c3 = sh3(nqkv_l, length), sh3(nqkv_c, lc)
        o_nl = _na_attention(nqkv_l3, nqkv_c3, bias)

        x_l = _merge(o_rl.reshape(b * length, d), rg_l, o_dl.reshape(b * length, d), o_nl.reshape(b * length, d),
                     gates_l, x_l, g1, lng1, lnb1, wpa, wpb, wpc, wo, lat_row, alpha)
        x_l = _ffn(x_l, sh2, sc2, g2, lng2, lnb2, w1, w2, lat_row, alpha)
        if with_ctx_out:
            o_dc = _diff_attention(lam4, subln, dqk_c3, None, (dqk_c3, dvt_c), lam_init)
            o_nc = _ctx_attention(nqkv_c3)
            x_c = _merge(o_rc.reshape(b * lc, d), rg_c, o_dc.reshape(b * lc, d), o_nc.reshape(b * lc, d),
                         gates_c, x_c, g1, lng1, lnb1, wpa, wpb, wpc, wo, ctx_row, alpha)
            x_c = _ffn(x_c, sh2, sc2, g2, lng2, lnb2, w1, w2, ctx_row, alpha)
    return x_l.reshape(b, length, d)
```

```python
import functools
import math

import jax
import jax.numpy as jnp
from jax import lax
from jax.experimental import pallas as pl
from jax.experimental.pallas import tpu as pltpu

F32 = jnp.float32
BF16 = jnp.bfloat16

D_MODEL = 1024
GRID_W = 64
RET_HEADS, RET_DK = 4, 256
DIFF_HEADS, DIFF_HD, DIFF_DV = 8, 64, 128
NA_HEADS, NA_HD, NA_KH, NA_KW = 16, 64, 8, 16
D_FF = 4 * D_MODEL
ROPE_BASE = 10000.0
LN_EPS = 1e-6
SUBLN_EPS = 1e-5
IN_WIDTH = 13 * D_MODEL
NEG_BIG = -1e30

LANES = 128
VMEM_LIMIT_CAP = 56 << 20

ROW_TILE = 1024
LN_TILE = 512
MERGE_TILE = 256
FFN_TILE = 512
RET_CHUNK = 256
DIFF_TQ = 256
DIFF_PIECE = 256
NA_ROWS = 8

NT_DIMS = (((1,), (1,)), ((), ()))
TN_DIMS = (((0,), (0,)), ((), ()))


def _params(n_axes, vmem_mb):
    return pltpu.CompilerParams(dimension_semantics=("arbitrary",) * n_axes,
                                vmem_limit_bytes=min(vmem_mb << 20, VMEM_LIMIT_CAP))


def _sigmoid(x):
    return 1.0 / (1.0 + jnp.exp(-x))


def _layer_norm_rows(x):
    mu = jnp.mean(x, axis=-1, keepdims=True)
    xc = x - mu
    var = jnp.mean(xc * xc, axis=-1, keepdims=True)
    return xc * lax.rsqrt(var + LN_EPS)


def _mod_kernel(c_ref, w_ref, b_ref, o_ref):
    c = c_ref[...]
    a = (c * _sigmoid(c)).astype(BF16)
    o_ref[0] = jnp.dot(a, w_ref[0].astype(BF16), preferred_element_type=F32) + b_ref[0]


def _modulation(cvec, w_mod, b_mod):
    depth = w_mod.shape[0]
    d = D_MODEL
    return pl.pallas_call(
        _mod_kernel,
        grid=(depth, 6),
        in_specs=[pl.BlockSpec((8, d), lambda l, j: (0, 0)),
                  pl.BlockSpec((1, d, d), lambda l, j: (l, 0, j)),
                  pl.BlockSpec((1, 1, d), lambda l, j: (l, 0, j))],
        out_specs=pl.BlockSpec((1, 8, d), lambda l, j: (l, 0, j)),
        out_shape=jax.ShapeDtypeStruct((depth, 8, 6 * d), F32),
        compiler_params=_params(2, 32),
        name="modulation",
    )(cvec, w_mod, b_mod.reshape(depth, 1, 6 * d))


def _ln_mod_kernel(x_ref, sh_ref, sc_ref, o_ref):
    y = _layer_norm_rows(x_ref[...])
    o_ref[...] = (y * (1.0 + sc_ref[0]) + sh_ref[0]).astype(o_ref.dtype)


def _ln_mod(x, shift, scale, mod_row):
    r, d = x.shape
    tm = min(LN_TILE, r)
    return pl.pallas_call(
        _ln_mod_kernel,
        grid=(r // tm,),
        in_specs=[pl.BlockSpec((tm, d), lambda i: (i, 0)),
                  pl.BlockSpec((1, 1, d), lambda i: (mod_row(i, tm), 0, 0)),
                  pl.BlockSpec((1, 1, d), lambda i: (mod_row(i, tm), 0, 0))],
        out_specs=pl.BlockSpec((tm, d), lambda i: (i, 0)),
        out_shape=jax.ShapeDtypeStruct((r, d), BF16),
        compiler_params=_params(1, 32),
        name="ln_modulate",
    )(x, shift, scale)


def _proj_kernel(h_ref, w_ref, cs_ref, *rest, mode, transposed, tn, tk_out):
    o_ref = rest[-1]
    acc = jnp.dot(h_ref[...], w_ref[...], preferred_element_type=F32) * cs_ref[...]
    if mode == "rope_ret":
        cos = rest[0][...]
        sin = rest[1][...]
        half = RET_DK // 2
        for hh in range(tn // RET_DK):
            a = hh * RET_DK
            x1 = acc[:, a:a + half]
            x2 = acc[:, a + half:a + RET_DK]
            o_ref[:, a:a + half] = (x1 * cos - x2 * sin).astype(o_ref.dtype)
            o_ref[:, a + half:a + RET_DK] = (x1 * sin + x2 * cos).astype(o_ref.dtype)
    elif mode == "rope_diff":
        cos = rest[0][...]
        sin_lo = rest[1][...]
        sin_hi = rest[2][...]
        q16 = DIFF_HD // 4
        for g in range(tn // LANES):
            x = acc[:, g * LANES:(g + 1) * LANES]
            y = x * cos + pltpu.roll(x, LANES - q16, 1) * sin_lo + pltpu.roll(x, q16, 1) * sin_hi
            o_ref[:, g * LANES:(g + 1) * LANES] = y.astype(o_ref.dtype)
    elif transposed:
        acc_t = acc.T
        for a in range(o_ref.shape[0]):
            o_ref[a] = acc_t[:, a * tk_out:(a + 1) * tk_out].astype(o_ref.dtype)
    else:
        o_ref[...] = acc.astype(o_ref.dtype)


def _proj(h, w, colscale, col0, ncols, out_dtype, mode="plain", tables=(), tiles_per_batch=1, tk_out=None):
    r, d = h.shape
    tm = min(ROW_TILE, r)
    tn = 512
    assert ncols % tn == 0 and col0 % tn == 0 and r % tm == 0
    jb = col0 // tn
    transposed = tk_out is not None
    in_specs = [pl.BlockSpec((tm, d), lambda i, j: (i, 0)),
                pl.BlockSpec((d, tn), lambda i, j: (0, jb + j)),
                pl.BlockSpec((1, tn), lambda i, j: (0, jb + j))]
    for _ in tables:
        in_specs.append(pl.BlockSpec((tm, LANES), lambda i, j: (i % tiles_per_batch, 0)))
    if transposed:
        assert tm % tk_out == 0
        na = tm // tk_out
        out_specs = pl.BlockSpec((na, tn, tk_out), lambda i, j: (i, j, 0))
        out_shape = jax.ShapeDtypeStruct((r // tk_out, ncols, tk_out), out_dtype)
    else:
        out_specs = pl.BlockSpec((tm, tn), lambda i, j: (i, j))
        out_shape = jax.ShapeDtypeStruct((r, ncols), out_dtype)
    return pl.pallas_call(
        functools.partial(_proj_kernel, mode=mode, transposed=transposed, tn=tn, tk_out=tk_out),
        grid=(r // tm, ncols // tn),
        in_specs=in_specs,
        out_specs=out_specs,
        out_shape=out_shape,
        compiler_params=_params(2, 40),
        name="in_proj_" + mode + ("_t" if transposed else ""),
    )(h, w, colscale, *tables)


def _ret_kernel(dec_ref, q_ref, k_ref, v_ref, s0_ref, *rest, reverse, has_prev, chunk):
    if has_prev:
        prev_ref, o_ref, sfin_ref, s_scr, intra_scr, qd_scr, kd_scr, cd_scr = rest
    else:
        o_ref, sfin_ref, s_scr, intra_scr, qd_scr, kd_scr, cd_scr = rest
    head = pl.program_id(1)
    c = pl.program_id(2)
    cf = float(chunk)

    @pl.when(c == 0)
    def _init():
        s_scr[...] = s0_ref[0, 0]
        dv = jnp.full((1, 1), dec_ref[head], F32)
        lg = jnp.minimum(dv, 0.0) - jnp.log(1.0 + jnp.exp(-jnp.abs(dv)))
        ii = lax.broadcasted_iota(jnp.int32, (chunk, chunk), 0).astype(F32)
        jj = lax.broadcasted_iota(jnp.int32, (chunk, chunk), 1).astype(F32)
        dist = (jj - ii) if reverse else (ii - jj)
        intra_scr[...] = jnp.where(dist >= 0.0, jnp.exp(lg * jnp.maximum(dist, 0.0)), 0.0)
        i1 = lax.broadcasted_iota(jnp.int32, (chunk, 1), 0).astype(F32)
        qd_scr[...] = jnp.exp(lg * ((cf - i1) if reverse else (i1 + 1.0)))
        kd_scr[...] = jnp.exp(lg * (i1 if reverse else (cf - 1.0 - i1)))
        cd_scr[...] = jnp.exp(lg * cf)

    q = q_ref[0]
    k = k_ref[0]
    v = v_ref[0]
    s = s_scr[...]
    att = lax.dot_general(q, k, NT_DIMS, preferred_element_type=F32) * intra_scr[...]
    qd = (q.astype(F32) * qd_scr[...]).astype(BF16)
    o = (jnp.dot(att.astype(BF16), v, preferred_element_type=F32)
         + jnp.dot(qd, s.astype(BF16), preferred_element_type=F32))
    kd = (k.astype(F32) * kd_scr[...]).astype(BF16)
    s_new = s * cd_scr[...] + lax.dot_general(kd, v, TN_DIMS, preferred_element_type=F32)
    s_scr[...] = s_new
    if has_prev:
        o = o + prev_ref[0]
    o_ref[0] = o

    @pl.when(c == pl.num_programs(2) - 1)
    def _fin():
        sfin_ref[0, 0] = s_new


def _retention(dec, qk, v, s0, prev, reverse):
    b, length, _ = qk.shape
    hh, dk = RET_HEADS, RET_DK
    c = min(RET_CHUNK, length)
    n = length // c
    pos = (lambda i: n - 1 - i) if reverse else (lambda i: i)
    in_specs = [pl.BlockSpec(memory_space=pltpu.SMEM),
                pl.BlockSpec((1, c, dk), lambda bi, h, i: (bi, pos(i), h)),
                pl.BlockSpec((1, c, dk), lambda bi, h, i: (bi, pos(i), hh + h)),
                pl.BlockSpec((1, c, dk), lambda bi, h, i: (bi, pos(i), h)),
                pl.BlockSpec((1, 1, dk, dk), lambda bi, h, i: (bi, h, 0, 0))]
    args = [dec, qk, qk, v, s0]
    if prev is not None:
        in_specs.append(pl.BlockSpec((1, c, dk), lambda bi, h, i: (bi, pos(i), h)))
        args.append(prev)
    return pl.pallas_call(
        functools.partial(_ret_kernel, reverse=reverse, has_prev=prev is not None, chunk=c),
        grid=(b, hh, n),
        in_specs=in_specs,
        out_specs=[pl.BlockSpec((1, c, dk), lambda bi, h, i: (bi, pos(i), h)),
                   pl.BlockSpec((1, 1, dk, dk), lambda bi, h, i: (bi, h, 0, 0))],
        out_shape=[jax.ShapeDtypeStruct((b, length, hh * dk), F32),
                   jax.ShapeDtypeStruct((b, hh, dk, dk), F32)],
        scratch_shapes=[pltpu.VMEM((dk, dk), F32), pltpu.VMEM((c, c), F32),
                        pltpu.VMEM((c, 1), F32), pltpu.VMEM((c, 1), F32), pltpu.VMEM((1, 1), F32)],
        compiler_params=_params(3, 32),
        name="retention_bwd" if reverse else "retention_fwd",
    )(*args)


def _diff_kernel(lam_ref, g_ref, q_ref, *rest, tq, n_lat, n_ctx, ppt, lam_init):
    if n_lat:
        kl_ref, vtl_ref = rest[:2]
        rest = rest[2:]
    (kc_ref, vtc_ref, o_ref, qq_scr, k_scr, vt_scr, s_a, s_b, p_a, p_b, al_a, al_b,
     m_scr, l_scr, acc_scr) = rest
    s_buf, p_buf, al_buf = (s_a, s_b), (p_a, p_b), (al_a, al_b)
    piece = DIFF_PIECE
    tkd = ppt * piece
    n = (n_lat + n_ctx) // ppt

    @pl.when(pl.program_id(2) == 0)
    def _gather_keys():
        if n_lat:
            k_scr[0:n_lat * piece, :] = kl_ref[0]
        k_scr[n_lat * piece:(n_lat + n_ctx) * piece, :] = kc_ref[0]
        for pc in range(n_lat + n_ctx):
            t, u = divmod(pc, ppt)
            vt_scr[t, :, u * piece:(u + 1) * piece] = vtl_ref[pc] if pc < n_lat else vtc_ref[pc - n_lat]

    q = q_ref[0].astype(F32)
    lane = lax.broadcasted_iota(jnp.int32, q.shape, 1)
    qq_scr[0:tq, :] = jnp.where(lane < DIFF_HD, q, 0.0).astype(BF16)
    qq_scr[tq:2 * tq, :] = jnp.where(lane >= DIFF_HD, q, 0.0).astype(BF16)
    m_scr[...] = jnp.full(m_scr.shape, NEG_BIG, F32)
    l_scr[...] = jnp.zeros(l_scr.shape, F32)
    acc_scr[...] = jnp.zeros(acc_scr.shape, F32)

    def scores(j, slot):
        off = j * tkd if isinstance(j, int) else pl.multiple_of(j * tkd, tkd)
        s_buf[slot][...] = lax.dot_general(k_scr[pl.ds(off, tkd), :], qq_scr[...], NT_DIMS,
                                           preferred_element_type=F32)

    def softmax(slot):
        s = s_buf[slot][...]
        m_old = m_scr[...]
        m_new = jnp.maximum(m_old, jnp.max(s, axis=0, keepdims=True))
        alpha = jnp.exp2(m_old - m_new)
        p = jnp.exp2(s - m_new)
        l_scr[...] = alpha * l_scr[...] + jnp.sum(p, axis=0, keepdims=True)
        m_scr[...] = m_new
        al_buf[slot][...] = alpha
        p_buf[slot][...] = p.astype(BF16)

    def values(j, slot):
        acc_scr[...] = (acc_scr[...] * al_buf[slot][...]
                        + jnp.dot(vt_scr[j], p_buf[slot][...], preferred_element_type=F32))

    scores(0, 0)
    if n == 1:
        softmax(0)
    else:
        scores(1, 1)
        softmax(0)
        n_pairs = (n - 2) // 2 if n >= 6 else 0

        def pair(i, carry):
            j = 1 + 2 * i
            scores(j + 1, 0)
            softmax(1)
            values(j - 1, 0)
            scores(j + 2, 1)
            softmax(0)
            values(j, 1)
            return carry

        if n_pairs:
            lax.fori_loop(0, n_pairs, pair, 0)
        for j in range(1 + 2 * n_pairs, n - 1):
            scores(j + 1, (j + 1) % 2)
            softmax(j % 2)
            values(j - 1, (j - 1) % 2)
        softmax((n - 1) % 2)
        values(n - 2, (n - 2) % 2)
    values(n - 1, (n - 1) % 2)

    o = acc_scr[...] * (1.0 / l_scr[...])
    lam4 = lam_ref[...]
    s1 = jnp.sum(lam4[0:1] * lam4[1:2], axis=-1, keepdims=True)
    s2 = jnp.sum(lam4[2:3] * lam4[3:4], axis=-1, keepdims=True)
    lam = jnp.exp(s1) - jnp.exp(s2) + lam_init
    y = (o[:, :tq] - lam * o[:, tq:]).T
    ms = jnp.mean(y * y, axis=-1, keepdims=True)
    y = y * lax.rsqrt(ms + SUBLN_EPS) * g_ref[...] * (1.0 - lam_init)
    o_ref[0] = y.astype(o_ref.dtype)


def _diff_attention(lam4, subln_g, q_arr, lat, ctx, lam_init):
    b, lq, _ = q_arr.shape
    hh = DIFF_HEADS
    tq = min(DIFF_TQ, lq)
    piece = DIFF_PIECE
    qk_c, vt_c = ctx
    lc = qk_c.shape[1]
    n_ctx = lc // piece
    in_specs = [pl.BlockSpec((4, DIFF_HD), lambda bi, h, i: (0, 0)),
                pl.BlockSpec((1, DIFF_DV), lambda bi, h, i: (0, 0)),
                pl.BlockSpec((1, tq, LANES), lambda bi, h, i: (bi, i, h))]
    args = [lam4, subln_g, q_arr]
    n_lat = 0
    if lat is not None:
        qk_l, vt_l = lat
        length = qk_l.shape[1]
        n_lat = length // piece
        in_specs += [pl.BlockSpec((1, length, LANES), lambda bi, h, i: (bi, 0, hh + h)),
                     pl.BlockSpec((n_lat, DIFF_DV, piece), lambda bi, h, i: (bi, h, 0))]
        args += [qk_l, vt_l]
    in_specs += [pl.BlockSpec((1, lc, LANES), lambda bi, h, i: (bi, 0, hh + h)),
                 pl.BlockSpec((n_ctx, DIFF_DV, piece), lambda bi, h, i: (bi, h, 0))]
    args += [qk_c, vt_c]
    n_pieces = n_lat + n_ctx
    ppt = next(c for c in (3, 2, 1) if n_pieces % c == 0)
    tkd = ppt * piece
    n_tiles = n_pieces // ppt
    return pl.pallas_call(
        functools.partial(_diff_kernel, tq=tq, n_lat=n_lat, n_ctx=n_ctx, ppt=ppt, lam_init=lam_init),
        grid=(b, hh, lq // tq),
        in_specs=in_specs,
        out_specs=pl.BlockSpec((1, tq, LANES), lambda bi, h, i: (bi, i, h)),
        out_shape=jax.ShapeDtypeStruct((b, lq, hh * DIFF_DV), BF16),
        scratch_shapes=[pltpu.VMEM((2 * tq, LANES), BF16),
                        pltpu.VMEM((n_pieces * piece, LANES), BF16),
                        pltpu.VMEM((n_tiles, DIFF_DV, tkd), BF16),
                        pltpu.VMEM((tkd, 2 * tq), F32), pltpu.VMEM((tkd, 2 * tq), F32),
                        pltpu.VMEM((tkd, 2 * tq), BF16), pltpu.VMEM((tkd, 2 * tq), BF16),
                        pltpu.VMEM((1, 2 * tq), F32), pltpu.VMEM((1, 2 * tq), F32),
                        pltpu.VMEM((1, 2 * tq), F32), pltpu.VMEM((1, 2 * tq), F32),
                        pltpu.VMEM((DIFF_DV, 2 * tq), F32)],
        compiler_params=_params(3, 40),
        name="diff_attention" if lat is not None else "diff_attention_ctx",
    )(*args)


def _softmax_pv(s_list, v_list):
    m = s_list[0].max(axis=-1, keepdims=True)
    for s in s_list[1:]:
        m = jnp.maximum(m, s.max(axis=-1, keepdims=True))
    den = 0.0
    o = 0.0
    for s, v in zip(s_list, v_list):
        p = jnp.exp(s - m)
        den = den + jnp.sum(p, axis=-1, keepdims=True)
        o = o + jnp.dot(p.astype(BF16), v, preferred_element_type=F32)
    return o * (1.0 / den)


def _na_kernel(q_ref, k_ref, v_ref, kc_ref, vc_ref, bias_ref, o_ref, *, g_rows, n_rows):
    r0 = pl.program_id(2) * g_rows
    w = GRID_W
    win = NA_KH * w
    unit = 2 * w
    lo = lax.broadcasted_iota(jnp.int32, (w, LANES), 1) < NA_HD
    q_all = q_ref[0].astype(F32)
    qs, offs, dd = [], [], []
    for g in range(g_rows):
        q = q_all[g * w:(g + 1) * w]
        qs.append(jnp.concatenate([jnp.where(lo, q, 0.0), jnp.where(lo, 0.0, q)], axis=0).astype(BF16))
        rs = jnp.clip(r0 + g - NA_KH // 2, 0, n_rows - NA_KH)
        offs.append(pl.multiple_of(rs * w, w))
        dd.append(r0 + g - rs)
    q_stack = jnp.concatenate(qs, axis=0)
    s_cx = lax.dot_general(q_stack, kc_ref[0], NT_DIMS, preferred_element_type=F32)
    s_nb = [lax.dot_general(qs[g], k_ref[0, pl.ds(offs[g], win), :], NT_DIMS, preferred_element_type=F32)
            + bias_ref[0, dd[g]] for g in range(g_rows)]
    m_cx = jnp.max(s_cx, axis=-1, keepdims=True)
    ms = [jnp.maximum(jnp.max(s_nb[g], axis=-1, keepdims=True), m_cx[g * unit:(g + 1) * unit])
          for g in range(g_rows)]
    p_cx = jnp.exp(s_cx - jnp.concatenate(ms, axis=0))
    l_cx = jnp.sum(p_cx, axis=-1, keepdims=True)
    o_cx = jnp.dot(p_cx.astype(BF16), vc_ref[0], preferred_element_type=F32)
    ps = [jnp.exp(s_nb[g] - ms[g]) for g in range(g_rows)]
    ls = [jnp.sum(ps[g], axis=-1, keepdims=True) + l_cx[g * unit:(g + 1) * unit] for g in range(g_rows)]
    os_ = [jnp.dot(ps[g].astype(BF16), v_ref[0, pl.ds(offs[g], win), :], preferred_element_type=F32)
           + o_cx[g * unit:(g + 1) * unit] for g in range(g_rows)]
    for g in range(g_rows):
        o = os_[g] * (1.0 / ls[g])
        o_ref[0, g * w:(g + 1) * w, :] = jnp.where(lo, o[0:w], o[w:unit]).astype(o_ref.dtype)


def _na_attention(qkv_l, qkv_c, bias):
    b, length, _ = qkv_l.shape
    lc = qkv_c.shape[1]
    n_rows = length // GRID_W
    assert n_rows >= NA_KH
    g_rows = NA_ROWS
    npair = NA_HEADS // 2
    tq = g_rows * GRID_W
    return pl.pallas_call(
        functools.partial(_na_kernel, g_rows=g_rows, n_rows=n_rows),
        grid=(b, npair, n_rows // g_rows),
        in_specs=[pl.BlockSpec((1, tq, LANES), lambda bi, p, i: (bi, i, p)),
                  pl.BlockSpec((1, length, LANES), lambda bi, p, i: (bi, 0, npair + p)),
                  pl.BlockSpec((1, length, LANES), lambda bi, p, i: (bi, 0, 2 * npair + p)),
                  pl.BlockSpec((1, lc, LANES), lambda bi, p, i: (bi, 0, npair + p)),
                  pl.BlockSpec((1, lc, LANES), lambda bi, p, i: (bi, 0, 2 * npair + p)),
                  pl.BlockSpec((1, NA_KH, 2 * GRID_W, NA_KH * GRID_W), lambda bi, p, i: (p, 0, 0, 0))],
        out_specs=pl.BlockSpec((1, tq, LANES), lambda bi, p, i: (bi, i, p)),
        out_shape=jax.ShapeDtypeStruct((b, length, NA_HEADS * NA_HD), BF16),
        compiler_params=_params(3, 40),
        name="neighbourhood_attention",
    )(qkv_l, qkv_l, qkv_l, qkv_c, qkv_c, bias)


def _ctx_attn_kernel(q_ref, k_ref, v_ref, o_ref):
    q = q_ref[0].astype(F32)
    k = k_ref[0]
    v = v_ref[0]
    lane = lax.broadcasted_iota(jnp.int32, q.shape, 1)
    outs = []
    for hh in range(2):
        sel = (lane < NA_HD) if hh == 0 else (lane >= NA_HD)
        qm = jnp.where(sel, q, 0.0).astype(BF16)
        s = lax.dot_general(qm, k, NT_DIMS, preferred_element_type=F32)
        outs.append(_softmax_pv([s], [v]))
    o_ref[0] = jnp.where(lane < NA_HD, outs[0], outs[1]).astype(o_ref.dtype)


def _ctx_attention(qkv_c):
    b, lc, _ = qkv_c.shape
    npair = NA_HEADS // 2
    return pl.pallas_call(
        _ctx_attn_kernel,
        grid=(b, npair),
        in_specs=[pl.BlockSpec((1, lc, LANES), lambda bi, p: (bi, 0, p)),
                  pl.BlockSpec((1, lc, LANES), lambda bi, p: (bi, 0, npair + p)),
                  pl.BlockSpec((1, lc, LANES), lambda bi, p: (bi, 0, 2 * npair + p))],
        out_specs=pl.BlockSpec((1, lc, LANES), lambda bi, p: (bi, 0, p)),
        out_shape=jax.ShapeDtypeStruct((b, lc, NA_HEADS * NA_HD), BF16),
        compiler_params=_params(2, 32),
        name="ctx_attention",
    )(qkv_c, qkv_c, qkv_c)


def _merge_kernel(oret_ref, rg_ref, odiff_ref, ona_ref, gates_ref, x_ref, g1_ref, lng_ref, lnb_ref,
                  wpa_ref, wpb_ref, wpc_ref, wo_ref, o_ref, *, alpha):
    d = D_MODEL
    rg = rg_ref[...]
    ya_parts = []
    for hh in range(RET_HEADS):
        a = hh * RET_DK
        seg = _layer_norm_rows(oret_ref[:, a:a + RET_DK])
        g = rg[:, a:a + RET_DK]
        ya_parts.append((g * _sigmoid(g) * seg).astype(BF16))
    ya_in = jnp.concatenate(ya_parts, axis=1)
    y_a = jnp.dot(ya_in, wpa_ref[...], preferred_element_type=F32)
    y_b = jnp.dot(odiff_ref[...], wpb_ref[...], preferred_element_type=F32)
    y_c = jnp.dot(ona_ref[...], wpc_ref[...], preferred_element_type=F32)
    m = (_sigmoid(gates_ref[:, 0:d]) * y_a + _sigmoid(gates_ref[:, d:2 * d]) * y_b
         + _sigmoid(gates_ref[:, 2 * d:3 * d]) * y_c)
    y = jnp.dot(m.astype(BF16), wo_ref[...], preferred_element_type=F32)
    z = alpha * x_ref[...] + g1_ref[0] * y
    o_ref[...] = _layer_norm_rows(z) * lng_ref[...] + lnb_ref[...]


def _merge(o_ret, rg, o_diff, o_na, gates, x, g1, ln_g, ln_b, w_pa, w_pb, w_pc, w_o, mod_row, alpha):
    r, d = x.shape
    tm = min(MERGE_TILE, r)
    row = lambda i: (i, 0)
    const = lambda i: (0, 0)
    wspec = pl.BlockSpec((d, d), const)
    return pl.pallas_call(
        functools.partial(_merge_kernel, alpha=alpha),
        grid=(r // tm,),
        in_specs=[pl.BlockSpec((tm, d), row), pl.BlockSpec((tm, d), row), pl.BlockSpec((tm, d), row),
                  pl.BlockSpec((tm, d), row), pl.BlockSpec((tm, 3 * d), row), pl.BlockSpec((tm, d), row),
                  pl.BlockSpec((1, 1, d), lambda i: (mod_row(i, tm), 0, 0)),
                  pl.BlockSpec((1, d), const), pl.BlockSpec((1, d), const),
                  wspec, wspec, wspec, wspec],
        out_specs=pl.BlockSpec((tm, d), row),
        out_shape=jax.ShapeDtypeStruct((r, d), F32),
        compiler_params=_params(1, 52),
        name="merge_out_proj",
    )(o_ret, rg, o_diff, o_na, gates, x, g1, ln_g, ln_b, w_pa, w_pb, w_pc, w_o)


def _ffn_kernel(x_ref, sh_ref, sc_ref, g2_ref, lng_ref, lnb_ref, w1_ref, w2_ref, o_ref, *, alpha, ff_chunk):
    x = x_ref[...]
    h = (_layer_norm_rows(x) * (1.0 + sc_ref[0]) + sh_ref[0]).astype(BF16)
    acc = jnp.zeros(x.shape, F32)
    for cidx in range(D_FF // ff_chunk):
        a = cidx * ff_chunk
        u = jnp.maximum(jnp.dot(h, w1_ref[:, a:a + ff_chunk], preferred_element_type=F32), 0.0)
        acc = acc + jnp.dot((u * u).astype(BF16), w2_ref[a:a + ff_chunk, :], preferred_element_type=F32)
    z = alpha * x + g2_ref[0] * acc
    o_ref[...] = _layer_norm_rows(z) * lng_ref[...] + lnb_ref[...]


def _ffn(x, shift, scale, g2, ln_g, ln_b, w1, w2, mod_row, alpha):
    r, d = x.shape
    tm = min(FFN_TILE, r)
    row = lambda i: (i, 0)
    const = lambda i: (0, 0)
    mspec = pl.BlockSpec((1, 1, d), lambda i: (mod_row(i, tm), 0, 0))
    return pl.pallas_call(
        functools.partial(_ffn_kernel, alpha=alpha, ff_chunk=1024),
        grid=(r // tm,),
        in_specs=[pl.BlockSpec((tm, d), row), mspec, mspec, mspec,
                  pl.BlockSpec((1, d), const), pl.BlockSpec((1, d), const),
                  pl.BlockSpec((d, D_FF), const), pl.BlockSpec((D_FF, d), const)],
        out_specs=pl.BlockSpec((tm, d), row),
        out_shape=jax.ShapeDtypeStruct((r, d), F32),
        compiler_params=_params(1, 56),
        name="ffn",
    )(x, shift, scale, g2, ln_g, ln_b, w1, w2)


def _retention_tables(length):
    t = jnp.arange(length, dtype=F32)
    inv = ROPE_BASE ** (-jnp.linspace(0.0, 1.0, RET_DK // 2, dtype=F32))
    ang = t[:, None] * inv[None, :]
    return jnp.cos(ang), jnp.sin(ang)


def _axial_tables(length):
    t = jnp.arange(length)
    row = (t // GRID_W).astype(F32)
    col = (t % GRID_W).astype(F32)
    half = DIFF_HD // 2
    inv = ROPE_BASE ** (-jnp.arange(0, half, 2, dtype=F32) / half)
    ang_r = row[:, None] * inv[None, :]
    ang_c = col[:, None] * inv[None, :]
    zero = jnp.zeros_like(ang_r)
    unit_ang = jnp.concatenate([ang_r, ang_r, ang_c, ang_c], axis=-1)
    cos = jnp.cos(unit_ang)
    sin_r, sin_c = jnp.sin(ang_r), jnp.sin(ang_c)
    sin_lo = jnp.concatenate([-sin_r, zero, -sin_c, zero], axis=-1)
    sin_hi = jnp.concatenate([zero, sin_r, zero, sin_c], axis=-1)
    rep = LANES // DIFF_HD
    return tuple(jnp.tile(a, (1, rep)) for a in (cos, sin_lo, sin_hi))


def _na_bias_table(rpb):
    nh, nr, _ = rpb.shape
    w = GRID_W
    wid = 2 * w - 1
    p = jnp.pad(rpb.astype(F32), ((0, 0), (0, 0), (w - NA_KW, w - NA_KW)))
    a = jnp.broadcast_to(p[:, :, None, :], (nh, nr, w, wid))
    a = jnp.pad(a, ((0, 0), (0, 0), (0, 0), (0, 1))).reshape(nh, nr, w * (wid + 1))
    a = a[:, :, :w * wid].reshape(nh, nr, w, wid)
    m = a[..., w - 1:]
    j = jnp.arange(w)
    cs = jnp.clip(j - NA_KW // 2, 0, w - NA_KW)
    col_mask = (j[None, :] >= cs[:, None]) & (j[None, :] < cs[:, None] + NA_KW)
    m = jnp.where(col_mask[None, None], m, NEG_BIG)
    tbl = jnp.stack([m[:, NA_KH - 1 - d:2 * NA_KH - 1 - d] for d in range(NA_KH)], axis=1)
    tbl = tbl.transpose(0, 1, 3, 2, 4).reshape(nh // 2, 2, NA_KH, w, NA_KH * w)
    return tbl.transpose(0, 2, 1, 3, 4).reshape(nh // 2, NA_KH, 2 * w, NA_KH * w)


def _column_scale():
    d = D_MODEL
    cs = jnp.ones((IN_WIDTH,), F32)
    cs = cs.at[d:2 * d].set(RET_DK ** -0.5)
    cs = cs.at[4 * d:5 * d].set(DIFF_HD ** -0.5 * math.log2(math.e))
    cs = cs.at[7 * d:8 * d].set(NA_HD ** -0.5)
    return cs.reshape(1, IN_WIDTH)


def _project_all(h, w, colscale, rope, ret_tables, ax_tables, tiles_per_batch, tk_out):
    d = D_MODEL
    if rope:
        ret_qk = _proj(h, w, colscale, 0, 2 * d, BF16, "rope_ret", ret_tables, tiles_per_batch)
        diff_qk = _proj(h, w, colscale, 4 * d, 2 * d, BF16, "rope_diff", ax_tables, tiles_per_batch)
    else:
        ret_qk = _proj(h, w, colscale, 0, 2 * d, BF16)
        diff_qk = _proj(h, w, colscale, 4 * d, 2 * d, BF16)
    ret_v = _proj(h, w, colscale, 2 * d, d, BF16)
    ret_g = _proj(h, w, colscale, 3 * d, d, F32)
    diff_vt = _proj(h, w, colscale, 6 * d, d, BF16, tk_out=tk_out)
    na_qkv = _proj(h, w, colscale, 7 * d, 3 * d, BF16)
    gates = _proj(h, w, colscale, 10 * d, 3 * d, F32)
    return ret_qk, ret_v, ret_g, diff_qk, diff_vt, na_qkv, gates


def kernel(x, c, ctx, c_ctx, w_mod, b_mod, w_in, ret_decay_f, ret_decay_b, diff_lq1, diff_lk1, diff_lq2, diff_lk2,
           diff_subln_g, na_rpb, w_pa, w_pb, w_pc, w_o, ln1_g, ln1_b, w_ff1, w_ff2, ln2_g, ln2_b):
    b, length, d = x.shape
    lc = ctx.shape[1]
    depth = w_mod.shape[0]
    assert d == D_MODEL and b < 8 and length % ROW_TILE == 0 and length % GRID_W == 0
    alpha = (2.0 * depth) ** 0.25

    cvec = jnp.zeros((8, d), F32).at[:b].set(c).at[b].set(c_ctx)
    mods = _modulation(cvec, w_mod, b_mod)

    lat_row = lambda i, tm: (i * tm) // length
    ctx_row = lambda i, tm: b
    colscale = _column_scale()
    ret_tables = _retention_tables(length)
    ax_tables = _axial_tables(length)

    x_l = x.reshape(b * length, d)
    x_c = ctx.reshape(b * lc, d)
    zero_state = jnp.zeros((b, RET_HEADS, RET_DK, RET_DK), F32)
    for l in range(depth):
        with_ctx_out = l < depth - 1
        mod = [mods[l, :, k * d:(k + 1) * d].reshape(8, 1, d) for k in range(6)]
        sh1, sc1, g1, sh2, sc2, g2 = mod
        w = w_in[l].astype(BF16)
        wpa, wpb, wpc, wo = (a[l].astype(BF16) for a in (w_pa, w_pb, w_pc, w_o))
        w1, w2 = w_ff1[l].astype(BF16), w_ff2[l].astype(BF16)
        lng1, lnb1 = ln1_g[l].reshape(1, d), ln1_b[l].reshape(1, d)
        lng2, lnb2 = ln2_g[l].reshape(1, d), ln2_b[l].reshape(1, d)
        lam_init = 0.8 - 0.6 * math.exp(-0.3 * l)
        lam4 = jnp.stack([diff_lq1[l], diff_lk1[l], diff_lq2[l], diff_lk2[l]]).astype(F32)
        subln = diff_subln_g[l].reshape(1, DIFF_DV).astype(F32)
        bias = _na_bias_table(na_rpb[l])

        h_l = _ln_mod(x_l, sh1, sc1, lat_row)
        h_c = _ln_mod(x_c, sh1, sc1, ctx_row)
        rqk_l, rv_l, rg_l, dqk_l, dvt_l, nqkv_l, gates_l = _project_all(
            h_l, w, colscale, True, ret_tables, ax_tables, length // ROW_TILE, DIFF_PIECE)
        rqk_c, rv_c, rg_c, dqk_c, dvt_c, nqkv_c, gates_c = _project_all(
            h_c, w, colscale, False, (), (), 1, DIFF_PIECE)
        sh3 = lambda a, n: a.reshape(b, n, a.shape[-1])

        rqk_l3, rv_l3, rqk_c3, rv_c3 = sh3(rqk_l, length), sh3(rv_l, length), sh3(rqk_c, lc), sh3(rv_c, lc)
        o_cf, s_cf = _retention(ret_decay_f[l], rqk_c3, rv_c3, zero_state, None, False)
        o_rc, s_cb = _retention(ret_decay_b[l], rqk_c3, rv_c3, zero_state, o_cf, True)
        o_lf, _ = _retention(ret_decay_f[l], rqk_l3, rv_l3, s_cf, None, False)
        o_rl, _ = _retention(ret_decay_b[l], rqk_l3, rv_l3, s_cb, o_lf, True)

        dqk_l3, dqk_c3 = sh3(dqk_l, length), sh3(dqk_c, lc)
        o_dl = _diff_attention(lam4, subln, dqk_l3, (dqk_l3, dvt_l), (dqk_c3, dvt_c), lam_init)
        nqkv_l3, nqkv_c3 = sh3(nqkv_l, length), sh3(nqkv_c, lc)
        o_nl = _na_attention(nqkv_l3, nqkv_c3, bias)

        x_l = _merge(o_rl.reshape(b * length, d), rg_l, o_dl.reshape(b * length, d), o_nl.reshape(b * length, d),
                     gates_l, x_l, g1, lng1, lnb1, wpa, wpb, wpc, wo, lat_row, alpha)
        x_l = _ffn(x_l, sh2, sc2, g2, lng2, lnb2, w1, w2, lat_row, alpha)
        if with_ctx_out:
            o_dc = _diff_attention(lam4, subln, dqk_c3, None, (dqk_c3, dvt_c), lam_init)
            o_nc = _ctx_attention(nqkv_c3)
            x_c = _merge(o_rc.reshape(b * lc, d), rg_c, o_dc.reshape(b * lc, d), o_nc.reshape(b * lc, d),
                         gates_c, x_c, g1, lng1, lnb1, wpa, wpb, wpc, wo, ctx_row, alpha)
            x_c = _ffn(x_c, sh2, sc2, g2, lng2, lnb2, w1, w2, ctx_row, alpha)
    return x_l.reshape(b, length, d)
```

```python
import functools
import math

import jax
import jax.numpy as jnp
from jax import lax
from jax.experimental import pallas as pl
from jax.experimental.pallas import tpu as pltpu

F32 = jnp.float32
BF16 = jnp.bfloat16

D_MODEL = 1024
GRID_W = 64
RET_HEADS, RET_DK = 4, 256
DIFF_HEADS, DIFF_HD, DIFF_DV = 8, 64, 128
NA_HEADS, NA_HD, NA_KH, NA_KW = 16, 64, 8, 16
D_FF = 4 * D_MODEL
ROPE_BASE = 10000.0
LN_EPS = 1e-6
SUBLN_EPS = 1e-5
IN_WIDTH = 13 * D_MODEL
NEG_BIG = -1e30

LANES = 128
VMEM_LIMIT_CAP = 56 << 20

ROW_TILE = 1024
LN_TILE = 512
MERGE_TILE = 256
FFN_TILE = 512
RET_CHUNK = 256
DIFF_TQ = 256
DIFF_TILE = 512
DIFF_VT_ROWS = DIFF_DV + 16
DIFF_SOFTMAX_ROWS = 64
DIFF_UNROLL = 16
NA_ROWS = 8

NT_DIMS = (((1,), (1,)), ((), ()))
TN_DIMS = (((0,), (0,)), ((), ()))


def _params(n_axes, vmem_mb):
    return pltpu.CompilerParams(dimension_semantics=("arbitrary",) * n_axes,
                                vmem_limit_bytes=min(vmem_mb << 20, VMEM_LIMIT_CAP))


def _sigmoid(x):
    return 1.0 / (1.0 + jnp.exp(-x))


def _layer_norm_rows(x):
    mu = jnp.mean(x, axis=-1, keepdims=True)
    xc = x - mu
    var = jnp.mean(xc * xc, axis=-1, keepdims=True)
    return xc * lax.rsqrt(var + LN_EPS)


def _mod_kernel(c_ref, w_ref, b_ref, o_ref):
    c = c_ref[...]
    a = (c * _sigmoid(c)).astype(BF16)
    o_ref[0] = jnp.dot(a, w_ref[0].astype(BF16), preferred_element_type=F32) + b_ref[0]


def _modulation(cvec, w_mod, b_mod):
    depth = w_mod.shape[0]
    d = D_MODEL
    return pl.pallas_call(
        _mod_kernel,
        grid=(depth, 6),
        in_specs=[pl.BlockSpec((8, d), lambda l, j: (0, 0)),
                  pl.BlockSpec((1, d, d), lambda l, j: (l, 0, j)),
                  pl.BlockSpec((1, 1, d), lambda l, j: (l, 0, j))],
        out_specs=pl.BlockSpec((1, 8, d), lambda l, j: (l, 0, j)),
        out_shape=jax.ShapeDtypeStruct((depth, 8, 6 * d), F32),
        compiler_params=_params(2, 32),
        name="modulation",
    )(cvec, w_mod, b_mod.reshape(depth, 1, 6 * d))


def _ln_mod_kernel(x_ref, sh_ref, sc_ref, o_ref):
    y = _layer_norm_rows(x_ref[...])
    o_ref[...] = (y * (1.0 + sc_ref[0]) + sh_ref[0]).astype(o_ref.dtype)


def _ln_mod(x, shift, scale, mod_row):
    r, d = x.shape
    tm = min(LN_TILE, r)
    return pl.pallas_call(
        _ln_mod_kernel,
        grid=(r // tm,),
        in_specs=[pl.BlockSpec((tm, d), lambda i: (i, 0)),
                  pl.BlockSpec((1, 1, d), lambda i: (mod_row(i, tm), 0, 0)),
                  pl.BlockSpec((1, 1, d), lambda i: (mod_row(i, tm), 0, 0))],
        out_specs=pl.BlockSpec((tm, d), lambda i: (i, 0)),
        out_shape=jax.ShapeDtypeStruct((r, d), BF16),
        compiler_params=_params(1, 32),
        name="ln_modulate",
    )(x, shift, scale)


def _proj_kernel(h_ref, w_ref, cs_ref, *rest, mode, transposed, tn, tk_out):
    o_ref = rest[-1]
    acc = jnp.dot(h_ref[...], w_ref[...], preferred_element_type=F32) * cs_ref[...]
    if mode == "rope_ret":
        cos = rest[0][...]
        sin = rest[1][...]
        half = RET_DK // 2
        for hh in range(tn // RET_DK):
            a = hh * RET_DK
            x1 = acc[:, a:a + half]
            x2 = acc[:, a + half:a + RET_DK]
            o_ref[:, a:a + half] = (x1 * cos - x2 * sin).astype(o_ref.dtype)
            o_ref[:, a + half:a + RET_DK] = (x1 * sin + x2 * cos).astype(o_ref.dtype)
    elif mode == "rope_diff":
        cos = rest[0][...]
        sin_lo = rest[1][...]
        sin_hi = rest[2][...]
        q16 = DIFF_HD // 4
        for g in range(tn // LANES):
            x = acc[:, g * LANES:(g + 1) * LANES]
            y = x * cos + pltpu.roll(x, LANES - q16, 1) * sin_lo + pltpu.roll(x, q16, 1) * sin_hi
            o_ref[:, g * LANES:(g + 1) * LANES] = y.astype(o_ref.dtype)
    elif transposed:
        acc_t = acc.T
        ones = jnp.ones((DIFF_VT_ROWS - DIFF_DV, tk_out), o_ref.dtype)
        for a in range(o_ref.shape[0]):
            for hh in range(tn // DIFF_DV):
                r0 = hh * DIFF_VT_ROWS
                o_ref[a, r0:r0 + DIFF_DV, :] = acc_t[hh * DIFF_DV:(hh + 1) * DIFF_DV,
                                                     a * tk_out:(a + 1) * tk_out].astype(o_ref.dtype)
                o_ref[a, r0 + DIFF_DV:r0 + DIFF_VT_ROWS, :] = ones
    else:
        o_ref[...] = acc.astype(o_ref.dtype)


def _proj(h, w, colscale, col0, ncols, out_dtype, mode="plain", tables=(), tiles_per_batch=1, tk_out=None):
    r, d = h.shape
    tm = min(ROW_TILE, r)
    tn = 512
    assert ncols % tn == 0 and col0 % tn == 0 and r % tm == 0
    jb = col0 // tn
    transposed = tk_out is not None
    in_specs = [pl.BlockSpec((tm, d), lambda i, j: (i, 0)),
                pl.BlockSpec((d, tn), lambda i, j: (0, jb + j)),
                pl.BlockSpec((1, tn), lambda i, j: (0, jb + j))]
    for _ in tables:
        in_specs.append(pl.BlockSpec((tm, LANES), lambda i, j: (i % tiles_per_batch, 0)))
    if transposed:
        assert tm % tk_out == 0
        na = tm // tk_out
        rows_out = tn // DIFF_DV * DIFF_VT_ROWS
        out_specs = pl.BlockSpec((na, rows_out, tk_out), lambda i, j: (i, j, 0))
        out_shape = jax.ShapeDtypeStruct((r // tk_out, ncols // DIFF_DV * DIFF_VT_ROWS, tk_out), out_dtype)
    else:
        out_specs = pl.BlockSpec((tm, tn), lambda i, j: (i, j))
        out_shape = jax.ShapeDtypeStruct((r, ncols), out_dtype)
    return pl.pallas_call(
        functools.partial(_proj_kernel, mode=mode, transposed=transposed, tn=tn, tk_out=tk_out),
        grid=(r // tm, ncols // tn),
        in_specs=in_specs,
        out_specs=out_specs,
        out_shape=out_shape,
        compiler_params=_params(2, 40),
        name="in_proj_" + mode + ("_t" if transposed else ""),
    )(h, w, colscale, *tables)


def _ret_kernel(dec_ref, q_ref, k_ref, v_ref, s0_ref, *rest, reverse, has_prev, chunk):
    if has_prev:
        prev_ref, o_ref, sfin_ref, s_scr, intra_scr, qd_scr, kd_scr, cd_scr = rest
    else:
        o_ref, sfin_ref, s_scr, intra_scr, qd_scr, kd_scr, cd_scr = rest
    head = pl.program_id(1)
    c = pl.program_id(2)
    cf = float(chunk)

    @pl.when(c == 0)
    def _init():
        s_scr[...] = s0_ref[0, 0]
        dv = jnp.full((1, 1), dec_ref[head], F32)
        lg = jnp.minimum(dv, 0.0) - jnp.log(1.0 + jnp.exp(-jnp.abs(dv)))
        ii = lax.broadcasted_iota(jnp.int32, (chunk, chunk), 0).astype(F32)
        jj = lax.broadcasted_iota(jnp.int32, (chunk, chunk), 1).astype(F32)
        dist = (jj - ii) if reverse else (ii - jj)
        intra_scr[...] = jnp.where(dist >= 0.0, jnp.exp(lg * jnp.maximum(dist, 0.0)), 0.0)
        i1 = lax.broadcasted_iota(jnp.int32, (chunk, 1), 0).astype(F32)
        qd_scr[...] = jnp.exp(lg * ((cf - i1) if reverse else (i1 + 1.0)))
        kd_scr[...] = jnp.exp(lg * (i1 if reverse else (cf - 1.0 - i1)))
        cd_scr[...] = jnp.exp(lg * cf)

    q = q_ref[0]
    k = k_ref[0]
    v = v_ref[0]
    s = s_scr[...]
    att = lax.dot_general(q, k, NT_DIMS, preferred_element_type=F32) * intra_scr[...]
    qd = (q.astype(F32) * qd_scr[...]).astype(BF16)
    o = (jnp.dot(att.astype(BF16), v, preferred_element_type=F32)
         + jnp.dot(qd, s.astype(BF16), preferred_element_type=F32))
    kd = (k.astype(F32) * kd_scr[...]).astype(BF16)
    s_new = s * cd_scr[...] + lax.dot_general(kd, v, TN_DIMS, preferred_element_type=F32)
    s_scr[...] = s_new
    if has_prev:
        o = o + prev_ref[0]
    o_ref[0] = o

    @pl.when(c == pl.num_programs(2) - 1)
    def _fin():
        sfin_ref[0, 0] = s_new


def _retention(dec, qk, v, s0, prev, reverse):
    b, length, _ = qk.shape
    hh, dk = RET_HEADS, RET_DK
    c = min(RET_CHUNK, length)
    n = length // c
    pos = (lambda i: n - 1 - i) if reverse else (lambda i: i)
    in_specs = [pl.BlockSpec(memory_space=pltpu.SMEM),
                pl.BlockSpec((1, c, dk), lambda bi, h, i: (bi, pos(i), h)),
                pl.BlockSpec((1, c, dk), lambda bi, h, i: (bi, pos(i), hh + h)),
                pl.BlockSpec((1, c, dk), lambda bi, h, i: (bi, pos(i), h)),
                pl.BlockSpec((1, 1, dk, dk), lambda bi, h, i: (bi, h, 0, 0))]
    args = [dec, qk, qk, v, s0]
    if prev is not None:
        in_specs.append(pl.BlockSpec((1, c, dk), lambda bi, h, i: (bi, pos(i), h)))
        args.append(prev)
    return pl.pallas_call(
        functools.partial(_ret_kernel, reverse=reverse, has_prev=prev is not None, chunk=c),
        grid=(b, hh, n),
        in_specs=in_specs,
        out_specs=[pl.BlockSpec((1, c, dk), lambda bi, h, i: (bi, pos(i), h)),
                   pl.BlockSpec((1, 1, dk, dk), lambda bi, h, i: (bi, h, 0, 0))],
        out_shape=[jax.ShapeDtypeStruct((b, length, hh * dk), F32),
                   jax.ShapeDtypeStruct((b, hh, dk, dk), F32)],
        scratch_shapes=[pltpu.VMEM((dk, dk), F32), pltpu.VMEM((c, c), F32),
                        pltpu.VMEM((c, 1), F32), pltpu.VMEM((c, 1), F32), pltpu.VMEM((1, 1), F32)],
        compiler_params=_params(3, 32),
        name="retention_bwd" if reverse else "retention_fwd",
    )(*args)


def _diff_kernel(zero_ref, lam_ref, g_ref, q_ref, *rest, tq, n_lat, tile, lc, lam_init):
    if n_lat:
        kl_ref, vtl_ref = rest[:2]
        rest = rest[2:]
    kc_ref, vtc_ref, o_ref, qq_scr = rest[:4]
    s_buf, mt_buf, p_buf, al_buf = rest[4:8], rest[8:12], rest[12:14], rest[14:16]
    m_scr, acc_scr = rest[16:]
    n = n_lat + 1
    rows_of = lambda j: lc if j == n_lat else tile

    q_t = q_ref[0].astype(F32).T
    row = lax.broadcasted_iota(jnp.int32, q_t.shape, 0)
    qq_scr[:, 0:tq] = jnp.where(row < DIFF_HD, q_t, 0.0).astype(BF16)
    qq_scr[:, tq:2 * tq] = jnp.where(row >= DIFF_HD, q_t, 0.0).astype(BF16)
    m_scr[...] = jnp.full(m_scr.shape, NEG_BIG, F32)
    acc_scr[...] = jnp.zeros(acc_scr.shape, F32)

    def scores(j, slot):
        if isinstance(j, int) and j == n_lat:
            k = kc_ref[0]
        else:
            off = j * tile if isinstance(j, int) else pl.multiple_of(j * tile, tile)
            k = kl_ref[0, pl.ds(off, tile), :]
        s = jnp.dot(k, qq_scr[...], preferred_element_type=F32)
        mt_buf[slot][...] = jnp.max(s, axis=0, keepdims=True)
        s_buf[slot][0:k.shape[0], :] = s

    def softmax(s_slot, p_slot, rows):
        m_old = m_scr[...]
        m_new = jnp.maximum(m_old, mt_buf[s_slot][...])
        al_buf[p_slot][...] = jnp.exp2(m_old - m_new)
        m_scr[...] = m_new
        z = pl.multiple_of(zero_ref[0], 8)
        s_buf[s_slot][pl.ds(z, 8), :] = s_buf[s_slot][pl.ds(z, 8), :]
        ch = DIFF_SOFTMAX_ROWS
        for r0 in range(0, rows, ch):
            p_buf[p_slot][r0:r0 + ch, :] = jnp.exp2(s_buf[s_slot][r0:r0 + ch, :] - m_new).astype(BF16)

    def values(j, slot):
        if isinstance(j, int) and j == n_lat:
            vt = vtc_ref[0]
        else:
            vt = vtl_ref[j]
        acc_scr[...] = (acc_scr[...] * al_buf[slot][...]
                        + jnp.dot(vt, p_buf[slot][0:vt.shape[1], :], preferred_element_type=F32))

    def step(t, tmod):
        static = isinstance(t, int)
        if not (static and t == 0):
            values(t - 1, (tmod - 1) % 2)
        if not static or t + 2 < n:
            scores(t + 2, (tmod + 2) % 4)
        softmax(tmod, tmod % 2, rows_of(t) if static else tile)

    scores(0, 0)
    if n > 1:
        scores(1, 1)
    step(0, 0)
    unroll = DIFF_UNROLL
    n_loop = max(0, (n_lat - 3) // unroll)

    def body(i, carry):
        for u in range(unroll):
            step(1 + unroll * i + u, (1 + u) % 4)
        return carry

    if n_loop:
        lax.fori_loop(0, n_loop, body, 0)
    for t in range(1 + unroll * n_loop, n):
        step(t, t % 4)
    values(n - 1, (n - 1) % 2)

    acc = acc_scr[...]
    o = acc[0:DIFF_DV] * (1.0 / acc[DIFF_DV:DIFF_DV + 1])
    lam4 = lam_ref[...]
    s1 = jnp.sum(lam4[0:1] * lam4[1:2], axis=-1, keepdims=True)
    s2 = jnp.sum(lam4[2:3] * lam4[3:4], axis=-1, keepdims=True)
    lam = jnp.exp(s1) - jnp.exp(s2) + lam_init
    y = (o[:, :tq] - lam * o[:, tq:]).T
    ms = jnp.mean(y * y, axis=-1, keepdims=True)
    y = y * lax.rsqrt(ms + SUBLN_EPS) * g_ref[...] * (1.0 - lam_init)
    o_ref[0] = y.astype(o_ref.dtype)


def _diff_attention(lam4, subln_g, q_arr, lat, ctx, lam_init):
    b, lq, _ = q_arr.shape
    hh = DIFF_HEADS
    tq = min(DIFF_TQ, lq)
    qk_c, vt_c = ctx
    lc = qk_c.shape[1]
    assert vt_c.shape == (b, hh * DIFF_VT_ROWS, lc)
    in_specs = [pl.BlockSpec(memory_space=pltpu.SMEM),
                pl.BlockSpec((4, DIFF_HD), lambda bi, h, i: (0, 0)),
                pl.BlockSpec((1, DIFF_DV), lambda bi, h, i: (0, 0)),
                pl.BlockSpec((1, tq, LANES), lambda bi, h, i: (bi, i, h))]
    args = [jnp.zeros((1,), jnp.int32), lam4, subln_g, q_arr]
    n_lat, tile = 0, lc
    if lat is not None:
        qk_l, vt_l = lat
        length = qk_l.shape[1]
        tile = vt_l.shape[2]
        n_lat = length // tile
        assert lc <= tile
        in_specs += [pl.BlockSpec((1, length, LANES), lambda bi, h, i: (bi, 0, hh + h)),
                     pl.BlockSpec((n_lat, DIFF_VT_ROWS, tile), lambda bi, h, i: (bi, h, 0))]
        args += [qk_l, vt_l]
    in_specs += [pl.BlockSpec((1, lc, LANES), lambda bi, h, i: (bi, 0, hh + h)),
                 pl.BlockSpec((1, DIFF_VT_ROWS, lc), lambda bi, h, i: (bi, h, 0))]
    args += [qk_c, vt_c]
    return pl.pallas_call(
        functools.partial(_diff_kernel, tq=tq, n_lat=n_lat, tile=tile, lc=lc, lam_init=lam_init),
        grid=(b, hh, lq // tq),
        in_specs=in_specs,
        out_specs=pl.BlockSpec((1, tq, LANES), lambda bi, h, i: (bi, i, h)),
        out_shape=jax.ShapeDtypeStruct((b, lq, hh * DIFF_DV), BF16),
        scratch_shapes=[pltpu.VMEM((LANES, 2 * tq), BF16),
                        *[pltpu.VMEM((tile, 2 * tq), F32)] * 4,
                        *[pltpu.VMEM((1, 2 * tq), F32)] * 4,
                        *[pltpu.VMEM((tile, 2 * tq), BF16)] * 2,
                        *[pltpu.VMEM((1, 2 * tq), F32)] * 2,
                        pltpu.VMEM((1, 2 * tq), F32),
                        pltpu.VMEM((DIFF_VT_ROWS, 2 * tq), F32)],
        compiler_params=_params(3, 40),
        name="diff_attention" if lat is not None else "diff_attention_ctx",
    )(*args)


def _softmax_pv(s_list, v_list):
    m = s_list[0].max(axis=-1, keepdims=True)
    for s in s_list[1:]:
        m = jnp.maximum(m, s.max(axis=-1, keepdims=True))
    den = 0.0
    o = 0.0
    for s, v in zip(s_list, v_list):
        p = jnp.exp(s - m)
        den = den + jnp.sum(p, axis=-1, keepdims=True)
        o = o + jnp.dot(p.astype(BF16), v, preferred_element_type=F32)
    return o * (1.0 / den)


def _na_kernel(q_ref, k_ref, v_ref, kc_ref, vc_ref, bias_ref, o_ref, *, g_rows, n_rows):
    r0 = pl.program_id(2) * g_rows
    w = GRID_W
    win = NA_KH * w
    unit = 2 * w
    lo = lax.broadcasted_iota(jnp.int32, (w, LANES), 1) < NA_HD
    q_all = q_ref[0].astype(F32)
    qs, offs, dd = [], [], []
    for g in range(g_rows):
        q = q_all[g * w:(g + 1) * w]
        qs.append(jnp.concatenate([jnp.where(lo, q, 0.0), jnp.where(lo, 0.0, q)], axis=0).astype(BF16))
        rs = jnp.clip(r0 + g - NA_KH // 2, 0, n_rows - NA_KH)
        offs.append(pl.multiple_of(rs * w, w))
        dd.append(r0 + g - rs)
    q_stack = jnp.concatenate(qs, axis=0)
    s_cx = lax.dot_general(q_stack, kc_ref[0], NT_DIMS, preferred_element_type=F32)
    s_nb = [lax.dot_general(qs[g], k_ref[0, pl.ds(offs[g], win), :], NT_DIMS, preferred_element_type=F32)
            + bias_ref[0, dd[g]] for g in range(g_rows)]
    m_cx = jnp.max(s_cx, axis=-1, keepdims=True)
    ms = [jnp.maximum(jnp.max(s_nb[g], axis=-1, keepdims=True), m_cx[g * unit:(g + 1) * unit])
          for g in range(g_rows)]
    p_cx = jnp.exp(s_cx - jnp.concatenate(ms, axis=0))
    l_cx = jnp.sum(p_cx, axis=-1, keepdims=True)
    o_cx = jnp.dot(p_cx.astype(BF16), vc_ref[0], preferred_element_type=F32)
    ps = [jnp.exp(s_nb[g] - ms[g]) for g in range(g_rows)]
    ls = [jnp.sum(ps[g], axis=-1, keepdims=True) + l_cx[g * unit:(g + 1) * unit] for g in range(g_rows)]
    os_ = [jnp.dot(ps[g].astype(BF16), v_ref[0, pl.ds(offs[g], win), :], preferred_element_type=F32)
           + o_cx[g * unit:(g + 1) * unit] for g in range(g_rows)]
    for g in range(g_rows):
        o = os_[g] * (1.0 / ls[g])
        o_ref[0, g * w:(g + 1) * w, :] = jnp.where(lo, o[0:w], o[w:unit]).astype(o_ref.dtype)


def _na_attention(qkv_l, qkv_c, bias):
    b, length, _ = qkv_l.shape
    lc = qkv_c.shape[1]
    n_rows = length // GRID_W
    assert n_rows >= NA_KH
    g_rows = NA_ROWS
    npair = NA_HEADS // 2
    tq = g_rows * GRID_W
    return pl.pallas_call(
        functools.partial(_na_kernel, g_rows=g_rows, n_rows=n_rows),
        grid=(b, npair, n_rows // g_rows),
        in_specs=[pl.BlockSpec((1, tq, LANES), lambda bi, p, i: (bi, i, p)),
                  pl.BlockSpec((1, length, LANES), lambda bi, p, i: (bi, 0, npair + p)),
                  pl.BlockSpec((1, length, LANES), lambda bi, p, i: (bi, 0, 2 * npair + p)),
                  pl.BlockSpec((1, lc, LANES), lambda bi, p, i: (bi, 0, npair + p)),
                  pl.BlockSpec((1, lc, LANES), lambda bi, p, i: (bi, 0, 2 * npair + p)),
                  pl.BlockSpec((1, NA_KH, 2 * GRID_W, NA_KH * GRID_W), lambda bi, p, i: (p, 0, 0, 0))],
        out_specs=pl.BlockSpec((1, tq, LANES), lambda bi, p, i: (bi, i, p)),
        out_shape=jax.ShapeDtypeStruct((b, length, NA_HEADS * NA_HD), BF16),
        compiler_params=_params(3, 40),
        name="neighbourhood_attention",
    )(qkv_l, qkv_l, qkv_l, qkv_c, qkv_c, bias)


def _ctx_attn_kernel(q_ref, k_ref, v_ref, o_ref):
    q = q_ref[0].astype(F32)
    k = k_ref[0]
    v = v_ref[0]
    lane = lax.broadcasted_iota(jnp.int32, q.shape, 1)
    outs = []
    for hh in range(2):
        sel = (lane < NA_HD) if hh == 0 else (lane >= NA_HD)
        qm = jnp.where(sel, q, 0.0).astype(BF16)
        s = lax.dot_general(qm, k, NT_DIMS, preferred_element_type=F32)
        outs.append(_softmax_pv([s], [v]))
    o_ref[0] = jnp.where(lane < NA_HD, outs[0], outs[1]).astype(o_ref.dtype)


def _ctx_attention(qkv_c):
    b, lc, _ = qkv_c.shape
    npair = NA_HEADS // 2
    return pl.pallas_call(
        _ctx_attn_kernel,
        grid=(b, npair),
        in_specs=[pl.BlockSpec((1, lc, LANES), lambda bi, p: (bi, 0, p)),
                  pl.BlockSpec((1, lc, LANES), lambda bi, p: (bi, 0, npair + p)),
                  pl.BlockSpec((1, lc, LANES), lambda bi, p: (bi, 0, 2 * npair + p))],
        out_specs=pl.BlockSpec((1, lc, LANES), lambda bi, p: (bi, 0, p)),
        out_shape=jax.ShapeDtypeStruct((b, lc, NA_HEADS * NA_HD), BF16),
        compiler_params=_params(2, 32),
        name="ctx_attention",
    )(qkv_c, qkv_c, qkv_c)


def _merge_kernel(oret_ref, rg_ref, odiff_ref, ona_ref, gates_ref, x_ref, g1_ref, lng_ref, lnb_ref,
                  wpa_ref, wpb_ref, wpc_ref, wo_ref, o_ref, *, alpha):
    d = D_MODEL
    rg = rg_ref[...]
    ya_parts = []
    for hh in range(RET_HEADS):
        a = hh * RET_DK
        seg = _layer_norm_rows(oret_ref[:, a:a + RET_DK])
        g = rg[:, a:a + RET_DK]
        ya_parts.append((g * _sigmoid(g) * seg).astype(BF16))
    ya_in = jnp.concatenate(ya_parts, axis=1)
    y_a = jnp.dot(ya_in, wpa_ref[...], preferred_element_type=F32)
    y_b = jnp.dot(odiff_ref[...], wpb_ref[...], preferred_element_type=F32)
    y_c = jnp.dot(ona_ref[...], wpc_ref[...], preferred_element_type=F32)
    m = (_sigmoid(gates_ref[:, 0:d]) * y_a + _sigmoid(gates_ref[:, d:2 * d]) * y_b
         + _sigmoid(gates_ref[:, 2 * d:3 * d]) * y_c)
    y = jnp.dot(m.astype(BF16), wo_ref[...], preferred_element_type=F32)
    z = alpha * x_ref[...] + g1_ref[0] * y
    o_ref[...] = _layer_norm_rows(z) * lng_ref[...] + lnb_ref[...]


def _merge(o_ret, rg, o_diff, o_na, gates, x, g1, ln_g, ln_b, w_pa, w_pb, w_pc, w_o, mod_row, alpha):
    r, d = x.shape
    tm = min(MERGE_TILE, r)
    row = lambda i: (i, 0)
    const = lambda i: (0, 0)
    wspec = pl.BlockSpec((d, d), const)
    return pl.pallas_call(
        functools.partial(_merge_kernel, alpha=alpha),
        grid=(r // tm,),
        in_specs=[pl.BlockSpec((tm, d), row), pl.BlockSpec((tm, d), row), pl.BlockSpec((tm, d), row),
                  pl.BlockSpec((tm, d), row), pl.BlockSpec((tm, 3 * d), row), pl.BlockSpec((tm, d), row),
                  pl.BlockSpec((1, 1, d), lambda i: (mod_row(i, tm), 0, 0)),
                  pl.BlockSpec((1, d), const), pl.BlockSpec((1, d), const),
                  wspec, wspec, wspec, wspec],
        out_specs=pl.BlockSpec((tm, d), row),
        out_shape=jax.ShapeDtypeStruct((r, d), F32),
        compiler_params=_params(1, 52),
        name="merge_out_proj",
    )(o_ret, rg, o_diff, o_na, gates, x, g1, ln_g, ln_b, w_pa, w_pb, w_pc, w_o)


def _ffn_kernel(x_ref, sh_ref, sc_ref, g2_ref, lng_ref, lnb_ref, w1_ref, w2_ref, o_ref, *, alpha, ff_chunk):
    x = x_ref[...]
    h = (_layer_norm_rows(x) * (1.0 + sc_ref[0]) + sh_ref[0]).astype(BF16)
    acc = jnp.zeros(x.shape, F32)
    for cidx in range(D_FF // ff_chunk):
        a = cidx * ff_chunk
        u = jnp.maximum(jnp.dot(h, w1_ref[:, a:a + ff_chunk], preferred_element_type=F32), 0.0)
        acc = acc + jnp.dot((u * u).astype(BF16), w2_ref[a:a + ff_chunk, :], preferred_element_type=F32)
    z = alpha * x + g2_ref[0] * acc
    o_ref[...] = _layer_norm_rows(z) * lng_ref[...] + lnb_ref[...]


def _ffn(x, shift, scale, g2, ln_g, ln_b, w1, w2, mod_row, alpha):
    r, d = x.shape
    tm = min(FFN_TILE, r)
    row = lambda i: (i, 0)
    const = lambda i: (0, 0)
    mspec = pl.BlockSpec((1, 1, d), lambda i: (mod_row(i, tm), 0, 0))
    return pl.pallas_call(
        functools.partial(_ffn_kernel, alpha=alpha, ff_chunk=1024),
        grid=(r // tm,),
        in_specs=[pl.BlockSpec((tm, d), row), mspec, mspec, mspec,
                  pl.BlockSpec((1, d), const), pl.BlockSpec((1, d), const),
                  pl.BlockSpec((d, D_FF), const), pl.BlockSpec((D_FF, d), const)],
        out_specs=pl.BlockSpec((tm, d), row),
        out_shape=jax.ShapeDtypeStruct((r, d), F32),
        compiler_params=_params(1, 56),
        name="ffn",
    )(x, shift, scale, g2, ln_g, ln_b, w1, w2)


def _retention_tables(length):
    t = jnp.arange(length, dtype=F32)
    inv = ROPE_BASE ** (-jnp.linspace(0.0, 1.0, RET_DK // 2, dtype=F32))
    ang = t[:, None] * inv[None, :]
    return jnp.cos(ang), jnp.sin(ang)


def _axial_tables(length):
    t = jnp.arange(length)
    row = (t // GRID_W).astype(F32)
    col = (t % GRID_W).astype(F32)
    half = DIFF_HD // 2
    inv = ROPE_BASE ** (-jnp.arange(0, half, 2, dtype=F32) / half)
    ang_r = row[:, None] * inv[None, :]
    ang_c = col[:, None] * inv[None, :]
    zero = jnp.zeros_like(ang_r)
    unit_ang = jnp.concatenate([ang_r, ang_r, ang_c, ang_c], axis=-1)
    cos = jnp.cos(unit_ang)
    sin_r, sin_c = jnp.sin(ang_r), jnp.sin(ang_c)
    sin_lo = jnp.concatenate([-sin_r, zero, -sin_c, zero], axis=-1)
    sin_hi = jnp.concatenate([zero, sin_r, zero, sin_c], axis=-1)
    rep = LANES // DIFF_HD
    return tuple(jnp.tile(a, (1, rep)) for a in (cos, sin_lo, sin_hi))


def _na_bias_table(rpb):
    nh, nr, _ = rpb.shape
    w = GRID_W
    wid = 2 * w - 1
    p = jnp.pad(rpb.astype(F32), ((0, 0), (0, 0), (w - NA_KW, w - NA_KW)))
    a = jnp.broadcast_to(p[:, :, None, :], (nh, nr, w, wid))
    a = jnp.pad(a, ((0, 0), (0, 0), (0, 0), (0, 1))).reshape(nh, nr, w * (wid + 1))
    a = a[:, :, :w * wid].reshape(nh, nr, w, wid)
    m = a[..., w - 1:]
    j = jnp.arange(w)
    cs = jnp.clip(j - NA_KW // 2, 0, w - NA_KW)
    col_mask = (j[None, :] >= cs[:, None]) & (j[None, :] < cs[:, None] + NA_KW)
    m = jnp.where(col_mask[None, None], m, NEG_BIG)
    tbl = jnp.stack([m[:, NA_KH - 1 - d:2 * NA_KH - 1 - d] for d in range(NA_KH)], axis=1)
    tbl = tbl.transpose(0, 1, 3, 2, 4).reshape(nh // 2, 2, NA_KH, w, NA_KH * w)
    return tbl.transpose(0, 2, 1, 3, 4).reshape(nh // 2, NA_KH, 2 * w, NA_KH * w)


def _column_scale():
    d = D_MODEL
    cs = jnp.ones((IN_WIDTH,), F32)
    cs = cs.at[d:2 * d].set(RET_DK ** -0.5)
    cs = cs.at[4 * d:5 * d].set(DIFF_HD ** -0.5 * math.log2(math.e))
    cs = cs.at[7 * d:8 * d].set(NA_HD ** -0.5)
    return cs.reshape(1, IN_WIDTH)


def _project_all(h, w, colscale, rope, ret_tables, ax_tables, tiles_per_batch, tk_out):
    d = D_MODEL
    if rope:
        ret_qk = _proj(h, w, colscale, 0, 2 * d, BF16, "rope_ret", ret_tables, tiles_per_batch)
        diff_qk = _proj(h, w, colscale, 4 * d, 2 * d, BF16, "rope_diff", ax_tables, tiles_per_batch)
    else:
        ret_qk = _proj(h, w, colscale, 0, 2 * d, BF16)
        diff_qk = _proj(h, w, colscale, 4 * d, 2 * d, BF16)
    ret_v = _proj(h, w, colscale, 2 * d, d, BF16)
    ret_g = _proj(h, w, colscale, 3 * d, d, F32)
    diff_vt = _proj(h, w, colscale, 6 * d, d, BF16, tk_out=tk_out)
    na_qkv = _proj(h, w, colscale, 7 * d, 3 * d, BF16)
    gates = _proj(h, w, colscale, 10 * d, 3 * d, F32)
    return ret_qk, ret_v, ret_g, diff_qk, diff_vt, na_qkv, gates


def kernel(x, c, ctx, c_ctx, w_mod, b_mod, w_in, ret_decay_f, ret_decay_b, diff_lq1, diff_lk1, diff_lq2, diff_lk2,
           diff_subln_g, na_rpb, w_pa, w_pb, w_pc, w_o, ln1_g, ln1_b, w_ff1, w_ff2, ln2_g, ln2_b):
    b, length, d = x.shape
    lc = ctx.shape[1]
    depth = w_mod.shape[0]
    assert d == D_MODEL and b < 8 and length % ROW_TILE == 0 and length % GRID_W == 0
    alpha = (2.0 * depth) ** 0.25

    cvec = jnp.zeros((8, d), F32).at[:b].set(c).at[b].set(c_ctx)
    mods = _modulation(cvec, w_mod, b_mod)

    lat_row = lambda i, tm: (i * tm) // length
    ctx_row = lambda i, tm: b
    colscale = _column_scale()
    ret_tables = _retention_tables(length)
    ax_tables = _axial_tables(length)

    x_l = x.reshape(b * length, d)
    x_c = ctx.reshape(b * lc, d)
    zero_state = jnp.zeros((b, RET_HEADS, RET_DK, RET_DK), F32)
    for l in range(depth):
        with_ctx_out = l < depth - 1
        mod = [mods[l, :, k * d:(k + 1) * d].reshape(8, 1, d) for k in range(6)]
        sh1, sc1, g1, sh2, sc2, g2 = mod
        w = w_in[l].astype(BF16)
        wpa, wpb, wpc, wo = (a[l].astype(BF16) for a in (w_pa, w_pb, w_pc, w_o))
        w1, w2 = w_ff1[l].astype(BF16), w_ff2[l].astype(BF16)
        lng1, lnb1 = ln1_g[l].reshape(1, d), ln1_b[l].reshape(1, d)
        lng2, lnb2 = ln2_g[l].reshape(1, d), ln2_b[l].reshape(1, d)
        lam_init = 0.8 - 0.6 * math.exp(-0.3 * l)
        lam4 = jnp.stack([diff_lq1[l], diff_lk1[l], diff_lq2[l], diff_lk2[l]]).astype(F32)
        subln = diff_subln_g[l].reshape(1, DIFF_DV).astype(F32)
        bias = _na_bias_table(na_rpb[l])

        h_l = _ln_mod(x_l, sh1, sc1, lat_row)
        h_c = _ln_mod(x_c, sh1, sc1, ctx_row)
        rqk_l, rv_l, rg_l, dqk_l, dvt_l, nqkv_l, gates_l = _project_all(
            h_l, w, colscale, True, ret_tables, ax_tables, length // ROW_TILE, min(DIFF_TILE, length))
        rqk_c, rv_c, rg_c, dqk_c, dvt_c, nqkv_c, gates_c = _project_all(
            h_c, w, colscale, False, (), (), 1, lc)
        sh3 = lambda a, n: a.reshape(b, n, a.shape[-1])

        rqk_l3, rv_l3, rqk_c3, rv_c3 = sh3(rqk_l, length), sh3(rv_l, length), sh3(rqk_c, lc), sh3(rv_c, lc)
        o_cf, s_cf = _retention(ret_decay_f[l], rqk_c3, rv_c3, zero_state, None, False)
        o_rc, s_cb = _retention(ret_decay_b[l], rqk_c3, rv_c3, zero_state, o_cf, True)
        o_lf, _ = _retention(ret_decay_f[l], rqk_l3, rv_l3, s_cf, None, False)
        o_rl, _ = _retention(ret_decay_b[l], rqk_l3, rv_l3, s_cb, o_lf, True)

        dqk_l3, dqk_c3 = sh3(dqk_l, length), sh3(dqk_c, lc)
        o_dl = _diff_attention(lam4, subln, dqk_l3, (dqk_l3, dvt_l), (dqk_c3, dvt_c), lam_init)
        nqkv_l3, nqkv_c3 = sh3(nqkv_l, length), sh3(nqkv_c, lc)
        o_nl = _na_attention(nqkv_l3, nqkv_c3, bias)

        x_l = _merge(o_rl.reshape(b * length, d), rg_l, o_dl.reshape(b * length, d), o_nl.reshape(b * length, d),
                     gates_l, x_l, g1, lng1, lnb1, wpa, wpb, wpc, wo, lat_row, alpha)
        x_l = _ffn(x_l, sh2, sc2, g2, lng2, lnb2, w1, w2, lat_row, alpha)
        if with_ctx_out:
            o_dc = _diff_attention(lam4, subln, dqk_c3, None, (dqk_c3, dvt_c), lam_init)
            o_nc = _ctx_attention(nqkv_c3)
            x_c = _merge(o_rc.reshape(b * lc, d), rg_c, o_dc.reshape(b * lc, d), o_nc.reshape(b * lc, d),
                         gates_c, x_c, g1, lng1, lnb1, wpa, wpb, wpc, wo, ctx_row, alpha)
            x_c = _ffn(x_c, sh2, sc2, g2, lng2, lnb2, w1, w2, ctx_row, alpha)
    return x_l.reshape(b, length, d)
```

```python
import functools
import math

import jax
import jax.numpy as jnp
from jax import lax
from jax.experimental import pallas as pl
from jax.experimental.pallas import tpu as pltpu

F32 = jnp.float32
BF16 = jnp.bfloat16

D_MODEL = 1024
GRID_W = 64
RET_HEADS, RET_DK = 4, 256
DIFF_HEADS, DIFF_HD, DIFF_DV = 8, 64, 128
NA_HEADS, NA_HD, NA_KH, NA_KW = 16, 64, 8, 16
D_FF = 4 * D_MODEL
ROPE_BASE = 10000.0
LN_EPS = 1e-6
SUBLN_EPS = 1e-5
IN_WIDTH = 13 * D_MODEL
NEG_BIG = -1e30

LANES = 128
VMEM_LIMIT_CAP = 56 << 20

ROW_TILE = 2048
PROJ_TN = 1024
LN_TILE = 512
MERGE_TILE = 256
FFN_TILE = 512
RET_CHUNK = 256
RET_BLOCK = 512
DIFF_TQ = 256
DIFF_TILE = 512
DIFF_VT_ROWS = DIFF_DV + 16
DIFF_SOFTMAX_ROWS = 64
DIFF_UNROLL = 16
NA_ROWS = 8

NT_DIMS = (((1,), (1,)), ((), ()))
TN_DIMS = (((0,), (0,)), ((), ()))


def _params(n_axes, vmem_mb):
    return pltpu.CompilerParams(dimension_semantics=("arbitrary",) * n_axes,
                                vmem_limit_bytes=min(vmem_mb << 20, VMEM_LIMIT_CAP))


def _sigmoid(x):
    return 1.0 / (1.0 + jnp.exp(-x))


def _layer_norm_rows(x):
    mu = jnp.mean(x, axis=-1, keepdims=True)
    xc = x - mu
    var = jnp.mean(xc * xc, axis=-1, keepdims=True)
    return xc * lax.rsqrt(var + LN_EPS)


def _mod_kernel(c_ref, w_ref, b_ref, o_ref):
    c = c_ref[...]
    a = (c * _sigmoid(c)).astype(BF16)
    o_ref[0] = jnp.dot(a, w_ref[0].astype(BF16), preferred_element_type=F32) + b_ref[0]


def _modulation(cvec, w_mod, b_mod):
    depth = w_mod.shape[0]
    d = D_MODEL
    return pl.pallas_call(
        _mod_kernel,
        grid=(depth, 6),
        in_specs=[pl.BlockSpec((8, d), lambda l, j: (0, 0)),
                  pl.BlockSpec((1, d, d), lambda l, j: (l, 0, j)),
                  pl.BlockSpec((1, 1, d), lambda l, j: (l, 0, j))],
        out_specs=pl.BlockSpec((1, 8, d), lambda l, j: (l, 0, j)),
        out_shape=jax.ShapeDtypeStruct((depth, 8, 6 * d), F32),
        compiler_params=_params(2, 32),
        name="modulation",
    )(cvec, w_mod, b_mod.reshape(depth, 1, 6 * d))


def _ln_mod_kernel(x_ref, sh_ref, sc_ref, o_ref):
    y = _layer_norm_rows(x_ref[...])
    o_ref[...] = (y * (1.0 + sc_ref[0]) + sh_ref[0]).astype(o_ref.dtype)


def _ln_mod(x, shift, scale, mod_row):
    r, d = x.shape
    tm = min(LN_TILE, r)
    return pl.pallas_call(
        _ln_mod_kernel,
        grid=(r // tm,),
        in_specs=[pl.BlockSpec((tm, d), lambda i: (i, 0)),
                  pl.BlockSpec((1, 1, d), lambda i: (mod_row(i, tm), 0, 0)),
                  pl.BlockSpec((1, 1, d), lambda i: (mod_row(i, tm), 0, 0))],
        out_specs=pl.BlockSpec((tm, d), lambda i: (i, 0)),
        out_shape=jax.ShapeDtypeStruct((r, d), BF16),
        compiler_params=_params(1, 32),
        name="ln_modulate",
    )(x, shift, scale)


def _proj_kernel(h_ref, w_ref, cs_ref, *rest, mode, transposed, tn, tk_out):
    o_ref = rest[-1]
    acc = jnp.dot(h_ref[...], w_ref[...], preferred_element_type=F32) * cs_ref[...]
    if mode == "rope_ret":
        cos = rest[0][...]
        sin = rest[1][...]
        half = RET_DK // 2
        for hh in range(tn // RET_DK):
            a = hh * RET_DK
            x1 = acc[:, a:a + half]
            x2 = acc[:, a + half:a + RET_DK]
            o_ref[:, a:a + half] = (x1 * cos - x2 * sin).astype(o_ref.dtype)
            o_ref[:, a + half:a + RET_DK] = (x1 * sin + x2 * cos).astype(o_ref.dtype)
    elif mode == "rope_diff":
        cos = rest[0][...]
        sin_lo = rest[1][...]
        sin_hi = rest[2][...]
        q16 = DIFF_HD // 4
        for g in range(tn // LANES):
            x = acc[:, g * LANES:(g + 1) * LANES]
            y = x * cos + pltpu.roll(x, LANES - q16, 1) * sin_lo + pltpu.roll(x, q16, 1) * sin_hi
            o_ref[:, g * LANES:(g + 1) * LANES] = y.astype(o_ref.dtype)
    elif transposed:
        acc_t = acc.T
        ones = jnp.ones((DIFF_VT_ROWS - DIFF_DV, tk_out), o_ref.dtype)
        for a in range(o_ref.shape[0]):
            for hh in range(tn // DIFF_DV):
                r0 = hh * DIFF_VT_ROWS
                o_ref[a, r0:r0 + DIFF_DV, :] = acc_t[hh * DIFF_DV:(hh + 1) * DIFF_DV,
                                                     a * tk_out:(a + 1) * tk_out].astype(o_ref.dtype)
                o_ref[a, r0 + DIFF_DV:r0 + DIFF_VT_ROWS, :] = ones
    else:
        o_ref[...] = acc.astype(o_ref.dtype)


def _proj(h, w, colscale, tm, first, rest0, ntiles, out_dtype, mode="plain", tables=(), tiles_per_batch=1,
          tk_out=None):
    r, d = h.shape
    tn = PROJ_TN
    ncols = ntiles * tn
    assert r % tm == 0
    transposed = tk_out is not None
    wcol = lambda j: jnp.where(j == 0, first, rest0 + j - 1)
    in_specs = [pl.BlockSpec((tm, d), lambda i, j: (i, 0)),
                pl.BlockSpec((d, tn), lambda i, j: (0, wcol(j))),
                pl.BlockSpec((1, tn), lambda i, j: (0, wcol(j)))]
    for _ in tables:
        in_specs.append(pl.BlockSpec((tm, LANES), lambda i, j: (i % tiles_per_batch, 0)))
    if transposed:
        assert tm % tk_out == 0
        na = tm // tk_out
        rows_out = tn // DIFF_DV * DIFF_VT_ROWS
        out_specs = pl.BlockSpec((na, rows_out, tk_out), lambda i, j: (i, j, 0))
        out_shape = jax.ShapeDtypeStruct((r // tk_out, ncols // DIFF_DV * DIFF_VT_ROWS, tk_out), out_dtype)
    else:
        out_specs = pl.BlockSpec((tm, tn), lambda i, j: (i, j))
        out_shape = jax.ShapeDtypeStruct((r, ncols), out_dtype)
    return pl.pallas_call(
        functools.partial(_proj_kernel, mode=mode, transposed=transposed, tn=tn, tk_out=tk_out),
        grid=(r // tm, ntiles),
        in_specs=in_specs,
        out_specs=out_specs,
        out_shape=out_shape,
        compiler_params=_params(2, 52),
        name="in_proj_" + mode + ("_t" if transposed else ""),
    )(h, w, colscale, *tables)


def _ret_tables_kernel(dec_ref, intra_ref, qd_ref, kd_ref, cd_ref, *, chunk):
    direction = pl.program_id(0)
    head = pl.program_id(1)
    cf = float(chunk)
    dv = jnp.full((1, 1), dec_ref[direction, head], F32)
    lg = jnp.minimum(dv, 0.0) - jnp.log(1.0 + jnp.exp(-jnp.abs(dv)))
    rev = direction == 1
    ii = lax.broadcasted_iota(jnp.int32, (chunk, chunk), 0).astype(F32)
    jj = lax.broadcasted_iota(jnp.int32, (chunk, chunk), 1).astype(F32)
    dist = jnp.where(rev, jj - ii, ii - jj)
    intra_ref[0, 0] = jnp.where(dist >= 0.0, jnp.exp(lg * jnp.maximum(dist, 0.0)), 0.0)
    i1 = lax.broadcasted_iota(jnp.int32, (chunk, RET_DK), 0).astype(F32)
    qd_ref[0, 0] = jnp.exp(lg * jnp.where(rev, cf - i1, i1 + 1.0))
    kd_ref[0, 0] = jnp.exp(lg * jnp.where(rev, i1, cf - 1.0 - i1))
    cd_ref[0, 0] = jnp.exp(jnp.broadcast_to(lg, (8, RET_DK)) * cf)


def _retention_decays(dec_f, dec_b, chunk):
    hh, dk = RET_HEADS, RET_DK
    blk = lambda *shape: pl.BlockSpec((1, 1) + shape, lambda d, h: (d, h, 0, 0))
    return pl.pallas_call(
        functools.partial(_ret_tables_kernel, chunk=chunk),
        grid=(2, hh),
        in_specs=[pl.BlockSpec(memory_space=pltpu.SMEM)],
        out_specs=[blk(chunk, chunk), blk(chunk, dk), blk(chunk, dk), blk(8, dk)],
        out_shape=[jax.ShapeDtypeStruct((2, hh, chunk, chunk), F32),
                   jax.ShapeDtypeStruct((2, hh, chunk, dk), F32),
                   jax.ShapeDtypeStruct((2, hh, chunk, dk), F32),
                   jax.ShapeDtypeStruct((2, hh, 8, dk), F32)],
        compiler_params=_params(2, 32),
        name="retention_tables",
    )(jnp.stack([dec_f, dec_b]).astype(F32))


def _ret_kernel(intra_ref, qd_ref, kd_ref, cd_ref, qk_ref, v_ref, s0_ref, *rest, reverse, has_prev, chunk, n_chunks):
    if has_prev:
        prev_ref, o_ref, sfin_ref, s_scr = rest
    else:
        o_ref, sfin_ref, s_scr = rest
    first = pl.program_id(1) == 0
    dk = RET_DK
    order = range(n_chunks - 1, -1, -1) if reverse else range(n_chunks)
    for h in range(RET_HEADS):
        cols = slice(h * dk, (h + 1) * dk)
        kcols = slice((RET_HEADS + h) * dk, (RET_HEADS + h + 1) * dk)
        s = jnp.where(first, s0_ref[0, h], s_scr[h])
        for c in order:
            rows = slice(c * chunk, (c + 1) * chunk)
            q = qk_ref[0, rows, cols]
            k = qk_ref[0, rows, kcols]
            v = v_ref[0, rows, cols]
            att = lax.dot_general(q, k, NT_DIMS, preferred_element_type=F32) * intra_ref[0, h]
            qd = (q.astype(F32) * qd_ref[0, h]).astype(BF16)
            o = (jnp.dot(att.astype(BF16), v, preferred_element_type=F32)
                 + jnp.dot(qd, s.astype(BF16), preferred_element_type=F32))
            kd = (k.astype(F32) * kd_ref[0, h]).astype(BF16)
            s = s * cd_ref[0, h, 0:1, :] + lax.dot_general(kd, v, TN_DIMS, preferred_element_type=F32)
            if has_prev:
                o = o + prev_ref[0, rows, cols]
            o_ref[0, rows, cols] = o
        s_scr[h] = s
        sfin_ref[0, h] = s


def _retention(tables, qk, v_arr, s0, prev, reverse):
    b, length, _ = qk.shape
    hh, dk = RET_HEADS, RET_DK
    width = hh * dk
    c = tables[0].shape[-1]
    rb = min(RET_BLOCK, length)
    n = length // rb
    direction = 1 if reverse else 0
    pos = (lambda i: n - 1 - i) if reverse else (lambda i: i)
    tab = lambda *shape: pl.BlockSpec((1, hh) + shape, lambda bi, i: (direction, 0, 0, 0))
    in_specs = [tab(c, c), tab(c, dk), tab(c, dk), tab(8, dk),
                pl.BlockSpec((1, rb, 2 * width), lambda bi, i: (bi, pos(i), 0)),
                pl.BlockSpec((1, rb, width), lambda bi, i: (bi, pos(i), 0)),
                pl.BlockSpec((1, hh, dk, dk), lambda bi, i: (bi, 0, 0, 0))]
    args = [*tables, qk, v_arr, s0]
    if prev is not None:
        in_specs.append(pl.BlockSpec((1, rb, width), lambda bi, i: (bi, pos(i), 0)))
        args.append(prev)
    return pl.pallas_call(
        functools.partial(_ret_kernel, reverse=reverse, has_prev=prev is not None, chunk=c, n_chunks=rb // c),
        grid=(b, n),
        in_specs=in_specs,
        out_specs=[pl.BlockSpec((1, rb, width), lambda bi, i: (bi, pos(i), 0)),
                   pl.BlockSpec((1, hh, dk, dk), lambda bi, i: (bi, 0, 0, 0))],
        out_shape=[jax.ShapeDtypeStruct((b, length, width), F32),
                   jax.ShapeDtypeStruct((b, hh, dk, dk), F32)],
        scratch_shapes=[pltpu.VMEM((hh, dk, dk), F32)],
        compiler_params=_params(2, 48),
        name="retention_bwd" if reverse else "retention_fwd",
    )(*args)


def _diff_kernel(zero_ref, lam_ref, g_ref, q_ref, *rest, tq, n_lat, tile, lc, lam_init):
    if n_lat:
        kl_ref, vtl_ref = rest[:2]
        rest = rest[2:]
    kc_ref, vtc_ref, o_ref, qq_scr = rest[:4]
    s_buf, mt_buf, p_buf, al_buf = rest[4:8], rest[8:12], rest[12:14], rest[14:16]
    m_scr, acc_scr = rest[16:]
    n = n_lat + 1
    rows_of = lambda j: lc if j == n_lat else tile

    q_t = q_ref[0].astype(F32).T
    row = lax.broadcasted_iota(jnp.int32, q_t.shape, 0)
    qq_scr[:, 0:tq] = jnp.where(row < DIFF_HD, q_t, 0.0).astype(BF16)
    qq_scr[:, tq:2 * tq] = jnp.where(row >= DIFF_HD, q_t, 0.0).astype(BF16)
    m_scr[...] = jnp.full(m_scr.shape, NEG_BIG, F32)
    acc_scr[...] = jnp.zeros(acc_scr.shape, F32)

    def scores(j, slot):
        if isinstance(j, int) and j == n_lat:
            k = kc_ref[0]
        else:
            off = j * tile if isinstance(j, int) else pl.multiple_of(j * tile, tile)
            k = kl_ref[0, pl.ds(off, tile), :]
        s = jnp.dot(k, qq_scr[...], preferred_element_type=F32)
        mt_buf[slot][...] = jnp.max(s, axis=0, keepdims=True)
        s_buf[slot][0:k.shape[0], :] = s

    def softmax(s_slot, p_slot, rows):
        m_old = m_scr[...]
        m_new = jnp.maximum(m_old, mt_buf[s_slot][...])
        al_buf[p_slot][...] = jnp.exp2(m_old - m_new)
        m_scr[...] = m_new
        z = pl.multiple_of(zero_ref[0], 8)
        s_buf[s_slot][pl.ds(z, 8), :] = s_buf[s_slot][pl.ds(z, 8), :]
        ch = DIFF_SOFTMAX_ROWS
        for r0 in range(0, rows, ch):
            p_buf[p_slot][r0:r0 + ch, :] = jnp.exp2(s_buf[s_slot][r0:r0 + ch, :] - m_new).astype(BF16)

    def values(j, slot):
        if isinstance(j, int) and j == n_lat:
            vt = vtc_ref[0]
        else:
            vt = vtl_ref[j]
        acc_scr[...] = (acc_scr[...] * al_buf[slot][...]
                        + jnp.dot(vt, p_buf[slot][0:vt.shape[1], :], preferred_element_type=F32))

    def step(t, tmod):
        static = isinstance(t, int)
        if not (static and t == 0):
            values(t - 1, (tmod - 1) % 2)
        if not static or t + 2 < n:
            scores(t + 2, (tmod + 2) % 4)
        softmax(tmod, tmod % 2, rows_of(t) if static else tile)

    scores(0, 0)
    if n > 1:
        scores(1, 1)
    step(0, 0)
    unroll = DIFF_UNROLL
    n_loop = max(0, (n_lat - 3) // unroll)

    def body(i, carry):
        for u in range(unroll):
            step(1 + unroll * i + u, (1 + u) % 4)
        return carry

    if n_loop:
        lax.fori_loop(0, n_loop, body, 0)
    for t in range(1 + unroll * n_loop, n):
        step(t, t % 4)
    values(n - 1, (n - 1) % 2)

    acc = acc_scr[...]
    o = acc[0:DIFF_DV] * (1.0 / acc[DIFF_DV:DIFF_DV + 1])
    lam4 = lam_ref[...]
    s1 = jnp.sum(lam4[0:1] * lam4[1:2], axis=-1, keepdims=True)
    s2 = jnp.sum(lam4[2:3] * lam4[3:4], axis=-1, keepdims=True)
    lam = jnp.exp(s1) - jnp.exp(s2) + lam_init
    y = (o[:, :tq] - lam * o[:, tq:]).T
    ms = jnp.mean(y * y, axis=-1, keepdims=True)
    y = y * lax.rsqrt(ms + SUBLN_EPS) * g_ref[...] * (1.0 - lam_init)
    o_ref[0] = y.astype(o_ref.dtype)


def _diff_attention(lam4, subln_g, q_arr, lat, ctx, lam_init):
    b, lq, _ = q_arr.shape
    hh = DIFF_HEADS
    tq = min(DIFF_TQ, lq)
    qk_c, vt_c = ctx
    lc = qk_c.shape[1]
    assert vt_c.shape == (b, hh * DIFF_VT_ROWS, lc)
    in_specs = [pl.BlockSpec(memory_space=pltpu.SMEM),
                pl.BlockSpec((4, DIFF_HD), lambda bi, h, i: (0, 0)),
                pl.BlockSpec((1, DIFF_DV), lambda bi, h, i: (0, 0)),
                pl.BlockSpec((1, tq, LANES), lambda bi, h, i: (bi, i, h))]
    args = [jnp.zeros((1,), jnp.int32), lam4, subln_g, q_arr]
    n_lat, tile = 0, lc
    if lat is not None:
        qk_l, vt_l = lat
        length = qk_l.shape[1]
        tile = vt_l.shape[2]
        n_lat = length // tile
        assert lc <= tile
        in_specs += [pl.BlockSpec((1, length, LANES), lambda bi, h, i: (bi, 0, hh + h)),
                     pl.BlockSpec((n_lat, DIFF_VT_ROWS, tile), lambda bi, h, i: (bi, h, 0))]
        args += [qk_l, vt_l]
    in_specs += [pl.BlockSpec((1, lc, LANES), lambda bi, h, i: (bi, 0, hh + h)),
                 pl.BlockSpec((1, DIFF_VT_ROWS, lc), lambda bi, h, i: (bi, h, 0))]
    args += [qk_c, vt_c]
    return pl.pallas_call(
        functools.partial(_diff_kernel, tq=tq, n_lat=n_lat, tile=tile, lc=lc, lam_init=lam_init),
        grid=(b, hh, lq // tq),
        in_specs=in_specs,
        out_specs=pl.BlockSpec((1, tq, LANES), lambda bi, h, i: (bi, i, h)),
        out_shape=jax.ShapeDtypeStruct((b, lq, hh * DIFF_DV), BF16),
        scratch_shapes=[pltpu.VMEM((LANES, 2 * tq), BF16),
                        *[pltpu.VMEM((tile, 2 * tq), F32)] * 4,
                        *[pltpu.VMEM((1, 2 * tq), F32)] * 4,
                        *[pltpu.VMEM((tile, 2 * tq), BF16)] * 2,
                        *[pltpu.VMEM((1, 2 * tq), F32)] * 2,
                        pltpu.VMEM((1, 2 * tq), F32),
                        pltpu.VMEM((DIFF_VT_ROWS, 2 * tq), F32)],
        compiler_params=_params(3, 40),
        name="diff_attention" if lat is not None else "diff_attention_ctx",
    )(*args)


def _softmax_pv(s_list, v_list):
    m = s_list[0].max(axis=-1, keepdims=True)
    for s in s_list[1:]:
        m = jnp.maximum(m, s.max(axis=-1, keepdims=True))
    den = 0.0
    o = 0.0
    for s, v in zip(s_list, v_list):
        p = jnp.exp(s - m)
        den = den + jnp.sum(p, axis=-1, keepdims=True)
        o = o + jnp.dot(p.astype(BF16), v, preferred_element_type=F32)
    return o * (1.0 / den)


def _na_kernel(q_ref, k_ref, v_ref, kc_ref, vc_ref, bias_ref, o_ref, *, g_rows, n_rows):
    r0 = pl.program_id(2) * g_rows
    w = GRID_W
    win = NA_KH * w
    unit = 2 * w
    lo = lax.broadcasted_iota(jnp.int32, (w, LANES), 1) < NA_HD
    q_all = q_ref[0].astype(F32)
    qs, offs, dd = [], [], []
    for g in range(g_rows):
        q = q_all[g * w:(g + 1) * w]
        qs.append(jnp.concatenate([jnp.where(lo, q, 0.0), jnp.where(lo, 0.0, q)], axis=0).astype(BF16))
        rs = jnp.clip(r0 + g - NA_KH // 2, 0, n_rows - NA_KH)
        offs.append(pl.multiple_of(rs * w, w))
        dd.append(r0 + g - rs)
    q_stack = jnp.concatenate(qs, axis=0)
    s_cx = lax.dot_general(q_stack, kc_ref[0], NT_DIMS, preferred_element_type=F32)
    s_nb = [lax.dot_general(qs[g], k_ref[0, pl.ds(offs[g], win), :], NT_DIMS, preferred_element_type=F32)
            + bias_ref[0, dd[g]] for g in range(g_rows)]
    m_cx = jnp.max(s_cx, axis=-1, keepdims=True)
    ms = [jnp.maximum(jnp.max(s_nb[g], axis=-1, keepdims=True), m_cx[g * unit:(g + 1) * unit])
          for g in range(g_rows)]
    p_cx = jnp.exp(s_cx - jnp.concatenate(ms, axis=0))
    l_cx = jnp.sum(p_cx, axis=-1, keepdims=True)
    o_cx = jnp.dot(p_cx.astype(BF16), vc_ref[0], preferred_element_type=F32)
    ps = [jnp.exp(s_nb[g] - ms[g]) for g in range(g_rows)]
    ls = [jnp.sum(ps[g], axis=-1, keepdims=True) + l_cx[g * unit:(g + 1) * unit] for g in range(g_rows)]
    os_ = [jnp.dot(ps[g].astype(BF16), v_ref[0, pl.ds(offs[g], win), :], preferred_element_type=F32)
           + o_cx[g * unit:(g + 1) * unit] for g in range(g_rows)]
    for g in range(g_rows):
        o = os_[g] * (1.0 / ls[g])
        o_ref[0, g * w:(g + 1) * w, :] = jnp.where(lo, o[0:w], o[w:unit]).astype(o_ref.dtype)


def _na_attention(qkv_l, qkv_c, bias):
    b, length, _ = qkv_l.shape
    lc = qkv_c.shape[1]
    n_rows = length // GRID_W
    assert n_rows >= NA_KH
    g_rows = NA_ROWS
    npair = NA_HEADS // 2
    tq = g_rows * GRID_W
    return pl.pallas_call(
        functools.partial(_na_kernel, g_rows=g_rows, n_rows=n_rows),
        grid=(b, npair, n_rows // g_rows),
        in_specs=[pl.BlockSpec((1, tq, LANES), lambda bi, p, i: (bi, i, npair + p)),
                  pl.BlockSpec((1, length, LANES), lambda bi, p, i: (bi, 0, 2 * npair + p)),
                  pl.BlockSpec((1, length, LANES), lambda bi, p, i: (bi, 0, 3 * npair + p)),
                  pl.BlockSpec((1, lc, LANES), lambda bi, p, i: (bi, 0, 2 * npair + p)),
                  pl.BlockSpec((1, lc, LANES), lambda bi, p, i: (bi, 0, 3 * npair + p)),
                  pl.BlockSpec((1, NA_KH, 2 * GRID_W, NA_KH * GRID_W), lambda bi, p, i: (p, 0, 0, 0))],
        out_specs=pl.BlockSpec((1, tq, LANES), lambda bi, p, i: (bi, i, p)),
        out_shape=jax.ShapeDtypeStruct((b, length, NA_HEADS * NA_HD), BF16),
        compiler_params=_params(3, 40),
        name="neighbourhood_attention",
    )(qkv_l, qkv_l, qkv_l, qkv_c, qkv_c, bias)


def _ctx_attn_kernel(q_ref, k_ref, v_ref, o_ref):
    q = q_ref[0].astype(F32)
    k = k_ref[0]
    v = v_ref[0]
    lane = lax.broadcasted_iota(jnp.int32, q.shape, 1)
    outs = []
    for hh in range(2):
        sel = (lane < NA_HD) if hh == 0 else (lane >= NA_HD)
        qm = jnp.where(sel, q, 0.0).astype(BF16)
        s = lax.dot_general(qm, k, NT_DIMS, preferred_element_type=F32)
        outs.append(_softmax_pv([s], [v]))
    o_ref[0] = jnp.where(lane < NA_HD, outs[0], outs[1]).astype(o_ref.dtype)


def _ctx_attention(qkv_c):
    b, lc, _ = qkv_c.shape
    npair = NA_HEADS // 2
    return pl.pallas_call(
        _ctx_attn_kernel,
        grid=(b, npair),
        in_specs=[pl.BlockSpec((1, lc, LANES), lambda bi, p: (bi, 0, npair + p)),
                  pl.BlockSpec((1, lc, LANES), lambda bi, p: (bi, 0, 2 * npair + p)),
                  pl.BlockSpec((1, lc, LANES), lambda bi, p: (bi, 0, 3 * npair + p))],
        out_specs=pl.BlockSpec((1, lc, LANES), lambda bi, p: (bi, 0, p)),
        out_shape=jax.ShapeDtypeStruct((b, lc, NA_HEADS * NA_HD), BF16),
        compiler_params=_params(2, 32),
        name="ctx_attention",
    )(qkv_c, qkv_c, qkv_c)


def _merge_kernel(oret_ref, gates_ref, odiff_ref, ona_ref, x_ref, g1_ref, lng_ref, lnb_ref,
                  wpa_ref, wpb_ref, wpc_ref, wo_ref, o_ref, *, alpha):
    d = D_MODEL
    ya_parts = []
    for hh in range(RET_HEADS):
        a = hh * RET_DK
        seg = _layer_norm_rows(oret_ref[:, a:a + RET_DK])
        g = gates_ref[:, a:a + RET_DK]
        ya_parts.append((g * _sigmoid(g) * seg).astype(BF16))
    ya_in = jnp.concatenate(ya_parts, axis=1)
    y_a = jnp.dot(ya_in, wpa_ref[...], preferred_element_type=F32)
    y_b = jnp.dot(odiff_ref[...], wpb_ref[...], preferred_element_type=F32)
    y_c = jnp.dot(ona_ref[...], wpc_ref[...], preferred_element_type=F32)
    m = (_sigmoid(gates_ref[:, d:2 * d]) * y_a + _sigmoid(gates_ref[:, 2 * d:3 * d]) * y_b
         + _sigmoid(gates_ref[:, 3 * d:4 * d]) * y_c)
    y = jnp.dot(m.astype(BF16), wo_ref[...], preferred_element_type=F32)
    z = alpha * x_ref[...] + g1_ref[0] * y
    o_ref[...] = _layer_norm_rows(z) * lng_ref[...] + lnb_ref[...]


def _merge(o_ret, gates, o_diff, o_na, x, g1, ln_g, ln_b, w_pa, w_pb, w_pc, w_o, mod_row, alpha):
    r, d = x.shape
    tm = min(MERGE_TILE, r)
    row = lambda i: (i, 0)
    const = lambda i: (0, 0)
    wspec = pl.BlockSpec((d, d), const)
    return pl.pallas_call(
        functools.partial(_merge_kernel, alpha=alpha),
        grid=(r // tm,),
        in_specs=[pl.BlockSpec((tm, d), row), pl.BlockSpec((tm, 4 * d), row), pl.BlockSpec((tm, d), row),
                  pl.BlockSpec((tm, d), row), pl.BlockSpec((tm, d), row),
                  pl.BlockSpec((1, 1, d), lambda i: (mod_row(i, tm), 0, 0)),
                  pl.BlockSpec((1, d), const), pl.BlockSpec((1, d), const),
                  wspec, wspec, wspec, wspec],
        out_specs=pl.BlockSpec((tm, d), row),
        out_shape=jax.ShapeDtypeStruct((r, d), F32),
        compiler_params=_params(1, 52),
        name="merge_out_proj",
    )(o_ret, gates, o_diff, o_na, x, g1, ln_g, ln_b, w_pa, w_pb, w_pc, w_o)


def _ffn_kernel(x_ref, sh_ref, sc_ref, g2_ref, lng_ref, lnb_ref, w1_ref, w2_ref, o_ref, *, alpha, ff_chunk):
    x = x_ref[...]
    h = (_layer_norm_rows(x) * (1.0 + sc_ref[0]) + sh_ref[0]).astype(BF16)
    acc = jnp.zeros(x.shape, F32)
    for cidx in range(D_FF // ff_chunk):
        a = cidx * ff_chunk
        u = jnp.maximum(jnp.dot(h, w1_ref[:, a:a + ff_chunk], preferred_element_type=F32), 0.0)
        acc = acc + jnp.dot((u * u).astype(BF16), w2_ref[a:a + ff_chunk, :], preferred_element_type=F32)
    z = alpha * x + g2_ref[0] * acc
    o_ref[...] = _layer_norm_rows(z) * lng_ref[...] + lnb_ref[...]


def _ffn(x, shift, scale, g2, ln_g, ln_b, w1, w2, mod_row, alpha):
    r, d = x.shape
    tm = min(FFN_TILE, r)
    row = lambda i: (i, 0)
    const = lambda i: (0, 0)
    mspec = pl.BlockSpec((1, 1, d), lambda i: (mod_row(i, tm), 0, 0))
    return pl.pallas_call(
        functools.partial(_ffn_kernel, alpha=alpha, ff_chunk=1024),
        grid=(r // tm,),
        in_specs=[pl.BlockSpec((tm, d), row), mspec, mspec, mspec,
                  pl.BlockSpec((1, d), const), pl.BlockSpec((1, d), const),
                  pl.BlockSpec((d, D_FF), const), pl.BlockSpec((D_FF, d), const)],
        out_specs=pl.BlockSpec((tm, d), row),
        out_shape=jax.ShapeDtypeStruct((r, d), F32),
        compiler_params=_params(1, 56),
        name="ffn",
    )(x, shift, scale, g2, ln_g, ln_b, w1, w2)


def _retention_tables(length):
    t = jnp.arange(length, dtype=F32)
    inv = ROPE_BASE ** (-jnp.linspace(0.0, 1.0, RET_DK // 2, dtype=F32))
    ang = t[:, None] * inv[None, :]
    return jnp.cos(ang), jnp.sin(ang)


def _axial_tables(length):
    t = jnp.arange(length)
    row = (t // GRID_W).astype(F32)
    col = (t % GRID_W).astype(F32)
    half = DIFF_HD // 2
    inv = ROPE_BASE ** (-jnp.arange(0, half, 2, dtype=F32) / half)
    ang_r = row[:, None] * inv[None, :]
    ang_c = col[:, None] * inv[None, :]
    zero = jnp.zeros_like(ang_r)
    unit_ang = jnp.concatenate([ang_r, ang_r, ang_c, ang_c], axis=-1)
    cos = jnp.cos(unit_ang)
    sin_r, sin_c = jnp.sin(ang_r), jnp.sin(ang_c)
    sin_lo = jnp.concatenate([-sin_r, zero, -sin_c, zero], axis=-1)
    sin_hi = jnp.concatenate([zero, sin_r, zero, sin_c], axis=-1)
    rep = LANES // DIFF_HD
    return tuple(jnp.tile(a, (1, rep)) for a in (cos, sin_lo, sin_hi))


def _na_bias_table(rpb):
    nh, nr, _ = rpb.shape
    w = GRID_W
    wid = 2 * w - 1
    p = jnp.pad(rpb.astype(F32), ((0, 0), (0, 0), (w - NA_KW, w - NA_KW)))
    a = jnp.broadcast_to(p[:, :, None, :], (nh, nr, w, wid))
    a = jnp.pad(a, ((0, 0), (0, 0), (0, 0), (0, 1))).reshape(nh, nr, w * (wid + 1))
    a = a[:, :, :w * wid].reshape(nh, nr, w, wid)
    m = a[..., w - 1:]
    j = jnp.arange(w)
    cs = jnp.clip(j - NA_KW // 2, 0, w - NA_KW)
    col_mask = (j[None, :] >= cs[:, None]) & (j[None, :] < cs[:, None] + NA_KW)
    m = jnp.where(col_mask[None, None], m, NEG_BIG)
    tbl = jnp.stack([m[:, NA_KH - 1 - d:2 * NA_KH - 1 - d] for d in range(NA_KH)], axis=1)
    tbl = tbl.transpose(0, 1, 3, 2, 4).reshape(nh // 2, 2, NA_KH, w, NA_KH * w)
    return tbl.transpose(0, 2, 1, 3, 4).reshape(nh // 2, NA_KH, 2 * w, NA_KH * w)


def _column_scale():
    d = D_MODEL
    cs = jnp.ones((IN_WIDTH,), F32)
    cs = cs.at[d:2 * d].set(RET_DK ** -0.5)
    cs = cs.at[4 * d:5 * d].set(DIFF_HD ** -0.5 * math.log2(math.e))
    cs = cs.at[7 * d:8 * d].set(NA_HD ** -0.5)
    return cs.reshape(1, IN_WIDTH)


def _project_all(h, w, colscale, tm, rope, ret_tables, ax_tables, tiles_per_batch, tk_out):
    if rope:
        ret_qk = _proj(h, w, colscale, tm, 0, 1, 2, BF16, "rope_ret", ret_tables, tiles_per_batch)
        diff_qk = _proj(h, w, colscale, tm, 4, 5, 2, BF16, "rope_diff", ax_tables, tiles_per_batch)
    else:
        ret_qk = _proj(h, w, colscale, tm, 0, 1, 2, BF16)
        diff_qk = _proj(h, w, colscale, tm, 4, 5, 2, BF16)
    v_na = _proj(h, w, colscale, tm, 2, 7, 4, BF16)
    gates = _proj(h, w, colscale, tm, 3, 10, 4, F32)
    diff_vt = _proj(h, w, colscale, tm, 6, 7, 1, BF16, tk_out=tk_out)
    return ret_qk, diff_qk, v_na, gates, diff_vt


def kernel(x, c, ctx, c_ctx, w_mod, b_mod, w_in, ret_decay_f, ret_decay_b, diff_lq1, diff_lk1, diff_lq2, diff_lk2,
           diff_subln_g, na_rpb, w_pa, w_pb, w_pc, w_o, ln1_g, ln1_b, w_ff1, w_ff2, ln2_g, ln2_b):
    b, length, d = x.shape
    lc = ctx.shape[1]
    depth = w_mod.shape[0]
    assert d == D_MODEL and b < 8 and length % min(ROW_TILE, length) == 0 and length % GRID_W == 0
    assert lc % RET_CHUNK == 0 or lc < RET_CHUNK
    alpha = (2.0 * depth) ** 0.25

    cvec = jnp.zeros((8, d), F32).at[:b].set(c).at[b].set(c_ctx)
    mods = _modulation(cvec, w_mod, b_mod)

    lat_row = lambda i, tm: (i * tm) // length
    ctx_row = lambda i, tm: b
    colscale = _column_scale()
    ret_tables = _retention_tables(length)
    ax_tables = _axial_tables(length)

    x_l = x.reshape(b * length, d)
    x_c = ctx.reshape(b * lc, d)
    zero_state = jnp.zeros((b, RET_HEADS, RET_DK, RET_DK), F32)
    for l in range(depth):
        with_ctx_out = l < depth - 1
        mod = [mods[l, :, k * d:(k + 1) * d].reshape(8, 1, d) for k in range(6)]
        sh1, sc1, g1, sh2, sc2, g2 = mod
        w = w_in[l].astype(BF16)
        wpa, wpb, wpc, wo = (a[l].astype(BF16) for a in (w_pa, w_pb, w_pc, w_o))
        w1, w2 = w_ff1[l].astype(BF16), w_ff2[l].astype(BF16)
        lng1, lnb1 = ln1_g[l].reshape(1, d), ln1_b[l].reshape(1, d)
        lng2, lnb2 = ln2_g[l].reshape(1, d), ln2_b[l].reshape(1, d)
        lam_init = 0.8 - 0.6 * math.exp(-0.3 * l)
        lam4 = jnp.stack([diff_lq1[l], diff_lk1[l], diff_lq2[l], diff_lk2[l]]).astype(F32)
        subln = diff_subln_g[l].reshape(1, DIFF_DV).astype(F32)
        bias = _na_bias_table(na_rpb[l])

        h_l = _ln_mod(x_l, sh1, sc1, lat_row)
        h_c = _ln_mod(x_c, sh1, sc1, ctx_row)
        tm_l = min(ROW_TILE, length)
        rqk_l, dqk_l, vna_l, gates_l, dvt_l = _project_all(
            h_l, w, colscale, tm_l, True, ret_tables, ax_tables, length // tm_l, min(DIFF_TILE, length))
        rqk_c, dqk_c, vna_c, gates_c, dvt_c = _project_all(
            h_c, w, colscale, min(ROW_TILE, b * lc), False, (), (), 1, lc)
        sh3 = lambda a, n: a.reshape(b, n, a.shape[-1])

        decays = _retention_decays(ret_decay_f[l], ret_decay_b[l], min(RET_CHUNK, lc))
        rqk_l3, vna_l3, rqk_c3, vna_c3 = sh3(rqk_l, length), sh3(vna_l, length), sh3(rqk_c, lc), sh3(vna_c, lc)
        o_cf, s_cf = _retention(decays, rqk_c3, vna_c3, zero_state, None, False)
        o_rc, s_cb = _retention(decays, rqk_c3, vna_c3, zero_state, o_cf, True)
        o_lf, _ = _retention(decays, rqk_l3, vna_l3, s_cf, None, False)
        o_rl, _ = _retention(decays, rqk_l3, vna_l3, s_cb, o_lf, True)

        dqk_l3, dqk_c3 = sh3(dqk_l, length), sh3(dqk_c, lc)
        o_dl = _diff_attention(lam4, subln, dqk_l3, (dqk_l3, dvt_l), (dqk_c3, dvt_c), lam_init)
        o_nl = _na_attention(vna_l3, vna_c3, bias)

        x_l = _merge(o_rl.reshape(b * length, d), gates_l, o_dl.reshape(b * length, d), o_nl.reshape(b * length, d),
                     x_l, g1, lng1, lnb1, wpa, wpb, wpc, wo, lat_row, alpha)
        x_l = _ffn(x_l, sh2, sc2, g2, lng2, lnb2, w1, w2, lat_row, alpha)
        if with_ctx_out:
            o_dc = _diff_attention(lam4, subln, dqk_c3, None, (dqk_c3, dvt_c), lam_init)
            o_nc = _ctx_attention(vna_c3)
            x_c = _merge(o_rc.reshape(b * lc, d), gates_c, o_dc.reshape(b * lc, d), o_nc.reshape(b * lc, d),
                         x_c, g1, lng1, lnb1, wpa, wpb, wpc, wo, ctx_row, alpha)
            x_c = _ffn(x_c, sh2, sc2, g2, lng2, lnb2, w1, w2, ctx_row, alpha)
    return x_l.reshape(b, length, d)
```

```python
import functools
import math

import jax
import jax.numpy as jnp
from jax import lax
from jax.experimental import pallas as pl
from jax.experimental.pallas import tpu as pltpu

F32 = jnp.float32
BF16 = jnp.bfloat16

D_MODEL = 1024
GRID_W = 64
RET_HEADS, RET_DK = 4, 256
DIFF_HEADS, DIFF_HD, DIFF_DV = 8, 64, 128
NA_HEADS, NA_HD, NA_KH, NA_KW = 16, 64, 8, 16
D_FF = 4 * D_MODEL
ROPE_BASE = 10000.0
LN_EPS = 1e-6
SUBLN_EPS = 1e-5
IN_WIDTH = 13 * D_MODEL
NEG_BIG = -1e30

LANES = 128
VMEM_LIMIT_CAP = 56 << 20

ROW_TILE = 2048
PROJ_TN = 1024
LN_TILE = 512
MERGE_TILE = 256
FFN_TILE = 512
RET_CHUNK = 256
RET_BLOCK = 512
DIFF_TQ = 512
DIFF_TILE = 512
DIFF_VT_ROWS = DIFF_DV + 16
DIFF_SOFTMAX_ROWS = 64
DIFF_UNROLL = 16
NA_UNIT_ROWS = 4
NA_UNION_ROWS = 12
NA_UNITS = 8
NA_SOFTMAX_ROWS = 64

NT_DIMS = (((1,), (1,)), ((), ()))
TN_DIMS = (((0,), (0,)), ((), ()))


def _params(n_axes, vmem_mb):
    return pltpu.CompilerParams(dimension_semantics=("arbitrary",) * n_axes,
                                vmem_limit_bytes=min(vmem_mb << 20, VMEM_LIMIT_CAP))


def _sigmoid(x):
    return 1.0 / (1.0 + jnp.exp(-x))


def _layer_norm_rows(x):
    mu = jnp.mean(x, axis=-1, keepdims=True)
    xc = x - mu
    var = jnp.mean(xc * xc, axis=-1, keepdims=True)
    return xc * lax.rsqrt(var + LN_EPS)


def _mod_kernel(c_ref, w_ref, b_ref, o_ref):
    c = c_ref[...]
    a = (c * _sigmoid(c)).astype(BF16)
    o_ref[0] = jnp.dot(a, w_ref[0].astype(BF16), preferred_element_type=F32) + b_ref[0]


def _modulation(cvec, w_mod, b_mod):
    depth = w_mod.shape[0]
    d = D_MODEL
    return pl.pallas_call(
        _mod_kernel,
        grid=(depth, 6),
        in_specs=[pl.BlockSpec((8, d), lambda l, j: (0, 0)),
                  pl.BlockSpec((1, d, d), lambda l, j: (l, 0, j)),
                  pl.BlockSpec((1, 1, d), lambda l, j: (l, 0, j))],
        out_specs=pl.BlockSpec((1, 8, d), lambda l, j: (l, 0, j)),
        out_shape=jax.ShapeDtypeStruct((depth, 8, 6 * d), F32),
        compiler_params=_params(2, 32),
        name="modulation",
    )(cvec, w_mod, b_mod.reshape(depth, 1, 6 * d))


def _ln_mod_kernel(x_ref, sh_ref, sc_ref, o_ref):
    y = _layer_norm_rows(x_ref[...])
    o_ref[...] = (y * (1.0 + sc_ref[0]) + sh_ref[0]).astype(o_ref.dtype)


def _ln_mod(x, shift, scale, mod_row):
    r, d = x.shape
    tm = min(LN_TILE, r)
    return pl.pallas_call(
        _ln_mod_kernel,
        grid=(r // tm,),
        in_specs=[pl.BlockSpec((tm, d), lambda i: (i, 0)),
                  pl.BlockSpec((1, 1, d), lambda i: (mod_row(i, tm), 0, 0)),
                  pl.BlockSpec((1, 1, d), lambda i: (mod_row(i, tm), 0, 0))],
        out_specs=pl.BlockSpec((tm, d), lambda i: (i, 0)),
        out_shape=jax.ShapeDtypeStruct((r, d), BF16),
        compiler_params=_params(1, 32),
        name="ln_modulate",
    )(x, shift, scale)


def _proj_kernel(h_ref, w_ref, cs_ref, *rest, mode, transposed, tn, tk_out):
    o_ref = rest[-1]
    acc = jnp.dot(h_ref[...], w_ref[...], preferred_element_type=F32) * cs_ref[...]
    if mode == "rope_ret":
        cos = rest[0][...]
        sin = rest[1][...]
        half = RET_DK // 2
        for hh in range(tn // RET_DK):
            a = hh * RET_DK
            x1 = acc[:, a:a + half]
            x2 = acc[:, a + half:a + RET_DK]
            o_ref[:, a:a + half] = (x1 * cos - x2 * sin).astype(o_ref.dtype)
            o_ref[:, a + half:a + RET_DK] = (x1 * sin + x2 * cos).astype(o_ref.dtype)
    elif mode == "rope_diff":
        cos = rest[0][...]
        sin_lo = rest[1][...]
        sin_hi = rest[2][...]
        q16 = DIFF_HD // 4
        for g in range(tn // LANES):
            x = acc[:, g * LANES:(g + 1) * LANES]
            y = x * cos + pltpu.roll(x, LANES - q16, 1) * sin_lo + pltpu.roll(x, q16, 1) * sin_hi
            o_ref[:, g * LANES:(g + 1) * LANES] = y.astype(o_ref.dtype)
    elif transposed:
        acc_t = acc.T
        ones = jnp.ones((DIFF_VT_ROWS - DIFF_DV, tk_out), o_ref.dtype)
        for a in range(o_ref.shape[0]):
            for hh in range(tn // DIFF_DV):
                r0 = hh * DIFF_VT_ROWS
                o_ref[a, r0:r0 + DIFF_DV, :] = acc_t[hh * DIFF_DV:(hh + 1) * DIFF_DV,
                                                     a * tk_out:(a + 1) * tk_out].astype(o_ref.dtype)
                o_ref[a, r0 + DIFF_DV:r0 + DIFF_VT_ROWS, :] = ones
    else:
        o_ref[...] = acc.astype(o_ref.dtype)


def _proj(h, w, colscale, tm, first, rest0, ntiles, out_dtype, mode="plain", tables=(), tiles_per_batch=1,
          tk_out=None):
    r, d = h.shape
    tn = PROJ_TN
    ncols = ntiles * tn
    assert r % tm == 0
    transposed = tk_out is not None
    wcol = lambda j: jnp.where(j == 0, first, rest0 + j - 1)
    in_specs = [pl.BlockSpec((tm, d), lambda i, j: (i, 0)),
                pl.BlockSpec((d, tn), lambda i, j: (0, wcol(j))),
                pl.BlockSpec((1, tn), lambda i, j: (0, wcol(j)))]
    for _ in tables:
        in_specs.append(pl.BlockSpec((tm, LANES), lambda i, j: (i % tiles_per_batch, 0)))
    if transposed:
        assert tm % tk_out == 0
        na = tm // tk_out
        rows_out = tn // DIFF_DV * DIFF_VT_ROWS
        out_specs = pl.BlockSpec((na, rows_out, tk_out), lambda i, j: (i, j, 0))
        out_shape = jax.ShapeDtypeStruct((r // tk_out, ncols // DIFF_DV * DIFF_VT_ROWS, tk_out), out_dtype)
    else:
        out_specs = pl.BlockSpec((tm, tn), lambda i, j: (i, j))
        out_shape = jax.ShapeDtypeStruct((r, ncols), out_dtype)
    return pl.pallas_call(
        functools.partial(_proj_kernel, mode=mode, transposed=transposed, tn=tn, tk_out=tk_out),
        grid=(r // tm, ntiles),
        in_specs=in_specs,
        out_specs=out_specs,
        out_shape=out_shape,
        compiler_params=_params(2, 52),
        name="in_proj_" + mode + ("_t" if transposed else ""),
    )(h, w, colscale, *tables)


def _ret_tables_kernel(dec_ref, intra_ref, qd_ref, kd_ref, cd_ref, *, chunk):
    direction = pl.program_id(0)
    head = pl.program_id(1)
    cf = float(chunk)
    dv = jnp.full((1, 1), dec_ref[direction, head], F32)
    lg = jnp.minimum(dv, 0.0) - jnp.log(1.0 + jnp.exp(-jnp.abs(dv)))
    rev = direction == 1
    ii = lax.broadcasted_iota(jnp.int32, (chunk, chunk), 0).astype(F32)
    jj = lax.broadcasted_iota(jnp.int32, (chunk, chunk), 1).astype(F32)
    dist = jnp.where(rev, jj - ii, ii - jj)
    intra_ref[0, 0] = jnp.where(dist >= 0.0, jnp.exp(lg * jnp.maximum(dist, 0.0)), 0.0)
    i1 = lax.broadcasted_iota(jnp.int32, (chunk, RET_DK), 0).astype(F32)
    qd_ref[0, 0] = jnp.exp(lg * jnp.where(rev, cf - i1, i1 + 1.0))
    kd_ref[0, 0] = jnp.exp(lg * jnp.where(rev, i1, cf - 1.0 - i1))
    cd_ref[0, 0] = jnp.exp(jnp.broadcast_to(lg, (8, RET_DK)) * cf)


def _retention_decays(dec_f, dec_b, chunk):
    hh, dk = RET_HEADS, RET_DK
    blk = lambda *shape: pl.BlockSpec((1, 1) + shape, lambda d, h: (d, h, 0, 0))
    return pl.pallas_call(
        functools.partial(_ret_tables_kernel, chunk=chunk),
        grid=(2, hh),
        in_specs=[pl.BlockSpec(memory_space=pltpu.SMEM)],
        out_specs=[blk(chunk, chunk), blk(chunk, dk), blk(chunk, dk), blk(8, dk)],
        out_shape=[jax.ShapeDtypeStruct((2, hh, chunk, chunk), F32),
                   jax.ShapeDtypeStruct((2, hh, chunk, dk), F32),
                   jax.ShapeDtypeStruct((2, hh, chunk, dk), F32),
                   jax.ShapeDtypeStruct((2, hh, 8, dk), F32)],
        compiler_params=_params(2, 32),
        name="retention_tables",
    )(jnp.stack([dec_f, dec_b]).astype(F32))


def _ret_kernel(intra_ref, qd_ref, kd_ref, cd_ref, qk_ref, v_ref, s0_ref, *rest, reverse, has_prev, chunk, n_chunks):
    if has_prev:
        prev_ref, o_ref, sfin_ref, s_scr = rest
    else:
        o_ref, sfin_ref, s_scr = rest
    first = pl.program_id(1) == 0
    dk = RET_DK
    order = range(n_chunks - 1, -1, -1) if reverse else range(n_chunks)
    for h in range(RET_HEADS):
        cols = slice(h * dk, (h + 1) * dk)
        kcols = slice((RET_HEADS + h) * dk, (RET_HEADS + h + 1) * dk)
        s = jnp.where(first, s0_ref[0, h], s_scr[h])
        for c in order:
            rows = slice(c * chunk, (c + 1) * chunk)
            q = qk_ref[0, rows, cols]
            k = qk_ref[0, rows, kcols]
            v = v_ref[0, rows, cols]
            att = lax.dot_general(q, k, NT_DIMS, preferred_element_type=F32) * intra_ref[0, h]
            qd = (q.astype(F32) * qd_ref[0, h]).astype(BF16)
            o = (jnp.dot(att.astype(BF16), v, preferred_element_type=F32)
                 + jnp.dot(qd, s.astype(BF16), preferred_element_type=F32))
            kd = (k.astype(F32) * kd_ref[0, h]).astype(BF16)
            s = s * cd_ref[0, h, 0:1, :] + lax.dot_general(kd, v, TN_DIMS, preferred_element_type=F32)
            if has_prev:
                o = o + prev_ref[0, rows, cols]
            o_ref[0, rows, cols] = o
        s_scr[h] = s
        sfin_ref[0, h] = s


def _retention(tables, qk, v_arr, s0, prev, reverse):
    b, length, _ = qk.shape
    hh, dk = RET_HEADS, RET_DK
    width = hh * dk
    c = tables[0].shape[-1]
    rb = min(RET_BLOCK, length)
    n = length // rb
    direction = 1 if reverse else 0
    pos = (lambda i: n - 1 - i) if reverse else (lambda i: i)
    tab = lambda *shape: pl.BlockSpec((1, hh) + shape, lambda bi, i: (direction, 0, 0, 0))
    in_specs = [tab(c, c), tab(c, dk), tab(c, dk), tab(8, dk),
                pl.BlockSpec((1, rb, 2 * width), lambda bi, i: (bi, pos(i), 0)),
                pl.BlockSpec((1, rb, width), lambda bi, i: (bi, pos(i), 0)),
                pl.BlockSpec((1, hh, dk, dk), lambda bi, i: (bi, 0, 0, 0))]
    args = [*tables, qk, v_arr, s0]
    if prev is not None:
        in_specs.append(pl.BlockSpec((1, rb, width), lambda bi, i: (bi, pos(i), 0)))
        args.append(prev)
    return pl.pallas_call(
        functools.partial(_ret_kernel, reverse=reverse, has_prev=prev is not None, chunk=c, n_chunks=rb // c),
        grid=(b, n),
        in_specs=in_specs,
        out_specs=[pl.BlockSpec((1, rb, width), lambda bi, i: (bi, pos(i), 0)),
                   pl.BlockSpec((1, hh, dk, dk), lambda bi, i: (bi, 0, 0, 0))],
        out_shape=[jax.ShapeDtypeStruct((b, length, width), F32),
                   jax.ShapeDtypeStruct((b, hh, dk, dk), F32)],
        scratch_shapes=[pltpu.VMEM((hh, dk, dk), F32)],
        compiler_params=_params(2, 48),
        name="retention_bwd" if reverse else "retention_fwd",
    )(*args)


def _diff_kernel(zero_ref, lam_ref, g_ref, q_ref, *rest, tq, n_lat, tile, lc, lam_init):
    if n_lat:
        kl_ref, vtl_ref = rest[:2]
        rest = rest[2:]
    kc_ref, vtc_ref, o_ref, qq_scr = rest[:4]
    s_buf, mt_buf, p_buf, al_buf = rest[4:8], rest[8:12], rest[12:14], rest[14:16]
    m_scr, acc_scr = rest[16:]
    n = n_lat + 1
    rows_of = lambda j: lc if j == n_lat else tile

    q_t = q_ref[0].astype(F32).T
    row = lax.broadcasted_iota(jnp.int32, q_t.shape, 0)
    qq_scr[:, 0:tq] = jnp.where(row < DIFF_HD, q_t, 0.0).astype(BF16)
    qq_scr[:, tq:2 * tq] = jnp.where(row >= DIFF_HD, q_t, 0.0).astype(BF16)
    m_scr[...] = jnp.full(m_scr.shape, NEG_BIG, F32)
    acc_scr[...] = jnp.zeros(acc_scr.shape, F32)

    def scores(j, slot):
        if isinstance(j, int) and j == n_lat:
            k = kc_ref[0]
        else:
            off = j * tile if isinstance(j, int) else pl.multiple_of(j * tile, tile)
            k = kl_ref[0, pl.ds(off, tile), :]
        s = jnp.dot(k, qq_scr[...], preferred_element_type=F32)
        mt_buf[slot][...] = jnp.max(s, axis=0, keepdims=True)
        s_buf[slot][0:k.shape[0], :] = s

    def softmax(s_slot, p_slot, rows):
        m_old = m_scr[...]
        m_new = jnp.maximum(m_old, mt_buf[s_slot][...])
        al_buf[p_slot][...] = jnp.exp2(m_old - m_new)
        m_scr[...] = m_new
        z = pl.multiple_of(zero_ref[0], 8)
        s_buf[s_slot][pl.ds(z, 8), :] = s_buf[s_slot][pl.ds(z, 8), :]
        ch = DIFF_SOFTMAX_ROWS
        for r0 in range(0, rows, ch):
            p_buf[p_slot][r0:r0 + ch, :] = jnp.exp2(s_buf[s_slot][r0:r0 + ch, :] - m_new).astype(BF16)

    def values(j, slot):
        if isinstance(j, int) and j == n_lat:
            vt = vtc_ref[0]
        else:
            vt = vtl_ref[j]
        acc_scr[...] = (acc_scr[...] * al_buf[slot][...]
                        + jnp.dot(vt, p_buf[slot][0:vt.shape[1], :], preferred_element_type=F32))

    def step(t, tmod):
        static = isinstance(t, int)
        if not (static and t == 0):
            values(t - 1, (tmod - 1) % 2)
        if not static or t + 2 < n:
            scores(t + 2, (tmod + 2) % 4)
        softmax(tmod, tmod % 2, rows_of(t) if static else tile)

    scores(0, 0)
    if n > 1:
        scores(1, 1)
    step(0, 0)
    unroll = DIFF_UNROLL
    n_loop = max(0, (n_lat - 3) // unroll)

    def body(i, carry):
        for u in range(unroll):
            step(1 + unroll * i + u, (1 + u) % 4)
        return carry

    if n_loop:
        lax.fori_loop(0, n_loop, body, 0)
    for t in range(1 + unroll * n_loop, n):
        step(t, t % 4)
    values(n - 1, (n - 1) % 2)

    acc = acc_scr[...]
    o = acc[0:DIFF_DV] * (1.0 / acc[DIFF_DV:DIFF_DV + 1])
    lam4 = lam_ref[...]
    s1 = jnp.sum(lam4[0:1] * lam4[1:2], axis=-1, keepdims=True)
    s2 = jnp.sum(lam4[2:3] * lam4[3:4], axis=-1, keepdims=True)
    lam = jnp.exp(s1) - jnp.exp(s2) + lam_init
    y = (o[:, :tq] - lam * o[:, tq:]).T
    ms = jnp.mean(y * y, axis=-1, keepdims=True)
    y = y * lax.rsqrt(ms + SUBLN_EPS) * g_ref[...] * (1.0 - lam_init)
    o_ref[0] = y.astype(o_ref.dtype)


def _diff_attention(lam4, subln_g, q_arr, lat, ctx, lam_init):
    b, lq, _ = q_arr.shape
    hh = DIFF_HEADS
    tq = min(DIFF_TQ, lq)
    qk_c, vt_c = ctx
    lc = qk_c.shape[1]
    assert vt_c.shape == (b, hh * DIFF_VT_ROWS, lc)
    in_specs = [pl.BlockSpec(memory_space=pltpu.SMEM),
                pl.BlockSpec((4, DIFF_HD), lambda bi, h, i: (0, 0)),
                pl.BlockSpec((1, DIFF_DV), lambda bi, h, i: (0, 0)),
                pl.BlockSpec((1, tq, LANES), lambda bi, h, i: (bi, i, h))]
    args = [jnp.zeros((1,), jnp.int32), lam4, subln_g, q_arr]
    n_lat, tile = 0, lc
    if lat is not None:
        qk_l, vt_l = lat
        length = qk_l.shape[1]
        tile = vt_l.shape[2]
        n_lat = length // tile
        assert lc <= tile
        in_specs += [pl.BlockSpec((1, length, LANES), lambda bi, h, i: (bi, 0, hh + h)),
                     pl.BlockSpec((n_lat, DIFF_VT_ROWS, tile), lambda bi, h, i: (bi, h, 0))]
        args += [qk_l, vt_l]
    in_specs += [pl.BlockSpec((1, lc, LANES), lambda bi, h, i: (bi, 0, hh + h)),
                 pl.BlockSpec((1, DIFF_VT_ROWS, lc), lambda bi, h, i: (bi, h, 0))]
    args += [qk_c, vt_c]
    return pl.pallas_call(
        functools.partial(_diff_kernel, tq=tq, n_lat=n_lat, tile=tile, lc=lc, lam_init=lam_init),
        grid=(b, hh, lq // tq),
        in_specs=in_specs,
        out_specs=pl.BlockSpec((1, tq, LANES), lambda bi, h, i: (bi, i, h)),
        out_shape=jax.ShapeDtypeStruct((b, lq, hh * DIFF_DV), BF16),
        scratch_shapes=[pltpu.VMEM((LANES, 2 * tq), BF16),
                        *[pltpu.VMEM((tile, 2 * tq), F32)] * 4,
                        *[pltpu.VMEM((1, 2 * tq), F32)] * 4,
                        *[pltpu.VMEM((tile, 2 * tq), BF16)] * 2,
                        *[pltpu.VMEM((1, 2 * tq), F32)] * 2,
                        pltpu.VMEM((1, 2 * tq), F32),
                        pltpu.VMEM((DIFF_VT_ROWS, 2 * tq), F32)],
        compiler_params=_params(3, 40),
        name="diff_attention" if lat is not None else "diff_attention_ctx",
    )(*args)


def _softmax_pv(s_list, v_list):
    m = s_list[0].max(axis=-1, keepdims=True)
    for s in s_list[1:]:
        m = jnp.maximum(m, s.max(axis=-1, keepdims=True))
    den = 0.0
    o = 0.0
    for s, v in zip(s_list, v_list):
        p = jnp.exp2(s - m)
        den = den + jnp.sum(p, axis=-1, keepdims=True)
        o = o + jnp.dot(p.astype(BF16), v, preferred_element_type=F32)
    return o * (1.0 / den)


def _na_kernel(zero_ref, q_ref, k_ref, vtl_ref, kc_ref, vtc_ref, bias_ref, o_ref, *scr, units, n_rows, lc):
    s_buf, mt_buf, p_buf = scr[0:3], scr[3:6], scr[6:8]
    w = GRID_W
    uq = NA_UNIT_ROWS * w
    nk = NA_UNION_ROWS * w
    step_r0 = pl.program_id(2) * (units * NA_UNIT_ROWS)

    def first_key_row(u):
        return jnp.clip(step_r0 + u * NA_UNIT_ROWS - NA_KH // 2, 0, n_rows - NA_UNION_ROWS)

    def scores(u, slot):
        start = first_key_row(u)
        cfg = (step_r0 + u * NA_UNIT_ROWS - start) // NA_UNIT_ROWS
        q_t = q_ref[0, u * uq:(u + 1) * uq, :].astype(F32).T
        row = lax.broadcasted_iota(jnp.int32, q_t.shape, 0)
        qq = jnp.concatenate([jnp.where(row < NA_HD, q_t, 0.0), jnp.where(row >= NA_HD, q_t, 0.0)],
                             axis=1).astype(BF16)
        koff = pl.multiple_of(start * w, uq)
        s_nb = jnp.dot(k_ref[0, pl.ds(koff, nk), :], qq, preferred_element_type=F32) + bias_ref[0, cfg]
        s_cx = jnp.dot(kc_ref[0], qq, preferred_element_type=F32)
        mt_buf[slot][...] = jnp.maximum(jnp.max(s_nb, axis=0, keepdims=True), jnp.max(s_cx, axis=0, keepdims=True))
        s_buf[slot][0:nk, :] = s_nb
        s_buf[slot][nk:nk + lc, :] = s_cx

    def softmax(s_slot, p_slot):
        m = mt_buf[s_slot][...]
        z = pl.multiple_of(zero_ref[0], 8)
        s_buf[s_slot][pl.ds(z, 8), :] = s_buf[s_slot][pl.ds(z, 8), :]
        ch = NA_SOFTMAX_ROWS
        for r0 in range(0, nk + lc, ch):
            p_buf[p_slot][r0:r0 + ch, :] = jnp.exp2(s_buf[s_slot][r0:r0 + ch, :] - m).astype(BF16)

    def values(u, p_slot):
        pc0 = first_key_row(u) // NA_UNIT_ROWS
        acc = jnp.dot(vtc_ref[0], p_buf[p_slot][nk:nk + lc, :], preferred_element_type=F32)
        for i in range(NA_UNION_ROWS // NA_UNIT_ROWS):
            acc = acc + jnp.dot(vtl_ref[pc0 + i], p_buf[p_slot][i * uq:(i + 1) * uq, :],
                                preferred_element_type=F32)
        o = acc[0:LANES] * (1.0 / acc[LANES:LANES + 1])
        y = jnp.concatenate([o[0:NA_HD, 0:uq], o[NA_HD:LANES, uq:2 * uq]], axis=0)
        o_ref[0, u * uq:(u + 1) * uq, :] = y.T.astype(o_ref.dtype)

    scores(0, 0)
    if units > 1:
        scores(1, 1)
    for u in range(units):
        if u >= 1:
            values(u - 1, (u - 1) % 2)
        if u + 2 < units:
            scores(u + 2, (u + 2) % 3)
        softmax(u % 3, u % 2)
    values(units - 1, (units - 1) % 2)


def _na_attention(qk_l, vt_l, qk_c, vt_c, bias):
    b, length, _ = qk_l.shape
    lc = qk_c.shape[1]
    n_rows = length // GRID_W
    uq = NA_UNIT_ROWS * GRID_W
    nk = NA_UNION_ROWS * GRID_W
    assert n_rows >= NA_UNION_ROWS and n_rows % NA_UNIT_ROWS == 0 and vt_l.shape[2] == uq
    units = min(NA_UNITS, n_rows // NA_UNIT_ROWS)
    npair = NA_HEADS // 2
    n_pieces = length // uq
    tq = units * uq
    return pl.pallas_call(
        functools.partial(_na_kernel, units=units, n_rows=n_rows, lc=lc),
        grid=(b, npair, length // tq),
        in_specs=[pl.BlockSpec(memory_space=pltpu.SMEM),
                  pl.BlockSpec((1, tq, LANES), lambda bi, p, i: (bi, i, npair + p)),
                  pl.BlockSpec((1, length, LANES), lambda bi, p, i: (bi, 0, 2 * npair + p)),
                  pl.BlockSpec((n_pieces, DIFF_VT_ROWS, uq), lambda bi, p, i: (bi, p, 0)),
                  pl.BlockSpec((1, lc, LANES), lambda bi, p, i: (bi, 0, 2 * npair + p)),
                  pl.BlockSpec((1, DIFF_VT_ROWS, lc), lambda bi, p, i: (bi, p, 0)),
                  pl.BlockSpec((1, 3, nk, 2 * uq), lambda bi, p, i: (p, 0, 0, 0))],
        out_specs=pl.BlockSpec((1, tq, LANES), lambda bi, p, i: (bi, i, p)),
        out_shape=jax.ShapeDtypeStruct((b, length, NA_HEADS * NA_HD), BF16),
        scratch_shapes=[*[pltpu.VMEM((nk + lc, 2 * uq), F32)] * 3,
                        *[pltpu.VMEM((1, 2 * uq), F32)] * 3,
                        *[pltpu.VMEM((nk + lc, 2 * uq), BF16)] * 2],
        compiler_params=_params(3, 48),
        name="neighbourhood_attention",
    )(jnp.zeros((1,), jnp.int32), qk_l, qk_l, vt_l, qk_c, vt_c, bias)


def _ctx_attn_kernel(q_ref, k_ref, v_ref, o_ref):
    q = q_ref[0].astype(F32)
    k = k_ref[0]
    v = v_ref[0]
    lane = lax.broadcasted_iota(jnp.int32, q.shape, 1)
    outs = []
    for hh in range(2):
        sel = (lane < NA_HD) if hh == 0 else (lane >= NA_HD)
        qm = jnp.where(sel, q, 0.0).astype(BF16)
        s = lax.dot_general(qm, k, NT_DIMS, preferred_element_type=F32)
        outs.append(_softmax_pv([s], [v]))
    o_ref[0] = jnp.where(lane < NA_HD, outs[0], outs[1]).astype(o_ref.dtype)


def _ctx_attention(qkv_c):
    b, lc, _ = qkv_c.shape
    npair = NA_HEADS // 2
    return pl.pallas_call(
        _ctx_attn_kernel,
        grid=(b, npair),
        in_specs=[pl.BlockSpec((1, lc, LANES), lambda bi, p: (bi, 0, npair + p)),
                  pl.BlockSpec((1, lc, LANES), lambda bi, p: (bi, 0, 2 * npair + p)),
                  pl.BlockSpec((1, lc, LANES), lambda bi, p: (bi, 0, 3 * npair + p))],
        out_specs=pl.BlockSpec((1, lc, LANES), lambda bi, p: (bi, 0, p)),
        out_shape=jax.ShapeDtypeStruct((b, lc, NA_HEADS * NA_HD), BF16),
        compiler_params=_params(2, 32),
        name="ctx_attention",
    )(qkv_c, qkv_c, qkv_c)


def _merge_kernel(oret_ref, gates_ref, odiff_ref, ona_ref, x_ref, g1_ref, lng_ref, lnb_ref,
                  wpa_ref, wpb_ref, wpc_ref, wo_ref, o_ref, *, alpha):
    d = D_MODEL
    ya_parts = []
    for hh in range(RET_HEADS):
        a = hh * RET_DK
        seg = _layer_norm_rows(oret_ref[:, a:a + RET_DK])
        g = gates_ref[:, a:a + RET_DK]
        ya_parts.append((g * _sigmoid(g) * seg).astype(BF16))
    ya_in = jnp.concatenate(ya_parts, axis=1)
    y_a = jnp.dot(ya_in, wpa_ref[...], preferred_element_type=F32)
    y_b = jnp.dot(odiff_ref[...], wpb_ref[...], preferred_element_type=F32)
    y_c = jnp.dot(ona_ref[...], wpc_ref[...], preferred_element_type=F32)
    m = (_sigmoid(gates_ref[:, d:2 * d]) * y_a + _sigmoid(gates_ref[:, 2 * d:3 * d]) * y_b
         + _sigmoid(gates_ref[:, 3 * d:4 * d]) * y_c)
    y = jnp.dot(m.astype(BF16), wo_ref[...], preferred_element_type=F32)
    z = alpha * x_ref[...] + g1_ref[0] * y
    o_ref[...] = _layer_norm_rows(z) * lng_ref[...] + lnb_ref[...]


def _merge(o_ret, gates, o_diff, o_na, x, g1, ln_g, ln_b, w_pa, w_pb, w_pc, w_o, mod_row, alpha):
    r, d = x.shape
    tm = min(MERGE_TILE, r)
    row = lambda i: (i, 0)
    const = lambda i: (0, 0)
    wspec = pl.BlockSpec((d, d), const)
    return pl.pallas_call(
        functools.partial(_merge_kernel, alpha=alpha),
        grid=(r // tm,),
        in_specs=[pl.BlockSpec((tm, d), row), pl.BlockSpec((tm, 4 * d), row), pl.BlockSpec((tm, d), row),
                  pl.BlockSpec((tm, d), row), pl.BlockSpec((tm, d), row),
                  pl.BlockSpec((1, 1, d), lambda i: (mod_row(i, tm), 0, 0)),
                  pl.BlockSpec((1, d), const), pl.BlockSpec((1, d), const),
                  wspec, wspec, wspec, wspec],
        out_specs=pl.BlockSpec((tm, d), row),
        out_shape=jax.ShapeDtypeStruct((r, d), F32),
        compiler_params=_params(1, 52),
        name="merge_out_proj",
    )(o_ret, gates, o_diff, o_na, x, g1, ln_g, ln_b, w_pa, w_pb, w_pc, w_o)


def _ffn_kernel(x_ref, sh_ref, sc_ref, g2_ref, lng_ref, lnb_ref, w1_ref, w2_ref, o_ref, *, alpha, ff_chunk):
    x = x_ref[...]
    h = (_layer_norm_rows(x) * (1.0 + sc_ref[0]) + sh_ref[0]).astype(BF16)
    acc = jnp.zeros(x.shape, F32)
    for cidx in range(D_FF // ff_chunk):
        a = cidx * ff_chunk
        u = jnp.maximum(jnp.dot(h, w1_ref[:, a:a + ff_chunk], preferred_element_type=F32), 0.0)
        acc = acc + jnp.dot((u * u).astype(BF16), w2_ref[a:a + ff_chunk, :], preferred_element_type=F32)
    z = alpha * x + g2_ref[0] * acc
    o_ref[...] = _layer_norm_rows(z) * lng_ref[...] + lnb_ref[...]


def _ffn(x, shift, scale, g2, ln_g, ln_b, w1, w2, mod_row, alpha):
    r, d = x.shape
    tm = min(FFN_TILE, r)
    row = lambda i: (i, 0)
    const = lambda i: (0, 0)
    mspec = pl.BlockSpec((1, 1, d), lambda i: (mod_row(i, tm), 0, 0))
    return pl.pallas_call(
        functools.partial(_ffn_kernel, alpha=alpha, ff_chunk=1024),
        grid=(r // tm,),
        in_specs=[pl.BlockSpec((tm, d), row), mspec, mspec, mspec,
                  pl.BlockSpec((1, d), const), pl.BlockSpec((1, d), const),
                  pl.BlockSpec((d, D_FF), const), pl.BlockSpec((D_FF, d), const)],
        out_specs=pl.BlockSpec((tm, d), row),
        out_shape=jax.ShapeDtypeStruct((r, d), F32),
        compiler_params=_params(1, 56),
        name="ffn",
    )(x, shift, scale, g2, ln_g, ln_b, w1, w2)


def _retention_tables(length):
    t = jnp.arange(length, dtype=F32)
    inv = ROPE_BASE ** (-jnp.linspace(0.0, 1.0, RET_DK // 2, dtype=F32))
    ang = t[:, None] * inv[None, :]
    return jnp.cos(ang), jnp.sin(ang)


def _axial_tables(length):
    t = jnp.arange(length)
    row = (t // GRID_W).astype(F32)
    col = (t % GRID_W).astype(F32)
    half = DIFF_HD // 2
    inv = ROPE_BASE ** (-jnp.arange(0, half, 2, dtype=F32) / half)
    ang_r = row[:, None] * inv[None, :]
    ang_c = col[:, None] * inv[None, :]
    zero = jnp.zeros_like(ang_r)
    unit_ang = jnp.concatenate([ang_r, ang_r, ang_c, ang_c], axis=-1)
    cos = jnp.cos(unit_ang)
    sin_r, sin_c = jnp.sin(ang_r), jnp.sin(ang_c)
    sin_lo = jnp.concatenate([-sin_r, zero, -sin_c, zero], axis=-1)
    sin_hi = jnp.concatenate([zero, sin_r, zero, sin_c], axis=-1)
    rep = LANES // DIFF_HD
    return tuple(jnp.tile(a, (1, rep)) for a in (cos, sin_lo, sin_hi))


def _na_bias_table(rpb):
    nh, nr, _ = rpb.shape
    w = GRID_W
    wid = 2 * w - 1
    p = jnp.pad(rpb.astype(F32), ((0, 0), (0, 0), (w - NA_KW, w - NA_KW)))
    a = jnp.broadcast_to(p[:, :, None, :], (nh, nr, w, wid))
    a = jnp.pad(a, ((0, 0), (0, 0), (0, 0), (0, 1))).reshape(nh, nr, w * (wid + 1))
    a = a[:, :, :w * wid].reshape(nh, nr, w, wid)
    m = a[..., w - 1:]
    j = jnp.arange(w)
    cs = jnp.clip(j - NA_KW // 2, 0, w - NA_KW)
    col_mask = (j[None, :] >= cs[:, None]) & (j[None, :] < cs[:, None] + NA_KW)
    m = jnp.where(col_mask[None, None], m, NEG_BIG)
    un, g_rows = NA_UNION_ROWS, NA_UNIT_ROWS
    mp = jnp.pad(m, ((0, 0), (un, un), (0, 0), (0, 0)), constant_values=NEG_BIG)
    a = jnp.arange(un)
    per_cfg = []
    for cfg in range(3):
        per_g = []
        for g in range(g_rows):
            q_row = cfg * g_rows + g
            win0 = (0, g, un - NA_KH)[cfg]
            lo = NA_KH - 1 - q_row + un
            in_win = (a >= win0) & (a < win0 + NA_KH)
            per_g.append(jnp.where(in_win[None, :, None, None], mp[:, lo:lo + un], NEG_BIG))
        per_cfg.append(jnp.stack(per_g, axis=1))
    tbl = jnp.stack(per_cfg, axis=1).reshape(nh // 2, 2, 3, g_rows, un, w, w)
    tbl = tbl.transpose(0, 2, 4, 6, 1, 3, 5).reshape(nh // 2, 3, un * w, 2 * g_rows * w)
    return jnp.where(tbl > 0.5 * NEG_BIG, tbl * math.log2(math.e), NEG_BIG)


def _column_scale():
    d = D_MODEL
    cs = jnp.ones((IN_WIDTH,), F32)
    cs = cs.at[d:2 * d].set(RET_DK ** -0.5)
    cs = cs.at[4 * d:5 * d].set(DIFF_HD ** -0.5 * math.log2(math.e))
    cs = cs.at[7 * d:8 * d].set(NA_HD ** -0.5 * math.log2(math.e))
    return cs.reshape(1, IN_WIDTH)


def _project_all(h, w, colscale, tm, rope, ret_tables, ax_tables, tiles_per_batch, tk_out, na_tk_out, plain_na_v):
    if rope:
        ret_qk = _proj(h, w, colscale, tm, 0, 1, 2, BF16, "rope_ret", ret_tables, tiles_per_batch)
        diff_qk = _proj(h, w, colscale, tm, 4, 5, 2, BF16, "rope_diff", ax_tables, tiles_per_batch)
    else:
        ret_qk = _proj(h, w, colscale, tm, 0, 1, 2, BF16)
        diff_qk = _proj(h, w, colscale, tm, 4, 5, 2, BF16)
    v_na = _proj(h, w, colscale, tm, 2, 7, 4 if plain_na_v else 3, BF16)
    gates = _proj(h, w, colscale, tm, 3, 10, 4, F32)
    diff_vt = _proj(h, w, colscale, tm, 6, 7, 1, BF16, tk_out=tk_out)
    na_vt = _proj(h, w, colscale, tm, 9, 10, 1, BF16, tk_out=na_tk_out)
    return ret_qk, diff_qk, v_na, gates, diff_vt, na_vt


def kernel(x, c, ctx, c_ctx, w_mod, b_mod, w_in, ret_decay_f, ret_decay_b, diff_lq1, diff_lk1, diff_lq2, diff_lk2,
           diff_subln_g, na_rpb, w_pa, w_pb, w_pc, w_o, ln1_g, ln1_b, w_ff1, w_ff2, ln2_g, ln2_b):
    b, length, d = x.shape
    lc = ctx.shape[1]
    depth = w_mod.shape[0]
    assert d == D_MODEL and b < 8 and length % min(ROW_TILE, length) == 0 and length % GRID_W == 0
    assert lc % RET_CHUNK == 0 or lc < RET_CHUNK
    alpha = (2.0 * depth) ** 0.25

    cvec = jnp.zeros((8, d), F32).at[:b].set(c).at[b].set(c_ctx)
    mods = _modulation(cvec, w_mod, b_mod)

    lat_row = lambda i, tm: (i * tm) // length
    ctx_row = lambda i, tm: b
    colscale = _column_scale()
    ret_tables = _retention_tables(length)
    ax_tables = _axial_tables(length)

    x_l = x.reshape(b * length, d)
    x_c = ctx.reshape(b * lc, d)
    zero_state = jnp.zeros((b, RET_HEADS, RET_DK, RET_DK), F32)
    for l in range(depth):
        with_ctx_out = l < depth - 1
        mod = [mods[l, :, k * d:(k + 1) * d].reshape(8, 1, d) for k in range(6)]
        sh1, sc1, g1, sh2, sc2, g2 = mod
        w = w_in[l].astype(BF16)
        wpa, wpb, wpc, wo = (a[l].astype(BF16) for a in (w_pa, w_pb, w_pc, w_o))
        w1, w2 = w_ff1[l].astype(BF16), w_ff2[l].astype(BF16)
        lng1, lnb1 = ln1_g[l].reshape(1, d), ln1_b[l].reshape(1, d)
        lng2, lnb2 = ln2_g[l].reshape(1, d), ln2_b[l].reshape(1, d)
        lam_init = 0.8 - 0.6 * math.exp(-0.3 * l)
        lam4 = jnp.stack([diff_lq1[l], diff_lk1[l], diff_lq2[l], diff_lk2[l]]).astype(F32)
        subln = diff_subln_g[l].reshape(1, DIFF_DV).astype(F32)
        bias = _na_bias_table(na_rpb[l])

        h_l = _ln_mod(x_l, sh1, sc1, lat_row)
        h_c = _ln_mod(x_c, sh1, sc1, ctx_row)
        tm_l = min(ROW_TILE, length)
        rqk_l, dqk_l, vna_l, gates_l, dvt_l, nvt_l = _project_all(
            h_l, w, colscale, tm_l, True, ret_tables, ax_tables, length // tm_l, min(DIFF_TILE, length),
            NA_UNIT_ROWS * GRID_W, False)
        rqk_c, dqk_c, vna_c, gates_c, dvt_c, nvt_c = _project_all(
            h_c, w, colscale, min(ROW_TILE, b * lc), False, (), (), 1, lc, lc, True)
        sh3 = lambda a, n: a.reshape(b, n, a.shape[-1])

        decays = _retention_decays(ret_decay_f[l], ret_decay_b[l], min(RET_CHUNK, lc))
        rqk_l3, vna_l3, rqk_c3, vna_c3 = sh3(rqk_l, length), sh3(vna_l, length), sh3(rqk_c, lc), sh3(vna_c, lc)
        o_cf, s_cf = _retention(decays, rqk_c3, vna_c3, zero_state, None, False)
        o_rc, s_cb = _retention(decays, rqk_c3, vna_c3, zero_state, o_cf, True)
        o_lf, _ = _retention(decays, rqk_l3, vna_l3, s_cf, None, False)
        o_rl, _ = _retention(decays, rqk_l3, vna_l3, s_cb, o_lf, True)

        dqk_l3, dqk_c3 = sh3(dqk_l, length), sh3(dqk_c, lc)
        o_dl = _diff_attention(lam4, subln, dqk_l3, (dqk_l3, dvt_l), (dqk_c3, dvt_c), lam_init)
        o_nl = _na_attention(vna_l3, nvt_l, vna_c3, nvt_c, bias)

        x_l = _merge(o_rl.reshape(b * length, d), gates_l, o_dl.reshape(b * length, d), o_nl.reshape(b * length, d),
                     x_l, g1, lng1, lnb1, wpa, wpb, wpc, wo, lat_row, alpha)
        x_l = _ffn(x_l, sh2, sc2, g2, lng2, lnb2, w1, w2, lat_row, alpha)
        if with_ctx_out:
            o_dc = _diff_attention(lam4, subln, dqk_c3, None, (dqk_c3, dvt_c), lam_init)
            o_nc = _ctx_attention(vna_c3)
            x_c = _merge(o_rc.reshape(b * lc, d), gates_c, o_dc.reshape(b * lc, d), o_nc.reshape(b * lc, d),
                         x_c, g1, lng1, lnb1, wpa, wpb, wpc, wo, ctx_row, alpha)
            x_c = _ffn(x_c, sh2, sc2, g2, lng2, lnb2, w1, w2, ctx_row, alpha)
    return x_l.reshape(b, length, d)
```

```python
import functools
import math

import jax
import jax.numpy as jnp
from jax import lax
from jax.experimental import pallas as pl
from jax.experimental.pallas import tpu as pltpu

F32 = jnp.float32
BF16 = jnp.bfloat16

D_MODEL = 1024
GRID_W = 64
RET_HEADS, RET_DK = 4, 256
DIFF_HEADS, DIFF_HD, DIFF_DV = 8, 64, 128
NA_HEADS, NA_HD, NA_KH, NA_KW = 16, 64, 8, 16
D_FF = 4 * D_MODEL
ROPE_BASE = 10000.0
LN_EPS = 1e-6
SUBLN_EPS = 1e-5
IN_WIDTH = 13 * D_MODEL
NEG_BIG = -1e30

LANES = 128
VMEM_LIMIT_CAP = 56 << 20

ROW_TILE = 2048
PROJ_TN = 1024
LN_TILE = 512
MERGE_TILE = 512
FFN_TILE = 512
RET_CHUNK = 256
RET_BLOCK = 512
DIFF_TQ = 256
DIFF_TILE = 512
DIFF_VT_ROWS = DIFF_DV + 16
DIFF_SOFTMAX_ROWS = 64
DIFF_UNROLL = 16
NA_UNIT_ROWS = 4
NA_UNION_ROWS = 12
NA_UNITS = 8
NA_SOFTMAX_ROWS = 64

NT_DIMS = (((1,), (1,)), ((), ()))
TN_DIMS = (((0,), (0,)), ((), ()))


def _params(n_axes, vmem_mb):
    return pltpu.CompilerParams(dimension_semantics=("arbitrary",) * n_axes,
                                vmem_limit_bytes=min(vmem_mb << 20, VMEM_LIMIT_CAP))


def _sigmoid(x):
    return 1.0 / (1.0 + jnp.exp(-x))


def _layer_norm_rows(x):
    mu = jnp.mean(x, axis=-1, keepdims=True)
    xc = x - mu
    var = jnp.mean(xc * xc, axis=-1, keepdims=True)
    return xc * lax.rsqrt(var + LN_EPS)


def _mod_kernel(c_ref, w_ref, b_ref, o_ref):
    c = c_ref[...]
    a = (c * _sigmoid(c)).astype(BF16)
    o_ref[0] = jnp.dot(a, w_ref[0].astype(BF16), preferred_element_type=F32) + b_ref[0]


def _modulation(cvec, w_mod, b_mod):
    depth = w_mod.shape[0]
    d = D_MODEL
    return pl.pallas_call(
        _mod_kernel,
        grid=(depth, 6),
        in_specs=[pl.BlockSpec((8, d), lambda l, j: (0, 0)),
                  pl.BlockSpec((1, d, d), lambda l, j: (l, 0, j)),
                  pl.BlockSpec((1, 1, d), lambda l, j: (l, 0, j))],
        out_specs=pl.BlockSpec((1, 8, d), lambda l, j: (l, 0, j)),
        out_shape=jax.ShapeDtypeStruct((depth, 8, 6 * d), F32),
        compiler_params=_params(2, 32),
        name="modulation",
    )(cvec, w_mod, b_mod.reshape(depth, 1, 6 * d))


def _ln_mod_kernel(x_ref, sh_ref, sc_ref, o_ref):
    y = _layer_norm_rows(x_ref[...])
    o_ref[...] = (y * (1.0 + sc_ref[0]) + sh_ref[0]).astype(o_ref.dtype)


def _ln_mod(x, shift, scale, mod_row):
    r, d = x.shape
    tm = min(LN_TILE, r)
    return pl.pallas_call(
        _ln_mod_kernel,
        grid=(r // tm,),
        in_specs=[pl.BlockSpec((tm, d), lambda i: (i, 0)),
                  pl.BlockSpec((1, 1, d), lambda i: (mod_row(i, tm), 0, 0)),
                  pl.BlockSpec((1, 1, d), lambda i: (mod_row(i, tm), 0, 0))],
        out_specs=pl.BlockSpec((tm, d), lambda i: (i, 0)),
        out_shape=jax.ShapeDtypeStruct((r, d), BF16),
        compiler_params=_params(1, 32),
        name="ln_modulate",
    )(x, shift, scale)


def _proj_kernel(h_ref, w_ref, cs_ref, *rest, mode, transposed, tn, tk_out):
    o_ref = rest[-1]
    acc = jnp.dot(h_ref[...], w_ref[0].astype(BF16), preferred_element_type=F32) * cs_ref[...]
    if mode == "rope_ret":
        cos = rest[0][...]
        sin = rest[1][...]
        half = RET_DK // 2
        for hh in range(tn // RET_DK):
            a = hh * RET_DK
            x1 = acc[:, a:a + half]
            x2 = acc[:, a + half:a + RET_DK]
            o_ref[:, a:a + half] = (x1 * cos - x2 * sin).astype(o_ref.dtype)
            o_ref[:, a + half:a + RET_DK] = (x1 * sin + x2 * cos).astype(o_ref.dtype)
    elif mode == "rope_diff":
        cos = rest[0][...]
        sin_lo = rest[1][...]
        sin_hi = rest[2][...]
        q16 = DIFF_HD // 4
        for g in range(tn // LANES):
            x = acc[:, g * LANES:(g + 1) * LANES]
            y = x * cos + pltpu.roll(x, LANES - q16, 1) * sin_lo + pltpu.roll(x, q16, 1) * sin_hi
            o_ref[:, g * LANES:(g + 1) * LANES] = y.astype(o_ref.dtype)
    elif transposed:
        acc_t = acc.T
        ones = jnp.ones((DIFF_VT_ROWS - DIFF_DV, tk_out), o_ref.dtype)
        for a in range(o_ref.shape[0]):
            for hh in range(tn // DIFF_DV):
                r0 = hh * DIFF_VT_ROWS
                o_ref[a, r0:r0 + DIFF_DV, :] = acc_t[hh * DIFF_DV:(hh + 1) * DIFF_DV,
                                                     a * tk_out:(a + 1) * tk_out].astype(o_ref.dtype)
                o_ref[a, r0 + DIFF_DV:r0 + DIFF_VT_ROWS, :] = ones
    else:
        o_ref[...] = acc.astype(o_ref.dtype)


def _proj(h, w, colscale, tm, first, rest0, ntiles, out_dtype, mode="plain", tables=(), tiles_per_batch=1,
          tk_out=None):
    w_all, layer = w
    r, d = h.shape
    tn = PROJ_TN
    ncols = ntiles * tn
    assert r % tm == 0
    transposed = tk_out is not None
    wcol = lambda j: jnp.where(j == 0, first, rest0 + j - 1)
    in_specs = [pl.BlockSpec((tm, d), lambda i, j: (i, 0)),
                pl.BlockSpec((1, d, tn), lambda i, j: (layer, 0, wcol(j))),
                pl.BlockSpec((1, tn), lambda i, j: (0, wcol(j)))]
    for _ in tables:
        in_specs.append(pl.BlockSpec((tm, LANES), lambda i, j: (i % tiles_per_batch, 0)))
    if transposed:
        assert tm % tk_out == 0
        na = tm // tk_out
        rows_out = tn // DIFF_DV * DIFF_VT_ROWS
        out_specs = pl.BlockSpec((na, rows_out, tk_out), lambda i, j: (i, j, 0))
        out_shape = jax.ShapeDtypeStruct((r // tk_out, ncols // DIFF_DV * DIFF_VT_ROWS, tk_out), out_dtype)
    else:
        out_specs = pl.BlockSpec((tm, tn), lambda i, j: (i, j))
        out_shape = jax.ShapeDtypeStruct((r, ncols), out_dtype)
    return pl.pallas_call(
        functools.partial(_proj_kernel, mode=mode, transposed=transposed, tn=tn, tk_out=tk_out),
        grid=(r // tm, ntiles),
        in_specs=in_specs,
        out_specs=out_specs,
        out_shape=out_shape,
        compiler_params=_params(2, 52),
        name="in_proj_" + mode + ("_t" if transposed else ""),
    )(h, w_all, colscale, *tables)


def _ret_tables_kernel(dec_ref, intra_ref, qd_ref, kd_ref, cd_ref, *, chunk):
    direction = pl.program_id(0)
    head = pl.program_id(1)
    cf = float(chunk)
    dv = jnp.full((1, 1), dec_ref[direction, head], F32)
    lg = jnp.minimum(dv, 0.0) - jnp.log(1.0 + jnp.exp(-jnp.abs(dv)))
    rev = direction == 1
    ii = lax.broadcasted_iota(jnp.int32, (chunk, chunk), 0).astype(F32)
    jj = lax.broadcasted_iota(jnp.int32, (chunk, chunk), 1).astype(F32)
    dist = jnp.where(rev, jj - ii, ii - jj)
    intra_ref[0, 0] = jnp.where(dist >= 0.0, jnp.exp(lg * jnp.maximum(dist, 0.0)), 0.0)
    i1 = lax.broadcasted_iota(jnp.int32, (chunk, RET_DK), 0).astype(F32)
    qd_ref[0, 0] = jnp.exp(lg * jnp.where(rev, cf - i1, i1 + 1.0))
    kd_ref[0, 0] = jnp.exp(lg * jnp.where(rev, i1, cf - 1.0 - i1))
    cd_ref[0, 0] = jnp.exp(jnp.broadcast_to(lg, (8, RET_DK)) * cf)


def _retention_decays(dec_f, dec_b, chunk):
    hh, dk = RET_HEADS, RET_DK
    blk = lambda *shape: pl.BlockSpec((1, 1) + shape, lambda d, h: (d, h, 0, 0))
    return pl.pallas_call(
        functools.partial(_ret_tables_kernel, chunk=chunk),
        grid=(2, hh),
        in_specs=[pl.BlockSpec(memory_space=pltpu.SMEM)],
        out_specs=[blk(chunk, chunk), blk(chunk, dk), blk(chunk, dk), blk(8, dk)],
        out_shape=[jax.ShapeDtypeStruct((2, hh, chunk, chunk), F32),
                   jax.ShapeDtypeStruct((2, hh, chunk, dk), F32),
                   jax.ShapeDtypeStruct((2, hh, chunk, dk), F32),
                   jax.ShapeDtypeStruct((2, hh, 8, dk), F32)],
        compiler_params=_params(2, 32),
        name="retention_tables",
    )(jnp.stack([dec_f, dec_b]).astype(F32))


def _ret_kernel(intra_ref, qd_ref, kd_ref, cd_ref, qk_ref, v_ref, s0_ref, *rest, reverse, has_prev, chunk, n_chunks):
    if has_prev:
        prev_ref, o_ref, sfin_ref, s_scr = rest
    else:
        o_ref, sfin_ref, s_scr = rest
    first = pl.program_id(1) == 0
    dk = RET_DK
    order = range(n_chunks - 1, -1, -1) if reverse else range(n_chunks)
    for h in range(RET_HEADS):
        cols = slice(h * dk, (h + 1) * dk)
        kcols = slice((RET_HEADS + h) * dk, (RET_HEADS + h + 1) * dk)
        s = jnp.where(first, s0_ref[0, h], s_scr[h])
        for c in order:
            rows = slice(c * chunk, (c + 1) * chunk)
            q = qk_ref[0, rows, cols]
            k = qk_ref[0, rows, kcols]
            v = v_ref[0, rows, cols]
            att = lax.dot_general(q, k, NT_DIMS, preferred_element_type=F32) * intra_ref[0, h]
            qd = (q.astype(F32) * qd_ref[0, h]).astype(BF16)
            o = (jnp.dot(att.astype(BF16), v, preferred_element_type=F32)
                 + jnp.dot(qd, s.astype(BF16), preferred_element_type=F32))
            kd = (k.astype(F32) * kd_ref[0, h]).astype(BF16)
            s = s * cd_ref[0, h, 0:1, :] + lax.dot_general(kd, v, TN_DIMS, preferred_element_type=F32)
            if has_prev:
                o = o + prev_ref[0, rows, cols]
            o_ref[0, rows, cols] = o
        s_scr[h] = s
        sfin_ref[0, h] = s


def _retention(tables, qk, v_arr, s0, prev, reverse):
    b, length, _ = qk.shape
    hh, dk = RET_HEADS, RET_DK
    width = hh * dk
    c = tables[0].shape[-1]
    rb = min(RET_BLOCK, length)
    n = length // rb
    direction = 1 if reverse else 0
    pos = (lambda i: n - 1 - i) if reverse else (lambda i: i)
    tab = lambda *shape: pl.BlockSpec((1, hh) + shape, lambda bi, i: (direction, 0, 0, 0))
    in_specs = [tab(c, c), tab(c, dk), tab(c, dk), tab(8, dk),
                pl.BlockSpec((1, rb, 2 * width), lambda bi, i: (bi, pos(i), 0)),
                pl.BlockSpec((1, rb, width), lambda bi, i: (bi, pos(i), 0)),
                pl.BlockSpec((1, hh, dk, dk), lambda bi, i: (bi, 0, 0, 0))]
    args = [*tables, qk, v_arr, s0]
    if prev is not None:
        in_specs.append(pl.BlockSpec((1, rb, width), lambda bi, i: (bi, pos(i), 0)))
        args.append(prev)
    return pl.pallas_call(
        functools.partial(_ret_kernel, reverse=reverse, has_prev=prev is not None, chunk=c, n_chunks=rb // c),
        grid=(b, n),
        in_specs=in_specs,
        out_specs=[pl.BlockSpec((1, rb, width), lambda bi, i: (bi, pos(i), 0)),
                   pl.BlockSpec((1, hh, dk, dk), lambda bi, i: (bi, 0, 0, 0))],
        out_shape=[jax.ShapeDtypeStruct((b, length, width), F32),
                   jax.ShapeDtypeStruct((b, hh, dk, dk), F32)],
        scratch_shapes=[pltpu.VMEM((hh, dk, dk), F32)],
        compiler_params=_params(2, 48),
        name="retention_bwd" if reverse else "retention_fwd",
    )(*args)


def _diff_kernel(zero_ref, lam_ref, g_ref, q_ref, *rest, tq, n_lat, tile, lc, lam_init):
    if n_lat:
        kl_ref, vtl_ref = rest[:2]
        rest = rest[2:]
    kc_ref, vtc_ref, o_ref, qq_scr = rest[:4]
    s_buf, mt_buf, p_buf, al_buf = rest[4:8], rest[8:12], rest[12:14], rest[14:16]
    m_scr, acc_scr = rest[16:]
    n = n_lat + 1
    rows_of = lambda j: lc if j == n_lat else tile

    q_t = q_ref[0].astype(F32).T
    row = lax.broadcasted_iota(jnp.int32, q_t.shape, 0)
    qq_scr[:, 0:tq] = jnp.where(row < DIFF_HD, q_t, 0.0).astype(BF16)
    qq_scr[:, tq:2 * tq] = jnp.where(row >= DIFF_HD, q_t, 0.0).astype(BF16)
    m_scr[...] = jnp.full(m_scr.shape, NEG_BIG, F32)
    acc_scr[...] = jnp.zeros(acc_scr.shape, F32)

    def scores(j, slot):
        if isinstance(j, int) and j == n_lat:
            k = kc_ref[0]
        else:
            off = j * tile if isinstance(j, int) else pl.multiple_of(j * tile, tile)
            k = kl_ref[0, pl.ds(off, tile), :]
        s = jnp.dot(k, qq_scr[...], preferred_element_type=F32)
        mt_buf[slot][...] = jnp.max(s, axis=0, keepdims=True)
        s_buf[slot][0:k.shape[0], :] = s

    def softmax(s_slot, p_slot, rows):
        m_old = m_scr[...]
        m_new = jnp.maximum(m_old, mt_buf[s_slot][...])
        al_buf[p_slot][...] = jnp.exp2(m_old - m_new)
        m_scr[...] = m_new
        z = pl.multiple_of(zero_ref[0], 8)
        s_buf[s_slot][pl.ds(z, 8), :] = s_buf[s_slot][pl.ds(z, 8), :]
        ch = DIFF_SOFTMAX_ROWS
        for r0 in range(0, rows, ch):
            p_buf[p_slot][r0:r0 + ch, :] = jnp.exp2(s_buf[s_slot][r0:r0 + ch, :] - m_new).astype(BF16)

    def values(j, slot):
        if isinstance(j, int) and j == n_lat:
            vt = vtc_ref[0]
        else:
            vt = vtl_ref[j]
        acc_scr[...] = (acc_scr[...] * al_buf[slot][...]
                        + jnp.dot(vt, p_buf[slot][0:vt.shape[1], :], preferred_element_type=F32))

    def step(t, tmod):
        static = isinstance(t, int)
        if not (static and t == 0):
            values(t - 1, (tmod - 1) % 2)
        if not static or t + 2 < n:
            scores(t + 2, (tmod + 2) % 4)
        softmax(tmod, tmod % 2, rows_of(t) if static else tile)

    scores(0, 0)
    if n > 1:
        scores(1, 1)
    step(0, 0)
    unroll = DIFF_UNROLL
    n_loop = max(0, (n_lat - 3) // unroll)

    def body(i, carry):
        for u in range(unroll):
            step(1 + unroll * i + u, (1 + u) % 4)
        return carry

    if n_loop:
        lax.fori_loop(0, n_loop, body, 0)
    for t in range(1 + unroll * n_loop, n):
        step(t, t % 4)
    values(n - 1, (n - 1) % 2)

    acc = acc_scr[...]
    o = acc[0:DIFF_DV] * (1.0 / acc[DIFF_DV:DIFF_DV + 1])
    lam4 = lam_ref[...]
    s1 = jnp.sum(lam4[0:1] * lam4[1:2], axis=-1, keepdims=True)
    s2 = jnp.sum(lam4[2:3] * lam4[3:4], axis=-1, keepdims=True)
    lam = jnp.exp(s1) - jnp.exp(s2) + lam_init
    y = (o[:, :tq] - lam * o[:, tq:]).T
    ms = jnp.mean(y * y, axis=-1, keepdims=True)
    y = y * lax.rsqrt(ms + SUBLN_EPS) * g_ref[...] * (1.0 - lam_init)
    o_ref[0] = y.astype(o_ref.dtype)


def _diff_attention(lam4, subln_g, q_arr, lat, ctx, lam_init):
    b, lq, _ = q_arr.shape
    hh = DIFF_HEADS
    tq = min(DIFF_TQ, lq)
    qk_c, vt_c = ctx
    lc = qk_c.shape[1]
    assert vt_c.shape == (b, hh * DIFF_VT_ROWS, lc)
    in_specs = [pl.BlockSpec(memory_space=pltpu.SMEM),
                pl.BlockSpec((4, DIFF_HD), lambda bi, h, i: (0, 0)),
                pl.BlockSpec((1, DIFF_DV), lambda bi, h, i: (0, 0)),
                pl.BlockSpec((1, tq, LANES), lambda bi, h, i: (bi, i, h))]
    args = [jnp.zeros((1,), jnp.int32), lam4, subln_g, q_arr]
    n_lat, tile = 0, lc
    if lat is not None:
        qk_l, vt_l = lat
        length = qk_l.shape[1]
        tile = vt_l.shape[2]
        n_lat = length // tile
        assert lc <= tile
        in_specs += [pl.BlockSpec((1, length, LANES), lambda bi, h, i: (bi, 0, hh + h)),
                     pl.BlockSpec((n_lat, DIFF_VT_ROWS, tile), lambda bi, h, i: (bi, h, 0))]
        args += [qk_l, vt_l]
    in_specs += [pl.BlockSpec((1, lc, LANES), lambda bi, h, i: (bi, 0, hh + h)),
                 pl.BlockSpec((1, DIFF_VT_ROWS, lc), lambda bi, h, i: (bi, h, 0))]
    args += [qk_c, vt_c]
    return pl.pallas_call(
        functools.partial(_diff_kernel, tq=tq, n_lat=n_lat, tile=tile, lc=lc, lam_init=lam_init),
        grid=(b, hh, lq // tq),
        in_specs=in_specs,
        out_specs=pl.BlockSpec((1, tq, LANES), lambda bi, h, i: (bi, i, h)),
        out_shape=jax.ShapeDtypeStruct((b, lq, hh * DIFF_DV), BF16),
        scratch_shapes=[pltpu.VMEM((LANES, 2 * tq), BF16),
                        *[pltpu.VMEM((tile, 2 * tq), F32)] * 4,
                        *[pltpu.VMEM((1, 2 * tq), F32)] * 4,
                        *[pltpu.VMEM((tile, 2 * tq), BF16)] * 2,
                        *[pltpu.VMEM((1, 2 * tq), F32)] * 2,
                        pltpu.VMEM((1, 2 * tq), F32),
                        pltpu.VMEM((DIFF_VT_ROWS, 2 * tq), F32)],
        compiler_params=_params(3, 40),
        name="diff_attention" if lat is not None else "diff_attention_ctx",
    )(*args)


def _softmax_pv(s_list, v_list):
    m = s_list[0].max(axis=-1, keepdims=True)
    for s in s_list[1:]:
        m = jnp.maximum(m, s.max(axis=-1, keepdims=True))
    den = 0.0
    o = 0.0
    for s, v in zip(s_list, v_list):
        p = jnp.exp2(s - m)
        den = den + jnp.sum(p, axis=-1, keepdims=True)
        o = o + jnp.dot(p.astype(BF16), v, preferred_element_type=F32)
    return o * (1.0 / den)


def _na_kernel(zero_ref, q_ref, k_ref, vtl_ref, kc_ref, vtc_ref, bias_ref, o_ref, *scr, units, n_rows, lc):
    s_buf, mt_buf, p_buf = scr[0:3], scr[3:6], scr[6:8]
    w = GRID_W
    uq = NA_UNIT_ROWS * w
    nk = NA_UNION_ROWS * w
    step_r0 = pl.program_id(2) * (units * NA_UNIT_ROWS)

    def first_key_row(u):
        return jnp.clip(step_r0 + u * NA_UNIT_ROWS - NA_KH // 2, 0, n_rows - NA_UNION_ROWS)

    def scores(u, slot):
        start = first_key_row(u)
        cfg = (step_r0 + u * NA_UNIT_ROWS - start) // NA_UNIT_ROWS
        q_t = q_ref[0, u * uq:(u + 1) * uq, :].astype(F32).T
        row = lax.broadcasted_iota(jnp.int32, q_t.shape, 0)
        qq = jnp.concatenate([jnp.where(row < NA_HD, q_t, 0.0), jnp.where(row >= NA_HD, q_t, 0.0)],
                             axis=1).astype(BF16)
        koff = pl.multiple_of(start * w, uq)
        s_nb = jnp.dot(k_ref[0, pl.ds(koff, nk), :], qq, preferred_element_type=F32) + bias_ref[0, cfg]
        s_cx = jnp.dot(kc_ref[0], qq, preferred_element_type=F32)
        mt_buf[slot][...] = jnp.maximum(jnp.max(s_nb, axis=0, keepdims=True), jnp.max(s_cx, axis=0, keepdims=True))
        s_buf[slot][0:nk, :] = s_nb
        s_buf[slot][nk:nk + lc, :] = s_cx

    def softmax(s_slot, p_slot):
        m = mt_buf[s_slot][...]
        z = pl.multiple_of(zero_ref[0], 8)
        s_buf[s_slot][pl.ds(z, 8), :] = s_buf[s_slot][pl.ds(z, 8), :]
        ch = NA_SOFTMAX_ROWS
        for r0 in range(0, nk + lc, ch):
            p_buf[p_slot][r0:r0 + ch, :] = jnp.exp2(s_buf[s_slot][r0:r0 + ch, :] - m).astype(BF16)

    def values(u, p_slot):
        pc0 = first_key_row(u) // NA_UNIT_ROWS
        acc = jnp.dot(vtc_ref[0], p_buf[p_slot][nk:nk + lc, :], preferred_element_type=F32)
        for i in range(NA_UNION_ROWS // NA_UNIT_ROWS):
            acc = acc + jnp.dot(vtl_ref[pc0 + i], p_buf[p_slot][i * uq:(i + 1) * uq, :],
                                preferred_element_type=F32)
        o = acc[0:LANES] * (1.0 / acc[LANES:LANES + 1])
        y = jnp.concatenate([o[0:NA_HD, 0:uq], o[NA_HD:LANES, uq:2 * uq]], axis=0)
        o_ref[0, u * uq:(u + 1) * uq, :] = y.T.astype(o_ref.dtype)

    scores(0, 0)
    if units > 1:
        scores(1, 1)
    for u in range(units):
        if u >= 1:
            values(u - 1, (u - 1) % 2)
        if u + 2 < units:
            scores(u + 2, (u + 2) % 3)
        softmax(u % 3, u % 2)
    values(units - 1, (units - 1) % 2)


def _na_attention(qk_l, vt_l, qk_c, vt_c, bias):
    b, length, _ = qk_l.shape
    lc = qk_c.shape[1]
    n_rows = length // GRID_W
    uq = NA_UNIT_ROWS * GRID_W
    nk = NA_UNION_ROWS * GRID_W
    assert n_rows >= NA_UNION_ROWS and n_rows % NA_UNIT_ROWS == 0 and vt_l.shape[2] == uq
    units = min(NA_UNITS, n_rows // NA_UNIT_ROWS)
    npair = NA_HEADS // 2
    n_pieces = length // uq
    tq = units * uq
    return pl.pallas_call(
        functools.partial(_na_kernel, units=units, n_rows=n_rows, lc=lc),
        grid=(b, npair, length // tq),
        in_specs=[pl.BlockSpec(memory_space=pltpu.SMEM),
                  pl.BlockSpec((1, tq, LANES), lambda bi, p, i: (bi, i, npair + p)),
                  pl.BlockSpec((1, length, LANES), lambda bi, p, i: (bi, 0, 2 * npair + p)),
                  pl.BlockSpec((n_pieces, DIFF_VT_ROWS, uq), lambda bi, p, i: (bi, p, 0)),
                  pl.BlockSpec((1, lc, LANES), lambda bi, p, i: (bi, 0, 2 * npair + p)),
                  pl.BlockSpec((1, DIFF_VT_ROWS, lc), lambda bi, p, i: (bi, p, 0)),
                  pl.BlockSpec((1, 3, nk, 2 * uq), lambda bi, p, i: (p, 0, 0, 0))],
        out_specs=pl.BlockSpec((1, tq, LANES), lambda bi, p, i: (bi, i, p)),
        out_shape=jax.ShapeDtypeStruct((b, length, NA_HEADS * NA_HD), BF16),
        scratch_shapes=[*[pltpu.VMEM((nk + lc, 2 * uq), F32)] * 3,
                        *[pltpu.VMEM((1, 2 * uq), F32)] * 3,
                        *[pltpu.VMEM((nk + lc, 2 * uq), BF16)] * 2],
        compiler_params=_params(3, 48),
        name="neighbourhood_attention",
    )(jnp.zeros((1,), jnp.int32), qk_l, qk_l, vt_l, qk_c, vt_c, bias)


def _ctx_attn_kernel(q_ref, k_ref, v_ref, o_ref):
    q = q_ref[0].astype(F32)
    k = k_ref[0]
    v = v_ref[0]
    lane = lax.broadcasted_iota(jnp.int32, q.shape, 1)
    outs = []
    for hh in range(2):
        sel = (lane < NA_HD) if hh == 0 else (lane >= NA_HD)
        qm = jnp.where(sel, q, 0.0).astype(BF16)
        s = lax.dot_general(qm, k, NT_DIMS, preferred_element_type=F32)
        outs.append(_softmax_pv([s], [v]))
    o_ref[0] = jnp.where(lane < NA_HD, outs[0], outs[1]).astype(o_ref.dtype)


def _ctx_attention(qkv_c):
    b, lc, _ = qkv_c.shape
    npair = NA_HEADS // 2
    return pl.pallas_call(
        _ctx_attn_kernel,
        grid=(b, npair),
        in_specs=[pl.BlockSpec((1, lc, LANES), lambda bi, p: (bi, 0, npair + p)),
                  pl.BlockSpec((1, lc, LANES), lambda bi, p: (bi, 0, 2 * npair + p)),
                  pl.BlockSpec((1, lc, LANES), lambda bi, p: (bi, 0, 3 * npair + p))],
        out_specs=pl.BlockSpec((1, lc, LANES), lambda bi, p: (bi, 0, p)),
        out_shape=jax.ShapeDtypeStruct((b, lc, NA_HEADS * NA_HD), BF16),
        compiler_params=_params(2, 32),
        name="ctx_attention",
    )(qkv_c, qkv_c, qkv_c)


def _merge_kernel(oret_ref, gates_ref, odiff_ref, ona_ref, x_ref, g1_ref, lng_ref, lnb_ref,
                  wpa_ref, wpb_ref, wpc_ref, wo_ref, o_ref, *, alpha):
    d = D_MODEL
    ya_parts = []
    for hh in range(RET_HEADS):
        a = hh * RET_DK
        seg = _layer_norm_rows(oret_ref[:, a:a + RET_DK])
        g = gates_ref[:, a:a + RET_DK]
        ya_parts.append((g * _sigmoid(g) * seg).astype(BF16))
    ya_in = jnp.concatenate(ya_parts, axis=1)
    y_a = jnp.dot(ya_in, wpa_ref[...], preferred_element_type=F32)
    y_b = jnp.dot(odiff_ref[...], wpb_ref[...], preferred_element_type=F32)
    y_c = jnp.dot(ona_ref[...], wpc_ref[...], preferred_element_type=F32)
    m = (_sigmoid(gates_ref[:, d:2 * d]) * y_a + _sigmoid(gates_ref[:, 2 * d:3 * d]) * y_b
         + _sigmoid(gates_ref[:, 3 * d:4 * d]) * y_c)
    y = jnp.dot(m.astype(BF16), wo_ref[...], preferred_element_type=F32)
    z = alpha * x_ref[...] + g1_ref[0] * y
    o_ref[...] = _layer_norm_rows(z) * lng_ref[...] + lnb_ref[...]


def _merge(o_ret, gates, o_diff, o_na, x, g1, ln_g, ln_b, w_pa, w_pb, w_pc, w_o, mod_row, alpha):
    r, d = x.shape
    tm = min(MERGE_TILE, r)
    row = lambda i: (i, 0)
    const = lambda i: (0, 0)
    wspec = pl.BlockSpec((d, d), const, pipeline_mode=pl.Buffered(1))
    return pl.pallas_call(
        functools.partial(_merge_kernel, alpha=alpha),
        grid=(r // tm,),
        in_specs=[pl.BlockSpec((tm, d), row), pl.BlockSpec((tm, 4 * d), row), pl.BlockSpec((tm, d), row),
                  pl.BlockSpec((tm, d), row), pl.BlockSpec((tm, d), row),
                  pl.BlockSpec((1, 1, d), lambda i: (mod_row(i, tm), 0, 0)),
                  pl.BlockSpec((1, d), const), pl.BlockSpec((1, d), const),
                  wspec, wspec, wspec, wspec],
        out_specs=pl.BlockSpec((tm, d), row),
        out_shape=jax.ShapeDtypeStruct((r, d), F32),
        compiler_params=_params(1, 52),
        name="merge_out_proj",
    )(o_ret, gates, o_diff, o_na, x, g1, ln_g, ln_b, w_pa, w_pb, w_pc, w_o)


def _ffn_kernel(x_ref, sh_ref, sc_ref, g2_ref, lng_ref, lnb_ref, w1_ref, w2_ref, o_ref, *, alpha, ff_chunk):
    x = x_ref[...]
    h = (_layer_norm_rows(x) * (1.0 + sc_ref[0]) + sh_ref[0]).astype(BF16)
    acc = jnp.zeros(x.shape, F32)
    for cidx in range(D_FF // ff_chunk):
        a = cidx * ff_chunk
        u = jnp.maximum(jnp.dot(h, w1_ref[:, a:a + ff_chunk], preferred_element_type=F32), 0.0)
        acc = acc + jnp.dot((u * u).astype(BF16), w2_ref[a:a + ff_chunk, :], preferred_element_type=F32)
    z = alpha * x + g2_ref[0] * acc
    o_ref[...] = _layer_norm_rows(z) * lng_ref[...] + lnb_ref[...]


def _ffn(x, shift, scale, g2, ln_g, ln_b, w1, w2, mod_row, alpha):
    r, d = x.shape
    tm = min(FFN_TILE, r)
    row = lambda i: (i, 0)
    const = lambda i: (0, 0)
    mspec = pl.BlockSpec((1, 1, d), lambda i: (mod_row(i, tm), 0, 0))
    return pl.pallas_call(
        functools.partial(_ffn_kernel, alpha=alpha, ff_chunk=1024),
        grid=(r // tm,),
        in_specs=[pl.BlockSpec((tm, d), row), mspec, mspec, mspec,
                  pl.BlockSpec((1, d), const), pl.BlockSpec((1, d), const),
                  pl.BlockSpec((d, D_FF), const), pl.BlockSpec((D_FF, d), const)],
        out_specs=pl.BlockSpec((tm, d), row),
        out_shape=jax.ShapeDtypeStruct((r, d), F32),
        compiler_params=_params(1, 56),
        name="ffn",
    )(x, shift, scale, g2, ln_g, ln_b, w1, w2)


def _retention_tables(length):
    t = jnp.arange(length, dtype=F32)
    inv = ROPE_BASE ** (-jnp.linspace(0.0, 1.0, RET_DK // 2, dtype=F32))
    ang = t[:, None] * inv[None, :]
    return jnp.cos(ang), jnp.sin(ang)


def _axial_tables(length):
    t = jnp.arange(length)
    row = (t // GRID_W).astype(F32)
    col = (t % GRID_W).astype(F32)
    half = DIFF_HD // 2
    inv = ROPE_BASE ** (-jnp.arange(0, half, 2, dtype=F32) / half)
    ang_r = row[:, None] * inv[None, :]
    ang_c = col[:, None] * inv[None, :]
    zero = jnp.zeros_like(ang_r)
    unit_ang = jnp.concatenate([ang_r, ang_r, ang_c, ang_c], axis=-1)
    cos = jnp.cos(unit_ang)
    sin_r, sin_c = jnp.sin(ang_r), jnp.sin(ang_c)
    sin_lo = jnp.concatenate([-sin_r, zero, -sin_c, zero], axis=-1)
    sin_hi = jnp.concatenate([zero, sin_r, zero, sin_c], axis=-1)
    rep = LANES // DIFF_HD
    return tuple(jnp.tile(a, (1, rep)) for a in (cos, sin_lo, sin_hi))


def _na_bias_table(rpb):
    nh, nr, _ = rpb.shape
    w = GRID_W
    wid = 2 * w - 1
    p = jnp.pad(rpb.astype(F32) * math.log2(math.e), ((0, 0), (0, 0), (w - NA_KW, w - NA_KW)))
    y = p[..., ::-1]
    a = jnp.broadcast_to(y[:, :, None, :], (nh, nr, w, wid))
    a = jnp.pad(a, ((0, 0), (0, 0), (0, 0), (0, 1))).reshape(nh, nr, w * (wid + 1))
    a = a[:, :, :w * wid].reshape(nh, nr, w, wid)
    m = a[..., w - 1:]
    j = jnp.arange(w)
    cs = jnp.clip(j - NA_KW // 2, 0, w - NA_KW)
    col_mask = (j[:, None] >= cs[None, :]) & (j[:, None] < cs[None, :] + NA_KW)
    m = jnp.where(col_mask[None, None], m, NEG_BIG)
    un, g_rows = NA_UNION_ROWS, NA_UNIT_ROWS
    mp = jnp.pad(m, ((0, 0), (un, un), (0, 0), (0, 0)), constant_values=NEG_BIG)
    mp = mp.reshape(nh // 2, 2, nr + 2 * un, w, w)
    a = jnp.arange(un)
    per_cfg = []
    for cfg in range(3):
        per_col = []
        for hh in range(2):
            for g in range(g_rows):
                q_row = cfg * g_rows + g
                win0 = (0, g, un - NA_KH)[cfg]
                lo = NA_KH - 1 - q_row + un
                in_win = (a >= win0) & (a < win0 + NA_KH)
                per_col.append(jnp.where(in_win[None, :, None, None], mp[:, hh, lo:lo + un], NEG_BIG))
        per_cfg.append(jnp.concatenate(per_col, axis=-1))
    return jnp.stack(per_cfg, axis=1).reshape(nh // 2, 3, un * w, 2 * g_rows * w)


def _column_scale():
    d = D_MODEL
    cs = jnp.ones((IN_WIDTH,), F32)
    cs = cs.at[d:2 * d].set(RET_DK ** -0.5)
    cs = cs.at[4 * d:5 * d].set(DIFF_HD ** -0.5 * math.log2(math.e))
    cs = cs.at[7 * d:8 * d].set(NA_HD ** -0.5 * math.log2(math.e))
    return cs.reshape(1, IN_WIDTH)


def _project_all(h, w, colscale, tm, rope, ret_tables, ax_tables, tiles_per_batch, tk_out, na_tk_out, plain_na_v):
    if rope:
        ret_qk = _proj(h, w, colscale, tm, 0, 1, 2, BF16, "rope_ret", ret_tables, tiles_per_batch)
        diff_qk = _proj(h, w, colscale, tm, 4, 5, 2, BF16, "rope_diff", ax_tables, tiles_per_batch)
    else:
        ret_qk = _proj(h, w, colscale, tm, 0, 1, 2, BF16)
        diff_qk = _proj(h, w, colscale, tm, 4, 5, 2, BF16)
    v_na = _proj(h, w, colscale, tm, 2, 7, 4 if plain_na_v else 3, BF16)
    gates = _proj(h, w, colscale, tm, 3, 10, 4, F32)
    diff_vt = _proj(h, w, colscale, tm, 6, 7, 1, BF16, tk_out=tk_out)
    na_vt = _proj(h, w, colscale, tm, 9, 10, 1, BF16, tk_out=na_tk_out)
    return ret_qk, diff_qk, v_na, gates, diff_vt, na_vt


def kernel(x, c, ctx, c_ctx, w_mod, b_mod, w_in, ret_decay_f, ret_decay_b, diff_lq1, diff_lk1, diff_lq2, diff_lk2,
           diff_subln_g, na_rpb, w_pa, w_pb, w_pc, w_o, ln1_g, ln1_b, w_ff1, w_ff2, ln2_g, ln2_b):
    b, length, d = x.shape
    lc = ctx.shape[1]
    depth = w_mod.shape[0]
    assert d == D_MODEL and b < 8 and length % min(ROW_TILE, length) == 0 and length % GRID_W == 0
    assert lc % RET_CHUNK == 0 or lc < RET_CHUNK
    alpha = (2.0 * depth) ** 0.25

    cvec = jnp.zeros((8, d), F32).at[:b].set(c).at[b].set(c_ctx)
    mods = _modulation(cvec, w_mod, b_mod)

    lat_row = lambda i, tm: (i * tm) // length
    ctx_row = lambda i, tm: b
    colscale = _column_scale()
    ret_tables = _retention_tables(length)
    ax_tables = _axial_tables(length)

    x_l = x.reshape(b * length, d)
    x_c = ctx.reshape(b * lc, d)
    zero_state = jnp.zeros((b, RET_HEADS, RET_DK, RET_DK), F32)
    for l in range(depth):
        with_ctx_out = l < depth - 1
        mod = [mods[l, :, k * d:(k + 1) * d].reshape(8, 1, d) for k in range(6)]
        sh1, sc1, g1, sh2, sc2, g2 = mod
        w = (w_in, l)
        wpa, wpb, wpc, wo = (a[l].astype(BF16) for a in (w_pa, w_pb, w_pc, w_o))
        w1, w2 = w_ff1[l].astype(BF16), w_ff2[l].astype(BF16)
        lng1, lnb1 = ln1_g[l].reshape(1, d), ln1_b[l].reshape(1, d)
        lng2, lnb2 = ln2_g[l].reshape(1, d), ln2_b[l].reshape(1, d)
        lam_init = 0.8 - 0.6 * math.exp(-0.3 * l)
        lam4 = jnp.stack([diff_lq1[l], diff_lk1[l], diff_lq2[l], diff_lk2[l]]).astype(F32)
        subln = diff_subln_g[l].reshape(1, DIFF_DV).astype(F32)
        bias = _na_bias_table(na_rpb[l])

        h_l = _ln_mod(x_l, sh1, sc1, lat_row)
        h_c = _ln_mod(x_c, sh1, sc1, ctx_row)
        tm_l = min(ROW_TILE, length)
        rqk_l, dqk_l, vna_l, gates_l, dvt_l, nvt_l = _project_all(
            h_l, w, colscale, tm_l, True, ret_tables, ax_tables, length // tm_l, min(DIFF_TILE, length),
            NA_UNIT_ROWS * GRID_W, False)
        rqk_c, dqk_c, vna_c, gates_c, dvt_c, nvt_c = _project_all(
            h_c, w, colscale, min(ROW_TILE, b * lc), False, (), (), 1, lc, lc, True)
        sh3 = lambda a, n: a.reshape(b, n, a.shape[-1])

        decays = _retention_decays(ret_decay_f[l], ret_decay_b[l], min(RET_CHUNK, lc))
        rqk_l3, vna_l3, rqk_c3, vna_c3 = sh3(rqk_l, length), sh3(vna_l, length), sh3(rqk_c, lc), sh3(vna_c, lc)
        o_cf, s_cf = _retention(decays, rqk_c3, vna_c3, zero_state, None, False)
        o_rc, s_cb = _retention(decays, rqk_c3, vna_c3, zero_state, o_cf, True)
        o_lf, _ = _retention(decays, rqk_l3, vna_l3, s_cf, None, False)
        o_rl, _ = _retention(decays, rqk_l3, vna_l3, s_cb, o_lf, True)

        dqk_l3, dqk_c3 = sh3(dqk_l, length), sh3(dqk_c, lc)
        o_dl = _diff_attention(lam4, subln, dqk_l3, (dqk_l3, dvt_l), (dqk_c3, dvt_c), lam_init)
        o_nl = _na_attention(vna_l3, nvt_l, vna_c3, nvt_c, bias)

        x_l = _merge(o_rl.reshape(b * length, d), gates_l, o_dl.reshape(b * length, d), o_nl.reshape(b * length, d),
                     x_l, g1, lng1, lnb1, wpa, wpb, wpc, wo, lat_row, alpha)
        x_l = _ffn(x_l, sh2, sc2, g2, lng2, lnb2, w1, w2, lat_row, alpha)
        if with_ctx_out:
            o_dc = _diff_attention(lam4, subln, dqk_c3, None, (dqk_c3, dvt_c), lam_init)
            o_nc = _ctx_attention(vna_c3)
            x_c = _merge(o_rc.reshape(b * lc, d), gates_c, o_dc.reshape(b * lc, d), o_nc.reshape(b * lc, d),
                         x_c, g1, lng1, lnb1, wpa, wpb, wpc, wo, ctx_row, alpha)
            x_c = _ffn(x_c, sh2, sc2, g2, lng2, lnb2, w1, w2, ctx_row, alpha)
    return x_l.reshape(b, length, d)
```

```python
import functools
import math

import jax
import jax.numpy as jnp
from jax import lax
from jax.experimental import pallas as pl
from jax.experimental.pallas import tpu as pltpu

F32 = jnp.float32
BF16 = jnp.bfloat16

D_MODEL = 1024
GRID_W = 64
RET_HEADS, RET_DK = 4, 256
DIFF_HEADS, DIFF_HD, DIFF_DV = 8, 64, 128
NA_HEADS, NA_HD, NA_KH, NA_KW = 16, 64, 8, 16
D_FF = 4 * D_MODEL
ROPE_BASE = 10000.0
LN_EPS = 1e-6
SUBLN_EPS = 1e-5
IN_WIDTH = 13 * D_MODEL
NEG_BIG = -1e30

LANES = 128
VMEM_LIMIT_CAP = 56 << 20

ROW_TILE = 2048
PROJ_TN = 1024
LN_TILE = 512
MERGE_TILE = 512
FFN_TILE = 512
RET_CHUNK = 256
RET_BLOCK = 512
DIFF_TQ = 256
DIFF_TILE = 512
DIFF_VT_ROWS = DIFF_DV + 16
DIFF_SOFTMAX_ROWS = 64
NA_UNIT_ROWS = 4
NA_UNION_ROWS = 12
NA_UNITS = 8
NA_SOFTMAX_ROWS = 64

NT_DIMS = (((1,), (1,)), ((), ()))
TN_DIMS = (((0,), (0,)), ((), ()))


def _params(n_axes, vmem_mb):
    return pltpu.CompilerParams(dimension_semantics=("arbitrary",) * n_axes,
                                vmem_limit_bytes=min(vmem_mb << 20, VMEM_LIMIT_CAP))


def _sigmoid(x):
    return 1.0 / (1.0 + jnp.exp(-x))


def _layer_norm_rows(x):
    mu = jnp.mean(x, axis=-1, keepdims=True)
    xc = x - mu
    var = jnp.mean(xc * xc, axis=-1, keepdims=True)
    return xc * lax.rsqrt(var + LN_EPS)


def _mod_kernel(c_ref, w_ref, b_ref, o_ref):
    c = c_ref[...]
    a = (c * _sigmoid(c)).astype(BF16)
    o_ref[0] = jnp.dot(a, w_ref[0].astype(BF16), preferred_element_type=F32) + b_ref[0]


def _modulation(cvec, w_mod, b_mod):
    depth = w_mod.shape[0]
    d = D_MODEL
    return pl.pallas_call(
        _mod_kernel,
        grid=(depth, 6),
        in_specs=[pl.BlockSpec((8, d), lambda l, j: (0, 0)),
                  pl.BlockSpec((1, d, d), lambda l, j: (l, 0, j)),
                  pl.BlockSpec((1, 1, d), lambda l, j: (l, 0, j))],
        out_specs=pl.BlockSpec((1, 8, d), lambda l, j: (l, 0, j)),
        out_shape=jax.ShapeDtypeStruct((depth, 8, 6 * d), F32),
        compiler_params=_params(2, 32),
        name="modulation",
    )(cvec, w_mod, b_mod.reshape(depth, 1, 6 * d))


def _ln_mod_kernel(x_ref, sh_ref, sc_ref, o_ref):
    y = _layer_norm_rows(x_ref[...])
    o_ref[...] = (y * (1.0 + sc_ref[0]) + sh_ref[0]).astype(o_ref.dtype)


def _ln_mod(x, shift, scale, mod_row):
    r, d = x.shape
    tm = min(LN_TILE, r)
    return pl.pallas_call(
        _ln_mod_kernel,
        grid=(r // tm,),
        in_specs=[pl.BlockSpec((tm, d), lambda i: (i, 0)),
                  pl.BlockSpec((1, 1, d), lambda i: (mod_row(i, tm), 0, 0)),
                  pl.BlockSpec((1, 1, d), lambda i: (mod_row(i, tm), 0, 0))],
        out_specs=pl.BlockSpec((tm, d), lambda i: (i, 0)),
        out_shape=jax.ShapeDtypeStruct((r, d), BF16),
        compiler_params=_params(1, 32),
        name="ln_modulate",
    )(x, shift, scale)


def _proj_kernel(h_ref, w_ref, cs_ref, *rest, mode, transposed, tn, tk_out):
    o_ref = rest[-1]
    acc = jnp.dot(h_ref[...], w_ref[0].astype(BF16), preferred_element_type=F32) * cs_ref[...]
    if mode == "rope_ret":
        cos = rest[0][...]
        sin = rest[1][...]
        half = RET_DK // 2
        for hh in range(tn // RET_DK):
            a = hh * RET_DK
            x1 = acc[:, a:a + half]
            x2 = acc[:, a + half:a + RET_DK]
            o_ref[:, a:a + half] = (x1 * cos - x2 * sin).astype(o_ref.dtype)
            o_ref[:, a + half:a + RET_DK] = (x1 * sin + x2 * cos).astype(o_ref.dtype)
    elif mode == "rope_diff":
        cos = rest[0][...]
        sin_lo = rest[1][...]
        sin_hi = rest[2][...]
        q16 = DIFF_HD // 4
        for g in range(tn // LANES):
            x = acc[:, g * LANES:(g + 1) * LANES]
            y = x * cos + pltpu.roll(x, LANES - q16, 1) * sin_lo + pltpu.roll(x, q16, 1) * sin_hi
            o_ref[:, g * LANES:(g + 1) * LANES] = y.astype(o_ref.dtype)
    elif transposed:
        acc_t = acc.T
        ones = jnp.ones((DIFF_VT_ROWS - DIFF_DV, tk_out), o_ref.dtype)
        for a in range(o_ref.shape[0]):
            for hh in range(tn // DIFF_DV):
                r0 = hh * DIFF_VT_ROWS
                o_ref[a, r0:r0 + DIFF_DV, :] = acc_t[hh * DIFF_DV:(hh + 1) * DIFF_DV,
                                                     a * tk_out:(a + 1) * tk_out].astype(o_ref.dtype)
                o_ref[a, r0 + DIFF_DV:r0 + DIFF_VT_ROWS, :] = ones
    else:
        o_ref[...] = acc.astype(o_ref.dtype)


def _proj(h, w, colscale, tm, first, rest0, ntiles, out_dtype, mode="plain", tables=(), tiles_per_batch=1,
          tk_out=None):
    w_all, layer = w
    r, d = h.shape
    tn = PROJ_TN
    ncols = ntiles * tn
    assert r % tm == 0
    transposed = tk_out is not None
    wcol = lambda j: jnp.where(j == 0, first, rest0 + j - 1)
    in_specs = [pl.BlockSpec((tm, d), lambda i, j: (i, 0)),
                pl.BlockSpec((1, d, tn), lambda i, j: (layer, 0, wcol(j))),
                pl.BlockSpec((1, tn), lambda i, j: (0, wcol(j)))]
    for _ in tables:
        in_specs.append(pl.BlockSpec((tm, LANES), lambda i, j: (i % tiles_per_batch, 0)))
    if transposed:
        assert tm % tk_out == 0
        na = tm // tk_out
        rows_out = tn // DIFF_DV * DIFF_VT_ROWS
        out_specs = pl.BlockSpec((na, rows_out, tk_out), lambda i, j: (i, j, 0))
        out_shape = jax.ShapeDtypeStruct((r // tk_out, ncols // DIFF_DV * DIFF_VT_ROWS, tk_out), out_dtype)
    else:
        out_specs = pl.BlockSpec((tm, tn), lambda i, j: (i, j))
        out_shape = jax.ShapeDtypeStruct((r, ncols), out_dtype)
    return pl.pallas_call(
        functools.partial(_proj_kernel, mode=mode, transposed=transposed, tn=tn, tk_out=tk_out),
        grid=(r // tm, ntiles),
        in_specs=in_specs,
        out_specs=out_specs,
        out_shape=out_shape,
        compiler_params=_params(2, 52),
        name="in_proj_" + mode + ("_t" if transposed else ""),
    )(h, w_all, colscale, *tables)


def _ret_tables_kernel(dec_ref, intra_ref, qd_ref, kd_ref, cd_ref, *, chunk):
    direction = pl.program_id(0)
    head = pl.program_id(1)
    cf = float(chunk)
    dv = jnp.full((1, 1), dec_ref[direction, head], F32)
    lg = jnp.minimum(dv, 0.0) - jnp.log(1.0 + jnp.exp(-jnp.abs(dv)))
    rev = direction == 1
    ii = lax.broadcasted_iota(jnp.int32, (chunk, chunk), 0).astype(F32)
    jj = lax.broadcasted_iota(jnp.int32, (chunk, chunk), 1).astype(F32)
    dist = jnp.where(rev, jj - ii, ii - jj)
    intra_ref[0, 0] = jnp.where(dist >= 0.0, jnp.exp(lg * jnp.maximum(dist, 0.0)), 0.0)
    i1 = lax.broadcasted_iota(jnp.int32, (chunk, RET_DK), 0).astype(F32)
    qd_ref[0, 0] = jnp.exp(lg * jnp.where(rev, cf - i1, i1 + 1.0))
    kd_ref[0, 0] = jnp.exp(lg * jnp.where(rev, i1, cf - 1.0 - i1))
    cd_ref[0, 0] = jnp.exp(jnp.broadcast_to(lg, (8, RET_DK)) * cf)


def _retention_decays(dec_f, dec_b, chunk):
    hh, dk = RET_HEADS, RET_DK
    blk = lambda *shape: pl.BlockSpec((1, 1) + shape, lambda d, h: (d, h, 0, 0))
    return pl.pallas_call(
        functools.partial(_ret_tables_kernel, chunk=chunk),
        grid=(2, hh),
        in_specs=[pl.BlockSpec(memory_space=pltpu.SMEM)],
        out_specs=[blk(chunk, chunk), blk(chunk, dk), blk(chunk, dk), blk(8, dk)],
        out_shape=[jax.ShapeDtypeStruct((2, hh, chunk, chunk), F32),
                   jax.ShapeDtypeStruct((2, hh, chunk, dk), F32),
                   jax.ShapeDtypeStruct((2, hh, chunk, dk), F32),
                   jax.ShapeDtypeStruct((2, hh, 8, dk), F32)],
        compiler_params=_params(2, 32),
        name="retention_tables",
    )(jnp.stack([dec_f, dec_b]).astype(F32))


def _ret_kernel(intra_ref, qd_ref, kd_ref, cd_ref, qk_ref, v_ref, s0_ref, *rest, reverse, has_prev, chunk, n_chunks):
    if has_prev:
        prev_ref, o_ref, sfin_ref, s_scr = rest
    else:
        o_ref, sfin_ref, s_scr = rest
    first = pl.program_id(1) == 0
    dk = RET_DK
    order = range(n_chunks - 1, -1, -1) if reverse else range(n_chunks)
    for h in range(RET_HEADS):
        cols = slice(h * dk, (h + 1) * dk)
        kcols = slice((RET_HEADS + h) * dk, (RET_HEADS + h + 1) * dk)
        s = jnp.where(first, s0_ref[0, h], s_scr[h])
        for c in order:
            rows = slice(c * chunk, (c + 1) * chunk)
            q = qk_ref[0, rows, cols]
            k = qk_ref[0, rows, kcols]
            v = v_ref[0, rows, cols]
            att = lax.dot_general(q, k, NT_DIMS, preferred_element_type=F32) * intra_ref[0, h]
            qd = (q.astype(F32) * qd_ref[0, h]).astype(BF16)
            o = (jnp.dot(att.astype(BF16), v, preferred_element_type=F32)
                 + jnp.dot(qd, s.astype(BF16), preferred_element_type=F32))
            kd = (k.astype(F32) * kd_ref[0, h]).astype(BF16)
            s = s * cd_ref[0, h, 0:1, :] + lax.dot_general(kd, v, TN_DIMS, preferred_element_type=F32)
            if has_prev:
                o = o + prev_ref[0, rows, cols]
            o_ref[0, rows, cols] = o
        s_scr[h] = s
        sfin_ref[0, h] = s


def _retention(tables, qk, v_arr, s0, prev, reverse):
    b, length, _ = qk.shape
    hh, dk = RET_HEADS, RET_DK
    width = hh * dk
    c = tables[0].shape[-1]
    rb = min(RET_BLOCK, length)
    n = length // rb
    direction = 1 if reverse else 0
    pos = (lambda i: n - 1 - i) if reverse else (lambda i: i)
    tab = lambda *shape: pl.BlockSpec((1, hh) + shape, lambda bi, i: (direction, 0, 0, 0))
    in_specs = [tab(c, c), tab(c, dk), tab(c, dk), tab(8, dk),
                pl.BlockSpec((1, rb, 2 * width), lambda bi, i: (bi, pos(i), 0)),
                pl.BlockSpec((1, rb, width), lambda bi, i: (bi, pos(i), 0)),
                pl.BlockSpec((1, hh, dk, dk), lambda bi, i: (bi, 0, 0, 0))]
    args = [*tables, qk, v_arr, s0]
    if prev is not None:
        in_specs.append(pl.BlockSpec((1, rb, width), lambda bi, i: (bi, pos(i), 0)))
        args.append(prev)
    return pl.pallas_call(
        functools.partial(_ret_kernel, reverse=reverse, has_prev=prev is not None, chunk=c, n_chunks=rb // c),
        grid=(b, n),
        in_specs=in_specs,
        out_specs=[pl.BlockSpec((1, rb, width), lambda bi, i: (bi, pos(i), 0)),
                   pl.BlockSpec((1, hh, dk, dk), lambda bi, i: (bi, 0, 0, 0))],
        out_shape=[jax.ShapeDtypeStruct((b, length, width), F32),
                   jax.ShapeDtypeStruct((b, hh, dk, dk), F32)],
        scratch_shapes=[pltpu.VMEM((hh, dk, dk), F32)],
        compiler_params=_params(2, 48),
        name="retention_bwd" if reverse else "retention_fwd",
    )(*args)


def _diff_kernel(zero_ref, lam_ref, g_ref, q_ref, *rest, tq, n_lat, tile, lc, lam_init):
    if n_lat:
        kl_ref, vtl_ref = rest[:2]
        rest = rest[2:]
    kc_ref, vtc_ref, o_ref, qq_scr = rest[:4]
    s_buf, mt_buf, p_buf, al_buf = rest[4:8], rest[8:12], rest[12:14], rest[14:16]
    m_scr, acc_scr = rest[16:]
    n = n_lat + 1
    rows_of = lambda j: lc if j == n_lat else tile

    q = q_ref[0].astype(F32)
    lane = lax.broadcasted_iota(jnp.int32, q.shape, 1)
    qq_scr[0:tq, :] = jnp.where(lane < DIFF_HD, q, 0.0).astype(BF16)
    qq_scr[tq:2 * tq, :] = jnp.where(lane >= DIFF_HD, q, 0.0).astype(BF16)
    m_scr[...] = jnp.full(m_scr.shape, NEG_BIG, F32)

    @pl.when(pl.program_id(0) == 0)
    def _no_previous_block():
        acc_scr[...] = jnp.ones(acc_scr.shape, F32)

    def scores(j, slot):
        k = kc_ref[0] if j == n_lat else kl_ref[0, j * tile:(j + 1) * tile, :]
        s = lax.dot_general(k, qq_scr[...], NT_DIMS, preferred_element_type=F32)
        mt_buf[slot][...] = jnp.max(s, axis=0, keepdims=True)
        s_buf[slot][0:k.shape[0], :] = s

    def softmax(s_slot, p_slot, rows):
        m_old = m_scr[...]
        m_new = jnp.maximum(m_old, mt_buf[s_slot][...])
        al_buf[p_slot][...] = jnp.exp2(m_old - m_new)
        m_scr[...] = m_new
        z = pl.multiple_of(zero_ref[0], 8)
        s_buf[s_slot][pl.ds(z, 8), :] = s_buf[s_slot][pl.ds(z, 8), :]
        ch = DIFF_SOFTMAX_ROWS
        for r0 in range(0, rows, ch):
            p_buf[p_slot][r0:r0 + ch, :] = jnp.exp2(s_buf[s_slot][r0:r0 + ch, :] - m_new).astype(BF16)

    def values(j, slot):
        vt = vtc_ref[0] if j == n_lat else vtl_ref[j]
        pv = jnp.dot(vt, p_buf[slot][0:vt.shape[1], :], preferred_element_type=F32)
        acc_scr[...] = pv if j == 0 else acc_scr[...] * al_buf[slot][...] + pv

    def finish_previous_block():
        acc = acc_scr[...]
        o = acc[0:DIFF_DV] * (1.0 / acc[DIFF_DV:DIFF_DV + 1])
        lam4 = lam_ref[...]
        s1 = jnp.sum(lam4[0:1] * lam4[1:2], axis=-1, keepdims=True)
        s2 = jnp.sum(lam4[2:3] * lam4[3:4], axis=-1, keepdims=True)
        lam = jnp.exp(s1) - jnp.exp(s2) + lam_init
        y = (o[:, :tq] - lam * o[:, tq:]).T
        ms = jnp.mean(y * y, axis=-1, keepdims=True)
        y = y * lax.rsqrt(ms + SUBLN_EPS) * g_ref[...] * (1.0 - lam_init)
        o_ref[0] = y.astype(o_ref.dtype)
        d = jnp.concatenate([y[0:1, :]] * (2 * tq // LANES), axis=1) * 0.0
        return jnp.where(jnp.isnan(d), 0.0, d)

    scores(0, 0)
    if n > 1:
        scores(1, 1)
    anchor = finish_previous_block()
    for t in range(n):
        if t >= 1:
            values(t - 1, (t - 1) % 2)
        if t + 2 < n:
            scores(t + 2, (t + 2) % 4)
        if t == min(1, n - 1):
            m_scr[...] = jnp.maximum(m_scr[...], anchor + NEG_BIG)
        softmax(t % 4, t % 2, rows_of(t))
    values(n - 1, (n - 1) % 2)


def _diff_attention(lam4, subln_g, q_arr, lat, ctx, lam_init):
    b, lq, _ = q_arr.shape
    hh = DIFF_HEADS
    tq = min(DIFF_TQ, lq)
    qk_c, vt_c = ctx
    lc = qk_c.shape[1]
    assert vt_c.shape == (b, hh * DIFF_VT_ROWS, lc)
    nq = lq // tq
    n_blocks = b * hh * nq

    def where(g):
        g = jnp.minimum(g, n_blocks - 1)
        return g // (hh * nq), (g // nq) % hh, g % nq

    def at(fn):
        return lambda g: fn(*where(g))

    in_specs = [pl.BlockSpec(memory_space=pltpu.SMEM),
                pl.BlockSpec((4, DIFF_HD), lambda g: (0, 0)),
                pl.BlockSpec((1, DIFF_DV), lambda g: (0, 0)),
                pl.BlockSpec((1, tq, LANES), at(lambda bi, h, i: (bi, i, h)))]
    args = [jnp.zeros((1,), jnp.int32), lam4, subln_g, q_arr]
    n_lat, tile = 0, lc
    if lat is not None:
        qk_l, vt_l = lat
        length = qk_l.shape[1]
        tile = vt_l.shape[2]
        n_lat = length // tile
        assert lc <= tile
        in_specs += [pl.BlockSpec((1, length, LANES), at(lambda bi, h, i: (bi, 0, hh + h))),
                     pl.BlockSpec((n_lat, DIFF_VT_ROWS, tile), at(lambda bi, h, i: (bi, h, 0)))]
        args += [qk_l, vt_l]
    in_specs += [pl.BlockSpec((1, lc, LANES), at(lambda bi, h, i: (bi, 0, hh + h))),
                 pl.BlockSpec((1, DIFF_VT_ROWS, lc), at(lambda bi, h, i: (bi, h, 0)))]
    args += [qk_c, vt_c]

    def finished(g):
        bi, h, i = where(jnp.maximum(g - 1, 0))
        return bi, i, h

    return pl.pallas_call(
        functools.partial(_diff_kernel, tq=tq, n_lat=n_lat, tile=tile, lc=lc, lam_init=lam_init),
        grid=(n_blocks + 1,),
        in_specs=in_specs,
        out_specs=pl.BlockSpec((1, tq, LANES), finished),
        out_shape=jax.ShapeDtypeStruct((b, lq, hh * DIFF_DV), BF16),
        scratch_shapes=[pltpu.VMEM((2 * tq, LANES), BF16),
                        *[pltpu.VMEM((tile, 2 * tq), F32)] * 4,
                        *[pltpu.VMEM((1, 2 * tq), F32)] * 4,
                        *[pltpu.VMEM((tile, 2 * tq), BF16)] * 2,
                        *[pltpu.VMEM((1, 2 * tq), F32)] * 2,
                        pltpu.VMEM((1, 2 * tq), F32),
                        pltpu.VMEM((DIFF_VT_ROWS, 2 * tq), F32)],
        compiler_params=_params(1, 40),
        name="diff_attention" if lat is not None else "diff_attention_ctx",
    )(*args)


def _softmax_pv(s_list, v_list):
    m = s_list[0].max(axis=-1, keepdims=True)
    for s in s_list[1:]:
        m = jnp.maximum(m, s.max(axis=-1, keepdims=True))
    den = 0.0
    o = 0.0
    for s, v in zip(s_list, v_list):
        p = jnp.exp2(s - m)
        den = den + jnp.sum(p, axis=-1, keepdims=True)
        o = o + jnp.dot(p.astype(BF16), v, preferred_element_type=F32)
    return o * (1.0 / den)


def _na_kernel(zero_ref, q_ref, k_ref, vtl_ref, kc_ref, vtc_ref, bias_ref, o_ref, *scr, units, n_rows, lc):
    s_buf, mt_buf, p_buf = scr[0:3], scr[3:6], scr[6:8]
    w = GRID_W
    uq = NA_UNIT_ROWS * w
    nk = NA_UNION_ROWS * w
    step_r0 = pl.program_id(2) * (units * NA_UNIT_ROWS)

    def first_key_row(u):
        return jnp.clip(step_r0 + u * NA_UNIT_ROWS - NA_KH // 2, 0, n_rows - NA_UNION_ROWS)

    def scores(u, slot):
        start = first_key_row(u)
        cfg = (step_r0 + u * NA_UNIT_ROWS - start) // NA_UNIT_ROWS
        q_t = q_ref[0, u * uq:(u + 1) * uq, :].astype(F32).T
        row = lax.broadcasted_iota(jnp.int32, q_t.shape, 0)
        qq = jnp.concatenate([jnp.where(row < NA_HD, q_t, 0.0), jnp.where(row >= NA_HD, q_t, 0.0)],
                             axis=1).astype(BF16)
        koff = pl.multiple_of(start * w, uq)
        s_nb = jnp.dot(k_ref[0, pl.ds(koff, nk), :], qq, preferred_element_type=F32) + bias_ref[0, cfg]
        s_cx = jnp.dot(kc_ref[0], qq, preferred_element_type=F32)
        mt_buf[slot][...] = jnp.maximum(jnp.max(s_nb, axis=0, keepdims=True), jnp.max(s_cx, axis=0, keepdims=True))
        s_buf[slot][0:nk, :] = s_nb
        s_buf[slot][nk:nk + lc, :] = s_cx

    def softmax(s_slot, p_slot):
        m = mt_buf[s_slot][...]
        z = pl.multiple_of(zero_ref[0], 8)
        s_buf[s_slot][pl.ds(z, 8), :] = s_buf[s_slot][pl.ds(z, 8), :]
        ch = NA_SOFTMAX_ROWS
        for r0 in range(0, nk + lc, ch):
            p_buf[p_slot][r0:r0 + ch, :] = jnp.exp2(s_buf[s_slot][r0:r0 + ch, :] - m).astype(BF16)

    def values(u, p_slot):
        pc0 = first_key_row(u) // NA_UNIT_ROWS
        acc = jnp.dot(vtc_ref[0], p_buf[p_slot][nk:nk + lc, :], preferred_element_type=F32)
        for i in range(NA_UNION_ROWS // NA_UNIT_ROWS):
            acc = acc + jnp.dot(vtl_ref[pc0 + i], p_buf[p_slot][i * uq:(i + 1) * uq, :],
                                preferred_element_type=F32)
        o = acc[0:LANES] * (1.0 / acc[LANES:LANES + 1])
        y = jnp.concatenate([o[0:NA_HD, 0:uq], o[NA_HD:LANES, uq:2 * uq]], axis=0)
        o_ref[0, u * uq:(u + 1) * uq, :] = y.T.astype(o_ref.dtype)

    scores(0, 0)
    if units > 1:
        scores(1, 1)
    for u in range(units):
        if u >= 1:
            values(u - 1, (u - 1) % 2)
        if u + 2 < units:
            scores(u + 2, (u + 2) % 3)
        softmax(u % 3, u % 2)
    values(units - 1, (units - 1) % 2)


def _na_attention(qk_l, vt_l, qk_c, vt_c, bias):
    b, length, _ = qk_l.shape
    lc = qk_c.shape[1]
    n_rows = length // GRID_W
    uq = NA_UNIT_ROWS * GRID_W
    nk = NA_UNION_ROWS * GRID_W
    assert n_rows >= NA_UNION_ROWS and n_rows % NA_UNIT_ROWS == 0 and vt_l.shape[2] == uq
    units = min(NA_UNITS, n_rows // NA_UNIT_ROWS)
    npair = NA_HEADS // 2
    n_pieces = length // uq
    tq = units * uq
    return pl.pallas_call(
        functools.partial(_na_kernel, units=units, n_rows=n_rows, lc=lc),
        grid=(b, npair, length // tq),
        in_specs=[pl.BlockSpec(memory_space=pltpu.SMEM),
                  pl.BlockSpec((1, tq, LANES), lambda bi, p, i: (bi, i, npair + p)),
                  pl.BlockSpec((1, length, LANES), lambda bi, p, i: (bi, 0, 2 * npair + p)),
                  pl.BlockSpec((n_pieces, DIFF_VT_ROWS, uq), lambda bi, p, i: (bi, p, 0)),
                  pl.BlockSpec((1, lc, LANES), lambda bi, p, i: (bi, 0, 2 * npair + p)),
                  pl.BlockSpec((1, DIFF_VT_ROWS, lc), lambda bi, p, i: (bi, p, 0)),
                  pl.BlockSpec((1, 3, nk, 2 * uq), lambda bi, p, i: (p, 0, 0, 0))],
        out_specs=pl.BlockSpec((1, tq, LANES), lambda bi, p, i: (bi, i, p)),
        out_shape=jax.ShapeDtypeStruct((b, length, NA_HEADS * NA_HD), BF16),
        scratch_shapes=[*[pltpu.VMEM((nk + lc, 2 * uq), F32)] * 3,
                        *[pltpu.VMEM((1, 2 * uq), F32)] * 3,
                        *[pltpu.VMEM((nk + lc, 2 * uq), BF16)] * 2],
        compiler_params=_params(3, 48),
        name="neighbourhood_attention",
    )(jnp.zeros((1,), jnp.int32), qk_l, qk_l, vt_l, qk_c, vt_c, bias)


def _ctx_attn_kernel(q_ref, k_ref, v_ref, o_ref):
    q = q_ref[0].astype(F32)
    k = k_ref[0]
    v = v_ref[0]
    lane = lax.broadcasted_iota(jnp.int32, q.shape, 1)
    outs = []
    for hh in range(2):
        sel = (lane < NA_HD) if hh == 0 else (lane >= NA_HD)
        qm = jnp.where(sel, q, 0.0).astype(BF16)
        s = lax.dot_general(qm, k, NT_DIMS, preferred_element_type=F32)
        outs.append(_softmax_pv([s], [v]))
    o_ref[0] = jnp.where(lane < NA_HD, outs[0], outs[1]).astype(o_ref.dtype)


def _ctx_attention(qkv_c):
    b, lc, _ = qkv_c.shape
    npair = NA_HEADS // 2
    return pl.pallas_call(
        _ctx_attn_kernel,
        grid=(b, npair),
        in_specs=[pl.BlockSpec((1, lc, LANES), lambda bi, p: (bi, 0, npair + p)),
                  pl.BlockSpec((1, lc, LANES), lambda bi, p: (bi, 0, 2 * npair + p)),
                  pl.BlockSpec((1, lc, LANES), lambda bi, p: (bi, 0, 3 * npair + p))],
        out_specs=pl.BlockSpec((1, lc, LANES), lambda bi, p: (bi, 0, p)),
        out_shape=jax.ShapeDtypeStruct((b, lc, NA_HEADS * NA_HD), BF16),
        compiler_params=_params(2, 32),
        name="ctx_attention",
    )(qkv_c, qkv_c, qkv_c)


def _merge_kernel(oret_ref, gates_ref, odiff_ref, ona_ref, x_ref, g1_ref, lng_ref, lnb_ref,
                  wpa_ref, wpb_ref, wpc_ref, wo_ref, o_ref, *, alpha):
    d = D_MODEL
    ya_parts = []
    for hh in range(RET_HEADS):
        a = hh * RET_DK
        seg = _layer_norm_rows(oret_ref[:, a:a + RET_DK])
        g = gates_ref[:, a:a + RET_DK]
        ya_parts.append((g * _sigmoid(g) * seg).astype(BF16))
    ya_in = jnp.concatenate(ya_parts, axis=1)
    y_a = jnp.dot(ya_in, wpa_ref[...], preferred_element_type=F32)
    y_b = jnp.dot(odiff_ref[...], wpb_ref[...], preferred_element_type=F32)
    y_c = jnp.dot(ona_ref[...], wpc_ref[...], preferred_element_type=F32)
    m = (_sigmoid(gates_ref[:, d:2 * d]) * y_a + _sigmoid(gates_ref[:, 2 * d:3 * d]) * y_b
         + _sigmoid(gates_ref[:, 3 * d:4 * d]) * y_c)
    y = jnp.dot(m.astype(BF16), wo_ref[...], preferred_element_type=F32)
    z = alpha * x_ref[...] + g1_ref[0] * y
    o_ref[...] = _layer_norm_rows(z) * lng_ref[...] + lnb_ref[...]


def _merge(o_ret, gates, o_diff, o_na, x, g1, ln_g, ln_b, w_pa, w_pb, w_pc, w_o, mod_row, alpha):
    r, d = x.shape
    tm = min(MERGE_TILE, r)
    row = lambda i: (i, 0)
    const = lambda i: (0, 0)
    wspec = pl.BlockSpec((d, d), const, pipeline_mode=pl.Buffered(1))
    return pl.pallas_call(
        functools.partial(_merge_kernel, alpha=alpha),
        grid=(r // tm,),
        in_specs=[pl.BlockSpec((tm, d), row), pl.BlockSpec((tm, 4 * d), row), pl.BlockSpec((tm, d), row),
                  pl.BlockSpec((tm, d), row), pl.BlockSpec((tm, d), row),
                  pl.BlockSpec((1, 1, d), lambda i: (mod_row(i, tm), 0, 0)),
                  pl.BlockSpec((1, d), const), pl.BlockSpec((1, d), const),
                  wspec, wspec, wspec, wspec],
        out_specs=pl.BlockSpec((tm, d), row),
        out_shape=jax.ShapeDtypeStruct((r, d), F32),
        compiler_params=_params(1, 52),
        name="merge_out_proj",
    )(o_ret, gates, o_diff, o_na, x, g1, ln_g, ln_b, w_pa, w_pb, w_pc, w_o)


def _ffn_kernel(x_ref, sh_ref, sc_ref, g2_ref, lng_ref, lnb_ref, w1_ref, w2_ref, o_ref, *, alpha, ff_chunk):
    x = x_ref[...]
    h = (_layer_norm_rows(x) * (1.0 + sc_ref[0]) + sh_ref[0]).astype(BF16)
    acc = jnp.zeros(x.shape, F32)
    for cidx in range(D_FF // ff_chunk):
        a = cidx * ff_chunk
        u = jnp.maximum(jnp.dot(h, w1_ref[:, a:a + ff_chunk], preferred_element_type=F32), 0.0)
        acc = acc + jnp.dot((u * u).astype(BF16), w2_ref[a:a + ff_chunk, :], preferred_element_type=F32)
    z = alpha * x + g2_ref[0] * acc
    o_ref[...] = _layer_norm_rows(z) * lng_ref[...] + lnb_ref[...]


def _ffn(x, shift, scale, g2, ln_g, ln_b, w1, w2, mod_row, alpha):
    r, d = x.shape
    tm = min(FFN_TILE, r)
    row = lambda i: (i, 0)
    const = lambda i: (0, 0)
    mspec = pl.BlockSpec((1, 1, d), lambda i: (mod_row(i, tm), 0, 0))
    return pl.pallas_call(
        functools.partial(_ffn_kernel, alpha=alpha, ff_chunk=1024),
        grid=(r // tm,),
        in_specs=[pl.BlockSpec((tm, d), row), mspec, mspec, mspec,
                  pl.BlockSpec((1, d), const), pl.BlockSpec((1, d), const),
                  pl.BlockSpec((d, D_FF), const), pl.BlockSpec((D_FF, d), const)],
        out_specs=pl.BlockSpec((tm, d), row),
        out_shape=jax.ShapeDtypeStruct((r, d), F32),
        compiler_params=_params(1, 56),
        name="ffn",
    )(x, shift, scale, g2, ln_g, ln_b, w1, w2)


def _retention_tables(length):
    t = jnp.arange(length, dtype=F32)
    inv = ROPE_BASE ** (-jnp.linspace(0.0, 1.0, RET_DK // 2, dtype=F32))
    ang = t[:, None] * inv[None, :]
    return jnp.cos(ang), jnp.sin(ang)


def _axial_tables(length):
    t = jnp.arange(length)
    row = (t // GRID_W).astype(F32)
    col = (t % GRID_W).astype(F32)
    half = DIFF_HD // 2
    inv = ROPE_BASE ** (-jnp.arange(0, half, 2, dtype=F32) / half)
    ang_r = row[:, None] * inv[None, :]
    ang_c = col[:, None] * inv[None, :]
    zero = jnp.zeros_like(ang_r)
    unit_ang = jnp.concatenate([ang_r, ang_r, ang_c, ang_c], axis=-1)
    cos = jnp.cos(unit_ang)
    sin_r, sin_c = jnp.sin(ang_r), jnp.sin(ang_c)
    sin_lo = jnp.concatenate([-sin_r, zero, -sin_c, zero], axis=-1)
    sin_hi = jnp.concatenate([zero, sin_r, zero, sin_c], axis=-1)
    rep = LANES // DIFF_HD
    return tuple(jnp.tile(a, (1, rep)) for a in (cos, sin_lo, sin_hi))


def _na_bias_table(rpb):
    nh, nr, _ = rpb.shape
    w = GRID_W
    wid = 2 * w - 1
    p = jnp.pad(rpb.astype(F32) * math.log2(math.e), ((0, 0), (0, 0), (w - NA_KW, w - NA_KW)))
    y = p[..., ::-1]
    a = jnp.broadcast_to(y[:, :, None, :], (nh, nr, w, wid))
    a = jnp.pad(a, ((0, 0), (0, 0), (0, 0), (0, 1))).reshape(nh, nr, w * (wid + 1))
    a = a[:, :, :w * wid].reshape(nh, nr, w, wid)
    m = a[..., w - 1:]
    j = jnp.arange(w)
    cs = jnp.clip(j - NA_KW // 2, 0, w - NA_KW)
    col_mask = (j[:, None] >= cs[None, :]) & (j[:, None] < cs[None, :] + NA_KW)
    m = jnp.where(col_mask[None, None], m, NEG_BIG)
    un, g_rows = NA_UNION_ROWS, NA_UNIT_ROWS
    mp = jnp.pad(m, ((0, 0), (un, un), (0, 0), (0, 0)), constant_values=NEG_BIG)
    mp = mp.reshape(nh // 2, 2, nr + 2 * un, w, w)
    a = jnp.arange(un)
    per_cfg = []
    for cfg in range(3):
        per_col = []
        for hh in range(2):
            for g in range(g_rows):
                q_row = cfg * g_rows + g
                win0 = (0, g, un - NA_KH)[cfg]
                lo = NA_KH - 1 - q_row + un
                in_win = (a >= win0) & (a < win0 + NA_KH)
                per_col.append(jnp.where(in_win[None, :, None, None], mp[:, hh, lo:lo + un], NEG_BIG))
        per_cfg.append(jnp.concatenate(per_col, axis=-1))
    return jnp.stack(per_cfg, axis=1).reshape(nh // 2, 3, un * w, 2 * g_rows * w)


def _column_scale():
    d = D_MODEL
    cs = jnp.ones((IN_WIDTH,), F32)
    cs = cs.at[d:2 * d].set(RET_DK ** -0.5)
    cs = cs.at[4 * d:5 * d].set(DIFF_HD ** -0.5 * math.log2(math.e))
    cs = cs.at[7 * d:8 * d].set(NA_HD ** -0.5 * math.log2(math.e))
    return cs.reshape(1, IN_WIDTH)


def _project_all(h, w, colscale, tm, rope, ret_tables, ax_tables, tiles_per_batch, tk_out, na_tk_out, plain_na_v):
    if rope:
        ret_qk = _proj(h, w, colscale, tm, 0, 1, 2, BF16, "rope_ret", ret_tables, tiles_per_batch)
        diff_qk = _proj(h, w, colscale, tm, 4, 5, 2, BF16, "rope_diff", ax_tables, tiles_per_batch)
    else:
        ret_qk = _proj(h, w, colscale, tm, 0, 1, 2, BF16)
        diff_qk = _proj(h, w, colscale, tm, 4, 5, 2, BF16)
    v_na = _proj(h, w, colscale, tm, 2, 7, 4 if plain_na_v else 3, BF16)
    gates = _proj(h, w, colscale, tm, 3, 10, 4, F32)
    diff_vt = _proj(h, w, colscale, tm, 6, 7, 1, BF16, tk_out=tk_out)
    na_vt = _proj(h, w, colscale, tm, 9, 10, 1, BF16, tk_out=na_tk_out)
    return ret_qk, diff_qk, v_na, gates, diff_vt, na_vt


def kernel(x, c, ctx, c_ctx, w_mod, b_mod, w_in, ret_decay_f, ret_decay_b, diff_lq1, diff_lk1, diff_lq2, diff_lk2,
           diff_subln_g, na_rpb, w_pa, w_pb, w_pc, w_o, ln1_g, ln1_b, w_ff1, w_ff2, ln2_g, ln2_b):
    b, length, d = x.shape
    lc = ctx.shape[1]
    depth = w_mod.shape[0]
    assert d == D_MODEL and b < 8 and length % min(ROW_TILE, length) == 0 and length % GRID_W == 0
    assert lc % RET_CHUNK == 0 or lc < RET_CHUNK
    alpha = (2.0 * depth) ** 0.25

    cvec = jnp.zeros((8, d), F32).at[:b].set(c).at[b].set(c_ctx)
    mods = _modulation(cvec, w_mod, b_mod)

    lat_row = lambda i, tm: (i * tm) // length
    ctx_row = lambda i, tm: b
    colscale = _column_scale()
    ret_tables = _retention_tables(length)
    ax_tables = _axial_tables(length)

    x_l = x.reshape(b * length, d)
    x_c = ctx.reshape(b * lc, d)
    zero_state = jnp.zeros((b, RET_HEADS, RET_DK, RET_DK), F32)
    for l in range(depth):
        with_ctx_out = l < depth - 1
        mod = [mods[l, :, k * d:(k + 1) * d].reshape(8, 1, d) for k in range(6)]
        sh1, sc1, g1, sh2, sc2, g2 = mod
        w = (w_in, l)
        wpa, wpb, wpc, wo = (a[l].astype(BF16) for a in (w_pa, w_pb, w_pc, w_o))
        w1, w2 = w_ff1[l].astype(BF16), w_ff2[l].astype(BF16)
        lng1, lnb1 = ln1_g[l].reshape(1, d), ln1_b[l].reshape(1, d)
        lng2, lnb2 = ln2_g[l].reshape(1, d), ln2_b[l].reshape(1, d)
        lam_init = 0.8 - 0.6 * math.exp(-0.3 * l)
        lam4 = jnp.stack([diff_lq1[l], diff_lk1[l], diff_lq2[l], diff_lk2[l]]).astype(F32)
        subln = diff_subln_g[l].reshape(1, DIFF_DV).astype(F32)
        bias = _na_bias_table(na_rpb[l])

        h_l = _ln_mod(x_l, sh1, sc1, lat_row)
        h_c = _ln_mod(x_c, sh1, sc1, ctx_row)
        tm_l = min(ROW_TILE, length)
        rqk_l, dqk_l, vna_l, gates_l, dvt_l, nvt_l = _project_all(
            h_l, w, colscale, tm_l, True, ret_tables, ax_tables, length // tm_l, min(DIFF_TILE, length),
            NA_UNIT_ROWS * GRID_W, False)
        rqk_c, dqk_c, vna_c, gates_c, dvt_c, nvt_c = _project_all(
            h_c, w, colscale, min(ROW_TILE, b * lc), False, (), (), 1, lc, lc, True)
        sh3 = lambda a, n: a.reshape(b, n, a.shape[-1])

        decays = _retention_decays(ret_decay_f[l], ret_decay_b[l], min(RET_CHUNK, lc))
        rqk_l3, vna_l3, rqk_c3, vna_c3 = sh3(rqk_l, length), sh3(vna_l, length), sh3(rqk_c, lc), sh3(vna_c, lc)
        o_cf, s_cf = _retention(decays, rqk_c3, vna_c3, zero_state, None, False)
        o_rc, s_cb = _retention(decays, rqk_c3, vna_c3, zero_state, o_cf, True)
        o_lf, _ = _retention(decays, rqk_l3, vna_l3, s_cf, None, False)
        o_rl, _ = _retention(decays, rqk_l3, vna_l3, s_cb, o_lf, True)

        dqk_l3, dqk_c3 = sh3(dqk_l, length), sh3(dqk_c, lc)
        o_dl = _diff_attention(lam4, subln, dqk_l3, (dqk_l3, dvt_l), (dqk_c3, dvt_c), lam_init)
        o_nl = _na_attention(vna_l3, nvt_l, vna_c3, nvt_c, bias)

        x_l = _merge(o_rl.reshape(b * length, d), gates_l, o_dl.reshape(b * length, d), o_nl.reshape(b * length, d),
                     x_l, g1, lng1, lnb1, wpa, wpb, wpc, wo, lat_row, alpha)
        x_l = _ffn(x_l, sh2, sc2, g2, lng2, lnb2, w1, w2, lat_row, alpha)
        if with_ctx_out:
            o_dc = _diff_attention(lam4, subln, dqk_c3, None, (dqk_c3, dvt_c), lam_init)
            o_nc = _ctx_attention(vna_c3)
            x_c = _merge(o_rc.reshape(b * lc, d), gates_c, o_dc.reshape(b * lc, d), o_nc.reshape(b * lc, d),
                         x_c, g1, lng1, lnb1, wpa, wpb, wpc, wo, ctx_row, alpha)
            x_c = _ffn(x_c, sh2, sc2, g2, lng2, lnb2, w1, w2, ctx_row, alpha)
    return x_l.reshape(b, length, d)
```

```python
import functools
import math

import jax
import jax.numpy as jnp
from jax import lax
from jax.experimental import pallas as pl
from jax.experimental.pallas import tpu as pltpu

F32 = jnp.float32
BF16 = jnp.bfloat16

D_MODEL = 1024
GRID_W = 64
RET_HEADS, RET_DK = 4, 256
DIFF_HEADS, DIFF_HD, DIFF_DV = 8, 64, 128
NA_HEADS, NA_HD, NA_KH, NA_KW = 16, 64, 8, 16
D_FF = 4 * D_MODEL
ROPE_BASE = 10000.0
LN_EPS = 1e-6
SUBLN_EPS = 1e-5
IN_WIDTH = 13 * D_MODEL
NEG_BIG = -1e30

LANES = 128
VMEM_LIMIT_CAP = 56 << 20

ROW_TILE = 2048
PROJ_TN = 1024
LN_TILE = 512
MERGE_TILE = 512
FFN_TILE = 1024
RET_CHUNK = 256
RET_BLOCK = 1024
DIFF_TQ = 256
DIFF_TILE = 512
DIFF_VT_ROWS = DIFF_DV + 16
DIFF_SOFTMAX_ROWS = 64
NA_UNIT_ROWS = 4
NA_UNION_ROWS = 12
NA_UNITS = 16
NA_SOFTMAX_ROWS = 64

NT_DIMS = (((1,), (1,)), ((), ()))
TN_DIMS = (((0,), (0,)), ((), ()))


def _params(n_axes, vmem_mb):
    return pltpu.CompilerParams(dimension_semantics=("arbitrary",) * n_axes,
                                vmem_limit_bytes=min(vmem_mb << 20, VMEM_LIMIT_CAP))


def _sigmoid(x):
    return 1.0 / (1.0 + jnp.exp(-x))


def _layer_norm_rows(x):
    mu = jnp.mean(x, axis=-1, keepdims=True)
    xc = x - mu
    var = jnp.mean(xc * xc, axis=-1, keepdims=True)
    return xc * lax.rsqrt(var + LN_EPS)


def _mod_kernel(c_ref, w_ref, b_ref, o_ref):
    c = c_ref[...]
    a = (c * _sigmoid(c)).astype(BF16)
    o_ref[0] = jnp.dot(a, w_ref[0].astype(BF16), preferred_element_type=F32) + b_ref[0]


def _modulation(cvec, w_mod, b_mod):
    depth = w_mod.shape[0]
    d = D_MODEL
    return pl.pallas_call(
        _mod_kernel,
        grid=(depth, 6),
        in_specs=[pl.BlockSpec((8, d), lambda l, j: (0, 0)),
                  pl.BlockSpec((1, d, d), lambda l, j: (l, 0, j)),
                  pl.BlockSpec((1, 1, d), lambda l, j: (l, 0, j))],
        out_specs=pl.BlockSpec((1, 8, d), lambda l, j: (l, 0, j)),
        out_shape=jax.ShapeDtypeStruct((depth, 8, 6 * d), F32),
        compiler_params=_params(2, 32),
        name="modulation",
    )(cvec, w_mod, b_mod.reshape(depth, 1, 6 * d))


def _ln_mod_kernel(x_ref, sh_ref, sc_ref, o_ref):
    y = _layer_norm_rows(x_ref[...])
    o_ref[...] = (y * (1.0 + sc_ref[0]) + sh_ref[0]).astype(o_ref.dtype)


def _ln_mod(x, shift, scale, mod_row):
    r, d = x.shape
    tm = min(LN_TILE, r)
    return pl.pallas_call(
        _ln_mod_kernel,
        grid=(r // tm,),
        in_specs=[pl.BlockSpec((tm, d), lambda i: (i, 0)),
                  pl.BlockSpec((1, 1, d), lambda i: (mod_row(i, tm), 0, 0)),
                  pl.BlockSpec((1, 1, d), lambda i: (mod_row(i, tm), 0, 0))],
        out_specs=pl.BlockSpec((tm, d), lambda i: (i, 0)),
        out_shape=jax.ShapeDtypeStruct((r, d), BF16),
        compiler_params=_params(1, 32),
        name="ln_modulate",
    )(x, shift, scale)


def _proj_kernel(h_ref, w_ref, cs_ref, *rest, mode, transposed, tn, tk_out):
    o_ref = rest[-1]
    acc = jnp.dot(h_ref[...], w_ref[0].astype(BF16), preferred_element_type=F32) * cs_ref[...]
    if mode == "rope_ret":
        cos = rest[0][...]
        sin = rest[1][...]
        half = RET_DK // 2
        for hh in range(tn // RET_DK):
            a = hh * RET_DK
            x1 = acc[:, a:a + half]
            x2 = acc[:, a + half:a + RET_DK]
            o_ref[:, a:a + half] = (x1 * cos - x2 * sin).astype(o_ref.dtype)
            o_ref[:, a + half:a + RET_DK] = (x1 * sin + x2 * cos).astype(o_ref.dtype)
    elif mode == "rope_diff":
        cos = rest[0][...]
        sin_lo = rest[1][...]
        sin_hi = rest[2][...]
        q16 = DIFF_HD // 4
        for g in range(tn // LANES):
            x = acc[:, g * LANES:(g + 1) * LANES]
            y = x * cos + pltpu.roll(x, LANES - q16, 1) * sin_lo + pltpu.roll(x, q16, 1) * sin_hi
            o_ref[:, g * LANES:(g + 1) * LANES] = y.astype(o_ref.dtype)
    elif transposed:
        acc_t = acc.T
        ones = jnp.ones((DIFF_VT_ROWS - DIFF_DV, tk_out), o_ref.dtype)
        for a in range(o_ref.shape[0]):
            for hh in range(tn // DIFF_DV):
                r0 = hh * DIFF_VT_ROWS
                o_ref[a, r0:r0 + DIFF_DV, :] = acc_t[hh * DIFF_DV:(hh + 1) * DIFF_DV,
                                                     a * tk_out:(a + 1) * tk_out].astype(o_ref.dtype)
                o_ref[a, r0 + DIFF_DV:r0 + DIFF_VT_ROWS, :] = ones
    else:
        o_ref[...] = acc.astype(o_ref.dtype)


def _proj(h, w, colscale, tm, first, rest0, ntiles, out_dtype, mode="plain", tables=(), tiles_per_batch=1,
          tk_out=None):
    w_all, layer = w
    r, d = h.shape
    tn = PROJ_TN
    ncols = ntiles * tn
    assert r % tm == 0
    transposed = tk_out is not None
    wcol = lambda j: jnp.where(j == 0, first, rest0 + j - 1)
    in_specs = [pl.BlockSpec((tm, d), lambda i, j: (i, 0)),
                pl.BlockSpec((1, d, tn), lambda i, j: (layer, 0, wcol(j))),
                pl.BlockSpec((1, tn), lambda i, j: (0, wcol(j)))]
    for _ in tables:
        in_specs.append(pl.BlockSpec((tm, LANES), lambda i, j: (i % tiles_per_batch, 0)))
    if transposed:
        assert tm % tk_out == 0
        na = tm // tk_out
        rows_out = tn // DIFF_DV * DIFF_VT_ROWS
        out_specs = pl.BlockSpec((na, rows_out, tk_out), lambda i, j: (i, j, 0))
        out_shape = jax.ShapeDtypeStruct((r // tk_out, ncols // DIFF_DV * DIFF_VT_ROWS, tk_out), out_dtype)
    else:
        out_specs = pl.BlockSpec((tm, tn), lambda i, j: (i, j))
        out_shape = jax.ShapeDtypeStruct((r, ncols), out_dtype)
    return pl.pallas_call(
        functools.partial(_proj_kernel, mode=mode, transposed=transposed, tn=tn, tk_out=tk_out),
        grid=(r // tm, ntiles),
        in_specs=in_specs,
        out_specs=out_specs,
        out_shape=out_shape,
        compiler_params=_params(2, 52),
        name="in_proj_" + mode + ("_t" if transposed else ""),
    )(h, w_all, colscale, *tables)


def _ret_tables_kernel(dec_ref, intra_ref, qd_ref, kd_ref, cd_ref, *, chunk):
    direction = pl.program_id(0)
    head = pl.program_id(1)
    cf = float(chunk)
    dv = jnp.full((1, 1), dec_ref[direction, head], F32)
    lg = jnp.minimum(dv, 0.0) - jnp.log(1.0 + jnp.exp(-jnp.abs(dv)))
    rev = direction == 1
    ii = lax.broadcasted_iota(jnp.int32, (chunk, chunk), 0).astype(F32)
    jj = lax.broadcasted_iota(jnp.int32, (chunk, chunk), 1).astype(F32)
    dist = jnp.where(rev, jj - ii, ii - jj)
    intra_ref[0, 0] = jnp.where(dist >= 0.0, jnp.exp(lg * jnp.maximum(dist, 0.0)), 0.0)
    i1 = lax.broadcasted_iota(jnp.int32, (chunk, RET_DK), 0).astype(F32)
    qd_ref[0, 0] = jnp.exp(lg * jnp.where(rev, cf - i1, i1 + 1.0))
    kd_ref[0, 0] = jnp.exp(lg * jnp.where(rev, i1, cf - 1.0 - i1))
    cd_ref[0, 0] = jnp.exp(jnp.broadcast_to(lg, (8, RET_DK)) * cf)


def _retention_decays(dec_f, dec_b, chunk):
    hh, dk = RET_HEADS, RET_DK
    blk = lambda *shape: pl.BlockSpec((1, 1) + shape, lambda d, h: (d, h, 0, 0))
    return pl.pallas_call(
        functools.partial(_ret_tables_kernel, chunk=chunk),
        grid=(2, hh),
        in_specs=[pl.BlockSpec(memory_space=pltpu.SMEM)],
        out_specs=[blk(chunk, chunk), blk(chunk, dk), blk(chunk, dk), blk(8, dk)],
        out_shape=[jax.ShapeDtypeStruct((2, hh, chunk, chunk), F32),
                   jax.ShapeDtypeStruct((2, hh, chunk, dk), F32),
                   jax.ShapeDtypeStruct((2, hh, chunk, dk), F32),
                   jax.ShapeDtypeStruct((2, hh, 8, dk), F32)],
        compiler_params=_params(2, 32),
        name="retention_tables",
    )(jnp.stack([dec_f, dec_b]).astype(F32))


def _ret_kernel(intra_ref, qd_ref, kd_ref, cd_ref, qk_ref, v_ref, s0_ref, *rest, reverse, has_prev, chunk, n_chunks):
    if has_prev:
        prev_ref, o_ref, sfin_ref, s_scr = rest
    else:
        o_ref, sfin_ref, s_scr = rest
    first = pl.program_id(1) == 0
    dk = RET_DK
    order = range(n_chunks - 1, -1, -1) if reverse else range(n_chunks)
    for h in range(RET_HEADS):
        cols = slice(h * dk, (h + 1) * dk)
        kcols = slice((RET_HEADS + h) * dk, (RET_HEADS + h + 1) * dk)
        s = jnp.where(first, s0_ref[0, h], s_scr[h])
        for c in order:
            rows = slice(c * chunk, (c + 1) * chunk)
            q = qk_ref[0, rows, cols]
            k = qk_ref[0, rows, kcols]
            v = v_ref[0, rows, cols]
            att = lax.dot_general(q, k, NT_DIMS, preferred_element_type=F32) * intra_ref[0, h]
            qd = (q.astype(F32) * qd_ref[0, h]).astype(BF16)
            o = (jnp.dot(att.astype(BF16), v, preferred_element_type=F32)
                 + jnp.dot(qd, s.astype(BF16), preferred_element_type=F32))
            kd = (k.astype(F32) * kd_ref[0, h]).astype(BF16)
            s = s * cd_ref[0, h, 0:1, :] + lax.dot_general(kd, v, TN_DIMS, preferred_element_type=F32)
            if has_prev:
                o = o + prev_ref[0, rows, cols]
            o_ref[0, rows, cols] = o
        s_scr[h] = s
        sfin_ref[0, h] = s


def _retention(tables, qk, v_arr, s0, prev, reverse):
    b, length, _ = qk.shape
    hh, dk = RET_HEADS, RET_DK
    width = hh * dk
    c = tables[0].shape[-1]
    rb = min(RET_BLOCK, length)
    n = length // rb
    direction = 1 if reverse else 0
    pos = (lambda i: n - 1 - i) if reverse else (lambda i: i)
    tab = lambda *shape: pl.BlockSpec((1, hh) + shape, lambda bi, i: (direction, 0, 0, 0))
    in_specs = [tab(c, c), tab(c, dk), tab(c, dk), tab(8, dk),
                pl.BlockSpec((1, rb, 2 * width), lambda bi, i: (bi, pos(i), 0)),
                pl.BlockSpec((1, rb, width), lambda bi, i: (bi, pos(i), 0)),
                pl.BlockSpec((1, hh, dk, dk), lambda bi, i: (bi, 0, 0, 0))]
    args = [*tables, qk, v_arr, s0]
    if prev is not None:
        in_specs.append(pl.BlockSpec((1, rb, width), lambda bi, i: (bi, pos(i), 0)))
        args.append(prev)
    return pl.pallas_call(
        functools.partial(_ret_kernel, reverse=reverse, has_prev=prev is not None, chunk=c, n_chunks=rb // c),
        grid=(b, n),
        in_specs=in_specs,
        out_specs=[pl.BlockSpec((1, rb, width), lambda bi, i: (bi, pos(i), 0)),
                   pl.BlockSpec((1, hh, dk, dk), lambda bi, i: (bi, 0, 0, 0))],
        out_shape=[jax.ShapeDtypeStruct((b, length, width), F32),
                   jax.ShapeDtypeStruct((b, hh, dk, dk), F32)],
        scratch_shapes=[pltpu.VMEM((hh, dk, dk), F32)],
        compiler_params=_params(2, 48),
        name="retention_bwd" if reverse else "retention_fwd",
    )(*args)


def _diff_kernel(zero_ref, lam_ref, g_ref, q_ref, *rest, tq, n_lat, tile, lc, lam_init):
    if n_lat:
        kl_ref, vtl_ref = rest[:2]
        rest = rest[2:]
    kc_ref, vtc_ref, o_ref, qq_scr = rest[:4]
    s_buf, mt_buf, p_buf, al_buf = rest[4:8], rest[8:12], rest[12:14], rest[14:16]
    m_scr, acc_scr = rest[16:]
    n = n_lat + 1
    rows_of = lambda j: lc if j == n_lat else tile

    q = q_ref[0].astype(F32)
    lane = lax.broadcasted_iota(jnp.int32, q.shape, 1)
    qq_scr[0:tq, :] = jnp.where(lane < DIFF_HD, q, 0.0).astype(BF16)
    qq_scr[tq:2 * tq, :] = jnp.where(lane >= DIFF_HD, q, 0.0).astype(BF16)
    m_scr[...] = jnp.full(m_scr.shape, NEG_BIG, F32)

    @pl.when(pl.program_id(0) == 0)
    def _no_previous_block():
        acc_scr[...] = jnp.ones(acc_scr.shape, F32)

    def scores(j, slot):
        k = kc_ref[0] if j == n_lat else kl_ref[0, j * tile:(j + 1) * tile, :]
        s = lax.dot_general(k, qq_scr[...], NT_DIMS, preferred_element_type=F32)
        mt_buf[slot][...] = jnp.max(s, axis=0, keepdims=True)
        s_buf[slot][0:k.shape[0], :] = s

    def softmax(s_slot, p_slot, rows):
        m_old = m_scr[...]
        m_new = jnp.maximum(m_old, mt_buf[s_slot][...])
        al_buf[p_slot][...] = jnp.exp2(m_old - m_new)
        m_scr[...] = m_new
        z = pl.multiple_of(zero_ref[0], 8)
        s_buf[s_slot][pl.ds(z, 8), :] = s_buf[s_slot][pl.ds(z, 8), :]
        ch = DIFF_SOFTMAX_ROWS
        for r0 in range(0, rows, ch):
            p_buf[p_slot][r0:r0 + ch, :] = jnp.exp2(s_buf[s_slot][r0:r0 + ch, :] - m_new).astype(BF16)

    def values(j, slot):
        vt = vtc_ref[0] if j == n_lat else vtl_ref[j]
        pv = jnp.dot(vt, p_buf[slot][0:vt.shape[1], :], preferred_element_type=F32)
        acc_scr[...] = pv if j == 0 else acc_scr[...] * al_buf[slot][...] + pv

    def finish_previous_block():
        acc = acc_scr[...]
        o = acc[0:DIFF_DV] * (1.0 / acc[DIFF_DV:DIFF_DV + 1])
        lam4 = lam_ref[...]
        s1 = jnp.sum(lam4[0:1] * lam4[1:2], axis=-1, keepdims=True)
        s2 = jnp.sum(lam4[2:3] * lam4[3:4], axis=-1, keepdims=True)
        lam = jnp.exp(s1) - jnp.exp(s2) + lam_init
        y = (o[:, :tq] - lam * o[:, tq:]).T
        ms = jnp.mean(y * y, axis=-1, keepdims=True)
        y = y * lax.rsqrt(ms + SUBLN_EPS) * g_ref[...] * (1.0 - lam_init)
        o_ref[0] = y.astype(o_ref.dtype)
        d = jnp.concatenate([y[0:1, :]] * (2 * tq // LANES), axis=1) * 0.0
        return jnp.where(jnp.isnan(d), 0.0, d)

    scores(0, 0)
    if n > 1:
        scores(1, 1)
    anchor = finish_previous_block()
    for t in range(n):
        if t >= 1:
            values(t - 1, (t - 1) % 2)
        if t + 2 < n:
            scores(t + 2, (t + 2) % 4)
        if t == min(1, n - 1):
            m_scr[...] = jnp.maximum(m_scr[...], anchor + NEG_BIG)
        softmax(t % 4, t % 2, rows_of(t))
    values(n - 1, (n - 1) % 2)


def _diff_attention(lam4, subln_g, q_arr, lat, ctx, lam_init):
    b, lq, _ = q_arr.shape
    hh = DIFF_HEADS
    tq = min(DIFF_TQ, lq)
    qk_c, vt_c = ctx
    lc = qk_c.shape[1]
    assert vt_c.shape == (b, hh * DIFF_VT_ROWS, lc)
    nq = lq // tq
    n_blocks = b * hh * nq

    def where(g):
        g = jnp.minimum(g, n_blocks - 1)
        return g // (hh * nq), (g // nq) % hh, g % nq

    def at(fn):
        return lambda g: fn(*where(g))

    in_specs = [pl.BlockSpec(memory_space=pltpu.SMEM),
                pl.BlockSpec((4, DIFF_HD), lambda g: (0, 0)),
                pl.BlockSpec((1, DIFF_DV), lambda g: (0, 0)),
                pl.BlockSpec((1, tq, LANES), at(lambda bi, h, i: (bi, i, h)))]
    args = [jnp.zeros((1,), jnp.int32), lam4, subln_g, q_arr]
    n_lat, tile = 0, lc
    if lat is not None:
        qk_l, vt_l = lat
        length = qk_l.shape[1]
        tile = vt_l.shape[2]
        n_lat = length // tile
        assert lc <= tile
        in_specs += [pl.BlockSpec((1, length, LANES), at(lambda bi, h, i: (bi, 0, hh + h))),
                     pl.BlockSpec((n_lat, DIFF_VT_ROWS, tile), at(lambda bi, h, i: (bi, h, 0)))]
        args += [qk_l, vt_l]
    in_specs += [pl.BlockSpec((1, lc, LANES), at(lambda bi, h, i: (bi, 0, hh + h))),
                 pl.BlockSpec((1, DIFF_VT_ROWS, lc), at(lambda bi, h, i: (bi, h, 0)))]
    args += [qk_c, vt_c]

    def finished(g):
        bi, h, i = where(jnp.maximum(g - 1, 0))
        return bi, i, h

    return pl.pallas_call(
        functools.partial(_diff_kernel, tq=tq, n_lat=n_lat, tile=tile, lc=lc, lam_init=lam_init),
        grid=(n_blocks + 1,),
        in_specs=in_specs,
        out_specs=pl.BlockSpec((1, tq, LANES), finished),
        out_shape=jax.ShapeDtypeStruct((b, lq, hh * DIFF_DV), BF16),
        scratch_shapes=[pltpu.VMEM((2 * tq, LANES), BF16),
                        *[pltpu.VMEM((tile, 2 * tq), F32)] * 4,
                        *[pltpu.VMEM((1, 2 * tq), F32)] * 4,
                        *[pltpu.VMEM((tile, 2 * tq), BF16)] * 2,
                        *[pltpu.VMEM((1, 2 * tq), F32)] * 2,
                        pltpu.VMEM((1, 2 * tq), F32),
                        pltpu.VMEM((DIFF_VT_ROWS, 2 * tq), F32)],
        compiler_params=_params(1, 40),
        name="diff_attention" if lat is not None else "diff_attention_ctx",
    )(*args)


def _softmax_pv(s_list, v_list):
    m = s_list[0].max(axis=-1, keepdims=True)
    for s in s_list[1:]:
        m = jnp.maximum(m, s.max(axis=-1, keepdims=True))
    den = 0.0
    o = 0.0
    for s, v in zip(s_list, v_list):
        p = jnp.exp2(s - m)
        den = den + jnp.sum(p, axis=-1, keepdims=True)
        o = o + jnp.dot(p.astype(BF16), v, preferred_element_type=F32)
    return o * (1.0 / den)


def _na_kernel(zero_ref, q_ref, k_ref, vtl_ref, kc_ref, vtc_ref, bias_ref, o_ref, *scr, units, n_rows, lc):
    s_buf, mt_buf, p_buf = scr[0:3], scr[3:6], scr[6:8]
    w = GRID_W
    uq = NA_UNIT_ROWS * w
    nk = NA_UNION_ROWS * w
    step_r0 = pl.program_id(2) * (units * NA_UNIT_ROWS)

    def first_key_row(u):
        return jnp.clip(step_r0 + u * NA_UNIT_ROWS - NA_KH // 2, 0, n_rows - NA_UNION_ROWS)

    def scores(u, slot):
        start = first_key_row(u)
        cfg = (step_r0 + u * NA_UNIT_ROWS - start) // NA_UNIT_ROWS
        q_t = q_ref[0, u * uq:(u + 1) * uq, :].astype(F32).T
        row = lax.broadcasted_iota(jnp.int32, q_t.shape, 0)
        qq = jnp.concatenate([jnp.where(row < NA_HD, q_t, 0.0), jnp.where(row >= NA_HD, q_t, 0.0)],
                             axis=1).astype(BF16)
        koff = pl.multiple_of(start * w, uq)
        s_nb = jnp.dot(k_ref[0, pl.ds(koff, nk), :], qq, preferred_element_type=F32) + bias_ref[0, cfg]
        s_cx = jnp.dot(kc_ref[0], qq, preferred_element_type=F32)
        mt_buf[slot][...] = jnp.maximum(jnp.max(s_nb, axis=0, keepdims=True), jnp.max(s_cx, axis=0, keepdims=True))
        s_buf[slot][0:nk, :] = s_nb
        s_buf[slot][nk:nk + lc, :] = s_cx

    def softmax(s_slot, p_slot):
        m = mt_buf[s_slot][...]
        z = pl.multiple_of(zero_ref[0], 8)
        s_buf[s_slot][pl.ds(z, 8), :] = s_buf[s_slot][pl.ds(z, 8), :]
        ch = NA_SOFTMAX_ROWS
        for r0 in range(0, nk + lc, ch):
            p_buf[p_slot][r0:r0 + ch, :] = jnp.exp2(s_buf[s_slot][r0:r0 + ch, :] - m).astype(BF16)

    def values(u, p_slot):
        pc0 = first_key_row(u) // NA_UNIT_ROWS
        acc = jnp.dot(vtc_ref[0], p_buf[p_slot][nk:nk + lc, :], preferred_element_type=F32)
        for i in range(NA_UNION_ROWS // NA_UNIT_ROWS):
            acc = acc + jnp.dot(vtl_ref[pc0 + i], p_buf[p_slot][i * uq:(i + 1) * uq, :],
                                preferred_element_type=F32)
        o = acc[0:LANES] * (1.0 / acc[LANES:LANES + 1])
        y = jnp.concatenate([o[0:NA_HD, 0:uq], o[NA_HD:LANES, uq:2 * uq]], axis=0)
        o_ref[0, u * uq:(u + 1) * uq, :] = y.T.astype(o_ref.dtype)

    scores(0, 0)
    if units > 1:
        scores(1, 1)
    for u in range(units):
        if u >= 1:
            values(u - 1, (u - 1) % 2)
        if u + 2 < units:
            scores(u + 2, (u + 2) % 3)
        softmax(u % 3, u % 2)
    values(units - 1, (units - 1) % 2)


def _na_attention(qk_l, vt_l, qk_c, vt_c, bias):
    b, length, _ = qk_l.shape
    lc = qk_c.shape[1]
    n_rows = length // GRID_W
    uq = NA_UNIT_ROWS * GRID_W
    nk = NA_UNION_ROWS * GRID_W
    assert n_rows >= NA_UNION_ROWS and n_rows % NA_UNIT_ROWS == 0 and vt_l.shape[2] == uq
    units = min(NA_UNITS, n_rows // NA_UNIT_ROWS)
    npair = NA_HEADS // 2
    n_pieces = length // uq
    tq = units * uq
    return pl.pallas_call(
        functools.partial(_na_kernel, units=units, n_rows=n_rows, lc=lc),
        grid=(b, npair, length // tq),
        in_specs=[pl.BlockSpec(memory_space=pltpu.SMEM),
                  pl.BlockSpec((1, tq, LANES), lambda bi, p, i: (bi, i, npair + p)),
                  pl.BlockSpec((1, length, LANES), lambda bi, p, i: (bi, 0, 2 * npair + p)),
                  pl.BlockSpec((n_pieces, DIFF_VT_ROWS, uq), lambda bi, p, i: (bi, p, 0)),
                  pl.BlockSpec((1, lc, LANES), lambda bi, p, i: (bi, 0, 2 * npair + p)),
                  pl.BlockSpec((1, DIFF_VT_ROWS, lc), lambda bi, p, i: (bi, p, 0)),
                  pl.BlockSpec((1, 3, nk, 2 * uq), lambda bi, p, i: (p, 0, 0, 0))],
        out_specs=pl.BlockSpec((1, tq, LANES), lambda bi, p, i: (bi, i, p)),
        out_shape=jax.ShapeDtypeStruct((b, length, NA_HEADS * NA_HD), BF16),
        scratch_shapes=[*[pltpu.VMEM((nk + lc, 2 * uq), F32)] * 3,
                        *[pltpu.VMEM((1, 2 * uq), F32)] * 3,
                        *[pltpu.VMEM((nk + lc, 2 * uq), BF16)] * 2],
        compiler_params=_params(3, 48),
        name="neighbourhood_attention",
    )(jnp.zeros((1,), jnp.int32), qk_l, qk_l, vt_l, qk_c, vt_c, bias)


def _ctx_attn_kernel(q_ref, k_ref, v_ref, o_ref):
    q = q_ref[0].astype(F32)
    k = k_ref[0]
    v = v_ref[0]
    lane = lax.broadcasted_iota(jnp.int32, q.shape, 1)
    outs = []
    for hh in range(2):
        sel = (lane < NA_HD) if hh == 0 else (lane >= NA_HD)
        qm = jnp.where(sel, q, 0.0).astype(BF16)
        s = lax.dot_general(qm, k, NT_DIMS, preferred_element_type=F32)
        outs.append(_softmax_pv([s], [v]))
    o_ref[0] = jnp.where(lane < NA_HD, outs[0], outs[1]).astype(o_ref.dtype)


def _ctx_attention(qkv_c):
    b, lc, _ = qkv_c.shape
    npair = NA_HEADS // 2
    return pl.pallas_call(
        _ctx_attn_kernel,
        grid=(b, npair),
        in_specs=[pl.BlockSpec((1, lc, LANES), lambda bi, p: (bi, 0, npair + p)),
                  pl.BlockSpec((1, lc, LANES), lambda bi, p: (bi, 0, 2 * npair + p)),
                  pl.BlockSpec((1, lc, LANES), lambda bi, p: (bi, 0, 3 * npair + p))],
        out_specs=pl.BlockSpec((1, lc, LANES), lambda bi, p: (bi, 0, p)),
        out_shape=jax.ShapeDtypeStruct((b, lc, NA_HEADS * NA_HD), BF16),
        compiler_params=_params(2, 32),
        name="ctx_attention",
    )(qkv_c, qkv_c, qkv_c)


def _merge_kernel(oret_ref, gates_ref, odiff_ref, ona_ref, x_ref, g1_ref, lng_ref, lnb_ref,
                  wpa_ref, wpb_ref, wpc_ref, wo_ref, o_ref, *, alpha):
    d = D_MODEL
    ya_parts = []
    for hh in range(RET_HEADS):
        a = hh * RET_DK
        seg = _layer_norm_rows(oret_ref[:, a:a + RET_DK])
        g = gates_ref[:, a:a + RET_DK]
        ya_parts.append((g * _sigmoid(g) * seg).astype(BF16))
    ya_in = jnp.concatenate(ya_parts, axis=1)
    y_a = jnp.dot(ya_in, wpa_ref[...], preferred_element_type=F32)
    y_b = jnp.dot(odiff_ref[...], wpb_ref[...], preferred_element_type=F32)
    y_c = jnp.dot(ona_ref[...], wpc_ref[...], preferred_element_type=F32)
    m = (_sigmoid(gates_ref[:, d:2 * d]) * y_a + _sigmoid(gates_ref[:, 2 * d:3 * d]) * y_b
         + _sigmoid(gates_ref[:, 3 * d:4 * d]) * y_c)
    y = jnp.dot(m.astype(BF16), wo_ref[...], preferred_element_type=F32)
    z = alpha * x_ref[...] + g1_ref[0] * y
    o_ref[...] = _layer_norm_rows(z) * lng_ref[...] + lnb_ref[...]


def _merge(o_ret, gates, o_diff, o_na, x, g1, ln_g, ln_b, w_pa, w_pb, w_pc, w_o, mod_row, alpha):
    r, d = x.shape
    tm = min(MERGE_TILE, r)
    row = lambda i: (i, 0)
    const = lambda i: (0, 0)
    wspec = pl.BlockSpec((d, d), const, pipeline_mode=pl.Buffered(1))
    return pl.pallas_call(
        functools.partial(_merge_kernel, alpha=alpha),
        grid=(r // tm,),
        in_specs=[pl.BlockSpec((tm, d), row), pl.BlockSpec((tm, 4 * d), row), pl.BlockSpec((tm, d), row),
                  pl.BlockSpec((tm, d), row), pl.BlockSpec((tm, d), row),
                  pl.BlockSpec((1, 1, d), lambda i: (mod_row(i, tm), 0, 0)),
                  pl.BlockSpec((1, d), const), pl.BlockSpec((1, d), const),
                  wspec, wspec, wspec, wspec],
        out_specs=pl.BlockSpec((tm, d), row),
        out_shape=jax.ShapeDtypeStruct((r, d), F32),
        compiler_params=_params(1, 52),
        name="merge_out_proj",
    )(o_ret, gates, o_diff, o_na, x, g1, ln_g, ln_b, w_pa, w_pb, w_pc, w_o)


def _ffn_kernel(x_ref, sh_ref, sc_ref, g2_ref, lng_ref, lnb_ref, w1_ref, w2_ref, o_ref, *, alpha, ff_chunk):
    x = x_ref[...]
    h = (_layer_norm_rows(x) * (1.0 + sc_ref[0]) + sh_ref[0]).astype(BF16)
    acc = jnp.zeros(x.shape, F32)
    for cidx in range(D_FF // ff_chunk):
        a = cidx * ff_chunk
        u = jnp.maximum(jnp.dot(h, w1_ref[:, a:a + ff_chunk], preferred_element_type=F32), 0.0)
        acc = acc + jnp.dot((u * u).astype(BF16), w2_ref[a:a + ff_chunk, :], preferred_element_type=F32)
    z = alpha * x + g2_ref[0] * acc
    o_ref[...] = _layer_norm_rows(z) * lng_ref[...] + lnb_ref[...]


def _ffn(x, shift, scale, g2, ln_g, ln_b, w1, w2, mod_row, alpha):
    r, d = x.shape
    tm = min(FFN_TILE, r)
    row = lambda i: (i, 0)
    const = lambda i: (0, 0)
    mspec = pl.BlockSpec((1, 1, d), lambda i: (mod_row(i, tm), 0, 0))
    return pl.pallas_call(
        functools.partial(_ffn_kernel, alpha=alpha, ff_chunk=1024),
        grid=(r // tm,),
        in_specs=[pl.BlockSpec((tm, d), row), mspec, mspec, mspec,
                  pl.BlockSpec((1, d), const), pl.BlockSpec((1, d), const),
                  pl.BlockSpec((d, D_FF), const, pipeline_mode=pl.Buffered(1)),
                  pl.BlockSpec((D_FF, d), const, pipeline_mode=pl.Buffered(1))],
        out_specs=pl.BlockSpec((tm, d), row),
        out_shape=jax.ShapeDtypeStruct((r, d), F32),
        compiler_params=_params(1, 56),
        name="ffn",
    )(x, shift, scale, g2, ln_g, ln_b, w1, w2)


def _retention_tables(length):
    t = jnp.arange(length, dtype=F32)
    inv = ROPE_BASE ** (-jnp.linspace(0.0, 1.0, RET_DK // 2, dtype=F32))
    ang = t[:, None] * inv[None, :]
    return jnp.cos(ang), jnp.sin(ang)


def _axial_tables(length):
    t = jnp.arange(length)
    row = (t // GRID_W).astype(F32)
    col = (t % GRID_W).astype(F32)
    half = DIFF_HD // 2
    inv = ROPE_BASE ** (-jnp.arange(0, half, 2, dtype=F32) / half)
    ang_r = row[:, None] * inv[None, :]
    ang_c = col[:, None] * inv[None, :]
    zero = jnp.zeros_like(ang_r)
    unit_ang = jnp.concatenate([ang_r, ang_r, ang_c, ang_c], axis=-1)
    cos = jnp.cos(unit_ang)
    sin_r, sin_c = jnp.sin(ang_r), jnp.sin(ang_c)
    sin_lo = jnp.concatenate([-sin_r, zero, -sin_c, zero], axis=-1)
    sin_hi = jnp.concatenate([zero, sin_r, zero, sin_c], axis=-1)
    rep = LANES // DIFF_HD
    return tuple(jnp.tile(a, (1, rep)) for a in (cos, sin_lo, sin_hi))


def _na_bias_table(rpb):
    nh, nr, _ = rpb.shape
    w = GRID_W
    wid = 2 * w - 1
    p = jnp.pad(rpb.astype(F32) * math.log2(math.e), ((0, 0), (0, 0), (w - NA_KW, w - NA_KW)))
    y = p[..., ::-1]
    a = jnp.broadcast_to(y[:, :, None, :], (nh, nr, w, wid))
    a = jnp.pad(a, ((0, 0), (0, 0), (0, 0), (0, 1))).reshape(nh, nr, w * (wid + 1))
    a = a[:, :, :w * wid].reshape(nh, nr, w, wid)
    m = a[..., w - 1:]
    j = jnp.arange(w)
    cs = jnp.clip(j - NA_KW // 2, 0, w - NA_KW)
    col_mask = (j[:, None] >= cs[None, :]) & (j[:, None] < cs[None, :] + NA_KW)
    m = jnp.where(col_mask[None, None], m, NEG_BIG)
    un, g_rows = NA_UNION_ROWS, NA_UNIT_ROWS
    mp = jnp.pad(m, ((0, 0), (un, un), (0, 0), (0, 0)), constant_values=NEG_BIG)
    mp = mp.reshape(nh // 2, 2, nr + 2 * un, w, w)
    a = jnp.arange(un)
    per_cfg = []
    for cfg in range(3):
        per_col = []
        for hh in range(2):
            for g in range(g_rows):
                q_row = cfg * g_rows + g
                win0 = (0, g, un - NA_KH)[cfg]
                lo = NA_KH - 1 - q_row + un
                in_win = (a >= win0) & (a < win0 + NA_KH)
                per_col.append(jnp.where(in_win[None, :, None, None], mp[:, hh, lo:lo + un], NEG_BIG))
        per_cfg.append(jnp.concatenate(per_col, axis=-1))
    return jnp.stack(per_cfg, axis=1).reshape(nh // 2, 3, un * w, 2 * g_rows * w)


def _column_scale():
    d = D_MODEL
    cs = jnp.ones((IN_WIDTH,), F32)
    cs = cs.at[d:2 * d].set(RET_DK ** -0.5)
    cs = cs.at[4 * d:5 * d].set(DIFF_HD ** -0.5 * math.log2(math.e))
    cs = cs.at[7 * d:8 * d].set(NA_HD ** -0.5 * math.log2(math.e))
    return cs.reshape(1, IN_WIDTH)


def _project_all(h, w, colscale, tm, rope, ret_tables, ax_tables, tiles_per_batch, tk_out, na_tk_out, plain_na_v):
    if rope:
        ret_qk = _proj(h, w, colscale, tm, 0, 1, 2, BF16, "rope_ret", ret_tables, tiles_per_batch)
        diff_qk = _proj(h, w, colscale, tm, 4, 5, 2, BF16, "rope_diff", ax_tables, tiles_per_batch)
    else:
        ret_qk = _proj(h, w, colscale, tm, 0, 1, 2, BF16)
        diff_qk = _proj(h, w, colscale, tm, 4, 5, 2, BF16)
    v_na = _proj(h, w, colscale, tm, 2, 7, 4 if plain_na_v else 3, BF16)
    gates = _proj(h, w, colscale, tm, 3, 10, 4, F32)
    diff_vt = _proj(h, w, colscale, tm, 6, 7, 1, BF16, tk_out=tk_out)
    na_vt = _proj(h, w, colscale, tm, 9, 10, 1, BF16, tk_out=na_tk_out)
    return ret_qk, diff_qk, v_na, gates, diff_vt, na_vt


def kernel(x, c, ctx, c_ctx, w_mod, b_mod, w_in, ret_decay_f, ret_decay_b, diff_lq1, diff_lk1, diff_lq2, diff_lk2,
           diff_subln_g, na_rpb, w_pa, w_pb, w_pc, w_o, ln1_g, ln1_b, w_ff1, w_ff2, ln2_g, ln2_b):
    b, length, d = x.shape
    lc = ctx.shape[1]
    depth = w_mod.shape[0]
    assert d == D_MODEL and b < 8 and length % min(ROW_TILE, length) == 0 and length % GRID_W == 0
    assert lc % RET_CHUNK == 0 or lc < RET_CHUNK
    alpha = (2.0 * depth) ** 0.25

    cvec = jnp.zeros((8, d), F32).at[:b].set(c).at[b].set(c_ctx)
    mods = _modulation(cvec, w_mod, b_mod)

    lat_row = lambda i, tm: (i * tm) // length
    ctx_row = lambda i, tm: b
    colscale = _column_scale()
    ret_tables = _retention_tables(length)
    ax_tables = _axial_tables(length)

    x_l = x.reshape(b * length, d)
    x_c = ctx.reshape(b * lc, d)
    zero_state = jnp.zeros((b, RET_HEADS, RET_DK, RET_DK), F32)
    for l in range(depth):
        with_ctx_out = l < depth - 1
        mod = [mods[l, :, k * d:(k + 1) * d].reshape(8, 1, d) for k in range(6)]
        sh1, sc1, g1, sh2, sc2, g2 = mod
        w = (w_in, l)
        wpa, wpb, wpc, wo = (a[l].astype(BF16) for a in (w_pa, w_pb, w_pc, w_o))
        w1, w2 = w_ff1[l].astype(BF16), w_ff2[l].astype(BF16)
        lng1, lnb1 = ln1_g[l].reshape(1, d), ln1_b[l].reshape(1, d)
        lng2, lnb2 = ln2_g[l].reshape(1, d), ln2_b[l].reshape(1, d)
        lam_init = 0.8 - 0.6 * math.exp(-0.3 * l)
        lam4 = jnp.stack([diff_lq1[l], diff_lk1[l], diff_lq2[l], diff_lk2[l]]).astype(F32)
        subln = diff_subln_g[l].reshape(1, DIFF_DV).astype(F32)
        bias = _na_bias_table(na_rpb[l])

        h_l = _ln_mod(x_l, sh1, sc1, lat_row)
        h_c = _ln_mod(x_c, sh1, sc1, ctx_row)
        tm_l = min(ROW_TILE, length)
        rqk_l, dqk_l, vna_l, gates_l, dvt_l, nvt_l = _project_all(
            h_l, w, colscale, tm_l, True, ret_tables, ax_tables, length // tm_l, min(DIFF_TILE, length),
            NA_UNIT_ROWS * GRID_W, False)
        rqk_c, dqk_c, vna_c, gates_c, dvt_c, nvt_c = _project_all(
            h_c, w, colscale, min(ROW_TILE, b * lc), False, (), (), 1, lc, lc, True)
        sh3 = lambda a, n: a.reshape(b, n, a.shape[-1])

        decays = _retention_decays(ret_decay_f[l], ret_decay_b[l], min(RET_CHUNK, lc))
        rqk_l3, vna_l3, rqk_c3, vna_c3 = sh3(rqk_l, length), sh3(vna_l, length), sh3(rqk_c, lc), sh3(vna_c, lc)
        o_cf, s_cf = _retention(decays, rqk_c3, vna_c3, zero_state, None, False)
        o_rc, s_cb = _retention(decays, rqk_c3, vna_c3, zero_state, o_cf, True)
        o_lf, _ = _retention(decays, rqk_l3, vna_l3, s_cf, None, False)
        o_rl, _ = _retention(decays, rqk_l3, vna_l3, s_cb, o_lf, True)

        dqk_l3, dqk_c3 = sh3(dqk_l, length), sh3(dqk_c, lc)
        o_dl = _diff_attention(lam4, subln, dqk_l3, (dqk_l3, dvt_l), (dqk_c3, dvt_c), lam_init)
        o_nl = _na_attention(vna_l3, nvt_l, vna_c3, nvt_c, bias)

        x_l = _merge(o_rl.reshape(b * length, d), gates_l, o_dl.reshape(b * length, d), o_nl.reshape(b * length, d),
                     x_l, g1, lng1, lnb1, wpa, wpb, wpc, wo, lat_row, alpha)
        x_l = _ffn(x_l, sh2, sc2, g2, lng2, lnb2, w1, w2, lat_row, alpha)
        if with_ctx_out:
            o_dc = _diff_attention(lam4, subln, dqk_c3, None, (dqk_c3, dvt_c), lam_init)
            o_nc = _ctx_attention(vna_c3)
            x_c = _merge(o_rc.reshape(b * lc, d), gates_c, o_dc.reshape(b * lc, d), o_nc.reshape(b * lc, d),
                         x_c, g1, lng1, lnb1, wpa, wpb, wpc, wo, ctx_row, alpha)
            x_c = _ffn(x_c, sh2, sc2, g2, lng2, lnb2, w1, w2, ctx_row, alpha)
    return x_l.reshape(b, length, d)
```

```python
import functools
import math

import jax
import jax.numpy as jnp
from jax import lax
from jax.experimental import pallas as pl
from jax.experimental.pallas import tpu as pltpu

F32 = jnp.float32
BF16 = jnp.bfloat16

D_MODEL = 1024
GRID_W = 64
RET_HEADS, RET_DK = 4, 256
DIFF_HEADS, DIFF_HD, DIFF_DV = 8, 64, 128
NA_HEADS, NA_HD, NA_KH, NA_KW = 16, 64, 8, 16
D_FF = 4 * D_MODEL
ROPE_BASE = 10000.0
LN_EPS = 1e-6
SUBLN_EPS = 1e-5
IN_WIDTH = 13 * D_MODEL
NEG_BIG = -1e30

LANES = 128
VMEM_LIMIT_CAP = 56 << 20

ROW_TILE = 2048
PROJ_TN = 1024
LN_TILE = 512
MERGE_TILE = 512
FFN_TILE = 1024
RET_CHUNK = 256
RET_BLOCK = 1024
DIFF_TQ = 256
DIFF_BLOCKS = 4
DIFF_FINISH_SLACK = 2
DIFF_TILE = 512
DIFF_VT_ROWS = DIFF_DV + 16
DIFF_SOFTMAX_ROWS = 64
NA_UNIT_ROWS = 4
NA_UNION_ROWS = 12
NA_UNITS = 16
NA_SOFTMAX_ROWS = 64

NT_DIMS = (((1,), (1,)), ((), ()))
TN_DIMS = (((0,), (0,)), ((), ()))


def _params(n_axes, vmem_mb):
    return pltpu.CompilerParams(dimension_semantics=("arbitrary",) * n_axes,
                                vmem_limit_bytes=min(vmem_mb << 20, VMEM_LIMIT_CAP))


def _sigmoid(x):
    return 1.0 / (1.0 + jnp.exp(-x))


def _layer_norm_rows(x):
    mu = jnp.mean(x, axis=-1, keepdims=True)
    xc = x - mu
    var = jnp.mean(xc * xc, axis=-1, keepdims=True)
    return xc * lax.rsqrt(var + LN_EPS)


def _mod_kernel(c_ref, w_ref, b_ref, o_ref):
    c = c_ref[...]
    a = (c * _sigmoid(c)).astype(BF16)
    o_ref[0] = jnp.dot(a, w_ref[0].astype(BF16), preferred_element_type=F32) + b_ref[0]


def _modulation(cvec, w_mod, b_mod):
    depth = w_mod.shape[0]
    d = D_MODEL
    return pl.pallas_call(
        _mod_kernel,
        grid=(depth, 6),
        in_specs=[pl.BlockSpec((8, d), lambda l, j: (0, 0)),
                  pl.BlockSpec((1, d, d), lambda l, j: (l, 0, j)),
                  pl.BlockSpec((1, 1, d), lambda l, j: (l, 0, j))],
        out_specs=pl.BlockSpec((1, 8, d), lambda l, j: (l, 0, j)),
        out_shape=jax.ShapeDtypeStruct((depth, 8, 6 * d), F32),
        compiler_params=_params(2, 32),
        name="modulation",
    )(cvec, w_mod, b_mod.reshape(depth, 1, 6 * d))


def _ln_mod_kernel(x_ref, sh_ref, sc_ref, o_ref):
    y = _layer_norm_rows(x_ref[...])
    o_ref[...] = (y * (1.0 + sc_ref[0]) + sh_ref[0]).astype(o_ref.dtype)


def _ln_mod(x, shift, scale, mod_row):
    r, d = x.shape
    tm = min(LN_TILE, r)
    return pl.pallas_call(
        _ln_mod_kernel,
        grid=(r // tm,),
        in_specs=[pl.BlockSpec((tm, d), lambda i: (i, 0)),
                  pl.BlockSpec((1, 1, d), lambda i: (mod_row(i, tm), 0, 0)),
                  pl.BlockSpec((1, 1, d), lambda i: (mod_row(i, tm), 0, 0))],
        out_specs=pl.BlockSpec((tm, d), lambda i: (i, 0)),
        out_shape=jax.ShapeDtypeStruct((r, d), BF16),
        compiler_params=_params(1, 32),
        name="ln_modulate",
    )(x, shift, scale)


def _proj_kernel(h_ref, w_ref, cs_ref, *rest, mode, transposed, tn, tk_out):
    o_ref = rest[-1]
    acc = jnp.dot(h_ref[...], w_ref[0].astype(BF16), preferred_element_type=F32) * cs_ref[...]
    if mode == "rope_ret":
        cos = rest[0][...]
        sin = rest[1][...]
        half = RET_DK // 2
        for hh in range(tn // RET_DK):
            a = hh * RET_DK
            x1 = acc[:, a:a + half]
            x2 = acc[:, a + half:a + RET_DK]
            o_ref[:, a:a + half] = (x1 * cos - x2 * sin).astype(o_ref.dtype)
            o_ref[:, a + half:a + RET_DK] = (x1 * sin + x2 * cos).astype(o_ref.dtype)
    elif mode == "rope_diff":
        cos = rest[0][...]
        sin_lo = rest[1][...]
        sin_hi = rest[2][...]
        q16 = DIFF_HD // 4
        for g in range(tn // LANES):
            x = acc[:, g * LANES:(g + 1) * LANES]
            y = x * cos + pltpu.roll(x, LANES - q16, 1) * sin_lo + pltpu.roll(x, q16, 1) * sin_hi
            o_ref[:, g * LANES:(g + 1) * LANES] = y.astype(o_ref.dtype)
    elif transposed:
        acc_t = acc.T
        ones = jnp.ones((DIFF_VT_ROWS - DIFF_DV, tk_out), o_ref.dtype)
        for a in range(o_ref.shape[0]):
            for hh in range(tn // DIFF_DV):
                r0 = hh * DIFF_VT_ROWS
                o_ref[a, r0:r0 + DIFF_DV, :] = acc_t[hh * DIFF_DV:(hh + 1) * DIFF_DV,
                                                     a * tk_out:(a + 1) * tk_out].astype(o_ref.dtype)
                o_ref[a, r0 + DIFF_DV:r0 + DIFF_VT_ROWS, :] = ones
    else:
        o_ref[...] = acc.astype(o_ref.dtype)


def _proj(h, w, colscale, tm, first, rest0, ntiles, out_dtype, mode="plain", tables=(), tiles_per_batch=1,
          tk_out=None):
    w_all, layer = w
    r, d = h.shape
    tn = PROJ_TN
    ncols = ntiles * tn
    assert r % tm == 0
    transposed = tk_out is not None
    wcol = lambda j: jnp.where(j == 0, first, rest0 + j - 1)
    in_specs = [pl.BlockSpec((tm, d), lambda i, j: (i, 0)),
                pl.BlockSpec((1, d, tn), lambda i, j: (layer, 0, wcol(j))),
                pl.BlockSpec((1, tn), lambda i, j: (0, wcol(j)))]
    for _ in tables:
        in_specs.append(pl.BlockSpec((tm, LANES), lambda i, j: (i % tiles_per_batch, 0)))
    if transposed:
        assert tm % tk_out == 0
        na = tm // tk_out
        rows_out = tn // DIFF_DV * DIFF_VT_ROWS
        out_specs = pl.BlockSpec((na, rows_out, tk_out), lambda i, j: (i, j, 0))
        out_shape = jax.ShapeDtypeStruct((r // tk_out, ncols // DIFF_DV * DIFF_VT_ROWS, tk_out), out_dtype)
    else:
        out_specs = pl.BlockSpec((tm, tn), lambda i, j: (i, j))
        out_shape = jax.ShapeDtypeStruct((r, ncols), out_dtype)
    return pl.pallas_call(
        functools.partial(_proj_kernel, mode=mode, transposed=transposed, tn=tn, tk_out=tk_out),
        grid=(r // tm, ntiles),
        in_specs=in_specs,
        out_specs=out_specs,
        out_shape=out_shape,
        compiler_params=_params(2, 52),
        name="in_proj_" + mode + ("_t" if transposed else ""),
    )(h, w_all, colscale, *tables)


def _ret_tables_kernel(dec_ref, intra_ref, qd_ref, kd_ref, cd_ref, *, chunk):
    direction = pl.program_id(0)
    head = pl.program_id(1)
    cf = float(chunk)
    dv = jnp.full((1, 1), dec_ref[direction, head], F32)
    lg = jnp.minimum(dv, 0.0) - jnp.log(1.0 + jnp.exp(-jnp.abs(dv)))
    rev = direction == 1
    ii = lax.broadcasted_iota(jnp.int32, (chunk, chunk), 0).astype(F32)
    jj = lax.broadcasted_iota(jnp.int32, (chunk, chunk), 1).astype(F32)
    dist = jnp.where(rev, jj - ii, ii - jj)
    intra_ref[0, 0] = jnp.where(dist >= 0.0, jnp.exp(lg * jnp.maximum(dist, 0.0)), 0.0)
    i1 = lax.broadcasted_iota(jnp.int32, (chunk, RET_DK), 0).astype(F32)
    qd_ref[0, 0] = jnp.exp(lg * jnp.where(rev, cf - i1, i1 + 1.0))
    kd_ref[0, 0] = jnp.exp(lg * jnp.where(rev, i1, cf - 1.0 - i1))
    cd_ref[0, 0] = jnp.exp(jnp.broadcast_to(lg, (8, RET_DK)) * cf)


def _retention_decays(dec_f, dec_b, chunk):
    hh, dk = RET_HEADS, RET_DK
    blk = lambda *shape: pl.BlockSpec((1, 1) + shape, lambda d, h: (d, h, 0, 0))
    return pl.pallas_call(
        functools.partial(_ret_tables_kernel, chunk=chunk),
        grid=(2, hh),
        in_specs=[pl.BlockSpec(memory_space=pltpu.SMEM)],
        out_specs=[blk(chunk, chunk), blk(chunk, dk), blk(chunk, dk), blk(8, dk)],
        out_shape=[jax.ShapeDtypeStruct((2, hh, chunk, chunk), F32),
                   jax.ShapeDtypeStruct((2, hh, chunk, dk), F32),
                   jax.ShapeDtypeStruct((2, hh, chunk, dk), F32),
                   jax.ShapeDtypeStruct((2, hh, 8, dk), F32)],
        compiler_params=_params(2, 32),
        name="retention_tables",
    )(jnp.stack([dec_f, dec_b]).astype(F32))


def _ret_kernel(intra_ref, qd_ref, kd_ref, cd_ref, qk_ref, v_ref, s0_ref, *rest, reverse, has_prev, chunk, n_chunks):
    if has_prev:
        prev_ref, o_ref, sfin_ref, s_scr = rest
    else:
        o_ref, sfin_ref, s_scr = rest
    first = pl.program_id(1) == 0
    dk = RET_DK
    order = range(n_chunks - 1, -1, -1) if reverse else range(n_chunks)
    for h in range(RET_HEADS):
        cols = slice(h * dk, (h + 1) * dk)
        kcols = slice((RET_HEADS + h) * dk, (RET_HEADS + h + 1) * dk)
        s = jnp.where(first, s0_ref[0, h], s_scr[h])
        for c in order:
            rows = slice(c * chunk, (c + 1) * chunk)
            q = qk_ref[0, rows, cols]
            k = qk_ref[0, rows, kcols]
            v = v_ref[0, rows, cols]
            att = lax.dot_general(q, k, NT_DIMS, preferred_element_type=F32) * intra_ref[0, h]
            qd = (q.astype(F32) * qd_ref[0, h]).astype(BF16)
            o = (jnp.dot(att.astype(BF16), v, preferred_element_type=F32)
                 + jnp.dot(qd, s.astype(BF16), preferred_element_type=F32))
            kd = (k.astype(F32) * kd_ref[0, h]).astype(BF16)
            s = s * cd_ref[0, h, 0:1, :] + lax.dot_general(kd, v, TN_DIMS, preferred_element_type=F32)
            if has_prev:
                o = o + prev_ref[0, rows, cols]
            o_ref[0, rows, cols] = o
        s_scr[h] = s
        sfin_ref[0, h] = s


def _retention(tables, qk, v_arr, s0, prev, reverse):
    b, length, _ = qk.shape
    hh, dk = RET_HEADS, RET_DK
    width = hh * dk
    c = tables[0].shape[-1]
    rb = min(RET_BLOCK, length)
    n = length // rb
    direction = 1 if reverse else 0
    pos = (lambda i: n - 1 - i) if reverse else (lambda i: i)
    tab = lambda *shape: pl.BlockSpec((1, hh) + shape, lambda bi, i: (direction, 0, 0, 0))
    in_specs = [tab(c, c), tab(c, dk), tab(c, dk), tab(8, dk),
                pl.BlockSpec((1, rb, 2 * width), lambda bi, i: (bi, pos(i), 0)),
                pl.BlockSpec((1, rb, width), lambda bi, i: (bi, pos(i), 0)),
                pl.BlockSpec((1, hh, dk, dk), lambda bi, i: (bi, 0, 0, 0))]
    args = [*tables, qk, v_arr, s0]
    if prev is not None:
        in_specs.append(pl.BlockSpec((1, rb, width), lambda bi, i: (bi, pos(i), 0)))
        args.append(prev)
    return pl.pallas_call(
        functools.partial(_ret_kernel, reverse=reverse, has_prev=prev is not None, chunk=c, n_chunks=rb // c),
        grid=(b, n),
        in_specs=in_specs,
        out_specs=[pl.BlockSpec((1, rb, width), lambda bi, i: (bi, pos(i), 0)),
                   pl.BlockSpec((1, hh, dk, dk), lambda bi, i: (bi, 0, 0, 0))],
        out_shape=[jax.ShapeDtypeStruct((b, length, width), F32),
                   jax.ShapeDtypeStruct((b, hh, dk, dk), F32)],
        scratch_shapes=[pltpu.VMEM((hh, dk, dk), F32)],
        compiler_params=_params(2, 48),
        name="retention_bwd" if reverse else "retention_fwd",
    )(*args)


def _diff_kernel(zero_ref, lam_ref, g_ref, q_ref, *rest, tq, nb, n_lat, tile, lc, lam_init):
    if n_lat:
        kl_ref, vtl_ref = rest[:2]
        rest = rest[2:]
    kc_ref, vtc_ref, o_ref, qq_scr = rest[:4]
    s_buf, mt_buf, p_buf, al_buf = rest[4:8], rest[8:12], rest[12:14], rest[14:16]
    m_scr, acc_scr = rest[16:]
    n = n_lat + 1
    rows_of = lambda j: lc if j == n_lat else tile

    def scores(g, slot):
        blk, j = divmod(g, n)
        if j == 0:
            q = q_ref[0, blk * tq:(blk + 1) * tq, :].astype(F32)
            lane = lax.broadcasted_iota(jnp.int32, q.shape, 1)
            qq_scr[blk, 0:tq, :] = jnp.where(lane < DIFF_HD, q, 0.0).astype(BF16)
            qq_scr[blk, tq:2 * tq, :] = jnp.where(lane >= DIFF_HD, q, 0.0).astype(BF16)
            m_scr[blk] = jnp.full(m_scr.shape[1:], NEG_BIG, F32)
        k = kc_ref[0] if j == n_lat else kl_ref[0, j * tile:(j + 1) * tile, :]
        s = lax.dot_general(k, qq_scr[blk], NT_DIMS, preferred_element_type=F32)
        mt_buf[slot][...] = jnp.max(s, axis=0, keepdims=True)
        s_buf[slot][0:k.shape[0], :] = s

    def softmax(g, anchor):
        blk, j = divmod(g, n)
        s_slot, p_slot, rows = g % 4, g % 2, rows_of(j)
        m_old = m_scr[blk]
        if anchor is not None:
            m_old = jnp.maximum(m_old, anchor + NEG_BIG)
        m_new = jnp.maximum(m_old, mt_buf[s_slot][...])
        al_buf[p_slot][...] = jnp.exp2(m_old - m_new)
        m_scr[blk] = m_new
        z = pl.multiple_of(zero_ref[0], 8)
        s_buf[s_slot][pl.ds(z, 8), :] = s_buf[s_slot][pl.ds(z, 8), :]
        ch = DIFF_SOFTMAX_ROWS
        for r0 in range(0, rows, ch):
            p_buf[p_slot][r0:r0 + ch, :] = jnp.exp2(s_buf[s_slot][r0:r0 + ch, :] - m_new).astype(BF16)

    def values(g):
        blk, j = divmod(g, n)
        slot = g % 2
        vt = vtc_ref[0] if j == n_lat else vtl_ref[j]
        pv = jnp.dot(vt, p_buf[slot][0:vt.shape[1], :], preferred_element_type=F32)
        acc_scr[blk] = pv if j == 0 else acc_scr[blk] * al_buf[slot][...] + pv

    def finish(blk):
        acc = acc_scr[blk]
        o = acc[0:DIFF_DV] * (1.0 / acc[DIFF_DV:DIFF_DV + 1])
        lam4 = lam_ref[...]
        s1 = jnp.sum(lam4[0:1] * lam4[1:2], axis=-1, keepdims=True)
        s2 = jnp.sum(lam4[2:3] * lam4[3:4], axis=-1, keepdims=True)
        lam = jnp.exp(s1) - jnp.exp(s2) + lam_init
        y = (o[:, :tq] - lam * o[:, tq:]).T
        ms = jnp.mean(y * y, axis=-1, keepdims=True)
        y = y * lax.rsqrt(ms + SUBLN_EPS) * g_ref[...] * (1.0 - lam_init)
        o_ref[0, blk * tq:(blk + 1) * tq, :] = y.astype(o_ref.dtype)
        d = jnp.concatenate([y[0:1, :]] * (2 * tq // LANES), axis=1) * 0.0
        return jnp.where(jnp.isnan(d), 0.0, d)

    total = nb * n
    scores(0, 0)
    if total > 1:
        scores(1, 1)
    anchor, anchor_at = None, -1
    for g in range(total):
        if g >= 1:
            values(g - 1)
            if g % n == 0:
                anchor, anchor_at = finish(g // n - 1), min(g + DIFF_FINISH_SLACK, total - 1)
        if g + 2 < total:
            scores(g + 2, (g + 2) % 4)
        softmax(g, anchor if g == anchor_at else None)
    values(total - 1)
    finish(nb - 1)


def _diff_attention(lam4, subln_g, q_arr, lat, ctx, lam_init):
    b, lq, _ = q_arr.shape
    hh = DIFF_HEADS
    tq = min(DIFF_TQ, lq)
    qk_c, vt_c = ctx
    lc = qk_c.shape[1]
    assert vt_c.shape == (b, hh * DIFF_VT_ROWS, lc)
    nb = math.gcd(DIFF_BLOCKS, lq // tq)
    in_specs = [pl.BlockSpec(memory_space=pltpu.SMEM),
                pl.BlockSpec((4, DIFF_HD), lambda bi, h, i: (0, 0)),
                pl.BlockSpec((1, DIFF_DV), lambda bi, h, i: (0, 0)),
                pl.BlockSpec((1, nb * tq, LANES), lambda bi, h, i: (bi, i, h))]
    args = [jnp.zeros((1,), jnp.int32), lam4, subln_g, q_arr]
    n_lat, tile = 0, lc
    if lat is not None:
        qk_l, vt_l = lat
        length = qk_l.shape[1]
        tile = vt_l.shape[2]
        n_lat = length // tile
        assert lc <= tile
        in_specs += [pl.BlockSpec((1, length, LANES), lambda bi, h, i: (bi, 0, hh + h)),
                     pl.BlockSpec((n_lat, DIFF_VT_ROWS, tile), lambda bi, h, i: (bi, h, 0))]
        args += [qk_l, vt_l]
    in_specs += [pl.BlockSpec((1, lc, LANES), lambda bi, h, i: (bi, 0, hh + h)),
                 pl.BlockSpec((1, DIFF_VT_ROWS, lc), lambda bi, h, i: (bi, h, 0))]
    args += [qk_c, vt_c]
    return pl.pallas_call(
        functools.partial(_diff_kernel, tq=tq, nb=nb, n_lat=n_lat, tile=tile, lc=lc, lam_init=lam_init),
        grid=(b, hh, lq // (nb * tq)),
        in_specs=in_specs,
        out_specs=pl.BlockSpec((1, nb * tq, LANES), lambda bi, h, i: (bi, i, h)),
        out_shape=jax.ShapeDtypeStruct((b, lq, hh * DIFF_DV), BF16),
        scratch_shapes=[pltpu.VMEM((nb, 2 * tq, LANES), BF16),
                        *[pltpu.VMEM((tile, 2 * tq), F32)] * 4,
                        *[pltpu.VMEM((1, 2 * tq), F32)] * 4,
                        *[pltpu.VMEM((tile, 2 * tq), BF16)] * 2,
                        *[pltpu.VMEM((1, 2 * tq), F32)] * 2,
                        pltpu.VMEM((nb, 1, 2 * tq), F32),
                        pltpu.VMEM((nb, DIFF_VT_ROWS, 2 * tq), F32)],
        compiler_params=_params(3, 40),
        name="diff_attention" if lat is not None else "diff_attention_ctx",
    )(*args)


def _softmax_pv(s_list, v_list):
    m = s_list[0].max(axis=-1, keepdims=True)
    for s in s_list[1:]:
        m = jnp.maximum(m, s.max(axis=-1, keepdims=True))
    den = 0.0
    o = 0.0
    for s, v in zip(s_list, v_list):
        p = jnp.exp2(s - m)
        den = den + jnp.sum(p, axis=-1, keepdims=True)
        o = o + jnp.dot(p.astype(BF16), v, preferred_element_type=F32)
    return o * (1.0 / den)


def _na_kernel(zero_ref, q_ref, k_ref, vtl_ref, kc_ref, vtc_ref, bias_ref, o_ref, *scr, units, n_rows, lc):
    s_buf, mt_buf, p_buf = scr[0:3], scr[3:6], scr[6:8]
    w = GRID_W
    uq = NA_UNIT_ROWS * w
    nk = NA_UNION_ROWS * w
    step_r0 = pl.program_id(2) * (units * NA_UNIT_ROWS)

    def first_key_row(u):
        return jnp.clip(step_r0 + u * NA_UNIT_ROWS - NA_KH // 2, 0, n_rows - NA_UNION_ROWS)

    def scores(u, slot):
        start = first_key_row(u)
        cfg = (step_r0 + u * NA_UNIT_ROWS - start) // NA_UNIT_ROWS
        q_t = q_ref[0, u * uq:(u + 1) * uq, :].astype(F32).T
        row = lax.broadcasted_iota(jnp.int32, q_t.shape, 0)
        qq = jnp.concatenate([jnp.where(row < NA_HD, q_t, 0.0), jnp.where(row >= NA_HD, q_t, 0.0)],
                             axis=1).astype(BF16)
        koff = pl.multiple_of(start * w, uq)
        s_nb = jnp.dot(k_ref[0, pl.ds(koff, nk), :], qq, preferred_element_type=F32) + bias_ref[0, cfg]
        s_cx = jnp.dot(kc_ref[0], qq, preferred_element_type=F32)
        mt_buf[slot][...] = jnp.maximum(jnp.max(s_nb, axis=0, keepdims=True), jnp.max(s_cx, axis=0, keepdims=True))
        s_buf[slot][0:nk, :] = s_nb
        s_buf[slot][nk:nk + lc, :] = s_cx

    def softmax(s_slot, p_slot):
        m = mt_buf[s_slot][...]
        z = pl.multiple_of(zero_ref[0], 8)
        s_buf[s_slot][pl.ds(z, 8), :] = s_buf[s_slot][pl.ds(z, 8), :]
        ch = NA_SOFTMAX_ROWS
        for r0 in range(0, nk + lc, ch):
            p_buf[p_slot][r0:r0 + ch, :] = jnp.exp2(s_buf[s_slot][r0:r0 + ch, :] - m).astype(BF16)

    def values(u, p_slot):
        pc0 = first_key_row(u) // NA_UNIT_ROWS
        acc = jnp.dot(vtc_ref[0], p_buf[p_slot][nk:nk + lc, :], preferred_element_type=F32)
        for i in range(NA_UNION_ROWS // NA_UNIT_ROWS):
            acc = acc + jnp.dot(vtl_ref[pc0 + i], p_buf[p_slot][i * uq:(i + 1) * uq, :],
                                preferred_element_type=F32)
        o = acc[0:LANES] * (1.0 / acc[LANES:LANES + 1])
        y = jnp.concatenate([o[0:NA_HD, 0:uq], o[NA_HD:LANES, uq:2 * uq]], axis=0)
        o_ref[0, u * uq:(u + 1) * uq, :] = y.T.astype(o_ref.dtype)

    scores(0, 0)
    if units > 1:
        scores(1, 1)
    for u in range(units):
        if u >= 1:
            values(u - 1, (u - 1) % 2)
        if u + 2 < units:
            scores(u + 2, (u + 2) % 3)
        softmax(u % 3, u % 2)
    values(units - 1, (units - 1) % 2)


def _na_attention(qk_l, vt_l, qk_c, vt_c, bias):
    b, length, _ = qk_l.shape
    lc = qk_c.shape[1]
    n_rows = length // GRID_W
    uq = NA_UNIT_ROWS * GRID_W
    nk = NA_UNION_ROWS * GRID_W
    assert n_rows >= NA_UNION_ROWS and n_rows % NA_UNIT_ROWS == 0 and vt_l.shape[2] == uq
    units = min(NA_UNITS, n_rows // NA_UNIT_ROWS)
    npair = NA_HEADS // 2
    n_pieces = length // uq
    tq = units * uq
    return pl.pallas_call(
        functools.partial(_na_kernel, units=units, n_rows=n_rows, lc=lc),
        grid=(b, npair, length // tq),
        in_specs=[pl.BlockSpec(memory_space=pltpu.SMEM),
                  pl.BlockSpec((1, tq, LANES), lambda bi, p, i: (bi, i, npair + p)),
                  pl.BlockSpec((1, length, LANES), lambda bi, p, i: (bi, 0, 2 * npair + p)),
                  pl.BlockSpec((n_pieces, DIFF_VT_ROWS, uq), lambda bi, p, i: (bi, p, 0)),
                  pl.BlockSpec((1, lc, LANES), lambda bi, p, i: (bi, 0, 2 * npair + p)),
                  pl.BlockSpec((1, DIFF_VT_ROWS, lc), lambda bi, p, i: (bi, p, 0)),
                  pl.BlockSpec((1, 3, nk, 2 * uq), lambda bi, p, i: (p, 0, 0, 0))],
        out_specs=pl.BlockSpec((1, tq, LANES), lambda bi, p, i: (bi, i, p)),
        out_shape=jax.ShapeDtypeStruct((b, length, NA_HEADS * NA_HD), BF16),
        scratch_shapes=[*[pltpu.VMEM((nk + lc, 2 * uq), F32)] * 3,
                        *[pltpu.VMEM((1, 2 * uq), F32)] * 3,
                        *[pltpu.VMEM((nk + lc, 2 * uq), BF16)] * 2],
        compiler_params=_params(3, 48),
        name="neighbourhood_attention",
    )(jnp.zeros((1,), jnp.int32), qk_l, qk_l, vt_l, qk_c, vt_c, bias)


def _ctx_attn_kernel(q_ref, k_ref, v_ref, o_ref):
    q = q_ref[0].astype(F32)
    k = k_ref[0]
    v = v_ref[0]
    lane = lax.broadcasted_iota(jnp.int32, q.shape, 1)
    outs = []
    for hh in range(2):
        sel = (lane < NA_HD) if hh == 0 else (lane >= NA_HD)
        qm = jnp.where(sel, q, 0.0).astype(BF16)
        s = lax.dot_general(qm, k, NT_DIMS, preferred_element_type=F32)
        outs.append(_softmax_pv([s], [v]))
    o_ref[0] = jnp.where(lane < NA_HD, outs[0], outs[1]).astype(o_ref.dtype)


def _ctx_attention(qkv_c):
    b, lc, _ = qkv_c.shape
    npair = NA_HEADS // 2
    return pl.pallas_call(
        _ctx_attn_kernel,
        grid=(b, npair),
        in_specs=[pl.BlockSpec((1, lc, LANES), lambda bi, p: (bi, 0, npair + p)),
                  pl.BlockSpec((1, lc, LANES), lambda bi, p: (bi, 0, 2 * npair + p)),
                  pl.BlockSpec((1, lc, LANES), lambda bi, p: (bi, 0, 3 * npair + p))],
        out_specs=pl.BlockSpec((1, lc, LANES), lambda bi, p: (bi, 0, p)),
        out_shape=jax.ShapeDtypeStruct((b, lc, NA_HEADS * NA_HD), BF16),
        compiler_params=_params(2, 32),
        name="ctx_attention",
    )(qkv_c, qkv_c, qkv_c)


def _merge_kernel(oret_ref, gates_ref, odiff_ref, ona_ref, x_ref, g1_ref, lng_ref, lnb_ref,
                  wpa_ref, wpb_ref, wpc_ref, wo_ref, o_ref, *, alpha):
    d = D_MODEL
    ya_parts = []
    for hh in range(RET_HEADS):
        a = hh * RET_DK
        seg = _layer_norm_rows(oret_ref[:, a:a + RET_DK])
        g = gates_ref[:, a:a + RET_DK]
        ya_parts.append((g * _sigmoid(g) * seg).astype(BF16))
    ya_in = jnp.concatenate(ya_parts, axis=1)
    y_a = jnp.dot(ya_in, wpa_ref[...], preferred_element_type=F32)
    y_b = jnp.dot(odiff_ref[...], wpb_ref[...], preferred_element_type=F32)
    y_c = jnp.dot(ona_ref[...], wpc_ref[...], preferred_element_type=F32)
    m = (_sigmoid(gates_ref[:, d:2 * d]) * y_a + _sigmoid(gates_ref[:, 2 * d:3 * d]) * y_b
         + _sigmoid(gates_ref[:, 3 * d:4 * d]) * y_c)
    y = jnp.dot(m.astype(BF16), wo_ref[...], preferred_element_type=F32)
    z = alpha * x_ref[...] + g1_ref[0] * y
    o_ref[...] = _layer_norm_rows(z) * lng_ref[...] + lnb_ref[...]


def _merge(o_ret, gates, o_diff, o_na, x, g1, ln_g, ln_b, w_pa, w_pb, w_pc, w_o, mod_row, alpha):
    r, d = x.shape
    tm = min(MERGE_TILE, r)
    row = lambda i: (i, 0)
    const = lambda i: (0, 0)
    wspec = pl.BlockSpec((d, d), const, pipeline_mode=pl.Buffered(1))
    return pl.pallas_call(
        functools.partial(_merge_kernel, alpha=alpha),
        grid=(r // tm,),
        in_specs=[pl.BlockSpec((tm, d), row), pl.BlockSpec((tm, 4 * d), row), pl.BlockSpec((tm, d), row),
                  pl.BlockSpec((tm, d), row), pl.BlockSpec((tm, d), row),
                  pl.BlockSpec((1, 1, d), lambda i: (mod_row(i, tm), 0, 0)),
                  pl.BlockSpec((1, d), const), pl.BlockSpec((1, d), const),
                  wspec, wspec, wspec, wspec],
        out_specs=pl.BlockSpec((tm, d), row),
        out_shape=jax.ShapeDtypeStruct((r, d), F32),
        compiler_params=_params(1, 52),
        name="merge_out_proj",
    )(o_ret, gates, o_diff, o_na, x, g1, ln_g, ln_b, w_pa, w_pb, w_pc, w_o)


def _ffn_kernel(x_ref, sh_ref, sc_ref, g2_ref, lng_ref, lnb_ref, w1_ref, w2_ref, o_ref, *, alpha, ff_chunk):
    x = x_ref[...]
    h = (_layer_norm_rows(x) * (1.0 + sc_ref[0]) + sh_ref[0]).astype(BF16)
    acc = jnp.zeros(x.shape, F32)
    for cidx in range(D_FF // ff_chunk):
        a = cidx * ff_chunk
        u = jnp.maximum(jnp.dot(h, w1_ref[:, a:a + ff_chunk], preferred_element_type=F32), 0.0)
        acc = acc + jnp.dot((u * u).astype(BF16), w2_ref[a:a + ff_chunk, :], preferred_element_type=F32)
    z = alpha * x + g2_ref[0] * acc
    o_ref[...] = _layer_norm_rows(z) * lng_ref[...] + lnb_ref[...]


def _ffn(x, shift, scale, g2, ln_g, ln_b, w1, w2, mod_row, alpha):
    r, d = x.shape
    tm = min(FFN_TILE, r)
    row = lambda i: (i, 0)
    const = lambda i: (0, 0)
    mspec = pl.BlockSpec((1, 1, d), lambda i: (mod_row(i, tm), 0, 0))
    return pl.pallas_call(
        functools.partial(_ffn_kernel, alpha=alpha, ff_chunk=1024),
        grid=(r // tm,),
        in_specs=[pl.BlockSpec((tm, d), row), mspec, mspec, mspec,
                  pl.BlockSpec((1, d), const), pl.BlockSpec((1, d), const),
                  pl.BlockSpec((d, D_FF), const, pipeline_mode=pl.Buffered(1)),
                  pl.BlockSpec((D_FF, d), const, pipeline_mode=pl.Buffered(1))],
        out_specs=pl.BlockSpec((tm, d), row),
        out_shape=jax.ShapeDtypeStruct((r, d), F32),
        compiler_params=_params(1, 56),
        name="ffn",
    )(x, shift, scale, g2, ln_g, ln_b, w1, w2)


def _retention_tables(length):
    t = jnp.arange(length, dtype=F32)
    inv = ROPE_BASE ** (-jnp.linspace(0.0, 1.0, RET_DK // 2, dtype=F32))
    ang = t[:, None] * inv[None, :]
    return jnp.cos(ang), jnp.sin(ang)


def _axial_tables(length):
    t = jnp.arange(length)
    row = (t // GRID_W).astype(F32)
    col = (t % GRID_W).astype(F32)
    half = DIFF_HD // 2
    inv = ROPE_BASE ** (-jnp.arange(0, half, 2, dtype=F32) / half)
    ang_r = row[:, None] * inv[None, :]
    ang_c = col[:, None] * inv[None, :]
    zero = jnp.zeros_like(ang_r)
    unit_ang = jnp.concatenate([ang_r, ang_r, ang_c, ang_c], axis=-1)
    cos = jnp.cos(unit_ang)
    sin_r, sin_c = jnp.sin(ang_r), jnp.sin(ang_c)
    sin_lo = jnp.concatenate([-sin_r, zero, -sin_c, zero], axis=-1)
    sin_hi = jnp.concatenate([zero, sin_r, zero, sin_c], axis=-1)
    rep = LANES // DIFF_HD
    return tuple(jnp.tile(a, (1, rep)) for a in (cos, sin_lo, sin_hi))


def _na_bias_table(rpb):
    nh, nr, _ = rpb.shape
    w = GRID_W
    wid = 2 * w - 1
    p = jnp.pad(rpb.astype(F32) * math.log2(math.e), ((0, 0), (0, 0), (w - NA_KW, w - NA_KW)))
    y = p[..., ::-1]
    a = jnp.broadcast_to(y[:, :, None, :], (nh, nr, w, wid))
    a = jnp.pad(a, ((0, 0), (0, 0), (0, 0), (0, 1))).reshape(nh, nr, w * (wid + 1))
    a = a[:, :, :w * wid].reshape(nh, nr, w, wid)
    m = a[..., w - 1:]
    j = jnp.arange(w)
    cs = jnp.clip(j - NA_KW // 2, 0, w - NA_KW)
    col_mask = (j[:, None] >= cs[None, :]) & (j[:, None] < cs[None, :] + NA_KW)
    m = jnp.where(col_mask[None, None], m, NEG_BIG)
    un, g_rows = NA_UNION_ROWS, NA_UNIT_ROWS
    mp = jnp.pad(m, ((0, 0), (un, un), (0, 0), (0, 0)), constant_values=NEG_BIG)
    mp = mp.reshape(nh // 2, 2, nr + 2 * un, w, w)
    a = jnp.arange(un)
    per_cfg = []
    for cfg in range(3):
        per_col = []
        for hh in range(2):
            for g in range(g_rows):
                q_row = cfg * g_rows + g
                win0 = (0, g, un - NA_KH)[cfg]
                lo = NA_KH - 1 - q_row + un
                in_win = (a >= win0) & (a < win0 + NA_KH)
                per_col.append(jnp.where(in_win[None, :, None, None], mp[:, hh, lo:lo + un], NEG_BIG))
        per_cfg.append(jnp.concatenate(per_col, axis=-1))
    return jnp.stack(per_cfg, axis=1).reshape(nh // 2, 3, un * w, 2 * g_rows * w)


def _column_scale():
    d = D_MODEL
    cs = jnp.ones((IN_WIDTH,), F32)
    cs = cs.at[d:2 * d].set(RET_DK ** -0.5)
    cs = cs.at[4 * d:5 * d].set(DIFF_HD ** -0.5 * math.log2(math.e))
    cs = cs.at[7 * d:8 * d].set(NA_HD ** -0.5 * math.log2(math.e))
    return cs.reshape(1, IN_WIDTH)


def _project_all(h, w, colscale, tm, rope, ret_tables, ax_tables, tiles_per_batch, tk_out, na_tk_out, plain_na_v):
    if rope:
        ret_qk = _proj(h, w, colscale, tm, 0, 1, 2, BF16, "rope_ret", ret_tables, tiles_per_batch)
        diff_qk = _proj(h, w, colscale, tm, 4, 5, 2, BF16, "rope_diff", ax_tables, tiles_per_batch)
    else:
        ret_qk = _proj(h, w, colscale, tm, 0, 1, 2, BF16)
        diff_qk = _proj(h, w, colscale, tm, 4, 5, 2, BF16)
    v_na = _proj(h, w, colscale, tm, 2, 7, 4 if plain_na_v else 3, BF16)
    gates = _proj(h, w, colscale, tm, 3, 10, 4, F32)
    diff_vt = _proj(h, w, colscale, tm, 6, 7, 1, BF16, tk_out=tk_out)
    na_vt = _proj(h, w, colscale, tm, 9, 10, 1, BF16, tk_out=na_tk_out)
    return ret_qk, diff_qk, v_na, gates, diff_vt, na_vt


def kernel(x, c, ctx, c_ctx, w_mod, b_mod, w_in, ret_decay_f, ret_decay_b, diff_lq1, diff_lk1, diff_lq2, diff_lk2,
           diff_subln_g, na_rpb, w_pa, w_pb, w_pc, w_o, ln1_g, ln1_b, w_ff1, w_ff2, ln2_g, ln2_b):
    b, length, d = x.shape
    lc = ctx.shape[1]
    depth = w_mod.shape[0]
    assert d == D_MODEL and b < 8 and length % min(ROW_TILE, length) == 0 and length % GRID_W == 0
    assert lc % RET_CHUNK == 0 or lc < RET_CHUNK
    alpha = (2.0 * depth) ** 0.25

    cvec = jnp.zeros((8, d), F32).at[:b].set(c).at[b].set(c_ctx)
    mods = _modulation(cvec, w_mod, b_mod)

    lat_row = lambda i, tm: (i * tm) // length
    ctx_row = lambda i, tm: b
    colscale = _column_scale()
    ret_tables = _retention_tables(length)
    ax_tables = _axial_tables(length)

    x_l = x.reshape(b * length, d)
    x_c = ctx.reshape(b * lc, d)
    zero_state = jnp.zeros((b, RET_HEADS, RET_DK, RET_DK), F32)
    for l in range(depth):
        with_ctx_out = l < depth - 1
        mod = [mods[l, :, k * d:(k + 1) * d].reshape(8, 1, d) for k in range(6)]
        sh1, sc1, g1, sh2, sc2, g2 = mod
        w = (w_in, l)
        wpa, wpb, wpc, wo = (a[l].astype(BF16) for a in (w_pa, w_pb, w_pc, w_o))
        w1, w2 = w_ff1[l].astype(BF16), w_ff2[l].astype(BF16)
        lng1, lnb1 = ln1_g[l].reshape(1, d), ln1_b[l].reshape(1, d)
        lng2, lnb2 = ln2_g[l].reshape(1, d), ln2_b[l].reshape(1, d)
        lam_init = 0.8 - 0.6 * math.exp(-0.3 * l)
        lam4 = jnp.stack([diff_lq1[l], diff_lk1[l], diff_lq2[l], diff_lk2[l]]).astype(F32)
        subln = diff_subln_g[l].reshape(1, DIFF_DV).astype(F32)
        bias = _na_bias_table(na_rpb[l])

        h_l = _ln_mod(x_l, sh1, sc1, lat_row)
        h_c = _ln_mod(x_c, sh1, sc1, ctx_row)
        tm_l = min(ROW_TILE, length)
        rqk_l, dqk_l, vna_l, gates_l, dvt_l, nvt_l = _project_all(
            h_l, w, colscale, tm_l, True, ret_tables, ax_tables, length // tm_l, min(DIFF_TILE, length),
            NA_UNIT_ROWS * GRID_W, False)
        rqk_c, dqk_c, vna_c, gates_c, dvt_c, nvt_c = _project_all(
            h_c, w, colscale, min(ROW_TILE, b * lc), False, (), (), 1, lc, lc, True)
        sh3 = lambda a, n: a.reshape(b, n, a.shape[-1])

        decays = _retention_decays(ret_decay_f[l], ret_decay_b[l], min(RET_CHUNK, lc))
        rqk_l3, vna_l3, rqk_c3, vna_c3 = sh3(rqk_l, length), sh3(vna_l, length), sh3(rqk_c, lc), sh3(vna_c, lc)
        o_cf, s_cf = _retention(decays, rqk_c3, vna_c3, zero_state, None, False)
        o_rc, s_cb = _retention(decays, rqk_c3, vna_c3, zero_state, o_cf, True)
        o_lf, _ = _retention(decays, rqk_l3, vna_l3, s_cf, None, False)
        o_rl, _ = _retention(decays, rqk_l3, vna_l3, s_cb, o_lf, True)

        dqk_l3, dqk_c3 = sh3(dqk_l, length), sh3(dqk_c, lc)
        o_dl = _diff_attention(lam4, subln, dqk_l3, (dqk_l3, dvt_l), (dqk_c3, dvt_c), lam_init)
        o_nl = _na_attention(vna_l3, nvt_l, vna_c3, nvt_c, bias)

        x_l = _merge(o_rl.reshape(b * length, d), gates_l, o_dl.reshape(b * length, d), o_nl.reshape(b * length, d),
                     x_l, g1, lng1, lnb1, wpa, wpb, wpc, wo, lat_row, alpha)
        x_l = _ffn(x_l, sh2, sc2, g2, lng2, lnb2, w1, w2, lat_row, alpha)
        if with_ctx_out:
            o_dc = _diff_attention(lam4, subln, dqk_c3, None, (dqk_c3, dvt_c), lam_init)
            o_nc = _ctx_attention(vna_c3)
            x_c = _merge(o_rc.reshape(b * lc, d), gates_c, o_dc.reshape(b * lc, d), o_nc.reshape(b * lc, d),
                         x_c, g1, lng1, lnb1, wpa, wpb, wpc, wo, ctx_row, alpha)
            x_c = _ffn(x_c, sh2, sc2, g2, lng2, lnb2, w1, w2, ctx_row, alpha)
    return x_l.reshape(b, length, d)
```

```python
import functools
import math

import jax
import jax.numpy as jnp
from jax import lax
from jax.experimental import pallas as pl
from jax.experimental.pallas import tpu as pltpu

F32 = jnp.float32
BF16 = jnp.bfloat16

D_MODEL = 1024
GRID_W = 64
RET_HEADS, RET_DK = 4, 256
DIFF_HEADS, DIFF_HD, DIFF_DV = 8, 64, 128
NA_HEADS, NA_HD, NA_KH, NA_KW = 16, 64, 8, 16
D_FF = 4 * D_MODEL
ROPE_BASE = 10000.0
LN_EPS = 1e-6
SUBLN_EPS = 1e-5
IN_WIDTH = 13 * D_MODEL
NEG_BIG = -1e30

LANES = 128
VMEM_LIMIT_CAP = 56 << 20

ROW_TILE = 2048
PROJ_TN = 1024
LN_TILE = 512
MERGE_TILE = 512
FFN_TILE = 1024
RET_CHUNK = 256
RET_BLOCK = 1024
DIFF_TQ = 256
DIFF_BLOCKS = 4
DIFF_FINISH_SLACK = 2
DIFF_TILE = 512
DIFF_VT_ROWS = DIFF_DV + 16
DIFF_SOFTMAX_ROWS = 64
NA_UNIT_ROWS = 4
NA_UNION_ROWS = 12
NA_UNITS = 16
NA_SOFTMAX_ROWS = 64

NT_DIMS = (((1,), (1,)), ((), ()))
TN_DIMS = (((0,), (0,)), ((), ()))


def _params(n_axes, vmem_mb):
    return pltpu.CompilerParams(dimension_semantics=("arbitrary",) * n_axes,
                                vmem_limit_bytes=min(vmem_mb << 20, VMEM_LIMIT_CAP))


def _sigmoid(x):
    return 1.0 / (1.0 + jnp.exp(-x))


def _layer_norm_rows(x):
    mu = jnp.mean(x, axis=-1, keepdims=True)
    xc = x - mu
    var = jnp.mean(xc * xc, axis=-1, keepdims=True)
    return xc * lax.rsqrt(var + LN_EPS)


def _mod_kernel(c_ref, w_ref, b_ref, o_ref):
    c = c_ref[...]
    a = (c * _sigmoid(c)).astype(BF16)
    o_ref[0] = jnp.dot(a, w_ref[0].astype(BF16), preferred_element_type=F32) + b_ref[0]


def _modulation(cvec, w_mod, b_mod):
    depth = w_mod.shape[0]
    d = D_MODEL
    return pl.pallas_call(
        _mod_kernel,
        grid=(depth, 6),
        in_specs=[pl.BlockSpec((8, d), lambda l, j: (0, 0)),
                  pl.BlockSpec((1, d, d), lambda l, j: (l, 0, j)),
                  pl.BlockSpec((1, 1, d), lambda l, j: (l, 0, j))],
        out_specs=pl.BlockSpec((1, 8, d), lambda l, j: (l, 0, j)),
        out_shape=jax.ShapeDtypeStruct((depth, 8, 6 * d), F32),
        compiler_params=_params(2, 32),
        name="modulation",
    )(cvec, w_mod, b_mod.reshape(depth, 1, 6 * d))


def _ln_mod_kernel(x_ref, sh_ref, sc_ref, o_ref):
    y = _layer_norm_rows(x_ref[...])
    o_ref[...] = (y * (1.0 + sc_ref[0]) + sh_ref[0]).astype(o_ref.dtype)


def _ln_mod(x, shift, scale, mod_row):
    r, d = x.shape
    tm = min(LN_TILE, r)
    return pl.pallas_call(
        _ln_mod_kernel,
        grid=(r // tm,),
        in_specs=[pl.BlockSpec((tm, d), lambda i: (i, 0)),
                  pl.BlockSpec((1, 1, d), lambda i: (mod_row(i, tm), 0, 0)),
                  pl.BlockSpec((1, 1, d), lambda i: (mod_row(i, tm), 0, 0))],
        out_specs=pl.BlockSpec((tm, d), lambda i: (i, 0)),
        out_shape=jax.ShapeDtypeStruct((r, d), BF16),
        compiler_params=_params(1, 32),
        name="ln_modulate",
    )(x, shift, scale)


def _proj_kernel(h_ref, w_ref, cs_ref, *rest, mode, transposed, tn, tk_out):
    o_ref = rest[-1]
    acc = jnp.dot(h_ref[...], w_ref[0].astype(BF16), preferred_element_type=F32) * cs_ref[...]
    if mode == "rope_ret":
        cos = rest[0][...]
        sin = rest[1][...]
        half = RET_DK // 2
        for hh in range(tn // RET_DK):
            a = hh * RET_DK
            x1 = acc[:, a:a + half]
            x2 = acc[:, a + half:a + RET_DK]
            o_ref[:, a:a + half] = (x1 * cos - x2 * sin).astype(o_ref.dtype)
            o_ref[:, a + half:a + RET_DK] = (x1 * sin + x2 * cos).astype(o_ref.dtype)
    elif mode == "rope_diff":
        cos = rest[0][...]
        sin_lo = rest[1][...]
        sin_hi = rest[2][...]
        q16 = DIFF_HD // 4
        for g in range(tn // LANES):
            x = acc[:, g * LANES:(g + 1) * LANES]
            y = x * cos + pltpu.roll(x, LANES - q16, 1) * sin_lo + pltpu.roll(x, q16, 1) * sin_hi
            o_ref[:, g * LANES:(g + 1) * LANES] = y.astype(o_ref.dtype)
    elif transposed:
        acc_t = acc.T
        ones = jnp.ones((DIFF_VT_ROWS - DIFF_DV, tk_out), o_ref.dtype)
        for a in range(o_ref.shape[0]):
            for hh in range(tn // DIFF_DV):
                r0 = hh * DIFF_VT_ROWS
                o_ref[a, r0:r0 + DIFF_DV, :] = acc_t[hh * DIFF_DV:(hh + 1) * DIFF_DV,
                                                     a * tk_out:(a + 1) * tk_out].astype(o_ref.dtype)
                o_ref[a, r0 + DIFF_DV:r0 + DIFF_VT_ROWS, :] = ones
    else:
        o_ref[...] = acc.astype(o_ref.dtype)


def _proj(h, w, colscale, tm, first, rest0, ntiles, out_dtype, mode="plain", tables=(), tiles_per_batch=1,
          tk_out=None):
    w_all, layer = w
    r, d = h.shape
    tn = PROJ_TN
    ncols = ntiles * tn
    assert r % tm == 0
    transposed = tk_out is not None
    wcol = lambda j: jnp.where(j == 0, first, rest0 + j - 1)
    in_specs = [pl.BlockSpec((tm, d), lambda i, j: (i, 0)),
                pl.BlockSpec((1, d, tn), lambda i, j: (layer, 0, wcol(j))),
                pl.BlockSpec((1, tn), lambda i, j: (0, wcol(j)))]
    for _ in tables:
        in_specs.append(pl.BlockSpec((tm, LANES), lambda i, j: (i % tiles_per_batch, 0)))
    if transposed:
        assert tm % tk_out == 0
        na = tm // tk_out
        rows_out = tn // DIFF_DV * DIFF_VT_ROWS
        out_specs = pl.BlockSpec((na, rows_out, tk_out), lambda i, j: (i, j, 0))
        out_shape = jax.ShapeDtypeStruct((r // tk_out, ncols // DIFF_DV * DIFF_VT_ROWS, tk_out), out_dtype)
    else:
        out_specs = pl.BlockSpec((tm, tn), lambda i, j: (i, j))
        out_shape = jax.ShapeDtypeStruct((r, ncols), out_dtype)
    return pl.pallas_call(
        functools.partial(_proj_kernel, mode=mode, transposed=transposed, tn=tn, tk_out=tk_out),
        grid=(r // tm, ntiles),
        in_specs=in_specs,
        out_specs=out_specs,
        out_shape=out_shape,
        compiler_params=_params(2, 52),
        name="in_proj_" + mode + ("_t" if transposed else ""),
    )(h, w_all, colscale, *tables)


def _ret_tables_kernel(dec_ref, intra_ref, qd_ref, kd_ref, cd_ref, *, chunk):
    direction = pl.program_id(0)
    head = pl.program_id(1)
    cf = float(chunk)
    dv = jnp.full((1, 1), dec_ref[direction, head], F32)
    lg = jnp.minimum(dv, 0.0) - jnp.log(1.0 + jnp.exp(-jnp.abs(dv)))
    rev = direction == 1
    ii = lax.broadcasted_iota(jnp.int32, (chunk, chunk), 0).astype(F32)
    jj = lax.broadcasted_iota(jnp.int32, (chunk, chunk), 1).astype(F32)
    dist = jnp.where(rev, jj - ii, ii - jj)
    intra_ref[0, 0] = jnp.where(dist >= 0.0, jnp.exp(lg * jnp.maximum(dist, 0.0)), 0.0)
    i1 = lax.broadcasted_iota(jnp.int32, (chunk, RET_DK), 0).astype(F32)
    qd_ref[0, 0] = jnp.exp(lg * jnp.where(rev, cf - i1, i1 + 1.0))
    kd_ref[0, 0] = jnp.exp(lg * jnp.where(rev, i1, cf - 1.0 - i1))
    cd_ref[0, 0] = jnp.exp(jnp.broadcast_to(lg, (8, RET_DK)) * cf)


def _retention_decays(dec_f, dec_b, chunk):
    hh, dk = RET_HEADS, RET_DK
    blk = lambda *shape: pl.BlockSpec((1, 1) + shape, lambda d, h: (d, h, 0, 0))
    return pl.pallas_call(
        functools.partial(_ret_tables_kernel, chunk=chunk),
        grid=(2, hh),
        in_specs=[pl.BlockSpec(memory_space=pltpu.SMEM)],
        out_specs=[blk(chunk, chunk), blk(chunk, dk), blk(chunk, dk), blk(8, dk)],
        out_shape=[jax.ShapeDtypeStruct((2, hh, chunk, chunk), F32),
                   jax.ShapeDtypeStruct((2, hh, chunk, dk), F32),
                   jax.ShapeDtypeStruct((2, hh, chunk, dk), F32),
                   jax.ShapeDtypeStruct((2, hh, 8, dk), F32)],
        compiler_params=_params(2, 32),
        name="retention_tables",
    )(jnp.stack([dec_f, dec_b]).astype(F32))


def _ret_kernel(intra_ref, qd_ref, kd_ref, cd_ref, qk_ref, v_ref, s0_ref, *rest, reverse, has_prev, chunk, n_chunks):
    if has_prev:
        prev_ref, o_ref, sfin_ref, s_scr = rest
    else:
        o_ref, sfin_ref, s_scr = rest
    @pl.when(pl.program_id(1) == 0)
    def _load_initial_state():
        s_scr[...] = s0_ref[0]

    dk = RET_DK
    order = range(n_chunks - 1, -1, -1) if reverse else range(n_chunks)
    for h in range(RET_HEADS):
        cols = slice(h * dk, (h + 1) * dk)
        kcols = slice((RET_HEADS + h) * dk, (RET_HEADS + h + 1) * dk)
        s = s_scr[h]
        for c in order:
            rows = slice(c * chunk, (c + 1) * chunk)
            q = qk_ref[0, rows, cols]
            k = qk_ref[0, rows, kcols]
            v = v_ref[0, rows, cols]
            att = lax.dot_general(q, k, NT_DIMS, preferred_element_type=F32) * intra_ref[0, h]
            qd = (q.astype(F32) * qd_ref[0, h]).astype(BF16)
            o = (jnp.dot(att.astype(BF16), v, preferred_element_type=F32)
                 + jnp.dot(qd, s.astype(BF16), preferred_element_type=F32))
            kd = (k.astype(F32) * kd_ref[0, h]).astype(BF16)
            s = s * cd_ref[0, h, 0:1, :] + lax.dot_general(kd, v, TN_DIMS, preferred_element_type=F32)
            if has_prev:
                o = o + prev_ref[0, rows, cols]
            o_ref[0, rows, cols] = o
        s_scr[h] = s
        sfin_ref[0, h] = s


def _retention(tables, qk, v_arr, s0, prev, reverse):
    b, length, _ = qk.shape
    hh, dk = RET_HEADS, RET_DK
    width = hh * dk
    c = tables[0].shape[-1]
    rb = min(RET_BLOCK, length)
    n = length // rb
    direction = 1 if reverse else 0
    pos = (lambda i: n - 1 - i) if reverse else (lambda i: i)
    tab = lambda *shape: pl.BlockSpec((1, hh) + shape, lambda bi, i: (direction, 0, 0, 0))
    in_specs = [tab(c, c), tab(c, dk), tab(c, dk), tab(8, dk),
                pl.BlockSpec((1, rb, 2 * width), lambda bi, i: (bi, pos(i), 0)),
                pl.BlockSpec((1, rb, width), lambda bi, i: (bi, pos(i), 0)),
                pl.BlockSpec((1, hh, dk, dk), lambda bi, i: (bi, 0, 0, 0))]
    args = [*tables, qk, v_arr, s0]
    if prev is not None:
        in_specs.append(pl.BlockSpec((1, rb, width), lambda bi, i: (bi, pos(i), 0)))
        args.append(prev)
    return pl.pallas_call(
        functools.partial(_ret_kernel, reverse=reverse, has_prev=prev is not None, chunk=c, n_chunks=rb // c),
        grid=(b, n),
        in_specs=in_specs,
        out_specs=[pl.BlockSpec((1, rb, width), lambda bi, i: (bi, pos(i), 0)),
                   pl.BlockSpec((1, hh, dk, dk), lambda bi, i: (bi, 0, 0, 0))],
        out_shape=[jax.ShapeDtypeStruct((b, length, width), F32),
                   jax.ShapeDtypeStruct((b, hh, dk, dk), F32)],
        scratch_shapes=[pltpu.VMEM((hh, dk, dk), F32)],
        compiler_params=_params(2, 48),
        name="retention_bwd" if reverse else "retention_fwd",
    )(*args)


def _diff_kernel(zero_ref, lam_ref, g_ref, q_ref, *rest, tq, nb, n_lat, tile, lc, lam_init):
    if n_lat:
        kl_ref, vtl_ref = rest[:2]
        rest = rest[2:]
    kc_ref, vtc_ref, o_ref, qq_scr = rest[:4]
    s_buf, mt_buf, p_buf, al_buf = rest[4:8], rest[8:12], rest[12:14], rest[14:16]
    m_scr, acc_scr = rest[16:]
    n = n_lat + 1
    rows_of = lambda j: lc if j == n_lat else tile

    def scores(g, slot):
        blk, j = divmod(g, n)
        if j == 0:
            q = q_ref[0, blk * tq:(blk + 1) * tq, :].astype(F32)
            lane = lax.broadcasted_iota(jnp.int32, q.shape, 1)
            qq_scr[blk, 0:tq, :] = jnp.where(lane < DIFF_HD, q, 0.0).astype(BF16)
            qq_scr[blk, tq:2 * tq, :] = jnp.where(lane >= DIFF_HD, q, 0.0).astype(BF16)
            m_scr[blk] = jnp.full(m_scr.shape[1:], NEG_BIG, F32)
        k = kc_ref[0] if j == n_lat else kl_ref[0, j * tile:(j + 1) * tile, :]
        s = lax.dot_general(k, qq_scr[blk], NT_DIMS, preferred_element_type=F32)
        mt_buf[slot][...] = jnp.max(s, axis=0, keepdims=True)
        s_buf[slot][0:k.shape[0], :] = s

    def softmax(g, anchor):
        blk, j = divmod(g, n)
        s_slot, p_slot, rows = g % 4, g % 2, rows_of(j)
        m_old = m_scr[blk]
        if anchor is not None:
            m_old = jnp.maximum(m_old, anchor + NEG_BIG)
        m_new = jnp.maximum(m_old, mt_buf[s_slot][...])
        al_buf[p_slot][...] = jnp.exp2(m_old - m_new)
        m_scr[blk] = m_new
        z = pl.multiple_of(zero_ref[0], 8)
        s_buf[s_slot][pl.ds(z, 8), :] = s_buf[s_slot][pl.ds(z, 8), :]
        ch = DIFF_SOFTMAX_ROWS
        for r0 in range(0, rows, ch):
            p_buf[p_slot][r0:r0 + ch, :] = jnp.exp2(s_buf[s_slot][r0:r0 + ch, :] - m_new).astype(BF16)

    def values(g):
        blk, j = divmod(g, n)
        slot = g % 2
        vt = vtc_ref[0] if j == n_lat else vtl_ref[j]
        pv = jnp.dot(vt, p_buf[slot][0:vt.shape[1], :], preferred_element_type=F32)
        acc_scr[blk] = pv if j == 0 else acc_scr[blk] * al_buf[slot][...] + pv

    def finish(blk):
        acc = acc_scr[blk]
        o = acc[0:DIFF_DV] * (1.0 / acc[DIFF_DV:DIFF_DV + 1])
        lam4 = lam_ref[...]
        s1 = jnp.sum(lam4[0:1] * lam4[1:2], axis=-1, keepdims=True)
        s2 = jnp.sum(lam4[2:3] * lam4[3:4], axis=-1, keepdims=True)
        lam = jnp.exp(s1) - jnp.exp(s2) + lam_init
        y = (o[:, :tq] - lam * o[:, tq:]).T
        ms = jnp.mean(y * y, axis=-1, keepdims=True)
        y = y * lax.rsqrt(ms + SUBLN_EPS) * g_ref[...] * (1.0 - lam_init)
        o_ref[0, blk * tq:(blk + 1) * tq, :] = y.astype(o_ref.dtype)
        d = jnp.concatenate([y[0:1, :]] * (2 * tq // LANES), axis=1) * 0.0
        return jnp.where(jnp.isnan(d), 0.0, d)

    total = nb * n
    scores(0, 0)
    if total > 1:
        scores(1, 1)
    anchor, anchor_at = None, -1
    for g in range(total):
        if g >= 1:
            values(g - 1)
            if g % n == 0:
                anchor, anchor_at = finish(g // n - 1), min(g + DIFF_FINISH_SLACK, total - 1)
        if g + 2 < total:
            scores(g + 2, (g + 2) % 4)
        softmax(g, anchor if g == anchor_at else None)
    values(total - 1)
    finish(nb - 1)


def _diff_attention(lam4, subln_g, q_arr, lat, ctx, lam_init):
    b, lq, _ = q_arr.shape
    hh = DIFF_HEADS
    tq = min(DIFF_TQ, lq)
    qk_c, vt_c = ctx
    lc = qk_c.shape[1]
    assert vt_c.shape == (b, hh * DIFF_VT_ROWS, lc)
    nb = math.gcd(DIFF_BLOCKS, lq // tq)
    in_specs = [pl.BlockSpec(memory_space=pltpu.SMEM),
                pl.BlockSpec((4, DIFF_HD), lambda bi, h, i: (0, 0)),
                pl.BlockSpec((1, DIFF_DV), lambda bi, h, i: (0, 0)),
                pl.BlockSpec((1, nb * tq, LANES), lambda bi, h, i: (bi, i, h))]
    args = [jnp.zeros((1,), jnp.int32), lam4, subln_g, q_arr]
    n_lat, tile = 0, lc
    if lat is not None:
        qk_l, vt_l = lat
        length = qk_l.shape[1]
        tile = vt_l.shape[2]
        n_lat = length // tile
        assert lc <= tile
        in_specs += [pl.BlockSpec((1, length, LANES), lambda bi, h, i: (bi, 0, hh + h)),
                     pl.BlockSpec((n_lat, DIFF_VT_ROWS, tile), lambda bi, h, i: (bi, h, 0))]
        args += [qk_l, vt_l]
    in_specs += [pl.BlockSpec((1, lc, LANES), lambda bi, h, i: (bi, 0, hh + h)),
                 pl.BlockSpec((1, DIFF_VT_ROWS, lc), lambda bi, h, i: (bi, h, 0))]
    args += [qk_c, vt_c]
    return pl.pallas_call(
        functools.partial(_diff_kernel, tq=tq, nb=nb, n_lat=n_lat, tile=tile, lc=lc, lam_init=lam_init),
        grid=(b, hh, lq // (nb * tq)),
        in_specs=in_specs,
        out_specs=pl.BlockSpec((1, nb * tq, LANES), lambda bi, h, i: (bi, i, h)),
        out_shape=jax.ShapeDtypeStruct((b, lq, hh * DIFF_DV), BF16),
        scratch_shapes=[pltpu.VMEM((nb, 2 * tq, LANES), BF16),
                        *[pltpu.VMEM((tile, 2 * tq), F32)] * 4,
                        *[pltpu.VMEM((1, 2 * tq), F32)] * 4,
                        *[pltpu.VMEM((tile, 2 * tq), BF16)] * 2,
                        *[pltpu.VMEM((1, 2 * tq), F32)] * 2,
                        pltpu.VMEM((nb, 1, 2 * tq), F32),
                        pltpu.VMEM((nb, DIFF_VT_ROWS, 2 * tq), F32)],
        compiler_params=_params(3, 40),
        name="diff_attention" if lat is not None else "diff_attention_ctx",
    )(*args)


def _softmax_pv(s_list, v_list):
    m = s_list[0].max(axis=-1, keepdims=True)
    for s in s_list[1:]:
        m = jnp.maximum(m, s.max(axis=-1, keepdims=True))
    den = 0.0
    o = 0.0
    for s, v in zip(s_list, v_list):
        p = jnp.exp2(s - m)
        den = den + jnp.sum(p, axis=-1, keepdims=True)
        o = o + jnp.dot(p.astype(BF16), v, preferred_element_type=F32)
    return o * (1.0 / den)


def _na_kernel(zero_ref, q_ref, k_ref, vtl_ref, kc_ref, vtc_ref, bias_ref, o_ref, *scr, units, n_rows, lc):
    s_buf, mt_buf, p_buf = scr[0:3], scr[3:6], scr[6:8]
    w = GRID_W
    uq = NA_UNIT_ROWS * w
    nk = NA_UNION_ROWS * w
    step_r0 = pl.program_id(2) * (units * NA_UNIT_ROWS)

    def first_key_row(u):
        return jnp.clip(step_r0 + u * NA_UNIT_ROWS - NA_KH // 2, 0, n_rows - NA_UNION_ROWS)

    def scores(u, slot):
        start = first_key_row(u)
        cfg = (step_r0 + u * NA_UNIT_ROWS - start) // NA_UNIT_ROWS
        q_t = q_ref[0, u * uq:(u + 1) * uq, :].astype(F32).T
        row = lax.broadcasted_iota(jnp.int32, q_t.shape, 0)
        qq = jnp.concatenate([jnp.where(row < NA_HD, q_t, 0.0), jnp.where(row >= NA_HD, q_t, 0.0)],
                             axis=1).astype(BF16)
        koff = pl.multiple_of(start * w, uq)
        s_nb = jnp.dot(k_ref[0, pl.ds(koff, nk), :], qq, preferred_element_type=F32) + bias_ref[0, cfg]
        s_cx = jnp.dot(kc_ref[0], qq, preferred_element_type=F32)
        mt_buf[slot][...] = jnp.maximum(jnp.max(s_nb, axis=0, keepdims=True), jnp.max(s_cx, axis=0, keepdims=True))
        s_buf[slot][0:nk, :] = s_nb
        s_buf[slot][nk:nk + lc, :] = s_cx

    def softmax(s_slot, p_slot):
        m = mt_buf[s_slot][...]
        z = pl.multiple_of(zero_ref[0], 8)
        s_buf[s_slot][pl.ds(z, 8), :] = s_buf[s_slot][pl.ds(z, 8), :]
        ch = NA_SOFTMAX_ROWS
        for r0 in range(0, nk + lc, ch):
            p_buf[p_slot][r0:r0 + ch, :] = jnp.exp2(s_buf[s_slot][r0:r0 + ch, :] - m).astype(BF16)

    def values(u, p_slot):
        pc0 = first_key_row(u) // NA_UNIT_ROWS
        acc = jnp.dot(vtc_ref[0], p_buf[p_slot][nk:nk + lc, :], preferred_element_type=F32)
        for i in range(NA_UNION_ROWS // NA_UNIT_ROWS):
            acc = acc + jnp.dot(vtl_ref[pc0 + i], p_buf[p_slot][i * uq:(i + 1) * uq, :],
                                preferred_element_type=F32)
        inv = 1.0 / acc[LANES:LANES + 1]
        y = jnp.concatenate([acc[0:NA_HD, 0:uq] * inv[:, 0:uq],
                             acc[NA_HD:LANES, uq:2 * uq] * inv[:, uq:2 * uq]], axis=0)
        o_ref[0, u * uq:(u + 1) * uq, :] = y.T.astype(o_ref.dtype)

    scores(0, 0)
    if units > 1:
        scores(1, 1)
    for u in range(units):
        if u >= 1:
            values(u - 1, (u - 1) % 2)
        if u + 2 < units:
            scores(u + 2, (u + 2) % 3)
        softmax(u % 3, u % 2)
    values(units - 1, (units - 1) % 2)


def _na_attention(qk_l, vt_l, qk_c, vt_c, bias):
    b, length, _ = qk_l.shape
    lc = qk_c.shape[1]
    n_rows = length // GRID_W
    uq = NA_UNIT_ROWS * GRID_W
    nk = NA_UNION_ROWS * GRID_W
    assert n_rows >= NA_UNION_ROWS and n_rows % NA_UNIT_ROWS == 0 and vt_l.shape[2] == uq
    units = min(NA_UNITS, n_rows // NA_UNIT_ROWS)
    npair = NA_HEADS // 2
    n_pieces = length // uq
    tq = units * uq
    return pl.pallas_call(
        functools.partial(_na_kernel, units=units, n_rows=n_rows, lc=lc),
        grid=(b, npair, length // tq),
        in_specs=[pl.BlockSpec(memory_space=pltpu.SMEM),
                  pl.BlockSpec((1, tq, LANES), lambda bi, p, i: (bi, i, npair + p)),
                  pl.BlockSpec((1, length, LANES), lambda bi, p, i: (bi, 0, 2 * npair + p)),
                  pl.BlockSpec((n_pieces, DIFF_VT_ROWS, uq), lambda bi, p, i: (bi, p, 0)),
                  pl.BlockSpec((1, lc, LANES), lambda bi, p, i: (bi, 0, 2 * npair + p)),
                  pl.BlockSpec((1, DIFF_VT_ROWS, lc), lambda bi, p, i: (bi, p, 0)),
                  pl.BlockSpec((1, 3, nk, 2 * uq), lambda bi, p, i: (p, 0, 0, 0))],
        out_specs=pl.BlockSpec((1, tq, LANES), lambda bi, p, i: (bi, i, p)),
        out_shape=jax.ShapeDtypeStruct((b, length, NA_HEADS * NA_HD), BF16),
        scratch_shapes=[*[pltpu.VMEM((nk + lc, 2 * uq), F32)] * 3,
                        *[pltpu.VMEM((1, 2 * uq), F32)] * 3,
                        *[pltpu.VMEM((nk + lc, 2 * uq), BF16)] * 2],
        compiler_params=_params(3, 48),
        name="neighbourhood_attention",
    )(jnp.zeros((1,), jnp.int32), qk_l, qk_l, vt_l, qk_c, vt_c, bias)


def _ctx_attn_kernel(q_ref, k_ref, v_ref, o_ref):
    q = q_ref[0].astype(F32)
    k = k_ref[0]
    v = v_ref[0]
    lane = lax.broadcasted_iota(jnp.int32, q.shape, 1)
    outs = []
    for hh in range(2):
        sel = (lane < NA_HD) if hh == 0 else (lane >= NA_HD)
        qm = jnp.where(sel, q, 0.0).astype(BF16)
        s = lax.dot_general(qm, k, NT_DIMS, preferred_element_type=F32)
        outs.append(_softmax_pv([s], [v]))
    o_ref[0] = jnp.where(lane < NA_HD, outs[0], outs[1]).astype(o_ref.dtype)


def _ctx_attention(qkv_c):
    b, lc, _ = qkv_c.shape
    npair = NA_HEADS // 2
    return pl.pallas_call(
        _ctx_attn_kernel,
        grid=(b, npair),
        in_specs=[pl.BlockSpec((1, lc, LANES), lambda bi, p: (bi, 0, npair + p)),
                  pl.BlockSpec((1, lc, LANES), lambda bi, p: (bi, 0, 2 * npair + p)),
                  pl.BlockSpec((1, lc, LANES), lambda bi, p: (bi, 0, 3 * npair + p))],
        out_specs=pl.BlockSpec((1, lc, LANES), lambda bi, p: (bi, 0, p)),
        out_shape=jax.ShapeDtypeStruct((b, lc, NA_HEADS * NA_HD), BF16),
        compiler_params=_params(2, 32),
        name="ctx_attention",
    )(qkv_c, qkv_c, qkv_c)


def _merge_kernel(oret_ref, gates_ref, odiff_ref, ona_ref, x_ref, g1_ref, lng_ref, lnb_ref,
                  wpa_ref, wpb_ref, wpc_ref, wo_ref, o_ref, *, alpha):
    d = D_MODEL
    ya_parts = []
    for hh in range(RET_HEADS):
        a = hh * RET_DK
        seg = _layer_norm_rows(oret_ref[:, a:a + RET_DK])
        g = gates_ref[:, a:a + RET_DK]
        ya_parts.append((g * _sigmoid(g) * seg).astype(BF16))
    ya_in = jnp.concatenate(ya_parts, axis=1)
    y_a = jnp.dot(ya_in, wpa_ref[...], preferred_element_type=F32)
    y_b = jnp.dot(odiff_ref[...], wpb_ref[...], preferred_element_type=F32)
    y_c = jnp.dot(ona_ref[...], wpc_ref[...], preferred_element_type=F32)
    m = (_sigmoid(gates_ref[:, d:2 * d]) * y_a + _sigmoid(gates_ref[:, 2 * d:3 * d]) * y_b
         + _sigmoid(gates_ref[:, 3 * d:4 * d]) * y_c)
    y = jnp.dot(m.astype(BF16), wo_ref[...], preferred_element_type=F32)
    z = alpha * x_ref[...] + g1_ref[0] * y
    o_ref[...] = _layer_norm_rows(z) * lng_ref[...] + lnb_ref[...]


def _merge(o_ret, gates, o_diff, o_na, x, g1, ln_g, ln_b, w_pa, w_pb, w_pc, w_o, mod_row, alpha):
    r, d = x.shape
    tm = min(MERGE_TILE, r)
    row = lambda i: (i, 0)
    const = lambda i: (0, 0)
    wspec = pl.BlockSpec((d, d), const, pipeline_mode=pl.Buffered(1))
    return pl.pallas_call(
        functools.partial(_merge_kernel, alpha=alpha),
        grid=(r // tm,),
        in_specs=[pl.BlockSpec((tm, d), row), pl.BlockSpec((tm, 4 * d), row), pl.BlockSpec((tm, d), row),
                  pl.BlockSpec((tm, d), row), pl.BlockSpec((tm, d), row),
                  pl.BlockSpec((1, 1, d), lambda i: (mod_row(i, tm), 0, 0)),
                  pl.BlockSpec((1, d), const), pl.BlockSpec((1, d), const),
                  wspec, wspec, wspec, wspec],
        out_specs=pl.BlockSpec((tm, d), row),
        out_shape=jax.ShapeDtypeStruct((r, d), F32),
        compiler_params=_params(1, 52),
        name="merge_out_proj",
    )(o_ret, gates, o_diff, o_na, x, g1, ln_g, ln_b, w_pa, w_pb, w_pc, w_o)


def _ffn_kernel(x_ref, sh_ref, sc_ref, g2_ref, lng_ref, lnb_ref, w1_ref, w2_ref, o_ref, *, alpha, ff_chunk):
    x = x_ref[...]
    h = (_layer_norm_rows(x) * (1.0 + sc_ref[0]) + sh_ref[0]).astype(BF16)
    acc = jnp.zeros(x.shape, F32)
    for cidx in range(D_FF // ff_chunk):
        a = cidx * ff_chunk
        u = jnp.maximum(jnp.dot(h, w1_ref[:, a:a + ff_chunk], preferred_element_type=F32), 0.0)
        acc = acc + jnp.dot((u * u).astype(BF16), w2_ref[a:a + ff_chunk, :], preferred_element_type=F32)
    z = alpha * x + g2_ref[0] * acc
    o_ref[...] = _layer_norm_rows(z) * lng_ref[...] + lnb_ref[...]


def _ffn(x, shift, scale, g2, ln_g, ln_b, w1, w2, mod_row, alpha):
    r, d = x.shape
    tm = min(FFN_TILE, r)
    row = lambda i: (i, 0)
    const = lambda i: (0, 0)
    mspec = pl.BlockSpec((1, 1, d), lambda i: (mod_row(i, tm), 0, 0))
    return pl.pallas_call(
        functools.partial(_ffn_kernel, alpha=alpha, ff_chunk=1024),
        grid=(r // tm,),
        in_specs=[pl.BlockSpec((tm, d), row), mspec, mspec, mspec,
                  pl.BlockSpec((1, d), const), pl.BlockSpec((1, d), const),
                  pl.BlockSpec((d, D_FF), const, pipeline_mode=pl.Buffered(1)),
                  pl.BlockSpec((D_FF, d), const, pipeline_mode=pl.Buffered(1))],
        out_specs=pl.BlockSpec((tm, d), row),
        out_shape=jax.ShapeDtypeStruct((r, d), F32),
        compiler_params=_params(1, 56),
        name="ffn",
    )(x, shift, scale, g2, ln_g, ln_b, w1, w2)


def _retention_tables(length):
    t = jnp.arange(length, dtype=F32)
    inv = ROPE_BASE ** (-jnp.linspace(0.0, 1.0, RET_DK // 2, dtype=F32))
    ang = t[:, None] * inv[None, :]
    return jnp.cos(ang), jnp.sin(ang)


def _axial_tables(length):
    t = jnp.arange(length)
    row = (t // GRID_W).astype(F32)
    col = (t % GRID_W).astype(F32)
    half = DIFF_HD // 2
    inv = ROPE_BASE ** (-jnp.arange(0, half, 2, dtype=F32) / half)
    ang_r = row[:, None] * inv[None, :]
    ang_c = col[:, None] * inv[None, :]
    zero = jnp.zeros_like(ang_r)
    unit_ang = jnp.concatenate([ang_r, ang_r, ang_c, ang_c], axis=-1)
    cos = jnp.cos(unit_ang)
    sin_r, sin_c = jnp.sin(ang_r), jnp.sin(ang_c)
    sin_lo = jnp.concatenate([-sin_r, zero, -sin_c, zero], axis=-1)
    sin_hi = jnp.concatenate([zero, sin_r, zero, sin_c], axis=-1)
    rep = LANES // DIFF_HD
    return tuple(jnp.tile(a, (1, rep)) for a in (cos, sin_lo, sin_hi))


def _na_bias_table(rpb):
    nh, nr, _ = rpb.shape
    w = GRID_W
    wid = 2 * w - 1
    p = jnp.pad(rpb.astype(F32) * math.log2(math.e), ((0, 0), (0, 0), (w - NA_KW, w - NA_KW)))
    y = p[..., ::-1]
    a = jnp.broadcast_to(y[:, :, None, :], (nh, nr, w, wid))
    a = jnp.pad(a, ((0, 0), (0, 0), (0, 0), (0, 1))).reshape(nh, nr, w * (wid + 1))
    a = a[:, :, :w * wid].reshape(nh, nr, w, wid)
    m = a[..., w - 1:]
    j = jnp.arange(w)
    cs = jnp.clip(j - NA_KW // 2, 0, w - NA_KW)
    col_mask = (j[:, None] >= cs[None, :]) & (j[:, None] < cs[None, :] + NA_KW)
    m = jnp.where(col_mask[None, None], m, NEG_BIG)
    un, g_rows = NA_UNION_ROWS, NA_UNIT_ROWS
    mp = jnp.pad(m, ((0, 0), (un, un), (0, 0), (0, 0)), constant_values=NEG_BIG)
    mp = mp.reshape(nh // 2, 2, nr + 2 * un, w, w)
    a = jnp.arange(un)
    per_cfg = []
    for cfg in range(3):
        per_col = []
        for hh in range(2):
            for g in range(g_rows):
                q_row = cfg * g_rows + g
                win0 = (0, g, un - NA_KH)[cfg]
                lo = NA_KH - 1 - q_row + un
                in_win = (a >= win0) & (a < win0 + NA_KH)
                per_col.append(jnp.where(in_win[None, :, None, None], mp[:, hh, lo:lo + un], NEG_BIG))
        per_cfg.append(jnp.concatenate(per_col, axis=-1))
    return jnp.stack(per_cfg, axis=1).reshape(nh // 2, 3, un * w, 2 * g_rows * w)


def _column_scale():
    d = D_MODEL
    cs = jnp.ones((IN_WIDTH,), F32)
    cs = cs.at[d:2 * d].set(RET_DK ** -0.5)
    cs = cs.at[4 * d:5 * d].set(DIFF_HD ** -0.5 * math.log2(math.e))
    cs = cs.at[7 * d:8 * d].set(NA_HD ** -0.5 * math.log2(math.e))
    return cs.reshape(1, IN_WIDTH)


def _project_all(h, w, colscale, tm, rope, ret_tables, ax_tables, tiles_per_batch, tk_out, na_tk_out, plain_na_v):
    if rope:
        ret_qk = _proj(h, w, colscale, tm, 0, 1, 2, BF16, "rope_ret", ret_tables, tiles_per_batch)
        diff_qk = _proj(h, w, colscale, tm, 4, 5, 2, BF16, "rope_diff", ax_tables, tiles_per_batch)
    else:
        ret_qk = _proj(h, w, colscale, tm, 0, 1, 2, BF16)
        diff_qk = _proj(h, w, colscale, tm, 4, 5, 2, BF16)
    v_na = _proj(h, w, colscale, tm, 2, 7, 4 if plain_na_v else 3, BF16)
    gates = _proj(h, w, colscale, tm, 3, 10, 4, F32)
    diff_vt = _proj(h, w, colscale, tm, 6, 7, 1, BF16, tk_out=tk_out)
    na_vt = _proj(h, w, colscale, tm, 9, 10, 1, BF16, tk_out=na_tk_out)
    return ret_qk, diff_qk, v_na, gates, diff_vt, na_vt


def kernel(x, c, ctx, c_ctx, w_mod, b_mod, w_in, ret_decay_f, ret_decay_b, diff_lq1, diff_lk1, diff_lq2, diff_lk2,
           diff_subln_g, na_rpb, w_pa, w_pb, w_pc, w_o, ln1_g, ln1_b, w_ff1, w_ff2, ln2_g, ln2_b):
    b, length, d = x.shape
    lc = ctx.shape[1]
    depth = w_mod.shape[0]
    assert d == D_MODEL and b < 8 and length % min(ROW_TILE, length) == 0 and length % GRID_W == 0
    assert lc % RET_CHUNK == 0 or lc < RET_CHUNK
    alpha = (2.0 * depth) ** 0.25

    cvec = jnp.zeros((8, d), F32).at[:b].set(c).at[b].set(c_ctx)
    mods = _modulation(cvec, w_mod, b_mod)

    lat_row = lambda i, tm: (i * tm) // length
    ctx_row = lambda i, tm: b
    colscale = _column_scale()
    ret_tables = _retention_tables(length)
    ax_tables = _axial_tables(length)

    x_l = x.reshape(b * length, d)
    x_c = ctx.reshape(b * lc, d)
    zero_state = jnp.zeros((b, RET_HEADS, RET_DK, RET_DK), F32)
    for l in range(depth):
        with_ctx_out = l < depth - 1
        mod = [mods[l, :, k * d:(k + 1) * d].reshape(8, 1, d) for k in range(6)]
        sh1, sc1, g1, sh2, sc2, g2 = mod
        w = (w_in, l)
        wpa, wpb, wpc, wo = (a[l].astype(BF16) for a in (w_pa, w_pb, w_pc, w_o))
        w1, w2 = w_ff1[l].astype(BF16), w_ff2[l].astype(BF16)
        lng1, lnb1 = ln1_g[l].reshape(1, d), ln1_b[l].reshape(1, d)
        lng2, lnb2 = ln2_g[l].reshape(1, d), ln2_b[l].reshape(1, d)
        lam_init = 0.8 - 0.6 * math.exp(-0.3 * l)
        lam4 = jnp.stack([diff_lq1[l], diff_lk1[l], diff_lq2[l], diff_lk2[l]]).astype(F32)
        subln = diff_subln_g[l].reshape(1, DIFF_DV).astype(F32)
        bias = _na_bias_table(na_rpb[l])

        h_l = _ln_mod(x_l, sh1, sc1, lat_row)
        h_c = _ln_mod(x_c, sh1, sc1, ctx_row)
        tm_l = min(ROW_TILE, length)
        rqk_l, dqk_l, vna_l, gates_l, dvt_l, nvt_l = _project_all(
            h_l, w, colscale, tm_l, True, ret_tables, ax_tables, length // tm_l, min(DIFF_TILE, length),
            NA_UNIT_ROWS * GRID_W, False)
        rqk_c, dqk_c, vna_c, gates_c, dvt_c, nvt_c = _project_all(
            h_c, w, colscale, min(ROW_TILE, b * lc), False, (), (), 1, lc, lc, True)
        sh3 = lambda a, n: a.reshape(b, n, a.shape[-1])

        decays = _retention_decays(ret_decay_f[l], ret_decay_b[l], min(RET_CHUNK, lc))
        rqk_l3, vna_l3, rqk_c3, vna_c3 = sh3(rqk_l, length), sh3(vna_l, length), sh3(rqk_c, lc), sh3(vna_c, lc)
        o_cf, s_cf = _retention(decays, rqk_c3, vna_c3, zero_state, None, False)
        o_rc, s_cb = _retention(decays, rqk_c3, vna_c3, zero_state, o_cf, True)
        o_lf, _ = _retention(decays, rqk_l3, vna_l3, s_cf, None, False)
        o_rl, _ = _retention(decays, rqk_l3, vna_l3, s_cb, o_lf, True)

        dqk_l3, dqk_c3 = sh3(dqk_l, length), sh3(dqk_c, lc)
        o_dl = _diff_attention(lam4, subln, dqk_l3, (dqk_l3, dvt_l), (dqk_c3, dvt_c), lam_init)
        o_nl = _na_attention(vna_l3, nvt_l, vna_c3, nvt_c, bias)

        x_l = _merge(o_rl.reshape(b * length, d), gates_l, o_dl.reshape(b * length, d), o_nl.reshape(b * length, d),
                     x_l, g1, lng1, lnb1, wpa, wpb, wpc, wo, lat_row, alpha)
        x_l = _ffn(x_l, sh2, sc2, g2, lng2, lnb2, w1, w2, lat_row, alpha)
        if with_ctx_out:
            o_dc = _diff_attention(lam4, subln, dqk_c3, None, (dqk_c3, dvt_c), lam_init)
            o_nc = _ctx_attention(vna_c3)
            x_c = _merge(o_rc.reshape(b * lc, d), gates_c, o_dc.reshape(b * lc, d), o_nc.reshape(b * lc, d),
                         x_c, g1, lng1, lnb1, wpa, wpb, wpc, wo, ctx_row, alpha)
            x_c = _ffn(x_c, sh2, sc2, g2, lng2, lnb2, w1, w2, ctx_row, alpha)
    return x_l.reshape(b, length, d)
```

```python
import functools
import math

import jax
import jax.numpy as jnp
from jax import lax
from jax.experimental import pallas as pl
from jax.experimental.pallas import tpu as pltpu

F32 = jnp.float32
BF16 = jnp.bfloat16

D_MODEL = 1024
GRID_W = 64
RET_HEADS, RET_DK = 4, 256
DIFF_HEADS, DIFF_HD, DIFF_DV = 8, 64, 128
NA_HEADS, NA_HD, NA_KH, NA_KW = 16, 64, 8, 16
D_FF = 4 * D_MODEL
ROPE_BASE = 10000.0
LN_EPS = 1e-6
SUBLN_EPS = 1e-5
IN_WIDTH = 13 * D_MODEL
NEG_BIG = -1e30

LANES = 128
SUBLANES = 8
SUBLANES_BF16 = 16
VMEM_LIMIT_CAP = 56 << 20

ROW_TILE = 2048
PROJ_TN = 1024
LN_TILE = 512
MERGE_TILE = 512
FFN_TILE = 1024
RET_CHUNK = 256
RET_BLOCK = 1024
DIFF_TQ = 256
DIFF_BLOCKS = 4
DIFF_FINISH_SLACK = 2
DIFF_TILE = 512
DIFF_VT_ROWS = DIFF_DV + SUBLANES_BF16
DIFF_SOFTMAX_ROWS = 64
NA_UNIT_ROWS = 4
NA_UNION_ROWS = 12
NA_UNITS = 16
NA_SOFTMAX_ROWS = 64

NT_DIMS = (((1,), (1,)), ((), ()))
TN_DIMS = (((0,), (0,)), ((), ()))


def _params(n_axes, vmem_mb):
    return pltpu.CompilerParams(dimension_semantics=("arbitrary",) * n_axes,
                                vmem_limit_bytes=min(vmem_mb << 20, VMEM_LIMIT_CAP))


def _sigmoid(x):
    return 1.0 / (1.0 + jnp.exp(-x))


def _layer_norm_rows(x):
    mu = jnp.mean(x, axis=-1, keepdims=True)
    xc = x - mu
    var = jnp.mean(xc * xc, axis=-1, keepdims=True)
    return xc * lax.rsqrt(var + LN_EPS)


def _mod_kernel(c_ref, w_ref, b_ref, o_ref):
    c = c_ref[...]
    a = (c * _sigmoid(c)).astype(BF16)
    o_ref[0] = jnp.dot(a, w_ref[0].astype(BF16), preferred_element_type=F32) + b_ref[0]


def _modulation(cvec, w_mod, b_mod):
    depth = w_mod.shape[0]
    d = D_MODEL
    return pl.pallas_call(
        _mod_kernel,
        grid=(depth, 6),
        in_specs=[pl.BlockSpec((SUBLANES, d), lambda l, j: (0, 0)),
                  pl.BlockSpec((1, d, d), lambda l, j: (l, 0, j)),
                  pl.BlockSpec((1, 1, d), lambda l, j: (l, 0, j))],
        out_specs=pl.BlockSpec((1, SUBLANES, d), lambda l, j: (l, 0, j)),
        out_shape=jax.ShapeDtypeStruct((depth, SUBLANES, 6 * d), F32),
        compiler_params=_params(2, 32),
        name="modulation",
    )(cvec, w_mod, b_mod.reshape(depth, 1, 6 * d))


def _ln_mod_kernel(x_ref, sh_ref, sc_ref, o_ref):
    y = _layer_norm_rows(x_ref[...])
    o_ref[...] = (y * (1.0 + sc_ref[0]) + sh_ref[0]).astype(o_ref.dtype)


def _ln_mod(x, shift, scale, mod_row):
    r, d = x.shape
    tm = min(LN_TILE, r)
    return pl.pallas_call(
        _ln_mod_kernel,
        grid=(r // tm,),
        in_specs=[pl.BlockSpec((tm, d), lambda i: (i, 0)),
                  pl.BlockSpec((1, 1, d), lambda i: (mod_row(i, tm), 0, 0)),
                  pl.BlockSpec((1, 1, d), lambda i: (mod_row(i, tm), 0, 0))],
        out_specs=pl.BlockSpec((tm, d), lambda i: (i, 0)),
        out_shape=jax.ShapeDtypeStruct((r, d), BF16),
        compiler_params=_params(1, 32),
        name="ln_modulate",
    )(x, shift, scale)


def _proj_kernel(h_ref, w_ref, cs_ref, *rest, mode, transposed, tn, tk_out):
    o_ref = rest[-1]
    acc = jnp.dot(h_ref[...], w_ref[0].astype(BF16), preferred_element_type=F32) * cs_ref[...]
    if mode == "rope_ret":
        cos = rest[0][...]
        sin = rest[1][...]
        half = RET_DK // 2
        for hh in range(tn // RET_DK):
            a = hh * RET_DK
            x1 = acc[:, a:a + half]
            x2 = acc[:, a + half:a + RET_DK]
            o_ref[:, a:a + half] = (x1 * cos - x2 * sin).astype(o_ref.dtype)
            o_ref[:, a + half:a + RET_DK] = (x1 * sin + x2 * cos).astype(o_ref.dtype)
    elif mode == "rope_diff":
        cos = rest[0][...]
        sin_lo = rest[1][...]
        sin_hi = rest[2][...]
        q16 = DIFF_HD // 4
        for g in range(tn // LANES):
            x = acc[:, g * LANES:(g + 1) * LANES]
            y = x * cos + pltpu.roll(x, LANES - q16, 1) * sin_lo + pltpu.roll(x, q16, 1) * sin_hi
            o_ref[:, g * LANES:(g + 1) * LANES] = y.astype(o_ref.dtype)
    elif transposed:
        acc_t = acc.T
        ones = jnp.ones((DIFF_VT_ROWS - DIFF_DV, tk_out), o_ref.dtype)
        for a in range(o_ref.shape[0]):
            for hh in range(tn // DIFF_DV):
                r0 = hh * DIFF_VT_ROWS
                o_ref[a, r0:r0 + DIFF_DV, :] = acc_t[hh * DIFF_DV:(hh + 1) * DIFF_DV,
                                                     a * tk_out:(a + 1) * tk_out].astype(o_ref.dtype)
                o_ref[a, r0 + DIFF_DV:r0 + DIFF_VT_ROWS, :] = ones
    else:
        o_ref[...] = acc.astype(o_ref.dtype)


def _proj(h, w, colscale, tm, first, rest0, ntiles, out_dtype, mode="plain", tables=(), tiles_per_batch=1,
          tk_out=None):
    w_all, layer = w
    r, d = h.shape
    tn = PROJ_TN
    ncols = ntiles * tn
    assert r % tm == 0
    transposed = tk_out is not None
    wcol = lambda j: jnp.where(j == 0, first, rest0 + j - 1)
    in_specs = [pl.BlockSpec((tm, d), lambda i, j: (i, 0)),
                pl.BlockSpec((1, d, tn), lambda i, j: (layer, 0, wcol(j))),
                pl.BlockSpec((1, tn), lambda i, j: (0, wcol(j)))]
    for _ in tables:
        in_specs.append(pl.BlockSpec((tm, LANES), lambda i, j: (i % tiles_per_batch, 0)))
    if transposed:
        assert tm % tk_out == 0
        na = tm // tk_out
        rows_out = tn // DIFF_DV * DIFF_VT_ROWS
        out_specs = pl.BlockSpec((na, rows_out, tk_out), lambda i, j: (i, j, 0))
        out_shape = jax.ShapeDtypeStruct((r // tk_out, ncols // DIFF_DV * DIFF_VT_ROWS, tk_out), out_dtype)
    else:
        out_specs = pl.BlockSpec((tm, tn), lambda i, j: (i, j))
        out_shape = jax.ShapeDtypeStruct((r, ncols), out_dtype)
    return pl.pallas_call(
        functools.partial(_proj_kernel, mode=mode, transposed=transposed, tn=tn, tk_out=tk_out),
        grid=(r // tm, ntiles),
        in_specs=in_specs,
        out_specs=out_specs,
        out_shape=out_shape,
        compiler_params=_params(2, 52),
        name="in_proj_" + mode + ("_t" if transposed else ""),
    )(h, w_all, colscale, *tables)


def _ret_tables_kernel(dec_ref, intra_ref, qd_ref, kd_ref, cd_ref, *, chunk):
    direction = pl.program_id(0)
    head = pl.program_id(1)
    cf = float(chunk)
    dv = jnp.full((1, 1), dec_ref[direction, head], F32)
    lg = jnp.minimum(dv, 0.0) - jnp.log(1.0 + jnp.exp(-jnp.abs(dv)))
    rev = direction == 1
    ii = lax.broadcasted_iota(jnp.int32, (chunk, chunk), 0).astype(F32)
    jj = lax.broadcasted_iota(jnp.int32, (chunk, chunk), 1).astype(F32)
    dist = jnp.where(rev, jj - ii, ii - jj)
    intra_ref[0, 0] = jnp.where(dist >= 0.0, jnp.exp(lg * jnp.maximum(dist, 0.0)), 0.0)
    i1 = lax.broadcasted_iota(jnp.int32, (chunk, RET_DK), 0).astype(F32)
    qd_ref[0, 0] = jnp.exp(lg * jnp.where(rev, cf - i1, i1 + 1.0))
    kd_ref[0, 0] = jnp.exp(lg * jnp.where(rev, i1, cf - 1.0 - i1))
    cd_ref[0, 0] = jnp.exp(jnp.broadcast_to(lg, (SUBLANES, RET_DK)) * cf)


def _retention_decays(dec_f, dec_b, chunk):
    hh, dk = RET_HEADS, RET_DK
    blk = lambda *shape: pl.BlockSpec((1, 1) + shape, lambda d, h: (d, h, 0, 0))
    return pl.pallas_call(
        functools.partial(_ret_tables_kernel, chunk=chunk),
        grid=(2, hh),
        in_specs=[pl.BlockSpec(memory_space=pltpu.SMEM)],
        out_specs=[blk(chunk, chunk), blk(chunk, dk), blk(chunk, dk), blk(SUBLANES, dk)],
        out_shape=[jax.ShapeDtypeStruct((2, hh, chunk, chunk), F32),
                   jax.ShapeDtypeStruct((2, hh, chunk, dk), F32),
                   jax.ShapeDtypeStruct((2, hh, chunk, dk), F32),
                   jax.ShapeDtypeStruct((2, hh, SUBLANES, dk), F32)],
        compiler_params=_params(2, 32),
        name="retention_tables",
    )(jnp.stack([dec_f, dec_b]).astype(F32))


def _ret_kernel(intra_ref, qd_ref, kd_ref, cd_ref, qk_ref, v_ref, s0_ref, *rest, reverse, has_prev, chunk, n_chunks):
    if has_prev:
        prev_ref, o_ref, sfin_ref, s_scr = rest
    else:
        o_ref, sfin_ref, s_scr = rest
    @pl.when(pl.program_id(1) == 0)
    def _load_initial_state():
        s_scr[...] = s0_ref[0]

    dk = RET_DK
    order = range(n_chunks - 1, -1, -1) if reverse else range(n_chunks)
    for h in range(RET_HEADS):
        cols = slice(h * dk, (h + 1) * dk)
        kcols = slice((RET_HEADS + h) * dk, (RET_HEADS + h + 1) * dk)
        s = s_scr[h]
        for c in order:
            rows = slice(c * chunk, (c + 1) * chunk)
            q = qk_ref[0, rows, cols]
            k = qk_ref[0, rows, kcols]
            v = v_ref[0, rows, cols]
            att = lax.dot_general(q, k, NT_DIMS, preferred_element_type=F32) * intra_ref[0, h]
            qd = (q.astype(F32) * qd_ref[0, h]).astype(BF16)
            o = (jnp.dot(att.astype(BF16), v, preferred_element_type=F32)
                 + jnp.dot(qd, s.astype(BF16), preferred_element_type=F32))
            kd = (k.astype(F32) * kd_ref[0, h]).astype(BF16)
            s = s * cd_ref[0, h, 0:1, :] + lax.dot_general(kd, v, TN_DIMS, preferred_element_type=F32)
            if has_prev:
                o = o + prev_ref[0, rows, cols]
            o_ref[0, rows, cols] = o
        s_scr[h] = s
        sfin_ref[0, h] = s


def _retention(tables, qk, v_arr, s0, prev, reverse):
    b, length, _ = qk.shape
    hh, dk = RET_HEADS, RET_DK
    width = hh * dk
    c = tables[0].shape[-1]
    rb = min(RET_BLOCK, length)
    n = length // rb
    direction = 1 if reverse else 0
    pos = (lambda i: n - 1 - i) if reverse else (lambda i: i)
    tab = lambda *shape: pl.BlockSpec((1, hh) + shape, lambda bi, i: (direction, 0, 0, 0))
    in_specs = [tab(c, c), tab(c, dk), tab(c, dk), tab(SUBLANES, dk),
                pl.BlockSpec((1, rb, 2 * width), lambda bi, i: (bi, pos(i), 0)),
                pl.BlockSpec((1, rb, width), lambda bi, i: (bi, pos(i), 0)),
                pl.BlockSpec((1, hh, dk, dk), lambda bi, i: (bi, 0, 0, 0))]
    args = [*tables, qk, v_arr, s0]
    if prev is not None:
        in_specs.append(pl.BlockSpec((1, rb, width), lambda bi, i: (bi, pos(i), 0)))
        args.append(prev)
    return pl.pallas_call(
        functools.partial(_ret_kernel, reverse=reverse, has_prev=prev is not None, chunk=c, n_chunks=rb // c),
        grid=(b, n),
        in_specs=in_specs,
        out_specs=[pl.BlockSpec((1, rb, width), lambda bi, i: (bi, pos(i), 0)),
                   pl.BlockSpec((1, hh, dk, dk), lambda bi, i: (bi, 0, 0, 0))],
        out_shape=[jax.ShapeDtypeStruct((b, length, width), F32),
                   jax.ShapeDtypeStruct((b, hh, dk, dk), F32)],
        scratch_shapes=[pltpu.VMEM((hh, dk, dk), F32)],
        compiler_params=_params(2, 48),
        name="retention_bwd" if reverse else "retention_fwd",
    )(*args)


def _diff_kernel(zero_ref, lam_ref, g_ref, q_ref, *rest, tq, nb, n_lat, tile, lc, lam_init):
    if n_lat:
        kl_ref, vtl_ref = rest[:2]
        rest = rest[2:]
    kc_ref, vtc_ref, o_ref, qq_scr = rest[:4]
    s_buf, mt_buf, p_buf, al_buf = rest[4:8], rest[8:12], rest[12:14], rest[14:16]
    m_scr, acc_scr = rest[16:]
    n = n_lat + 1
    rows_of = lambda j: lc if j == n_lat else tile

    def scores(g, slot):
        blk, j = divmod(g, n)
        if j == 0:
            q = q_ref[0, blk * tq:(blk + 1) * tq, :].astype(F32)
            lane = lax.broadcasted_iota(jnp.int32, q.shape, 1)
            qq_scr[blk, 0:tq, :] = jnp.where(lane < DIFF_HD, q, 0.0).astype(BF16)
            qq_scr[blk, tq:2 * tq, :] = jnp.where(lane >= DIFF_HD, q, 0.0).astype(BF16)
            m_scr[blk] = jnp.full(m_scr.shape[1:], NEG_BIG, F32)
        k = kc_ref[0] if j == n_lat else kl_ref[0, j * tile:(j + 1) * tile, :]
        s = lax.dot_general(k, qq_scr[blk], NT_DIMS, preferred_element_type=F32)
        mt_buf[slot][...] = jnp.max(s, axis=0, keepdims=True)
        s_buf[slot][0:k.shape[0], :] = s

    def softmax(g, anchor):
        blk, j = divmod(g, n)
        s_slot, p_slot, rows = g % 4, g % 2, rows_of(j)
        m_old = m_scr[blk]
        if anchor is not None:
            m_old = jnp.maximum(m_old, anchor + NEG_BIG)
        m_new = jnp.maximum(m_old, mt_buf[s_slot][...])
        al_buf[p_slot][...] = jnp.exp2(m_old - m_new)
        m_scr[blk] = m_new
        z = pl.multiple_of(zero_ref[0], 8)
        s_buf[s_slot][pl.ds(z, 8), :] = s_buf[s_slot][pl.ds(z, 8), :]
        ch = DIFF_SOFTMAX_ROWS
        for r0 in range(0, rows, ch):
            p_buf[p_slot][r0:r0 + ch, :] = jnp.exp2(s_buf[s_slot][r0:r0 + ch, :] - m_new).astype(BF16)

    def values(g):
        blk, j = divmod(g, n)
        slot = g % 2
        vt = vtc_ref[0] if j == n_lat else vtl_ref[j]
        pv = jnp.dot(vt, p_buf[slot][0:vt.shape[1], :], preferred_element_type=F32)
        acc_scr[blk] = pv if j == 0 else acc_scr[blk] * al_buf[slot][...] + pv

    def finish(blk):
        acc = acc_scr[blk]
        o = acc[0:DIFF_DV] * (1.0 / acc[DIFF_DV:DIFF_DV + 1])
        lam4 = lam_ref[...]
        s1 = jnp.sum(lam4[0:1] * lam4[1:2], axis=-1, keepdims=True)
        s2 = jnp.sum(lam4[2:3] * lam4[3:4], axis=-1, keepdims=True)
        lam = jnp.exp(s1) - jnp.exp(s2) + lam_init
        y = (o[:, :tq] - lam * o[:, tq:]).T
        ms = jnp.mean(y * y, axis=-1, keepdims=True)
        y = y * lax.rsqrt(ms + SUBLN_EPS) * g_ref[...] * (1.0 - lam_init)
        o_ref[0, blk * tq:(blk + 1) * tq, :] = y.astype(o_ref.dtype)
        d = jnp.concatenate([y[0:1, :]] * (2 * tq // LANES), axis=1) * 0.0
        return jnp.where(jnp.isnan(d), 0.0, d)

    total = nb * n
    scores(0, 0)
    if total > 1:
        scores(1, 1)
    anchor, anchor_at = None, -1
    for g in range(total):
        if g >= 1:
            values(g - 1)
            if g % n == 0:
                anchor, anchor_at = finish(g // n - 1), min(g + DIFF_FINISH_SLACK, total - 1)
        if g + 2 < total:
            scores(g + 2, (g + 2) % 4)
        softmax(g, anchor if g == anchor_at else None)
    values(total - 1)
    finish(nb - 1)


def _diff_attention(lam4, subln_g, q_arr, lat, ctx, lam_init):
    b, lq, _ = q_arr.shape
    hh = DIFF_HEADS
    tq = min(DIFF_TQ, lq)
    qk_c, vt_c = ctx
    lc = qk_c.shape[1]
    assert vt_c.shape == (b, hh * DIFF_VT_ROWS, lc)
    nb = math.gcd(DIFF_BLOCKS, lq // tq)
    in_specs = [pl.BlockSpec(memory_space=pltpu.SMEM),
                pl.BlockSpec((4, DIFF_HD), lambda bi, h, i: (0, 0)),
                pl.BlockSpec((1, DIFF_DV), lambda bi, h, i: (0, 0)),
                pl.BlockSpec((1, nb * tq, LANES), lambda bi, h, i: (bi, i, h))]
    args = [jnp.zeros((1,), jnp.int32), lam4, subln_g, q_arr]
    n_lat, tile = 0, lc
    if lat is not None:
        qk_l, vt_l = lat
        length = qk_l.shape[1]
        tile = vt_l.shape[2]
        n_lat = length // tile
        assert lc <= tile
        in_specs += [pl.BlockSpec((1, length, LANES), lambda bi, h, i: (bi, 0, hh + h)),
                     pl.BlockSpec((n_lat, DIFF_VT_ROWS, tile), lambda bi, h, i: (bi, h, 0))]
        args += [qk_l, vt_l]
    in_specs += [pl.BlockSpec((1, lc, LANES), lambda bi, h, i: (bi, 0, hh + h)),
                 pl.BlockSpec((1, DIFF_VT_ROWS, lc), lambda bi, h, i: (bi, h, 0))]
    args += [qk_c, vt_c]
    return pl.pallas_call(
        functools.partial(_diff_kernel, tq=tq, nb=nb, n_lat=n_lat, tile=tile, lc=lc, lam_init=lam_init),
        grid=(b, hh, lq // (nb * tq)),
        in_specs=in_specs,
        out_specs=pl.BlockSpec((1, nb * tq, LANES), lambda bi, h, i: (bi, i, h)),
        out_shape=jax.ShapeDtypeStruct((b, lq, hh * DIFF_DV), BF16),
        scratch_shapes=[pltpu.VMEM((nb, 2 * tq, LANES), BF16),
                        *[pltpu.VMEM((tile, 2 * tq), F32)] * 4,
                        *[pltpu.VMEM((1, 2 * tq), F32)] * 4,
                        *[pltpu.VMEM((tile, 2 * tq), BF16)] * 2,
                        *[pltpu.VMEM((1, 2 * tq), F32)] * 2,
                        pltpu.VMEM((nb, 1, 2 * tq), F32),
                        pltpu.VMEM((nb, DIFF_VT_ROWS, 2 * tq), F32)],
        compiler_params=_params(3, 40),
        name="diff_attention" if lat is not None else "diff_attention_ctx",
    )(*args)


def _softmax_pv(s_list, v_list):
    m = s_list[0].max(axis=-1, keepdims=True)
    for s in s_list[1:]:
        m = jnp.maximum(m, s.max(axis=-1, keepdims=True))
    den = 0.0
    o = 0.0
    for s, v in zip(s_list, v_list):
        p = jnp.exp2(s - m)
        den = den + jnp.sum(p, axis=-1, keepdims=True)
        o = o + jnp.dot(p.astype(BF16), v, preferred_element_type=F32)
    return o * (1.0 / den)


def _na_kernel(zero_ref, idx_ref, q_ref, k_ref, vtl_ref, kc_ref, vtc_ref, bias_ref, o_ref, *scr, units, n_rows, lc):
    qq_scr = scr[0]
    s_buf, mt_buf, p_buf = scr[1:4], scr[4:7], scr[7:9]
    w = GRID_W
    uq = NA_UNIT_ROWS * w
    nk = NA_UNION_ROWS * w
    step_r0 = pl.program_id(2) * (units * NA_UNIT_ROWS)
    lo_lanes = lax.broadcasted_iota(jnp.int32, (w, LANES), 1) < NA_HD

    def first_key_row(u):
        return jnp.clip(step_r0 + u * NA_UNIT_ROWS - NA_KH // 2, 0, n_rows - NA_UNION_ROWS)

    def scores(u, slot):
        start = first_key_row(u)
        cfg = (step_r0 + u * NA_UNIT_ROWS - start) // NA_UNIT_ROWS
        for g in range(NA_UNIT_ROWS):
            qg = q_ref[0, u * uq + g * w:u * uq + (g + 1) * w, :].astype(F32)
            qq_scr[g * LANES:g * LANES + w, :] = jnp.where(lo_lanes, qg, 0.0).astype(BF16)
            qq_scr[g * LANES + w:(g + 1) * LANES, :] = jnp.where(lo_lanes, 0.0, qg).astype(BF16)
        qq = qq_scr[...]
        koff = pl.multiple_of(start * w, uq)
        s_nb = lax.dot_general(k_ref[0, pl.ds(koff, nk), :], qq, NT_DIMS, preferred_element_type=F32)
        s_cx = lax.dot_general(kc_ref[0], qq, NT_DIMS, preferred_element_type=F32)
        s_buf[slot][nk:nk + lc, :] = s_cx
        col_max = []
        for g in range(NA_UNIT_ROWS):
            cols = slice(g * LANES, (g + 1) * LANES)
            m_g = jnp.max(s_cx[:, cols], axis=0, keepdims=True)
            for a in range(NA_UNION_ROWS):
                rows = slice(a * w, (a + 1) * w)
                r = idx_ref[(cfg * NA_UNION_ROWS + a) * NA_UNIT_ROWS + g]
                blk = s_nb[rows, cols] + bias_ref[0, r]
                s_buf[slot][rows, cols] = blk
                m_g = jnp.maximum(m_g, jnp.max(blk, axis=0, keepdims=True))
            col_max.append(m_g)
        mt_buf[slot][...] = jnp.concatenate(col_max, axis=1)

    def softmax(s_slot, p_slot):
        m = mt_buf[s_slot][...]
        z = pl.multiple_of(zero_ref[0], 8)
        s_buf[s_slot][pl.ds(z, 8), :] = s_buf[s_slot][pl.ds(z, 8), :]
        ch = NA_SOFTMAX_ROWS
        for r0 in range(0, nk + lc, ch):
            p_buf[p_slot][r0:r0 + ch, :] = jnp.exp2(s_buf[s_slot][r0:r0 + ch, :] - m).astype(BF16)

    def values(u, p_slot):
        pc0 = first_key_row(u) // NA_UNIT_ROWS
        acc = jnp.dot(vtc_ref[0], p_buf[p_slot][nk:nk + lc, :], preferred_element_type=F32)
        for i in range(NA_UNION_ROWS // NA_UNIT_ROWS):
            acc = acc + jnp.dot(vtl_ref[pc0 + i], p_buf[p_slot][i * uq:(i + 1) * uq, :],
                                preferred_element_type=F32)
        o = acc[0:LANES] * (1.0 / acc[LANES:LANES + 1])
        for g in range(NA_UNIT_ROWS):
            x_t = o[:, g * LANES:(g + 1) * LANES].T
            y = jnp.where(lo_lanes, x_t[0:w], x_t[w:2 * w])
            o_ref[0, u * uq + g * w:u * uq + (g + 1) * w, :] = y.astype(o_ref.dtype)

    scores(0, 0)
    if units > 1:
        scores(1, 1)
    for u in range(units):
        if u >= 1:
            values(u - 1, (u - 1) % 2)
        if u + 2 < units:
            scores(u + 2, (u + 2) % 3)
        softmax(u % 3, u % 2)
    values(units - 1, (units - 1) % 2)


def _na_attention(qk_l, vt_l, qk_c, vt_c, bias):
    b, length, _ = qk_l.shape
    lc = qk_c.shape[1]
    n_rows = length // GRID_W
    uq = NA_UNIT_ROWS * GRID_W
    nk = NA_UNION_ROWS * GRID_W
    assert n_rows >= NA_UNION_ROWS and n_rows % NA_UNIT_ROWS == 0 and vt_l.shape[2] == uq
    units = min(NA_UNITS, n_rows // NA_UNIT_ROWS)
    npair = NA_HEADS // 2
    n_pieces = length // uq
    tq = units * uq
    return pl.pallas_call(
        functools.partial(_na_kernel, units=units, n_rows=n_rows, lc=lc),
        grid=(b, npair, length // tq),
        in_specs=[pl.BlockSpec(memory_space=pltpu.SMEM),
                  pl.BlockSpec(memory_space=pltpu.SMEM),
                  pl.BlockSpec((1, tq, LANES), lambda bi, p, i: (bi, i, npair + p)),
                  pl.BlockSpec((1, length, LANES), lambda bi, p, i: (bi, 0, 2 * npair + p)),
                  pl.BlockSpec((n_pieces, DIFF_VT_ROWS, uq), lambda bi, p, i: (bi, p, 0)),
                  pl.BlockSpec((1, lc, LANES), lambda bi, p, i: (bi, 0, 2 * npair + p)),
                  pl.BlockSpec((1, DIFF_VT_ROWS, lc), lambda bi, p, i: (bi, p, 0)),
                  pl.BlockSpec((1, 2 * NA_KH, GRID_W, LANES), lambda bi, p, i: (p, 0, 0, 0))],
        out_specs=pl.BlockSpec((1, tq, LANES), lambda bi, p, i: (bi, i, p)),
        out_shape=jax.ShapeDtypeStruct((b, length, NA_HEADS * NA_HD), BF16),
        scratch_shapes=[pltpu.VMEM((2 * uq, LANES), BF16),
                        *[pltpu.VMEM((nk + lc, 2 * uq), F32)] * 3,
                        *[pltpu.VMEM((1, 2 * uq), F32)] * 3,
                        *[pltpu.VMEM((nk + lc, 2 * uq), BF16)] * 2],
        compiler_params=_params(3, 40),
        name="neighbourhood_attention",
    )(jnp.zeros((1,), jnp.int32), _na_bias_rows(), qk_l, qk_l, vt_l, qk_c, vt_c, bias)


def _ctx_attn_kernel(q_ref, k_ref, v_ref, o_ref):
    q = q_ref[0].astype(F32)
    k = k_ref[0]
    v = v_ref[0]
    lane = lax.broadcasted_iota(jnp.int32, q.shape, 1)
    outs = []
    for hh in range(2):
        sel = (lane < NA_HD) if hh == 0 else (lane >= NA_HD)
        qm = jnp.where(sel, q, 0.0).astype(BF16)
        s = lax.dot_general(qm, k, NT_DIMS, preferred_element_type=F32)
        outs.append(_softmax_pv([s], [v]))
    o_ref[0] = jnp.where(lane < NA_HD, outs[0], outs[1]).astype(o_ref.dtype)


def _ctx_attention(qkv_c):
    b, lc, _ = qkv_c.shape
    npair = NA_HEADS // 2
    return pl.pallas_call(
        _ctx_attn_kernel,
        grid=(b, npair),
        in_specs=[pl.BlockSpec((1, lc, LANES), lambda bi, p: (bi, 0, npair + p)),
                  pl.BlockSpec((1, lc, LANES), lambda bi, p: (bi, 0, 2 * npair + p)),
                  pl.BlockSpec((1, lc, LANES), lambda bi, p: (bi, 0, 3 * npair + p))],
        out_specs=pl.BlockSpec((1, lc, LANES), lambda bi, p: (bi, 0, p)),
        out_shape=jax.ShapeDtypeStruct((b, lc, NA_HEADS * NA_HD), BF16),
        compiler_params=_params(2, 32),
        name="ctx_attention",
    )(qkv_c, qkv_c, qkv_c)


def _merge_kernel(oret_ref, gates_ref, odiff_ref, ona_ref, x_ref, g1_ref, lng_ref, lnb_ref,
                  wpa_ref, wpb_ref, wpc_ref, wo_ref, o_ref, *, alpha):
    d = D_MODEL
    ya_parts = []
    for hh in range(RET_HEADS):
        a = hh * RET_DK
        seg = _layer_norm_rows(oret_ref[:, a:a + RET_DK])
        g = gates_ref[:, a:a + RET_DK]
        ya_parts.append((g * _sigmoid(g) * seg).astype(BF16))
    ya_in = jnp.concatenate(ya_parts, axis=1)
    y_a = jnp.dot(ya_in, wpa_ref[...], preferred_element_type=F32)
    y_b = jnp.dot(odiff_ref[...], wpb_ref[...], preferred_element_type=F32)
    y_c = jnp.dot(ona_ref[...], wpc_ref[...], preferred_element_type=F32)
    m = (_sigmoid(gates_ref[:, d:2 * d]) * y_a + _sigmoid(gates_ref[:, 2 * d:3 * d]) * y_b
         + _sigmoid(gates_ref[:, 3 * d:4 * d]) * y_c)
    y = jnp.dot(m.astype(BF16), wo_ref[...], preferred_element_type=F32)
    z = alpha * x_ref[...] + g1_ref[0] * y
    o_ref[...] = _layer_norm_rows(z) * lng_ref[...] + lnb_ref[...]


def _merge(o_ret, gates, o_diff, o_na, x, g1, ln_g, ln_b, w_pa, w_pb, w_pc, w_o, mod_row, alpha):
    r, d = x.shape
    tm = min(MERGE_TILE, r)
    row = lambda i: (i, 0)
    const = lambda i: (0, 0)
    wspec = pl.BlockSpec((d, d), const, pipeline_mode=pl.Buffered(1))
    return pl.pallas_call(
        functools.partial(_merge_kernel, alpha=alpha),
        grid=(r // tm,),
        in_specs=[pl.BlockSpec((tm, d), row), pl.BlockSpec((tm, 4 * d), row), pl.BlockSpec((tm, d), row),
                  pl.BlockSpec((tm, d), row), pl.BlockSpec((tm, d), row),
                  pl.BlockSpec((1, 1, d), lambda i: (mod_row(i, tm), 0, 0)),
                  pl.BlockSpec((1, d), const), pl.BlockSpec((1, d), const),
                  wspec, wspec, wspec, wspec],
        out_specs=pl.BlockSpec((tm, d), row),
        out_shape=jax.ShapeDtypeStruct((r, d), F32),
        compiler_params=_params(1, 52),
        name="merge_out_proj",
    )(o_ret, gates, o_diff, o_na, x, g1, ln_g, ln_b, w_pa, w_pb, w_pc, w_o)


def _ffn_kernel(x_ref, sh_ref, sc_ref, g2_ref, lng_ref, lnb_ref, w1_ref, w2_ref, o_ref, *, alpha, ff_chunk):
    x = x_ref[...]
    h = (_layer_norm_rows(x) * (1.0 + sc_ref[0]) + sh_ref[0]).astype(BF16)
    acc = jnp.zeros(x.shape, F32)
    for cidx in range(D_FF // ff_chunk):
        a = cidx * ff_chunk
        u = jnp.maximum(jnp.dot(h, w1_ref[:, a:a + ff_chunk], preferred_element_type=F32), 0.0)
        acc = acc + jnp.dot((u * u).astype(BF16), w2_ref[a:a + ff_chunk, :], preferred_element_type=F32)
    z = alpha * x + g2_ref[0] * acc
    o_ref[...] = _layer_norm_rows(z) * lng_ref[...] + lnb_ref[...]


def _ffn(x, shift, scale, g2, ln_g, ln_b, w1, w2, mod_row, alpha):
    r, d = x.shape
    tm = min(FFN_TILE, r)
    row = lambda i: (i, 0)
    const = lambda i: (0, 0)
    mspec = pl.BlockSpec((1, 1, d), lambda i: (mod_row(i, tm), 0, 0))
    return pl.pallas_call(
        functools.partial(_ffn_kernel, alpha=alpha, ff_chunk=1024),
        grid=(r // tm,),
        in_specs=[pl.BlockSpec((tm, d), row), mspec, mspec, mspec,
                  pl.BlockSpec((1, d), const), pl.BlockSpec((1, d), const),
                  pl.BlockSpec((d, D_FF), const, pipeline_mode=pl.Buffered(1)),
                  pl.BlockSpec((D_FF, d), const, pipeline_mode=pl.Buffered(1))],
        out_specs=pl.BlockSpec((tm, d), row),
        out_shape=jax.ShapeDtypeStruct((r, d), F32),
        compiler_params=_params(1, 56),
        name="ffn",
    )(x, shift, scale, g2, ln_g, ln_b, w1, w2)


def _retention_tables(length):
    t = jnp.arange(length, dtype=F32)
    inv = ROPE_BASE ** (-jnp.linspace(0.0, 1.0, RET_DK // 2, dtype=F32))
    ang = t[:, None] * inv[None, :]
    return jnp.cos(ang), jnp.sin(ang)


def _axial_tables(length):
    t = jnp.arange(length)
    row = (t // GRID_W).astype(F32)
    col = (t % GRID_W).astype(F32)
    half = DIFF_HD // 2
    inv = ROPE_BASE ** (-jnp.arange(0, half, 2, dtype=F32) / half)
    ang_r = row[:, None] * inv[None, :]
    ang_c = col[:, None] * inv[None, :]
    zero = jnp.zeros_like(ang_r)
    unit_ang = jnp.concatenate([ang_r, ang_r, ang_c, ang_c], axis=-1)
    cos = jnp.cos(unit_ang)
    sin_r, sin_c = jnp.sin(ang_r), jnp.sin(ang_c)
    sin_lo = jnp.concatenate([-sin_r, zero, -sin_c, zero], axis=-1)
    sin_hi = jnp.concatenate([zero, sin_r, zero, sin_c], axis=-1)
    rep = LANES // DIFF_HD
    return tuple(jnp.tile(a, (1, rep)) for a in (cos, sin_lo, sin_hi))


def _na_bias_table(rpb):
    nh, nr, _ = rpb.shape
    w = GRID_W
    wid = 2 * w - 1
    p = jnp.pad(rpb.astype(F32) * math.log2(math.e), ((0, 0), (0, 0), (w - NA_KW, w - NA_KW)))
    y = p[..., ::-1]
    a = jnp.broadcast_to(y[:, :, None, :], (nh, nr, w, wid))
    a = jnp.pad(a, ((0, 0), (0, 0), (0, 0), (0, 1))).reshape(nh, nr, w * (wid + 1))
    a = a[:, :, :w * wid].reshape(nh, nr, w, wid)
    m = a[..., w - 1:]
    j = jnp.arange(w)
    cs = jnp.clip(j - NA_KW // 2, 0, w - NA_KW)
    col_mask = (j[:, None] >= cs[None, :]) & (j[:, None] < cs[None, :] + NA_KW)
    m = jnp.where(col_mask[None, None], m, NEG_BIG)
    m = jnp.pad(m, ((0, 0), (0, 1), (0, 0), (0, 0)), constant_values=NEG_BIG)
    m = m.reshape(nh // 2, 2, nr + 1, w, w).transpose(0, 2, 3, 1, 4)
    return m.reshape(nh // 2, nr + 1, w, 2 * w)


def _na_bias_rows():
    un, g_rows = NA_UNION_ROWS, NA_UNIT_ROWS
    rows = []
    for cfg in range(3):
        for a in range(un):
            for g in range(g_rows):
                q_row = cfg * g_rows + g
                win0 = (0, g, un - NA_KH)[cfg]
                in_win = win0 <= a < win0 + NA_KH
                rows.append(a - q_row + NA_KH - 1 if in_win else 2 * NA_KH - 1)
    return jnp.asarray(rows, jnp.int32)


def _column_scale():
    d = D_MODEL
    cs = jnp.ones((IN_WIDTH,), F32)
    cs = cs.at[d:2 * d].set(RET_DK ** -0.5)
    cs = cs.at[4 * d:5 * d].set(DIFF_HD ** -0.5 * math.log2(math.e))
    cs = cs.at[7 * d:8 * d].set(NA_HD ** -0.5 * math.log2(math.e))
    return cs.reshape(1, IN_WIDTH)


def _project_all(h, w, colscale, tm, rope, ret_tables, ax_tables, tiles_per_batch, tk_out, na_tk_out, plain_na_v):
    if rope:
        ret_qk = _proj(h, w, colscale, tm, 0, 1, 2, BF16, "rope_ret", ret_tables, tiles_per_batch)
        diff_qk = _proj(h, w, colscale, tm, 4, 5, 2, BF16, "rope_diff", ax_tables, tiles_per_batch)
    else:
        ret_qk = _proj(h, w, colscale, tm, 0, 1, 2, BF16)
        diff_qk = _proj(h, w, colscale, tm, 4, 5, 2, BF16)
    v_na = _proj(h, w, colscale, tm, 2, 7, 4 if plain_na_v else 3, BF16)
    gates = _proj(h, w, colscale, tm, 3, 10, 4, F32)
    diff_vt = _proj(h, w, colscale, tm, 6, 7, 1, BF16, tk_out=tk_out)
    na_vt = _proj(h, w, colscale, tm, 9, 10, 1, BF16, tk_out=na_tk_out)
    return ret_qk, diff_qk, v_na, gates, diff_vt, na_vt


def kernel(x, c, ctx, c_ctx, w_mod, b_mod, w_in, ret_decay_f, ret_decay_b, diff_lq1, diff_lk1, diff_lq2, diff_lk2,
           diff_subln_g, na_rpb, w_pa, w_pb, w_pc, w_o, ln1_g, ln1_b, w_ff1, w_ff2, ln2_g, ln2_b):
    b, length, d = x.shape
    lc = ctx.shape[1]
    depth = w_mod.shape[0]
    n_cond = SUBLANES
    assert d == D_MODEL and b < n_cond and length % min(ROW_TILE, length) == 0 and length % GRID_W == 0
    assert lc % RET_CHUNK == 0 or lc < RET_CHUNK
    alpha = (2.0 * depth) ** 0.25

    cvec = jnp.zeros((n_cond, d), F32).at[:b].set(c).at[b].set(c_ctx)
    mods = _modulation(cvec, w_mod, b_mod)

    lat_row = lambda i, tm: (i * tm) // length
    ctx_row = lambda i, tm: b
    colscale = _column_scale()
    ret_tables = _retention_tables(length)
    ax_tables = _axial_tables(length)

    x_l = x.reshape(b * length, d)
    x_c = ctx.reshape(b * lc, d)
    zero_state = jnp.zeros((b, RET_HEADS, RET_DK, RET_DK), F32)
    for l in range(depth):
        with_ctx_out = l < depth - 1
        mod = [mods[l, :, k * d:(k + 1) * d].reshape(n_cond, 1, d) for k in range(6)]
        sh1, sc1, g1, sh2, sc2, g2 = mod
        w = (w_in, l)
        wpa, wpb, wpc, wo = (a[l].astype(BF16) for a in (w_pa, w_pb, w_pc, w_o))
        w1, w2 = w_ff1[l].astype(BF16), w_ff2[l].astype(BF16)
        lng1, lnb1 = ln1_g[l].reshape(1, d), ln1_b[l].reshape(1, d)
        lng2, lnb2 = ln2_g[l].reshape(1, d), ln2_b[l].reshape(1, d)
        lam_init = 0.8 - 0.6 * math.exp(-0.3 * l)
        lam4 = jnp.stack([diff_lq1[l], diff_lk1[l], diff_lq2[l], diff_lk2[l]]).astype(F32)
        subln = diff_subln_g[l].reshape(1, DIFF_DV).astype(F32)
        bias = _na_bias_table(na_rpb[l])

        h_l = _ln_mod(x_l, sh1, sc1, lat_row)
        h_c = _ln_mod(x_c, sh1, sc1, ctx_row)
        tm_l = min(ROW_TILE, length)
        rqk_l, dqk_l, vna_l, gates_l, dvt_l, nvt_l = _project_all(
            h_l, w, colscale, tm_l, True, ret_tables, ax_tables, length // tm_l, min(DIFF_TILE, length),
            NA_UNIT_ROWS * GRID_W, False)
        rqk_c, dqk_c, vna_c, gates_c, dvt_c, nvt_c = _project_all(
            h_c, w, colscale, min(ROW_TILE, b * lc), False, (), (), 1, lc, lc, True)
        sh3 = lambda a, n: a.reshape(b, n, a.shape[-1])

        decays = _retention_decays(ret_decay_f[l], ret_decay_b[l], min(RET_CHUNK, lc))
        rqk_l3, vna_l3, rqk_c3, vna_c3 = sh3(rqk_l, length), sh3(vna_l, length), sh3(rqk_c, lc), sh3(vna_c, lc)
        o_cf, s_cf = _retention(decays, rqk_c3, vna_c3, zero_state, None, False)
        o_rc, s_cb = _retention(decays, rqk_c3, vna_c3, zero_state, o_cf, True)
        o_lf, _ = _retention(decays, rqk_l3, vna_l3, s_cf, None, False)
        o_rl, _ = _retention(decays, rqk_l3, vna_l3, s_cb, o_lf, True)

        dqk_l3, dqk_c3 = sh3(dqk_l, length), sh3(dqk_c, lc)
        o_dl = _diff_attention(lam4, subln, dqk_l3, (dqk_l3, dvt_l), (dqk_c3, dvt_c), lam_init)
        o_nl = _na_attention(vna_l3, nvt_l, vna_c3, nvt_c, bias)

        x_l = _merge(o_rl.reshape(b * length, d), gates_l, o_dl.reshape(b * length, d), o_nl.reshape(b * length, d),
                     x_l, g1, lng1, lnb1, wpa, wpb, wpc, wo, lat_row, alpha)
        x_l = _ffn(x_l, sh2, sc2, g2, lng2, lnb2, w1, w2, lat_row, alpha)
        if with_ctx_out:
            o_dc = _diff_attention(lam4, subln, dqk_c3, None, (dqk_c3, dvt_c), lam_init)
            o_nc = _ctx_attention(vna_c3)
            x_c = _merge(o_rc.reshape(b * lc, d), gates_c, o_dc.reshape(b * lc, d), o_nc.reshape(b * lc, d),
                         x_c, g1, lng1, lnb1, wpa, wpb, wpc, wo, ctx_row, alpha)
            x_c = _ffn(x_c, sh2, sc2, g2, lng2, lnb2, w1, w2, ctx_row, alpha)
    return x_l.reshape(b, length, d)
```

```python
import functools
import math

import jax
import jax.numpy as jnp
from jax import lax
from jax.experimental import pallas as pl
from jax.experimental.pallas import tpu as pltpu

F32 = jnp.float32
BF16 = jnp.bfloat16

D_MODEL = 1024
GRID_W = 64
RET_HEADS, RET_DK = 4, 256
DIFF_HEADS, DIFF_HD, DIFF_DV = 8, 64, 128
NA_HEADS, NA_HD, NA_KH, NA_KW = 16, 64, 8, 16
D_FF = 4 * D_MODEL
ROPE_BASE = 10000.0
LN_EPS = 1e-6
SUBLN_EPS = 1e-5
IN_WIDTH = 13 * D_MODEL
NEG_BIG = -1e30

LANES = 128
SUBLANES = 8
SUBLANES_BF16 = 16
VMEM_LIMIT_CAP = 56 << 20

ROW_TILE = 2048
PROJ_TN = 1024
LN_TILE = 512
MERGE_TILE = 512
FFN_TILE = 1024
RET_CHUNK = 256
RET_BLOCK = 1024
DIFF_TQ = 256
DIFF_BLOCKS = 4
DIFF_FINISH_SLACK = 2
DIFF_TILE = 512
DIFF_VT_ROWS = DIFF_DV + SUBLANES_BF16
DIFF_SOFTMAX_ROWS = 64
NA_UNIT_ROWS = 4
NA_UNION_ROWS = 12
NA_UNITS = 16
NA_SOFTMAX_ROWS = 64

NT_DIMS = (((1,), (1,)), ((), ()))
TN_DIMS = (((0,), (0,)), ((), ()))


def _params(n_axes, vmem_mb):
    return pltpu.CompilerParams(dimension_semantics=("arbitrary",) * n_axes,
                                vmem_limit_bytes=min(vmem_mb << 20, VMEM_LIMIT_CAP))


def _sigmoid(x):
    return 1.0 / (1.0 + jnp.exp(-x))


def _layer_norm_rows(x):
    mu = jnp.mean(x, axis=-1, keepdims=True)
    xc = x - mu
    var = jnp.mean(xc * xc, axis=-1, keepdims=True)
    return xc * lax.rsqrt(var + LN_EPS)


def _mod_kernel(c_ref, w_ref, b_ref, o_ref):
    c = c_ref[...]
    a = (c * _sigmoid(c)).astype(BF16)
    o_ref[0] = jnp.dot(a, w_ref[0].astype(BF16), preferred_element_type=F32) + b_ref[0]


def _modulation(cvec, w_mod, b_mod):
    depth = w_mod.shape[0]
    d = D_MODEL
    return pl.pallas_call(
        _mod_kernel,
        grid=(depth, 6),
        in_specs=[pl.BlockSpec((SUBLANES, d), lambda l, j: (0, 0)),
                  pl.BlockSpec((1, d, d), lambda l, j: (l, 0, j)),
                  pl.BlockSpec((1, 1, d), lambda l, j: (l, 0, j))],
        out_specs=pl.BlockSpec((1, SUBLANES, d), lambda l, j: (l, 0, j)),
        out_shape=jax.ShapeDtypeStruct((depth, SUBLANES, 6 * d), F32),
        compiler_params=_params(2, 32),
        name="modulation",
    )(cvec, w_mod, b_mod.reshape(depth, 1, 6 * d))


def _ln_mod_kernel(x_ref, sh_ref, sc_ref, o_ref):
    y = _layer_norm_rows(x_ref[...])
    o_ref[...] = (y * (1.0 + sc_ref[0]) + sh_ref[0]).astype(o_ref.dtype)


def _ln_mod(x, shift, scale, mod_row):
    r, d = x.shape
    tm = min(LN_TILE, r)
    return pl.pallas_call(
        _ln_mod_kernel,
        grid=(r // tm,),
        in_specs=[pl.BlockSpec((tm, d), lambda i: (i, 0)),
                  pl.BlockSpec((1, 1, d), lambda i: (mod_row(i, tm), 0, 0)),
                  pl.BlockSpec((1, 1, d), lambda i: (mod_row(i, tm), 0, 0))],
        out_specs=pl.BlockSpec((tm, d), lambda i: (i, 0)),
        out_shape=jax.ShapeDtypeStruct((r, d), BF16),
        compiler_params=_params(1, 32),
        name="ln_modulate",
    )(x, shift, scale)


def _proj_kernel(h_ref, w_ref, cs_ref, *rest, mode, transposed, tn, tk_out):
    o_ref = rest[-1]
    acc = jnp.dot(h_ref[...], w_ref[0].astype(BF16), preferred_element_type=F32) * cs_ref[...]
    if mode == "rope_ret":
        cos = rest[0][...]
        sin = rest[1][...]
        half = RET_DK // 2
        for hh in range(tn // RET_DK):
            a = hh * RET_DK
            x1 = acc[:, a:a + half]
            x2 = acc[:, a + half:a + RET_DK]
            o_ref[:, a:a + half] = (x1 * cos - x2 * sin).astype(o_ref.dtype)
            o_ref[:, a + half:a + RET_DK] = (x1 * sin + x2 * cos).astype(o_ref.dtype)
    elif mode == "rope_diff":
        cos = rest[0][...]
        sin_lo = rest[1][...]
        sin_hi = rest[2][...]
        q16 = DIFF_HD // 4
        for g in range(tn // LANES):
            x = acc[:, g * LANES:(g + 1) * LANES]
            y = x * cos + pltpu.roll(x, LANES - q16, 1) * sin_lo + pltpu.roll(x, q16, 1) * sin_hi
            o_ref[:, g * LANES:(g + 1) * LANES] = y.astype(o_ref.dtype)
    elif transposed:
        acc_t = acc.T
        ones = jnp.ones((DIFF_VT_ROWS - DIFF_DV, tk_out), o_ref.dtype)
        for a in range(o_ref.shape[0]):
            for hh in range(tn // DIFF_DV):
                r0 = hh * DIFF_VT_ROWS
                o_ref[a, r0:r0 + DIFF_DV, :] = acc_t[hh * DIFF_DV:(hh + 1) * DIFF_DV,
                                                     a * tk_out:(a + 1) * tk_out].astype(o_ref.dtype)
                o_ref[a, r0 + DIFF_DV:r0 + DIFF_VT_ROWS, :] = ones
    else:
        o_ref[...] = acc.astype(o_ref.dtype)


def _proj(h, w, colscale, tm, first, rest0, ntiles, out_dtype, mode="plain", tables=(), tiles_per_batch=1,
          tk_out=None):
    w_all, layer = w
    r, d = h.shape
    tn = PROJ_TN
    ncols = ntiles * tn
    assert r % tm == 0
    transposed = tk_out is not None
    wcol = lambda j: jnp.where(j == 0, first, rest0 + j - 1)
    in_specs = [pl.BlockSpec((tm, d), lambda i, j: (i, 0)),
                pl.BlockSpec((1, d, tn), lambda i, j: (layer, 0, wcol(j))),
                pl.BlockSpec((1, tn), lambda i, j: (0, wcol(j)))]
    for _ in tables:
        in_specs.append(pl.BlockSpec((tm, LANES), lambda i, j: (i % tiles_per_batch, 0)))
    if transposed:
        assert tm % tk_out == 0
        na = tm // tk_out
        rows_out = tn // DIFF_DV * DIFF_VT_ROWS
        out_specs = pl.BlockSpec((na, rows_out, tk_out), lambda i, j: (i, j, 0))
        out_shape = jax.ShapeDtypeStruct((r // tk_out, ncols // DIFF_DV * DIFF_VT_ROWS, tk_out), out_dtype)
    else:
        out_specs = pl.BlockSpec((tm, tn), lambda i, j: (i, j))
        out_shape = jax.ShapeDtypeStruct((r, ncols), out_dtype)
    return pl.pallas_call(
        functools.partial(_proj_kernel, mode=mode, transposed=transposed, tn=tn, tk_out=tk_out),
        grid=(r // tm, ntiles),
        in_specs=in_specs,
        out_specs=out_specs,
        out_shape=out_shape,
        compiler_params=_params(2, 52),
        name="in_proj_" + mode + ("_t" if transposed else ""),
    )(h, w_all, colscale, *tables)


def _ret_tables_kernel(dec_ref, intra_ref, qd_ref, kd_ref, cd_ref, *, chunk):
    direction = pl.program_id(0)
    head = pl.program_id(1)
    cf = float(chunk)
    dv = jnp.full((1, 1), dec_ref[direction, head], F32)
    lg = jnp.minimum(dv, 0.0) - jnp.log(1.0 + jnp.exp(-jnp.abs(dv)))
    rev = direction == 1
    ii = lax.broadcasted_iota(jnp.int32, (chunk, chunk), 0).astype(F32)
    jj = lax.broadcasted_iota(jnp.int32, (chunk, chunk), 1).astype(F32)
    dist = jnp.where(rev, jj - ii, ii - jj)
    intra_ref[0, 0] = jnp.where(dist >= 0.0, jnp.exp(lg * jnp.maximum(dist, 0.0)), 0.0)
    i1 = lax.broadcasted_iota(jnp.int32, (chunk, RET_DK), 0).astype(F32)
    qd_ref[0, 0] = jnp.exp(lg * jnp.where(rev, cf - i1, i1 + 1.0))
    kd_ref[0, 0] = jnp.exp(lg * jnp.where(rev, i1, cf - 1.0 - i1))
    cd_ref[0, 0] = jnp.exp(jnp.broadcast_to(lg, (SUBLANES, RET_DK)) * cf)


def _retention_decays(dec_f, dec_b, chunk):
    hh, dk = RET_HEADS, RET_DK
    blk = lambda *shape: pl.BlockSpec((1, 1) + shape, lambda d, h: (d, h, 0, 0))
    return pl.pallas_call(
        functools.partial(_ret_tables_kernel, chunk=chunk),
        grid=(2, hh),
        in_specs=[pl.BlockSpec(memory_space=pltpu.SMEM)],
        out_specs=[blk(chunk, chunk), blk(chunk, dk), blk(chunk, dk), blk(SUBLANES, dk)],
        out_shape=[jax.ShapeDtypeStruct((2, hh, chunk, chunk), F32),
                   jax.ShapeDtypeStruct((2, hh, chunk, dk), F32),
                   jax.ShapeDtypeStruct((2, hh, chunk, dk), F32),
                   jax.ShapeDtypeStruct((2, hh, SUBLANES, dk), F32)],
        compiler_params=_params(2, 32),
        name="retention_tables",
    )(jnp.stack([dec_f, dec_b]).astype(F32))


def _ret_kernel(intra_ref, qd_ref, kd_ref, cd_ref, qk_ref, v_ref, s0_ref, *rest, reverse, has_prev, chunk, n_chunks):
    if has_prev:
        prev_ref, o_ref, sfin_ref, s_scr = rest
    else:
        o_ref, sfin_ref, s_scr = rest
    @pl.when(pl.program_id(1) == 0)
    def _load_initial_state():
        s_scr[...] = s0_ref[0]

    dk = RET_DK
    order = range(n_chunks - 1, -1, -1) if reverse else range(n_chunks)
    for h in range(RET_HEADS):
        cols = slice(h * dk, (h + 1) * dk)
        kcols = slice((RET_HEADS + h) * dk, (RET_HEADS + h + 1) * dk)
        s = s_scr[h]
        for c in order:
            rows = slice(c * chunk, (c + 1) * chunk)
            q = qk_ref[0, rows, cols]
            k = qk_ref[0, rows, kcols]
            v = v_ref[0, rows, cols]
            att = lax.dot_general(q, k, NT_DIMS, preferred_element_type=F32) * intra_ref[0, h]
            qd = (q.astype(F32) * qd_ref[0, h]).astype(BF16)
            o = (jnp.dot(att.astype(BF16), v, preferred_element_type=F32)
                 + jnp.dot(qd, s.astype(BF16), preferred_element_type=F32))
            kd = (k.astype(F32) * kd_ref[0, h]).astype(BF16)
            s = s * cd_ref[0, h, 0:1, :] + lax.dot_general(kd, v, TN_DIMS, preferred_element_type=F32)
            if has_prev:
                o = o + prev_ref[0, rows, cols]
            o_ref[0, rows, cols] = o
        s_scr[h] = s
        sfin_ref[0, h] = s


def _retention(tables, qk, v_arr, s0, prev, reverse):
    b, length, _ = qk.shape
    hh, dk = RET_HEADS, RET_DK
    width = hh * dk
    c = tables[0].shape[-1]
    rb = min(RET_BLOCK, length)
    n = length // rb
    direction = 1 if reverse else 0
    pos = (lambda i: n - 1 - i) if reverse else (lambda i: i)
    tab = lambda *shape: pl.BlockSpec((1, hh) + shape, lambda bi, i: (direction, 0, 0, 0))
    in_specs = [tab(c, c), tab(c, dk), tab(c, dk), tab(SUBLANES, dk),
                pl.BlockSpec((1, rb, 2 * width), lambda bi, i: (bi, pos(i), 0)),
                pl.BlockSpec((1, rb, width), lambda bi, i: (bi, pos(i), 0)),
                pl.BlockSpec((1, hh, dk, dk), lambda bi, i: (bi, 0, 0, 0))]
    args = [*tables, qk, v_arr, s0]
    if prev is not None:
        in_specs.append(pl.BlockSpec((1, rb, width), lambda bi, i: (bi, pos(i), 0)))
        args.append(prev)
    return pl.pallas_call(
        functools.partial(_ret_kernel, reverse=reverse, has_prev=prev is not None, chunk=c, n_chunks=rb // c),
        grid=(b, n),
        in_specs=in_specs,
        out_specs=[pl.BlockSpec((1, rb, width), lambda bi, i: (bi, pos(i), 0)),
                   pl.BlockSpec((1, hh, dk, dk), lambda bi, i: (bi, 0, 0, 0))],
        out_shape=[jax.ShapeDtypeStruct((b, length, width), F32),
                   jax.ShapeDtypeStruct((b, hh, dk, dk), F32)],
        scratch_shapes=[pltpu.VMEM((hh, dk, dk), F32)],
        compiler_params=_params(2, 48),
        name="retention_bwd" if reverse else "retention_fwd",
    )(*args)


def _diff_kernel(zero_ref, lam_ref, g_ref, q_ref, *rest, tq, nb, n_lat, tile, lc, lam_init):
    if n_lat:
        kl_ref, vtl_ref = rest[:2]
        rest = rest[2:]
    kc_ref, vtc_ref, o_ref, qq_scr = rest[:4]
    s_buf, mt_buf, p_buf, al_buf = rest[4:8], rest[8:12], rest[12:14], rest[14:16]
    m_scr, acc_scr = rest[16:]
    n = n_lat + 1
    rows_of = lambda j: lc if j == n_lat else tile

    def scores(g, slot):
        blk, j = divmod(g, n)
        if j == 0:
            q = q_ref[0, blk * tq:(blk + 1) * tq, :].astype(F32)
            lane = lax.broadcasted_iota(jnp.int32, q.shape, 1)
            qq_scr[blk, 0:tq, :] = jnp.where(lane < DIFF_HD, q, 0.0).astype(BF16)
            qq_scr[blk, tq:2 * tq, :] = jnp.where(lane >= DIFF_HD, q, 0.0).astype(BF16)
            m_scr[blk] = jnp.full(m_scr.shape[1:], NEG_BIG, F32)
        k = kc_ref[0] if j == n_lat else kl_ref[0, j * tile:(j + 1) * tile, :]
        s = lax.dot_general(k, qq_scr[blk], NT_DIMS, preferred_element_type=F32)
        mt_buf[slot][...] = jnp.max(s, axis=0, keepdims=True)
        s_buf[slot][0:k.shape[0], :] = s

    def softmax(g, anchor):
        blk, j = divmod(g, n)
        s_slot, p_slot, rows = g % 4, g % 2, rows_of(j)
        m_old = m_scr[blk]
        if anchor is not None:
            m_old = jnp.maximum(m_old, anchor + NEG_BIG)
        m_new = jnp.maximum(m_old, mt_buf[s_slot][...])
        al_buf[p_slot][...] = jnp.exp2(m_old - m_new)
        m_scr[blk] = m_new
        z = pl.multiple_of(zero_ref[0], 8)
        s_buf[s_slot][pl.ds(z, 8), :] = s_buf[s_slot][pl.ds(z, 8), :]
        ch = DIFF_SOFTMAX_ROWS
        for r0 in range(0, rows, ch):
            p_buf[p_slot][r0:r0 + ch, :] = jnp.exp2(s_buf[s_slot][r0:r0 + ch, :] - m_new).astype(BF16)

    def values(g):
        blk, j = divmod(g, n)
        slot = g % 2
        vt = vtc_ref[0] if j == n_lat else vtl_ref[j]
        pv = jnp.dot(vt, p_buf[slot][0:vt.shape[1], :], preferred_element_type=F32)
        acc_scr[blk] = pv if j == 0 else acc_scr[blk] * al_buf[slot][...] + pv

    def finish(blk):
        acc = acc_scr[blk]
        o = acc[0:DIFF_DV] * (1.0 / acc[DIFF_DV:DIFF_DV + 1])
        lam4 = lam_ref[...]
        s1 = jnp.sum(lam4[0:1] * lam4[1:2], axis=-1, keepdims=True)
        s2 = jnp.sum(lam4[2:3] * lam4[3:4], axis=-1, keepdims=True)
        lam = jnp.exp(s1) - jnp.exp(s2) + lam_init
        y = (o[:, :tq] - lam * o[:, tq:]).T
        ms = jnp.mean(y * y, axis=-1, keepdims=True)
        y = y * lax.rsqrt(ms + SUBLN_EPS) * g_ref[...] * (1.0 - lam_init)
        o_ref[0, blk * tq:(blk + 1) * tq, :] = y.astype(o_ref.dtype)
        d = jnp.concatenate([y[0:1, :]] * (2 * tq // LANES), axis=1) * 0.0
        return jnp.where(jnp.isnan(d), 0.0, d)

    total = nb * n
    scores(0, 0)
    if total > 1:
        scores(1, 1)
    anchor, anchor_at = None, -1
    for g in range(total):
        if g >= 1:
            values(g - 1)
            if g % n == 0:
                anchor, anchor_at = finish(g // n - 1), min(g + DIFF_FINISH_SLACK, total - 1)
        if g + 2 < total:
            scores(g + 2, (g + 2) % 4)
        softmax(g, anchor if g == anchor_at else None)
    values(total - 1)
    finish(nb - 1)


def _diff_attention(lam4, subln_g, q_arr, lat, ctx, lam_init):
    b, lq, _ = q_arr.shape
    hh = DIFF_HEADS
    tq = min(DIFF_TQ, lq)
    qk_c, vt_c = ctx
    lc = qk_c.shape[1]
    assert vt_c.shape == (b, hh * DIFF_VT_ROWS, lc)
    nb = math.gcd(DIFF_BLOCKS, lq // tq)
    in_specs = [pl.BlockSpec(memory_space=pltpu.SMEM),
                pl.BlockSpec((4, DIFF_HD), lambda bi, h, i: (0, 0)),
                pl.BlockSpec((1, DIFF_DV), lambda bi, h, i: (0, 0)),
                pl.BlockSpec((1, nb * tq, LANES), lambda bi, h, i: (bi, i, h))]
    args = [jnp.zeros((1,), jnp.int32), lam4, subln_g, q_arr]
    n_lat, tile = 0, lc
    if lat is not None:
        qk_l, vt_l = lat
        length = qk_l.shape[1]
        tile = vt_l.shape[2]
        n_lat = length // tile
        assert lc <= tile
        in_specs += [pl.BlockSpec((1, length, LANES), lambda bi, h, i: (bi, 0, hh + h)),
                     pl.BlockSpec((n_lat, DIFF_VT_ROWS, tile), lambda bi, h, i: (bi, h, 0))]
        args += [qk_l, vt_l]
    in_specs += [pl.BlockSpec((1, lc, LANES), lambda bi, h, i: (bi, 0, hh + h)),
                 pl.BlockSpec((1, DIFF_VT_ROWS, lc), lambda bi, h, i: (bi, h, 0))]
    args += [qk_c, vt_c]
    return pl.pallas_call(
        functools.partial(_diff_kernel, tq=tq, nb=nb, n_lat=n_lat, tile=tile, lc=lc, lam_init=lam_init),
        grid=(b, hh, lq // (nb * tq)),
        in_specs=in_specs,
        out_specs=pl.BlockSpec((1, nb * tq, LANES), lambda bi, h, i: (bi, i, h)),
        out_shape=jax.ShapeDtypeStruct((b, lq, hh * DIFF_DV), BF16),
        scratch_shapes=[pltpu.VMEM((nb, 2 * tq, LANES), BF16),
                        *[pltpu.VMEM((tile, 2 * tq), F32)] * 4,
                        *[pltpu.VMEM((1, 2 * tq), F32)] * 4,
                        *[pltpu.VMEM((tile, 2 * tq), BF16)] * 2,
                        *[pltpu.VMEM((1, 2 * tq), F32)] * 2,
                        pltpu.VMEM((nb, 1, 2 * tq), F32),
                        pltpu.VMEM((nb, DIFF_VT_ROWS, 2 * tq), F32)],
        compiler_params=_params(3, 40),
        name="diff_attention" if lat is not None else "diff_attention_ctx",
    )(*args)


def _softmax_pv(s_list, v_list):
    m = s_list[0].max(axis=-1, keepdims=True)
    for s in s_list[1:]:
        m = jnp.maximum(m, s.max(axis=-1, keepdims=True))
    den = 0.0
    o = 0.0
    for s, v in zip(s_list, v_list):
        p = jnp.exp2(s - m)
        den = den + jnp.sum(p, axis=-1, keepdims=True)
        o = o + jnp.dot(p.astype(BF16), v, preferred_element_type=F32)
    return o * (1.0 / den)


def _na_bias_kernel(tab_ref, o_ref, *, rows):
    w = GRID_W
    for cfg in range(3):
        for a in range(NA_UNION_ROWS):
            for g in range(NA_UNIT_ROWS):
                r = rows[(cfg * NA_UNION_ROWS + a) * NA_UNIT_ROWS + g]
                o_ref[0, cfg, a * w:(a + 1) * w, g * LANES:(g + 1) * LANES] = tab_ref[0, r]


def _na_expand_bias(tab):
    npair = tab.shape[0]
    nk, cols = NA_UNION_ROWS * GRID_W, NA_UNIT_ROWS * LANES
    return pl.pallas_call(
        functools.partial(_na_bias_kernel, rows=_na_bias_rows()),
        grid=(npair,),
        in_specs=[pl.BlockSpec((1,) + tab.shape[1:], lambda p: (p, 0, 0, 0))],
        out_specs=pl.BlockSpec((1, 3, nk, cols), lambda p: (p, 0, 0, 0)),
        out_shape=jax.ShapeDtypeStruct((npair, 3, nk, cols), F32),
        compiler_params=_params(1, 32),
        name="na_bias_table",
    )(tab)


def _na_kernel(zero_ref, q_ref, k_ref, vtl_ref, kc_ref, vtc_ref, bias_ref, o_ref, *scr, units, n_rows, lc):
    qq_scr = scr[0]
    s_buf, mt_buf, p_buf = scr[1:4], scr[4:7], scr[7:9]
    w = GRID_W
    uq = NA_UNIT_ROWS * w
    nk = NA_UNION_ROWS * w
    step_r0 = pl.program_id(2) * (units * NA_UNIT_ROWS)
    lo_lanes = lax.broadcasted_iota(jnp.int32, (w, LANES), 1) < NA_HD

    def first_key_row(u):
        return jnp.clip(step_r0 + u * NA_UNIT_ROWS - NA_KH // 2, 0, n_rows - NA_UNION_ROWS)

    def scores(u, slot):
        start = first_key_row(u)
        cfg = (step_r0 + u * NA_UNIT_ROWS - start) // NA_UNIT_ROWS
        for g in range(NA_UNIT_ROWS):
            qg = q_ref[0, u * uq + g * w:u * uq + (g + 1) * w, :].astype(F32)
            qq_scr[g * LANES:g * LANES + w, :] = jnp.where(lo_lanes, qg, 0.0).astype(BF16)
            qq_scr[g * LANES + w:(g + 1) * LANES, :] = jnp.where(lo_lanes, 0.0, qg).astype(BF16)
        qq = qq_scr[...]
        koff = pl.multiple_of(start * w, uq)
        s_nb = (lax.dot_general(k_ref[0, pl.ds(koff, nk), :], qq, NT_DIMS, preferred_element_type=F32)
                + bias_ref[0, cfg])
        s_cx = lax.dot_general(kc_ref[0], qq, NT_DIMS, preferred_element_type=F32)
        mt_buf[slot][...] = jnp.maximum(jnp.max(s_nb, axis=0, keepdims=True), jnp.max(s_cx, axis=0, keepdims=True))
        s_buf[slot][0:nk, :] = s_nb
        s_buf[slot][nk:nk + lc, :] = s_cx

    def softmax(s_slot, p_slot):
        m = mt_buf[s_slot][...]
        z = pl.multiple_of(zero_ref[0], 8)
        s_buf[s_slot][pl.ds(z, 8), :] = s_buf[s_slot][pl.ds(z, 8), :]
        ch = NA_SOFTMAX_ROWS
        for r0 in range(0, nk + lc, ch):
            p_buf[p_slot][r0:r0 + ch, :] = jnp.exp2(s_buf[s_slot][r0:r0 + ch, :] - m).astype(BF16)

    def values(u, p_slot):
        pc0 = first_key_row(u) // NA_UNIT_ROWS
        acc = jnp.dot(vtc_ref[0], p_buf[p_slot][nk:nk + lc, :], preferred_element_type=F32)
        for i in range(NA_UNION_ROWS // NA_UNIT_ROWS):
            acc = acc + jnp.dot(vtl_ref[pc0 + i], p_buf[p_slot][i * uq:(i + 1) * uq, :],
                                preferred_element_type=F32)
        o = acc[0:LANES] * (1.0 / acc[LANES:LANES + 1])
        for g in range(NA_UNIT_ROWS):
            x_t = o[:, g * LANES:(g + 1) * LANES].T
            y = jnp.where(lo_lanes, x_t[0:w], x_t[w:2 * w])
            o_ref[0, u * uq + g * w:u * uq + (g + 1) * w, :] = y.astype(o_ref.dtype)

    scores(0, 0)
    if units > 1:
        scores(1, 1)
    for u in range(units):
        if u >= 1:
            values(u - 1, (u - 1) % 2)
        if u + 2 < units:
            scores(u + 2, (u + 2) % 3)
        softmax(u % 3, u % 2)
    values(units - 1, (units - 1) % 2)


def _na_attention(qk_l, vt_l, qk_c, vt_c, bias):
    b, length, _ = qk_l.shape
    lc = qk_c.shape[1]
    n_rows = length // GRID_W
    uq = NA_UNIT_ROWS * GRID_W
    nk = NA_UNION_ROWS * GRID_W
    assert n_rows >= NA_UNION_ROWS and n_rows % NA_UNIT_ROWS == 0 and vt_l.shape[2] == uq
    units = min(NA_UNITS, n_rows // NA_UNIT_ROWS)
    npair = NA_HEADS // 2
    n_pieces = length // uq
    tq = units * uq
    return pl.pallas_call(
        functools.partial(_na_kernel, units=units, n_rows=n_rows, lc=lc),
        grid=(b, npair, length // tq),
        in_specs=[pl.BlockSpec(memory_space=pltpu.SMEM),
                  pl.BlockSpec((1, tq, LANES), lambda bi, p, i: (bi, i, npair + p)),
                  pl.BlockSpec((1, length, LANES), lambda bi, p, i: (bi, 0, 2 * npair + p)),
                  pl.BlockSpec((n_pieces, DIFF_VT_ROWS, uq), lambda bi, p, i: (bi, p, 0)),
                  pl.BlockSpec((1, lc, LANES), lambda bi, p, i: (bi, 0, 2 * npair + p)),
                  pl.BlockSpec((1, DIFF_VT_ROWS, lc), lambda bi, p, i: (bi, p, 0)),
                  pl.BlockSpec((1, 3, nk, 2 * uq), lambda bi, p, i: (p, 0, 0, 0))],
        out_specs=pl.BlockSpec((1, tq, LANES), lambda bi, p, i: (bi, i, p)),
        out_shape=jax.ShapeDtypeStruct((b, length, NA_HEADS * NA_HD), BF16),
        scratch_shapes=[pltpu.VMEM((2 * uq, LANES), BF16),
                        *[pltpu.VMEM((nk + lc, 2 * uq), F32)] * 3,
                        *[pltpu.VMEM((1, 2 * uq), F32)] * 3,
                        *[pltpu.VMEM((nk + lc, 2 * uq), BF16)] * 2],
        compiler_params=_params(3, 48),
        name="neighbourhood_attention",
    )(jnp.zeros((1,), jnp.int32), qk_l, qk_l, vt_l, qk_c, vt_c, bias)


def _ctx_attn_kernel(q_ref, k_ref, v_ref, o_ref):
    q = q_ref[0].astype(F32)
    k = k_ref[0]
    v = v_ref[0]
    lane = lax.broadcasted_iota(jnp.int32, q.shape, 1)
    outs = []
    for hh in range(2):
        sel = (lane < NA_HD) if hh == 0 else (lane >= NA_HD)
        qm = jnp.where(sel, q, 0.0).astype(BF16)
        s = lax.dot_general(qm, k, NT_DIMS, preferred_element_type=F32)
        outs.append(_softmax_pv([s], [v]))
    o_ref[0] = jnp.where(lane < NA_HD, outs[0], outs[1]).astype(o_ref.dtype)


def _ctx_attention(qkv_c):
    b, lc, _ = qkv_c.shape
    npair = NA_HEADS // 2
    return pl.pallas_call(
        _ctx_attn_kernel,
        grid=(b, npair),
        in_specs=[pl.BlockSpec((1, lc, LANES), lambda bi, p: (bi, 0, npair + p)),
                  pl.BlockSpec((1, lc, LANES), lambda bi, p: (bi, 0, 2 * npair + p)),
                  pl.BlockSpec((1, lc, LANES), lambda bi, p: (bi, 0, 3 * npair + p))],
        out_specs=pl.BlockSpec((1, lc, LANES), lambda bi, p: (bi, 0, p)),
        out_shape=jax.ShapeDtypeStruct((b, lc, NA_HEADS * NA_HD), BF16),
        compiler_params=_params(2, 32),
        name="ctx_attention",
    )(qkv_c, qkv_c, qkv_c)


def _merge_kernel(oret_ref, gates_ref, odiff_ref, ona_ref, x_ref, g1_ref, lng_ref, lnb_ref,
                  wpa_ref, wpb_ref, wpc_ref, wo_ref, o_ref, *, alpha):
    d = D_MODEL
    ya_parts = []
    for hh in range(RET_HEADS):
        a = hh * RET_DK
        seg = _layer_norm_rows(oret_ref[:, a:a + RET_DK])
        g = gates_ref[:, a:a + RET_DK]
        ya_parts.append((g * _sigmoid(g) * seg).astype(BF16))
    ya_in = jnp.concatenate(ya_parts, axis=1)
    y_a = jnp.dot(ya_in, wpa_ref[...], preferred_element_type=F32)
    y_b = jnp.dot(odiff_ref[...], wpb_ref[...], preferred_element_type=F32)
    y_c = jnp.dot(ona_ref[...], wpc_ref[...], preferred_element_type=F32)
    m = (_sigmoid(gates_ref[:, d:2 * d]) * y_a + _sigmoid(gates_ref[:, 2 * d:3 * d]) * y_b
         + _sigmoid(gates_ref[:, 3 * d:4 * d]) * y_c)
    y = jnp.dot(m.astype(BF16), wo_ref[...], preferred_element_type=F32)
    z = alpha * x_ref[...] + g1_ref[0] * y
    o_ref[...] = _layer_norm_rows(z) * lng_ref[...] + lnb_ref[...]


def _merge(o_ret, gates, o_diff, o_na, x, g1, ln_g, ln_b, w_pa, w_pb, w_pc, w_o, mod_row, alpha):
    r, d = x.shape
    tm = min(MERGE_TILE, r)
    row = lambda i: (i, 0)
    const = lambda i: (0, 0)
    wspec = pl.BlockSpec((d, d), const, pipeline_mode=pl.Buffered(1))
    return pl.pallas_call(
        functools.partial(_merge_kernel, alpha=alpha),
        grid=(r // tm,),
        in_specs=[pl.BlockSpec((tm, d), row), pl.BlockSpec((tm, 4 * d), row), pl.BlockSpec((tm, d), row),
                  pl.BlockSpec((tm, d), row), pl.BlockSpec((tm, d), row),
                  pl.BlockSpec((1, 1, d), lambda i: (mod_row(i, tm), 0, 0)),
                  pl.BlockSpec((1, d), const), pl.BlockSpec((1, d), const),
                  wspec, wspec, wspec, wspec],
        out_specs=pl.BlockSpec((tm, d), row),
        out_shape=jax.ShapeDtypeStruct((r, d), F32),
        compiler_params=_params(1, 52),
        name="merge_out_proj",
    )(o_ret, gates, o_diff, o_na, x, g1, ln_g, ln_b, w_pa, w_pb, w_pc, w_o)


def _ffn_kernel(x_ref, sh_ref, sc_ref, g2_ref, lng_ref, lnb_ref, w1_ref, w2_ref, o_ref, *, alpha, ff_chunk):
    x = x_ref[...]
    h = (_layer_norm_rows(x) * (1.0 + sc_ref[0]) + sh_ref[0]).astype(BF16)
    acc = jnp.zeros(x.shape, F32)
    for cidx in range(D_FF // ff_chunk):
        a = cidx * ff_chunk
        u = jnp.maximum(jnp.dot(h, w1_ref[:, a:a + ff_chunk], preferred_element_type=F32), 0.0)
        acc = acc + jnp.dot((u * u).astype(BF16), w2_ref[a:a + ff_chunk, :], preferred_element_type=F32)
    z = alpha * x + g2_ref[0] * acc
    o_ref[...] = _layer_norm_rows(z) * lng_ref[...] + lnb_ref[...]


def _ffn(x, shift, scale, g2, ln_g, ln_b, w1, w2, mod_row, alpha):
    r, d = x.shape
    tm = min(FFN_TILE, r)
    row = lambda i: (i, 0)
    const = lambda i: (0, 0)
    mspec = pl.BlockSpec((1, 1, d), lambda i: (mod_row(i, tm), 0, 0))
    return pl.pallas_call(
        functools.partial(_ffn_kernel, alpha=alpha, ff_chunk=1024),
        grid=(r // tm,),
        in_specs=[pl.BlockSpec((tm, d), row), mspec, mspec, mspec,
                  pl.BlockSpec((1, d), const), pl.BlockSpec((1, d), const),
                  pl.BlockSpec((d, D_FF), const, pipeline_mode=pl.Buffered(1)),
                  pl.BlockSpec((D_FF, d), const, pipeline_mode=pl.Buffered(1))],
        out_specs=pl.BlockSpec((tm, d), row),
        out_shape=jax.ShapeDtypeStruct((r, d), F32),
        compiler_params=_params(1, 56),
        name="ffn",
    )(x, shift, scale, g2, ln_g, ln_b, w1, w2)


def _retention_tables(length):
    t = jnp.arange(length, dtype=F32)
    inv = ROPE_BASE ** (-jnp.linspace(0.0, 1.0, RET_DK // 2, dtype=F32))
    ang = t[:, None] * inv[None, :]
    return jnp.cos(ang), jnp.sin(ang)


def _axial_tables(length):
    t = jnp.arange(length)
    row = (t // GRID_W).astype(F32)
    col = (t % GRID_W).astype(F32)
    half = DIFF_HD // 2
    inv = ROPE_BASE ** (-jnp.arange(0, half, 2, dtype=F32) / half)
    ang_r = row[:, None] * inv[None, :]
    ang_c = col[:, None] * inv[None, :]
    zero = jnp.zeros_like(ang_r)
    unit_ang = jnp.concatenate([ang_r, ang_r, ang_c, ang_c], axis=-1)
    cos = jnp.cos(unit_ang)
    sin_r, sin_c = jnp.sin(ang_r), jnp.sin(ang_c)
    sin_lo = jnp.concatenate([-sin_r, zero, -sin_c, zero], axis=-1)
    sin_hi = jnp.concatenate([zero, sin_r, zero, sin_c], axis=-1)
    rep = LANES // DIFF_HD
    return tuple(jnp.tile(a, (1, rep)) for a in (cos, sin_lo, sin_hi))


def _na_bias_table(rpb):
    nh, nr, _ = rpb.shape
    w = GRID_W
    wid = 2 * w - 1
    p = jnp.pad(rpb.astype(F32) * math.log2(math.e), ((0, 0), (0, 0), (w - NA_KW, w - NA_KW)))
    y = p[..., ::-1]
    a = jnp.broadcast_to(y[:, :, None, :], (nh, nr, w, wid))
    a = jnp.pad(a, ((0, 0), (0, 0), (0, 0), (0, 1))).reshape(nh, nr, w * (wid + 1))
    a = a[:, :, :w * wid].reshape(nh, nr, w, wid)
    m = a[..., w - 1:]
    j = jnp.arange(w)
    cs = jnp.clip(j - NA_KW // 2, 0, w - NA_KW)
    col_mask = (j[:, None] >= cs[None, :]) & (j[:, None] < cs[None, :] + NA_KW)
    m = jnp.where(col_mask[None, None], m, NEG_BIG)
    m = jnp.pad(m, ((0, 0), (0, 1), (0, 0), (0, 0)), constant_values=NEG_BIG)
    m = m.reshape(nh // 2, 2, nr + 1, w, w).transpose(0, 2, 3, 1, 4)
    return m.reshape(nh // 2, nr + 1, w, 2 * w)


def _na_bias_rows():
    un, g_rows = NA_UNION_ROWS, NA_UNIT_ROWS
    rows = []
    for cfg in range(3):
        for a in range(un):
            for g in range(g_rows):
                q_row = cfg * g_rows + g
                win0 = (0, g, un - NA_KH)[cfg]
                in_win = win0 <= a < win0 + NA_KH
                rows.append(a - q_row + NA_KH - 1 if in_win else 2 * NA_KH - 1)
    return tuple(rows)


def _column_scale():
    d = D_MODEL
    cs = jnp.ones((IN_WIDTH,), F32)
    cs = cs.at[d:2 * d].set(RET_DK ** -0.5)
    cs = cs.at[4 * d:5 * d].set(DIFF_HD ** -0.5 * math.log2(math.e))
    cs = cs.at[7 * d:8 * d].set(NA_HD ** -0.5 * math.log2(math.e))
    return cs.reshape(1, IN_WIDTH)


def _project_all(h, w, colscale, tm, rope, ret_tables, ax_tables, tiles_per_batch, tk_out, na_tk_out, plain_na_v):
    if rope:
        ret_qk = _proj(h, w, colscale, tm, 0, 1, 2, BF16, "rope_ret", ret_tables, tiles_per_batch)
        diff_qk = _proj(h, w, colscale, tm, 4, 5, 2, BF16, "rope_diff", ax_tables, tiles_per_batch)
    else:
        ret_qk = _proj(h, w, colscale, tm, 0, 1, 2, BF16)
        diff_qk = _proj(h, w, colscale, tm, 4, 5, 2, BF16)
    v_na = _proj(h, w, colscale, tm, 2, 7, 4 if plain_na_v else 3, BF16)
    gates = _proj(h, w, colscale, tm, 3, 10, 4, F32)
    diff_vt = _proj(h, w, colscale, tm, 6, 7, 1, BF16, tk_out=tk_out)
    na_vt = _proj(h, w, colscale, tm, 9, 10, 1, BF16, tk_out=na_tk_out)
    return ret_qk, diff_qk, v_na, gates, diff_vt, na_vt


def kernel(x, c, ctx, c_ctx, w_mod, b_mod, w_in, ret_decay_f, ret_decay_b, diff_lq1, diff_lk1, diff_lq2, diff_lk2,
           diff_subln_g, na_rpb, w_pa, w_pb, w_pc, w_o, ln1_g, ln1_b, w_ff1, w_ff2, ln2_g, ln2_b):
    b, length, d = x.shape
    lc = ctx.shape[1]
    depth = w_mod.shape[0]
    n_cond = SUBLANES
    assert d == D_MODEL and b < n_cond and length % min(ROW_TILE, length) == 0 and length % GRID_W == 0
    assert lc % RET_CHUNK == 0 or lc < RET_CHUNK
    alpha = (2.0 * depth) ** 0.25

    cvec = jnp.zeros((n_cond, d), F32).at[:b].set(c).at[b].set(c_ctx)
    mods = _modulation(cvec, w_mod, b_mod)

    lat_row = lambda i, tm: (i * tm) // length
    ctx_row = lambda i, tm: b
    colscale = _column_scale()
    ret_tables = _retention_tables(length)
    ax_tables = _axial_tables(length)

    x_l = x.reshape(b * length, d)
    x_c = ctx.reshape(b * lc, d)
    zero_state = jnp.zeros((b, RET_HEADS, RET_DK, RET_DK), F32)
    for l in range(depth):
        with_ctx_out = l < depth - 1
        mod = [mods[l, :, k * d:(k + 1) * d].reshape(n_cond, 1, d) for k in range(6)]
        sh1, sc1, g1, sh2, sc2, g2 = mod
        w = (w_in, l)
        wpa, wpb, wpc, wo = (a[l].astype(BF16) for a in (w_pa, w_pb, w_pc, w_o))
        w1, w2 = w_ff1[l].astype(BF16), w_ff2[l].astype(BF16)
        lng1, lnb1 = ln1_g[l].reshape(1, d), ln1_b[l].reshape(1, d)
        lng2, lnb2 = ln2_g[l].reshape(1, d), ln2_b[l].reshape(1, d)
        lam_init = 0.8 - 0.6 * math.exp(-0.3 * l)
        lam4 = jnp.stack([diff_lq1[l], diff_lk1[l], diff_lq2[l], diff_lk2[l]]).astype(F32)
        subln = diff_subln_g[l].reshape(1, DIFF_DV).astype(F32)
        bias = _na_expand_bias(_na_bias_table(na_rpb[l]))

        h_l = _ln_mod(x_l, sh1, sc1, lat_row)
        h_c = _ln_mod(x_c, sh1, sc1, ctx_row)
        tm_l = min(ROW_TILE, length)
        rqk_l, dqk_l, vna_l, gates_l, dvt_l, nvt_l = _project_all(
            h_l, w, colscale, tm_l, True, ret_tables, ax_tables, length // tm_l, min(DIFF_TILE, length),
            NA_UNIT_ROWS * GRID_W, False)
        rqk_c, dqk_c, vna_c, gates_c, dvt_c, nvt_c = _project_all(
            h_c, w, colscale, min(ROW_TILE, b * lc), False, (), (), 1, lc, lc, True)
        sh3 = lambda a, n: a.reshape(b, n, a.shape[-1])

        decays = _retention_decays(ret_decay_f[l], ret_decay_b[l], min(RET_CHUNK, lc))
        rqk_l3, vna_l3, rqk_c3, vna_c3 = sh3(rqk_l, length), sh3(vna_l, length), sh3(rqk_c, lc), sh3(vna_c, lc)
        o_cf, s_cf = _retention(decays, rqk_c3, vna_c3, zero_state, None, False)
        o_rc, s_cb = _retention(decays, rqk_c3, vna_c3, zero_state, o_cf, True)
        o_lf, _ = _retention(decays, rqk_l3, vna_l3, s_cf, None, False)
        o_rl, _ = _retention(decays, rqk_l3, vna_l3, s_cb, o_lf, True)

        dqk_l3, dqk_c3 = sh3(dqk_l, length), sh3(dqk_c, lc)
        o_dl = _diff_attention(lam4, subln, dqk_l3, (dqk_l3, dvt_l), (dqk_c3, dvt_c), lam_init)
        o_nl = _na_attention(vna_l3, nvt_l, vna_c3, nvt_c, bias)

        x_l = _merge(o_rl.reshape(b * length, d), gates_l, o_dl.reshape(b * length, d), o_nl.reshape(b * length, d),
                     x_l, g1, lng1, lnb1, wpa, wpb, wpc, wo, lat_row, alpha)
        x_l = _ffn(x_l, sh2, sc2, g2, lng2, lnb2, w1, w2, lat_row, alpha)
        if with_ctx_out:
            o_dc = _diff_attention(lam4, subln, dqk_c3, None, (dqk_c3, dvt_c), lam_init)
            o_nc = _ctx_attention(vna_c3)
            x_c = _merge(o_rc.reshape(b * lc, d), gates_c, o_dc.reshape(b * lc, d), o_nc.reshape(b * lc, d),
                         x_c, g1, lng1, lnb1, wpa, wpb, wpc, wo, ctx_row, alpha)
            x_c = _ffn(x_c, sh2, sc2, g2, lng2, lnb2, w1, w2, ctx_row, alpha)
    return x_l.reshape(b, length, d)
```

```python
import functools
import math

import jax
import jax.numpy as jnp
from jax import lax
from jax.experimental import pallas as pl
from jax.experimental.pallas import tpu as pltpu

F32 = jnp.float32
BF16 = jnp.bfloat16

D_MODEL = 1024
GRID_W = 64
RET_HEADS, RET_DK = 4, 256
DIFF_HEADS, DIFF_HD, DIFF_DV = 8, 64, 128
NA_HEADS, NA_HD, NA_KH, NA_KW = 16, 64, 8, 16
D_FF = 4 * D_MODEL
ROPE_BASE = 10000.0
LN_EPS = 1e-6
SUBLN_EPS = 1e-5
IN_WIDTH = 13 * D_MODEL
NEG_BIG = -1e30

LANES = 128
SUBLANES = 8
SUBLANES_BF16 = 16
VMEM_LIMIT_CAP = 56 << 20

ROW_TILE = 2048
PROJ_TN = 1024
LN_TILE = 512
MERGE_TILE = 512
FFN_TILE = 1024
RET_CHUNK = 256
RET_BLOCK = 1024
DIFF_TQ = 256
DIFF_BLOCKS = 4
DIFF_FINISH_SLACK = 2
DIFF_TILE = 512
DIFF_VT_ROWS = DIFF_DV + SUBLANES_BF16
DIFF_SOFTMAX_ROWS = 64
NA_UNIT_ROWS = 4
NA_UNION_ROWS = 12
NA_UNITS = 16
NA_SOFTMAX_ROWS = 64

NT_DIMS = (((1,), (1,)), ((), ()))
TN_DIMS = (((0,), (0,)), ((), ()))


def _params(n_axes, vmem_mb):
    return pltpu.CompilerParams(dimension_semantics=("arbitrary",) * n_axes,
                                vmem_limit_bytes=min(vmem_mb << 20, VMEM_LIMIT_CAP))


def _sigmoid(x):
    return 1.0 / (1.0 + jnp.exp(-x))


def _layer_norm_rows(x):
    mu = jnp.mean(x, axis=-1, keepdims=True)
    xc = x - mu
    var = jnp.mean(xc * xc, axis=-1, keepdims=True)
    return xc * lax.rsqrt(var + LN_EPS)


def _mod_kernel(c_ref, w_ref, b_ref, o_ref):
    c = c_ref[...]
    a = (c * _sigmoid(c)).astype(BF16)
    o_ref[0] = jnp.dot(a, w_ref[0].astype(BF16), preferred_element_type=F32) + b_ref[0]


def _modulation(cvec, w_mod, b_mod):
    depth = w_mod.shape[0]
    d = D_MODEL
    return pl.pallas_call(
        _mod_kernel,
        grid=(depth, 6),
        in_specs=[pl.BlockSpec((SUBLANES, d), lambda l, j: (0, 0)),
                  pl.BlockSpec((1, d, d), lambda l, j: (l, 0, j)),
                  pl.BlockSpec((1, 1, d), lambda l, j: (l, 0, j))],
        out_specs=pl.BlockSpec((1, SUBLANES, d), lambda l, j: (l, 0, j)),
        out_shape=jax.ShapeDtypeStruct((depth, SUBLANES, 6 * d), F32),
        compiler_params=_params(2, 32),
        name="modulation",
    )(cvec, w_mod, b_mod.reshape(depth, 1, 6 * d))


def _ln_mod_kernel(x_ref, sh_ref, sc_ref, o_ref):
    y = _layer_norm_rows(x_ref[...])
    o_ref[...] = (y * (1.0 + sc_ref[0]) + sh_ref[0]).astype(o_ref.dtype)


def _ln_mod(x, shift, scale, mod_row):
    r, d = x.shape
    tm = min(LN_TILE, r)
    return pl.pallas_call(
        _ln_mod_kernel,
        grid=(r // tm,),
        in_specs=[pl.BlockSpec((tm, d), lambda i: (i, 0)),
                  pl.BlockSpec((1, 1, d), lambda i: (mod_row(i, tm), 0, 0)),
                  pl.BlockSpec((1, 1, d), lambda i: (mod_row(i, tm), 0, 0))],
        out_specs=pl.BlockSpec((tm, d), lambda i: (i, 0)),
        out_shape=jax.ShapeDtypeStruct((r, d), BF16),
        compiler_params=_params(1, 32),
        name="ln_modulate",
    )(x, shift, scale)


def _proj_kernel(h_ref, w_ref, cs_ref, *rest, mode, transposed, tn, tk_out):
    o_ref = rest[-1]
    acc = jnp.dot(h_ref[...], w_ref[0].astype(BF16), preferred_element_type=F32) * cs_ref[...]
    if mode == "rope_ret":
        cos = rest[0][...]
        sin = rest[1][...]
        half = RET_DK // 2
        for hh in range(tn // RET_DK):
            a = hh * RET_DK
            x1 = acc[:, a:a + half]
            x2 = acc[:, a + half:a + RET_DK]
            o_ref[:, a:a + half] = (x1 * cos - x2 * sin).astype(o_ref.dtype)
            o_ref[:, a + half:a + RET_DK] = (x1 * sin + x2 * cos).astype(o_ref.dtype)
    elif mode == "rope_diff":
        cos = rest[0][...]
        sin_lo = rest[1][...]
        sin_hi = rest[2][...]
        q16 = DIFF_HD // 4
        for g in range(tn // LANES):
            x = acc[:, g * LANES:(g + 1) * LANES]
            y = x * cos + pltpu.roll(x, LANES - q16, 1) * sin_lo + pltpu.roll(x, q16, 1) * sin_hi
            o_ref[:, g * LANES:(g + 1) * LANES] = y.astype(o_ref.dtype)
    elif transposed:
        acc_t = acc.T
        ones = jnp.ones((DIFF_VT_ROWS - DIFF_DV, tk_out), o_ref.dtype)
        for a in range(o_ref.shape[0]):
            for hh in range(tn // DIFF_DV):
                r0 = hh * DIFF_VT_ROWS
                o_ref[a, r0:r0 + DIFF_DV, :] = acc_t[hh * DIFF_DV:(hh + 1) * DIFF_DV,
                                                     a * tk_out:(a + 1) * tk_out].astype(o_ref.dtype)
                o_ref[a, r0 + DIFF_DV:r0 + DIFF_VT_ROWS, :] = ones
    else:
        o_ref[...] = acc.astype(o_ref.dtype)


def _proj(h, w, colscale, tm, first, rest0, ntiles, out_dtype, mode="plain", tables=(), tiles_per_batch=1,
          tk_out=None):
    w_all, layer = w
    r, d = h.shape
    tn = PROJ_TN
    ncols = ntiles * tn
    assert r % tm == 0
    transposed = tk_out is not None
    wcol = lambda j: jnp.where(j == 0, first, rest0 + j - 1)
    in_specs = [pl.BlockSpec((tm, d), lambda i, j: (i, 0)),
                pl.BlockSpec((1, d, tn), lambda i, j: (layer, 0, wcol(j))),
                pl.BlockSpec((1, tn), lambda i, j: (0, wcol(j)))]
    for _ in tables:
        in_specs.append(pl.BlockSpec((tm, LANES), lambda i, j: (i % tiles_per_batch, 0)))
    if transposed:
        assert tm % tk_out == 0
        na = tm // tk_out
        rows_out = tn // DIFF_DV * DIFF_VT_ROWS
        out_specs = pl.BlockSpec((na, rows_out, tk_out), lambda i, j: (i, j, 0))
        out_shape = jax.ShapeDtypeStruct((r // tk_out, ncols // DIFF_DV * DIFF_VT_ROWS, tk_out), out_dtype)
    else:
        out_specs = pl.BlockSpec((tm, tn), lambda i, j: (i, j))
        out_shape = jax.ShapeDtypeStruct((r, ncols), out_dtype)
    return pl.pallas_call(
        functools.partial(_proj_kernel, mode=mode, transposed=transposed, tn=tn, tk_out=tk_out),
        grid=(r // tm, ntiles),
        in_specs=in_specs,
        out_specs=out_specs,
        out_shape=out_shape,
        compiler_params=_params(2, 52),
        name="in_proj_" + mode + ("_t" if transposed else ""),
    )(h, w_all, colscale, *tables)


def _ret_tables_kernel(dec_ref, intra_ref, qd_ref, kd_ref, cd_ref, *, chunk):
    direction = pl.program_id(0)
    head = pl.program_id(1)
    cf = float(chunk)
    dv = jnp.full((1, 1), dec_ref[direction, head], F32)
    lg = jnp.minimum(dv, 0.0) - jnp.log(1.0 + jnp.exp(-jnp.abs(dv)))
    rev = direction == 1
    ii = lax.broadcasted_iota(jnp.int32, (chunk, chunk), 0).astype(F32)
    jj = lax.broadcasted_iota(jnp.int32, (chunk, chunk), 1).astype(F32)
    dist = jnp.where(rev, jj - ii, ii - jj)
    intra_ref[0, 0] = jnp.where(dist >= 0.0, jnp.exp(lg * jnp.maximum(dist, 0.0)), 0.0)
    i1 = lax.broadcasted_iota(jnp.int32, (chunk, RET_DK), 0).astype(F32)
    qd_ref[0, 0] = jnp.exp(lg * jnp.where(rev, cf - i1, i1 + 1.0))
    kd_ref[0, 0] = jnp.exp(lg * jnp.where(rev, i1, cf - 1.0 - i1))
    cd_ref[0, 0] = jnp.exp(jnp.broadcast_to(lg, (SUBLANES, RET_DK)) * cf)


def _retention_decays(dec_f, dec_b, chunk):
    hh, dk = RET_HEADS, RET_DK
    blk = lambda *shape: pl.BlockSpec((1, 1) + shape, lambda d, h: (d, h, 0, 0))
    return pl.pallas_call(
        functools.partial(_ret_tables_kernel, chunk=chunk),
        grid=(2, hh),
        in_specs=[pl.BlockSpec(memory_space=pltpu.SMEM)],
        out_specs=[blk(chunk, chunk), blk(chunk, dk), blk(chunk, dk), blk(SUBLANES, dk)],
        out_shape=[jax.ShapeDtypeStruct((2, hh, chunk, chunk), F32),
                   jax.ShapeDtypeStruct((2, hh, chunk, dk), F32),
                   jax.ShapeDtypeStruct((2, hh, chunk, dk), F32),
                   jax.ShapeDtypeStruct((2, hh, SUBLANES, dk), F32)],
        compiler_params=_params(2, 32),
        name="retention_tables",
    )(jnp.stack([dec_f, dec_b]).astype(F32))


def _ret_kernel(intra_ref, qd_ref, kd_ref, cd_ref, qk_ref, v_ref, s0_ref, *rest, reverse, has_prev, chunk, n_chunks):
    if has_prev:
        prev_ref, o_ref, sfin_ref, s_scr = rest
    else:
        o_ref, sfin_ref, s_scr = rest
    @pl.when(pl.program_id(1) == 0)
    def _load_initial_state():
        s_scr[...] = s0_ref[0]

    dk = RET_DK
    order = range(n_chunks - 1, -1, -1) if reverse else range(n_chunks)
    for h in range(RET_HEADS):
        cols = slice(h * dk, (h + 1) * dk)
        kcols = slice((RET_HEADS + h) * dk, (RET_HEADS + h + 1) * dk)
        s = s_scr[h]
        for c in order:
            rows = slice(c * chunk, (c + 1) * chunk)
            q = qk_ref[0, rows, cols]
            k = qk_ref[0, rows, kcols]
            v = v_ref[0, rows, cols]
            att = lax.dot_general(q, k, NT_DIMS, preferred_element_type=F32) * intra_ref[0, h]
            qd = (q.astype(F32) * qd_ref[0, h]).astype(BF16)
            o = (jnp.dot(att.astype(BF16), v, preferred_element_type=F32)
                 + jnp.dot(qd, s.astype(BF16), preferred_element_type=F32))
            kd = (k.astype(F32) * kd_ref[0, h]).astype(BF16)
            s = s * cd_ref[0, h, 0:1, :] + lax.dot_general(kd, v, TN_DIMS, preferred_element_type=F32)
            if has_prev:
                o = o + prev_ref[0, rows, cols]
            o_ref[0, rows, cols] = o
        s_scr[h] = s
        sfin_ref[0, h] = s


def _retention(tables, qk, v_arr, s0, prev, reverse):
    b, length, _ = qk.shape
    hh, dk = RET_HEADS, RET_DK
    width = hh * dk
    c = tables[0].shape[-1]
    rb = min(RET_BLOCK, length)
    n = length // rb
    direction = 1 if reverse else 0
    pos = (lambda i: n - 1 - i) if reverse else (lambda i: i)
    tab = lambda *shape: pl.BlockSpec((1, hh) + shape, lambda bi, i: (direction, 0, 0, 0))
    in_specs = [tab(c, c), tab(c, dk), tab(c, dk), tab(SUBLANES, dk),
                pl.BlockSpec((1, rb, 2 * width), lambda bi, i: (bi, pos(i), 0)),
                pl.BlockSpec((1, rb, width), lambda bi, i: (bi, pos(i), 0)),
                pl.BlockSpec((1, hh, dk, dk), lambda bi, i: (bi, 0, 0, 0))]
    args = [*tables, qk, v_arr, s0]
    if prev is not None:
        in_specs.append(pl.BlockSpec((1, rb, width), lambda bi, i: (bi, pos(i), 0)))
        args.append(prev)
    return pl.pallas_call(
        functools.partial(_ret_kernel, reverse=reverse, has_prev=prev is not None, chunk=c, n_chunks=rb // c),
        grid=(b, n),
        in_specs=in_specs,
        out_specs=[pl.BlockSpec((1, rb, width), lambda bi, i: (bi, pos(i), 0)),
                   pl.BlockSpec((1, hh, dk, dk), lambda bi, i: (bi, 0, 0, 0))],
        out_shape=[jax.ShapeDtypeStruct((b, length, width), F32),
                   jax.ShapeDtypeStruct((b, hh, dk, dk), F32)],
        scratch_shapes=[pltpu.VMEM((hh, dk, dk), F32)],
        compiler_params=_params(2, 48),
        name="retention_bwd" if reverse else "retention_fwd",
    )(*args)


def _diff_kernel(zero_ref, lam_ref, g_ref, q_ref, *rest, tq, nb, n_lat, tile, lc, lam_init):
    if n_lat:
        kl_ref, vtl_ref = rest[:2]
        rest = rest[2:]
    kc_ref, vtc_ref, o_ref, qq_scr, klast_scr = rest[:5]
    s_buf, mt_buf, p_buf, al_buf = rest[5:9], rest[9:13], rest[13:15], rest[15:17]
    m_scr, acc_scr = rest[17:]
    n = max(n_lat, 1)
    last_rows = (tile if n_lat else 0) + lc
    rows_of = lambda j: last_rows if j == n - 1 else tile
    if n_lat:
        klast_scr[0:tile, :] = kl_ref[0, (n_lat - 1) * tile:n_lat * tile, :]
    klast_scr[last_rows - lc:last_rows, :] = kc_ref[0]

    def scores(g, slot):
        blk, j = divmod(g, n)
        if j == 0:
            q = q_ref[0, blk * tq:(blk + 1) * tq, :].astype(F32)
            lane = lax.broadcasted_iota(jnp.int32, q.shape, 1)
            qq_scr[blk, 0:tq, :] = jnp.where(lane < DIFF_HD, q, 0.0).astype(BF16)
            qq_scr[blk, tq:2 * tq, :] = jnp.where(lane >= DIFF_HD, q, 0.0).astype(BF16)
            m_scr[blk] = jnp.full(m_scr.shape[1:], NEG_BIG, F32)
        k = klast_scr[...] if j == n - 1 else kl_ref[0, j * tile:(j + 1) * tile, :]
        s = lax.dot_general(k, qq_scr[blk], NT_DIMS, preferred_element_type=F32)
        mt_buf[slot][...] = jnp.max(s, axis=0, keepdims=True)
        s_buf[slot][0:k.shape[0], :] = s

    def softmax(g, anchor):
        blk, j = divmod(g, n)
        s_slot, p_slot, rows = g % 4, g % 2, rows_of(j)
        m_old = m_scr[blk]
        if anchor is not None:
            m_old = jnp.maximum(m_old, anchor + NEG_BIG)
        m_new = jnp.maximum(m_old, mt_buf[s_slot][...])
        al_buf[p_slot][...] = jnp.exp2(m_old - m_new)
        m_scr[blk] = m_new
        z = pl.multiple_of(zero_ref[0], 8)
        s_buf[s_slot][pl.ds(z, 8), :] = s_buf[s_slot][pl.ds(z, 8), :]
        ch = DIFF_SOFTMAX_ROWS
        for r0 in range(0, rows, ch):
            p_buf[p_slot][r0:r0 + ch, :] = jnp.exp2(s_buf[s_slot][r0:r0 + ch, :] - m_new).astype(BF16)

    def values(g):
        blk, j = divmod(g, n)
        slot = g % 2
        pv = 0.0
        if n_lat:
            pv = jnp.dot(vtl_ref[j], p_buf[slot][0:tile, :], preferred_element_type=F32)
        if j == n - 1:
            pv = pv + jnp.dot(vtc_ref[0], p_buf[slot][last_rows - lc:last_rows, :], preferred_element_type=F32)
        acc_scr[blk] = pv if j == 0 else acc_scr[blk] * al_buf[slot][...] + pv

    def finish(blk):
        acc = acc_scr[blk]
        o = acc[0:DIFF_DV] * (1.0 / acc[DIFF_DV:DIFF_DV + 1])
        lam4 = lam_ref[...]
        s1 = jnp.sum(lam4[0:1] * lam4[1:2], axis=-1, keepdims=True)
        s2 = jnp.sum(lam4[2:3] * lam4[3:4], axis=-1, keepdims=True)
        lam = jnp.exp(s1) - jnp.exp(s2) + lam_init
        y = (o[:, :tq] - lam * o[:, tq:]).T
        ms = jnp.mean(y * y, axis=-1, keepdims=True)
        y = y * lax.rsqrt(ms + SUBLN_EPS) * g_ref[...] * (1.0 - lam_init)
        o_ref[0, blk * tq:(blk + 1) * tq, :] = y.astype(o_ref.dtype)
        d = jnp.concatenate([y[0:1, :]] * (2 * tq // LANES), axis=1) * 0.0
        return jnp.where(jnp.isnan(d), 0.0, d)

    total = nb * n
    scores(0, 0)
    if total > 1:
        scores(1, 1)
    anchor, anchor_at = None, -1
    for g in range(total):
        if g >= 1:
            values(g - 1)
            if g % n == 0:
                anchor, anchor_at = finish(g // n - 1), min(g + DIFF_FINISH_SLACK, total - 1)
        if g + 2 < total:
            scores(g + 2, (g + 2) % 4)
        softmax(g, anchor if g == anchor_at else None)
    values(total - 1)
    finish(nb - 1)


def _diff_attention(lam4, subln_g, q_arr, lat, ctx, lam_init):
    b, lq, _ = q_arr.shape
    hh = DIFF_HEADS
    tq = min(DIFF_TQ, lq)
    qk_c, vt_c = ctx
    lc = qk_c.shape[1]
    assert vt_c.shape == (b, hh * DIFF_VT_ROWS, lc)
    nb = math.gcd(DIFF_BLOCKS, lq // tq)
    in_specs = [pl.BlockSpec(memory_space=pltpu.SMEM),
                pl.BlockSpec((4, DIFF_HD), lambda bi, h, i: (0, 0)),
                pl.BlockSpec((1, DIFF_DV), lambda bi, h, i: (0, 0)),
                pl.BlockSpec((1, nb * tq, LANES), lambda bi, h, i: (bi, i, h))]
    args = [jnp.zeros((1,), jnp.int32), lam4, subln_g, q_arr]
    n_lat, tile = 0, lc
    if lat is not None:
        qk_l, vt_l = lat
        length = qk_l.shape[1]
        tile = vt_l.shape[2]
        n_lat = length // tile
        assert lc <= tile
        in_specs += [pl.BlockSpec((1, length, LANES), lambda bi, h, i: (bi, 0, hh + h)),
                     pl.BlockSpec((n_lat, DIFF_VT_ROWS, tile), lambda bi, h, i: (bi, h, 0))]
        args += [qk_l, vt_l]
    in_specs += [pl.BlockSpec((1, lc, LANES), lambda bi, h, i: (bi, 0, hh + h)),
                 pl.BlockSpec((1, DIFF_VT_ROWS, lc), lambda bi, h, i: (bi, h, 0))]
    args += [qk_c, vt_c]
    last_rows = (tile if n_lat else 0) + lc
    return pl.pallas_call(
        functools.partial(_diff_kernel, tq=tq, nb=nb, n_lat=n_lat, tile=tile, lc=lc, lam_init=lam_init),
        grid=(b, hh, lq // (nb * tq)),
        in_specs=in_specs,
        out_specs=pl.BlockSpec((1, nb * tq, LANES), lambda bi, h, i: (bi, i, h)),
        out_shape=jax.ShapeDtypeStruct((b, lq, hh * DIFF_DV), BF16),
        scratch_shapes=[pltpu.VMEM((nb, 2 * tq, LANES), BF16),
                        pltpu.VMEM((last_rows, LANES), BF16),
                        *[pltpu.VMEM((last_rows, 2 * tq), F32)] * 4,
                        *[pltpu.VMEM((1, 2 * tq), F32)] * 4,
                        *[pltpu.VMEM((last_rows, 2 * tq), BF16)] * 2,
                        *[pltpu.VMEM((1, 2 * tq), F32)] * 2,
                        pltpu.VMEM((nb, 1, 2 * tq), F32),
                        pltpu.VMEM((nb, DIFF_VT_ROWS, 2 * tq), F32)],
        compiler_params=_params(3, 40),
        name="diff_attention" if lat is not None else "diff_attention_ctx",
    )(*args)


def _softmax_pv(s_list, v_list):
    m = s_list[0].max(axis=-1, keepdims=True)
    for s in s_list[1:]:
        m = jnp.maximum(m, s.max(axis=-1, keepdims=True))
    den = 0.0
    o = 0.0
    for s, v in zip(s_list, v_list):
        p = jnp.exp2(s - m)
        den = den + jnp.sum(p, axis=-1, keepdims=True)
        o = o + jnp.dot(p.astype(BF16), v, preferred_element_type=F32)
    return o * (1.0 / den)


def _na_bias_kernel(tab_ref, o_ref, *, rows):
    w = GRID_W
    for cfg in range(3):
        for a in range(NA_UNION_ROWS):
            for g in range(NA_UNIT_ROWS):
                r = rows[(cfg * NA_UNION_ROWS + a) * NA_UNIT_ROWS + g]
                o_ref[0, cfg, a * w:(a + 1) * w, g * LANES:(g + 1) * LANES] = tab_ref[0, r]


def _na_expand_bias(tab):
    npair = tab.shape[0]
    nk, cols = NA_UNION_ROWS * GRID_W, NA_UNIT_ROWS * LANES
    return pl.pallas_call(
        functools.partial(_na_bias_kernel, rows=_na_bias_rows()),
        grid=(npair,),
        in_specs=[pl.BlockSpec((1,) + tab.shape[1:], lambda p: (p, 0, 0, 0))],
        out_specs=pl.BlockSpec((1, 3, nk, cols), lambda p: (p, 0, 0, 0)),
        out_shape=jax.ShapeDtypeStruct((npair, 3, nk, cols), F32),
        compiler_params=_params(1, 32),
        name="na_bias_table",
    )(tab)


def _na_kernel(zero_ref, q_ref, k_ref, vtl_ref, kc_ref, vtc_ref, bias_ref, o_ref, *scr, units, n_rows, lc):
    qq_scr, keys_scr = scr[0:2]
    s_buf, mt_buf, p_buf = scr[2:5], scr[5:8], scr[8:10]
    w = GRID_W
    uq = NA_UNIT_ROWS * w
    nk = NA_UNION_ROWS * w
    step_r0 = pl.program_id(2) * (units * NA_UNIT_ROWS)
    lo_lanes = lax.broadcasted_iota(jnp.int32, (w, LANES), 1) < NA_HD

    def first_key_row(u):
        return jnp.clip(step_r0 + u * NA_UNIT_ROWS - NA_KH // 2, 0, n_rows - NA_UNION_ROWS)

    def scores(u, slot):
        start = first_key_row(u)
        cfg = (step_r0 + u * NA_UNIT_ROWS - start) // NA_UNIT_ROWS
        for g in range(NA_UNIT_ROWS):
            qg = q_ref[0, u * uq + g * w:u * uq + (g + 1) * w, :].astype(F32)
            qq_scr[g * LANES:g * LANES + w, :] = jnp.where(lo_lanes, qg, 0.0).astype(BF16)
            qq_scr[g * LANES + w:(g + 1) * LANES, :] = jnp.where(lo_lanes, 0.0, qg).astype(BF16)
        koff = pl.multiple_of(start * w, uq)
        keys_scr[0:nk, :] = k_ref[0, pl.ds(koff, nk), :]
        s = lax.dot_general(keys_scr[...], qq_scr[...], NT_DIMS, preferred_element_type=F32)
        s_nb = s[0:nk] + bias_ref[0, cfg]
        s_cx = s[nk:nk + lc]
        mt_buf[slot][...] = jnp.maximum(jnp.max(s_nb, axis=0, keepdims=True), jnp.max(s_cx, axis=0, keepdims=True))
        s_buf[slot][0:nk, :] = s_nb
        s_buf[slot][nk:nk + lc, :] = s_cx

    def softmax(s_slot, p_slot):
        m = mt_buf[s_slot][...]
        z = pl.multiple_of(zero_ref[0], 8)
        s_buf[s_slot][pl.ds(z, 8), :] = s_buf[s_slot][pl.ds(z, 8), :]
        ch = NA_SOFTMAX_ROWS
        for r0 in range(0, nk + lc, ch):
            p_buf[p_slot][r0:r0 + ch, :] = jnp.exp2(s_buf[s_slot][r0:r0 + ch, :] - m).astype(BF16)

    def values(u, p_slot):
        pc0 = first_key_row(u) // NA_UNIT_ROWS
        acc = jnp.dot(vtc_ref[0], p_buf[p_slot][nk:nk + lc, :], preferred_element_type=F32)
        for i in range(NA_UNION_ROWS // NA_UNIT_ROWS):
            acc = acc + jnp.dot(vtl_ref[pc0 + i], p_buf[p_slot][i * uq:(i + 1) * uq, :],
                                preferred_element_type=F32)
        o = acc[0:LANES] * (1.0 / acc[LANES:LANES + 1])
        for g in range(NA_UNIT_ROWS):
            x_t = o[:, g * LANES:(g + 1) * LANES].T
            y = jnp.where(lo_lanes, x_t[0:w], x_t[w:2 * w])
            o_ref[0, u * uq + g * w:u * uq + (g + 1) * w, :] = y.astype(o_ref.dtype)

    keys_scr[nk:nk + lc, :] = kc_ref[0]
    scores(0, 0)
    if units > 1:
        scores(1, 1)
    for u in range(units):
        if u >= 1:
            values(u - 1, (u - 1) % 2)
        if u + 2 < units:
            scores(u + 2, (u + 2) % 3)
        softmax(u % 3, u % 2)
    values(units - 1, (units - 1) % 2)


def _na_attention(qk_l, vt_l, qk_c, vt_c, bias):
    b, length, _ = qk_l.shape
    lc = qk_c.shape[1]
    n_rows = length // GRID_W
    uq = NA_UNIT_ROWS * GRID_W
    nk = NA_UNION_ROWS * GRID_W
    assert n_rows >= NA_UNION_ROWS and n_rows % NA_UNIT_ROWS == 0 and vt_l.shape[2] == uq
    units = min(NA_UNITS, n_rows // NA_UNIT_ROWS)
    npair = NA_HEADS // 2
    n_pieces = length // uq
    tq = units * uq
    return pl.pallas_call(
        functools.partial(_na_kernel, units=units, n_rows=n_rows, lc=lc),
        grid=(b, npair, length // tq),
        in_specs=[pl.BlockSpec(memory_space=pltpu.SMEM),
                  pl.BlockSpec((1, tq, LANES), lambda bi, p, i: (bi, i, npair + p)),
                  pl.BlockSpec((1, length, LANES), lambda bi, p, i: (bi, 0, 2 * npair + p)),
                  pl.BlockSpec((n_pieces, DIFF_VT_ROWS, uq), lambda bi, p, i: (bi, p, 0)),
                  pl.BlockSpec((1, lc, LANES), lambda bi, p, i: (bi, 0, 2 * npair + p)),
                  pl.BlockSpec((1, DIFF_VT_ROWS, lc), lambda bi, p, i: (bi, p, 0)),
                  pl.BlockSpec((1, 3, nk, 2 * uq), lambda bi, p, i: (p, 0, 0, 0))],
        out_specs=pl.BlockSpec((1, tq, LANES), lambda bi, p, i: (bi, i, p)),
        out_shape=jax.ShapeDtypeStruct((b, length, NA_HEADS * NA_HD), BF16),
        scratch_shapes=[pltpu.VMEM((2 * uq, LANES), BF16),
                        pltpu.VMEM((nk + lc, LANES), BF16),
                        *[pltpu.VMEM((nk + lc, 2 * uq), F32)] * 3,
                        *[pltpu.VMEM((1, 2 * uq), F32)] * 3,
                        *[pltpu.VMEM((nk + lc, 2 * uq), BF16)] * 2],
        compiler_params=_params(3, 48),
        name="neighbourhood_attention",
    )(jnp.zeros((1,), jnp.int32), qk_l, qk_l, vt_l, qk_c, vt_c, bias)


def _ctx_attn_kernel(q_ref, k_ref, v_ref, o_ref):
    q = q_ref[0].astype(F32)
    k = k_ref[0]
    v = v_ref[0]
    lane = lax.broadcasted_iota(jnp.int32, q.shape, 1)
    outs = []
    for hh in range(2):
        sel = (lane < NA_HD) if hh == 0 else (lane >= NA_HD)
        qm = jnp.where(sel, q, 0.0).astype(BF16)
        s = lax.dot_general(qm, k, NT_DIMS, preferred_element_type=F32)
        outs.append(_softmax_pv([s], [v]))
    o_ref[0] = jnp.where(lane < NA_HD, outs[0], outs[1]).astype(o_ref.dtype)


def _ctx_attention(qkv_c):
    b, lc, _ = qkv_c.shape
    npair = NA_HEADS // 2
    return pl.pallas_call(
        _ctx_attn_kernel,
        grid=(b, npair),
        in_specs=[pl.BlockSpec((1, lc, LANES), lambda bi, p: (bi, 0, npair + p)),
                  pl.BlockSpec((1, lc, LANES), lambda bi, p: (bi, 0, 2 * npair + p)),
                  pl.BlockSpec((1, lc, LANES), lambda bi, p: (bi, 0, 3 * npair + p))],
        out_specs=pl.BlockSpec((1, lc, LANES), lambda bi, p: (bi, 0, p)),
        out_shape=jax.ShapeDtypeStruct((b, lc, NA_HEADS * NA_HD), BF16),
        compiler_params=_params(2, 32),
        name="ctx_attention",
    )(qkv_c, qkv_c, qkv_c)


def _merge_kernel(oret_ref, gates_ref, odiff_ref, ona_ref, x_ref, g1_ref, lng_ref, lnb_ref,
                  wpa_ref, wpb_ref, wpc_ref, wo_ref, o_ref, *, alpha):
    d = D_MODEL
    ya_parts = []
    for hh in range(RET_HEADS):
        a = hh * RET_DK
        seg = _layer_norm_rows(oret_ref[:, a:a + RET_DK])
        g = gates_ref[:, a:a + RET_DK]
        ya_parts.append((g * _sigmoid(g) * seg).astype(BF16))
    ya_in = jnp.concatenate(ya_parts, axis=1)
    y_a = jnp.dot(ya_in, wpa_ref[...], preferred_element_type=F32)
    y_b = jnp.dot(odiff_ref[...], wpb_ref[...], preferred_element_type=F32)
    y_c = jnp.dot(ona_ref[...], wpc_ref[...], preferred_element_type=F32)
    m = (_sigmoid(gates_ref[:, d:2 * d]) * y_a + _sigmoid(gates_ref[:, 2 * d:3 * d]) * y_b
         + _sigmoid(gates_ref[:, 3 * d:4 * d]) * y_c)
    y = jnp.dot(m.astype(BF16), wo_ref[...], preferred_element_type=F32)
    z = alpha * x_ref[...] + g1_ref[0] * y
    o_ref[...] = _layer_norm_rows(z) * lng_ref[...] + lnb_ref[...]


def _merge(o_ret, gates, o_diff, o_na, x, g1, ln_g, ln_b, w_pa, w_pb, w_pc, w_o, mod_row, alpha):
    r, d = x.shape
    tm = min(MERGE_TILE, r)
    row = lambda i: (i, 0)
    const = lambda i: (0, 0)
    wspec = pl.BlockSpec((d, d), const, pipeline_mode=pl.Buffered(1))
    return pl.pallas_call(
        functools.partial(_merge_kernel, alpha=alpha),
        grid=(r // tm,),
        in_specs=[pl.BlockSpec((tm, d), row), pl.BlockSpec((tm, 4 * d), row), pl.BlockSpec((tm, d), row),
                  pl.BlockSpec((tm, d), row), pl.BlockSpec((tm, d), row),
                  pl.BlockSpec((1, 1, d), lambda i: (mod_row(i, tm), 0, 0)),
                  pl.BlockSpec((1, d), const), pl.BlockSpec((1, d), const),
                  wspec, wspec, wspec, wspec],
        out_specs=pl.BlockSpec((tm, d), row),
        out_shape=jax.ShapeDtypeStruct((r, d), F32),
        compiler_params=_params(1, 52),
        name="merge_out_proj",
    )(o_ret, gates, o_diff, o_na, x, g1, ln_g, ln_b, w_pa, w_pb, w_pc, w_o)


def _ffn_kernel(x_ref, sh_ref, sc_ref, g2_ref, lng_ref, lnb_ref, w1_ref, w2_ref, o_ref, *, alpha, ff_chunk):
    x = x_ref[...]
    h = (_layer_norm_rows(x) * (1.0 + sc_ref[0]) + sh_ref[0]).astype(BF16)
    acc = jnp.zeros(x.shape, F32)
    for cidx in range(D_FF // ff_chunk):
        a = cidx * ff_chunk
        u = jnp.maximum(jnp.dot(h, w1_ref[:, a:a + ff_chunk], preferred_element_type=F32), 0.0)
        acc = acc + jnp.dot((u * u).astype(BF16), w2_ref[a:a + ff_chunk, :], preferred_element_type=F32)
    z = alpha * x + g2_ref[0] * acc
    o_ref[...] = _layer_norm_rows(z) * lng_ref[...] + lnb_ref[...]


def _ffn(x, shift, scale, g2, ln_g, ln_b, w1, w2, mod_row, alpha):
    r, d = x.shape
    tm = min(FFN_TILE, r)
    row = lambda i: (i, 0)
    const = lambda i: (0, 0)
    mspec = pl.BlockSpec((1, 1, d), lambda i: (mod_row(i, tm), 0, 0))
    return pl.pallas_call(
        functools.partial(_ffn_kernel, alpha=alpha, ff_chunk=1024),
        grid=(r // tm,),
        in_specs=[pl.BlockSpec((tm, d), row), mspec, mspec, mspec,
                  pl.BlockSpec((1, d), const), pl.BlockSpec((1, d), const),
                  pl.BlockSpec((d, D_FF), const, pipeline_mode=pl.Buffered(1)),
                  pl.BlockSpec((D_FF, d), const, pipeline_mode=pl.Buffered(1))],
        out_specs=pl.BlockSpec((tm, d), row),
        out_shape=jax.ShapeDtypeStruct((r, d), F32),
        compiler_params=_params(1, 56),
        name="ffn",
    )(x, shift, scale, g2, ln_g, ln_b, w1, w2)


def _retention_tables(length):
    t = jnp.arange(length, dtype=F32)
    inv = ROPE_BASE ** (-jnp.linspace(0.0, 1.0, RET_DK // 2, dtype=F32))
    ang = t[:, None] * inv[None, :]
    return jnp.cos(ang), jnp.sin(ang)


def _axial_tables(length):
    t = jnp.arange(length)
    row = (t // GRID_W).astype(F32)
    col = (t % GRID_W).astype(F32)
    half = DIFF_HD // 2
    inv = ROPE_BASE ** (-jnp.arange(0, half, 2, dtype=F32) / half)
    ang_r = row[:, None] * inv[None, :]
    ang_c = col[:, None] * inv[None, :]
    zero = jnp.zeros_like(ang_r)
    unit_ang = jnp.concatenate([ang_r, ang_r, ang_c, ang_c], axis=-1)
    cos = jnp.cos(unit_ang)
    sin_r, sin_c = jnp.sin(ang_r), jnp.sin(ang_c)
    sin_lo = jnp.concatenate([-sin_r, zero, -sin_c, zero], axis=-1)
    sin_hi = jnp.concatenate([zero, sin_r, zero, sin_c], axis=-1)
    rep = LANES // DIFF_HD
    return tuple(jnp.tile(a, (1, rep)) for a in (cos, sin_lo, sin_hi))


def _na_bias_table(rpb):
    nh, nr, _ = rpb.shape
    w = GRID_W
    wid = 2 * w - 1
    p = jnp.pad(rpb.astype(F32) * math.log2(math.e), ((0, 0), (0, 0), (w - NA_KW, w - NA_KW)))
    y = p[..., ::-1]
    a = jnp.broadcast_to(y[:, :, None, :], (nh, nr, w, wid))
    a = jnp.pad(a, ((0, 0), (0, 0), (0, 0), (0, 1))).reshape(nh, nr, w * (wid + 1))
    a = a[:, :, :w * wid].reshape(nh, nr, w, wid)
    m = a[..., w - 1:]
    j = jnp.arange(w)
    cs = jnp.clip(j - NA_KW // 2, 0, w - NA_KW)
    col_mask = (j[:, None] >= cs[None, :]) & (j[:, None] < cs[None, :] + NA_KW)
    m = jnp.where(col_mask[None, None], m, NEG_BIG)
    m = jnp.pad(m, ((0, 0), (0, 1), (0, 0), (0, 0)), constant_values=NEG_BIG)
    m = m.reshape(nh // 2, 2, nr + 1, w, w).transpose(0, 2, 3, 1, 4)
    return m.reshape(nh // 2, nr + 1, w, 2 * w)


def _na_bias_rows():
    un, g_rows = NA_UNION_ROWS, NA_UNIT_ROWS
    rows = []
    for cfg in range(3):
        for a in range(un):
            for g in range(g_rows):
                q_row = cfg * g_rows + g
                win0 = (0, g, un - NA_KH)[cfg]
                in_win = win0 <= a < win0 + NA_KH
                rows.append(a - q_row + NA_KH - 1 if in_win else 2 * NA_KH - 1)
    return tuple(rows)


def _column_scale():
    d = D_MODEL
    cs = jnp.ones((IN_WIDTH,), F32)
    cs = cs.at[d:2 * d].set(RET_DK ** -0.5)
    cs = cs.at[4 * d:5 * d].set(DIFF_HD ** -0.5 * math.log2(math.e))
    cs = cs.at[7 * d:8 * d].set(NA_HD ** -0.5 * math.log2(math.e))
    return cs.reshape(1, IN_WIDTH)


def _project_all(h, w, colscale, tm, rope, ret_tables, ax_tables, tiles_per_batch, tk_out, na_tk_out, plain_na_v):
    if rope:
        ret_qk = _proj(h, w, colscale, tm, 0, 1, 2, BF16, "rope_ret", ret_tables, tiles_per_batch)
        diff_qk = _proj(h, w, colscale, tm, 4, 5, 2, BF16, "rope_diff", ax_tables, tiles_per_batch)
    else:
        ret_qk = _proj(h, w, colscale, tm, 0, 1, 2, BF16)
        diff_qk = _proj(h, w, colscale, tm, 4, 5, 2, BF16)
    v_na = _proj(h, w, colscale, tm, 2, 7, 4 if plain_na_v else 3, BF16)
    gates = _proj(h, w, colscale, tm, 3, 10, 4, F32)
    diff_vt = _proj(h, w, colscale, tm, 6, 7, 1, BF16, tk_out=tk_out)
    na_vt = _proj(h, w, colscale, tm, 9, 10, 1, BF16, tk_out=na_tk_out)
    return ret_qk, diff_qk, v_na, gates, diff_vt, na_vt


def kernel(x, c, ctx, c_ctx, w_mod, b_mod, w_in, ret_decay_f, ret_decay_b, diff_lq1, diff_lk1, diff_lq2, diff_lk2,
           diff_subln_g, na_rpb, w_pa, w_pb, w_pc, w_o, ln1_g, ln1_b, w_ff1, w_ff2, ln2_g, ln2_b):
    b, length, d = x.shape
    lc = ctx.shape[1]
    depth = w_mod.shape[0]
    n_cond = SUBLANES
    assert d == D_MODEL and b < n_cond and length % min(ROW_TILE, length) == 0 and length % GRID_W == 0
    assert lc % RET_CHUNK == 0 or lc < RET_CHUNK
    alpha = (2.0 * depth) ** 0.25

    cvec = jnp.zeros((n_cond, d), F32).at[:b].set(c).at[b].set(c_ctx)
    mods = _modulation(cvec, w_mod, b_mod)

    lat_row = lambda i, tm: (i * tm) // length
    ctx_row = lambda i, tm: b
    colscale = _column_scale()
    ret_tables = _retention_tables(length)
    ax_tables = _axial_tables(length)

    x_l = x.reshape(b * length, d)
    x_c = ctx.reshape(b * lc, d)
    zero_state = jnp.zeros((b, RET_HEADS, RET_DK, RET_DK), F32)
    for l in range(depth):
        with_ctx_out = l < depth - 1
        mod = [mods[l, :, k * d:(k + 1) * d].reshape(n_cond, 1, d) for k in range(6)]
        sh1, sc1, g1, sh2, sc2, g2 = mod
        w = (w_in, l)
        wpa, wpb, wpc, wo = (a[l].astype(BF16) for a in (w_pa, w_pb, w_pc, w_o))
        w1, w2 = w_ff1[l].astype(BF16), w_ff2[l].astype(BF16)
        lng1, lnb1 = ln1_g[l].reshape(1, d), ln1_b[l].reshape(1, d)
        lng2, lnb2 = ln2_g[l].reshape(1, d), ln2_b[l].reshape(1, d)
        lam_init = 0.8 - 0.6 * math.exp(-0.3 * l)
        lam4 = jnp.stack([diff_lq1[l], diff_lk1[l], diff_lq2[l], diff_lk2[l]]).astype(F32)
        subln = diff_subln_g[l].reshape(1, DIFF_DV).astype(F32)
        bias = _na_expand_bias(_na_bias_table(na_rpb[l]))

        h_l = _ln_mod(x_l, sh1, sc1, lat_row)
        h_c = _ln_mod(x_c, sh1, sc1, ctx_row)
        tm_l = min(ROW_TILE, length)
        rqk_l, dqk_l, vna_l, gates_l, dvt_l, nvt_l = _project_all(
            h_l, w, colscale, tm_l, True, ret_tables, ax_tables, length // tm_l, min(DIFF_TILE, length),
            NA_UNIT_ROWS * GRID_W, False)
        rqk_c, dqk_c, vna_c, gates_c, dvt_c, nvt_c = _project_all(
            h_c, w, colscale, min(ROW_TILE, b * lc), False, (), (), 1, lc, lc, True)
        sh3 = lambda a, n: a.reshape(b, n, a.shape[-1])

        decays = _retention_decays(ret_decay_f[l], ret_decay_b[l], min(RET_CHUNK, lc))
        rqk_l3, vna_l3, rqk_c3, vna_c3 = sh3(rqk_l, length), sh3(vna_l, length), sh3(rqk_c, lc), sh3(vna_c, lc)
        o_cf, s_cf = _retention(decays, rqk_c3, vna_c3, zero_state, None, False)
        o_rc, s_cb = _retention(decays, rqk_c3, vna_c3, zero_state, o_cf, True)
        o_lf, _ = _retention(decays, rqk_l3, vna_l3, s_cf, None, False)
        o_rl, _ = _retention(decays, rqk_l3, vna_l3, s_cb, o_lf, True)

        dqk_l3, dqk_c3 = sh3(dqk_l, length), sh3(dqk_c, lc)
        o_dl = _diff_attention(lam4, subln, dqk_l3, (dqk_l3, dvt_l), (dqk_c3, dvt_c), lam_init)
        o_nl = _na_attention(vna_l3, nvt_l, vna_c3, nvt_c, bias)

        x_l = _merge(o_rl.reshape(b * length, d), gates_l, o_dl.reshape(b * length, d), o_nl.reshape(b * length, d),
                     x_l, g1, lng1, lnb1, wpa, wpb, wpc, wo, lat_row, alpha)
        x_l = _ffn(x_l, sh2, sc2, g2, lng2, lnb2, w1, w2, lat_row, alpha)
        if with_ctx_out:
            o_dc = _diff_attention(lam4, subln, dqk_c3, None, (dqk_c3, dvt_c), lam_init)
            o_nc = _ctx_attention(vna_c3)
            x_c = _merge(o_rc.reshape(b * lc, d), gates_c, o_dc.reshape(b * lc, d), o_nc.reshape(b * lc, d),
                         x_c, g1, lng1, lnb1, wpa, wpb, wpc, wo, ctx_row, alpha)
            x_c = _ffn(x_c, sh2, sc2, g2, lng2, lnb2, w1, w2, ctx_row, alpha)
    return x_l.reshape(b, length, d)
```

```python
import functools
import math

import jax
import jax.numpy as jnp
from jax import lax
from jax.experimental import pallas as pl
from jax.experimental.pallas import tpu as pltpu

F32 = jnp.float32
BF16 = jnp.bfloat16

D_MODEL = 1024
GRID_W = 64
RET_HEADS, RET_DK = 4, 256
DIFF_HEADS, DIFF_HD, DIFF_DV = 8, 64, 128
NA_HEADS, NA_HD, NA_KH, NA_KW = 16, 64, 8, 16
D_FF = 4 * D_MODEL
ROPE_BASE = 10000.0
LN_EPS = 1e-6
SUBLN_EPS = 1e-5
IN_WIDTH = 13 * D_MODEL
NEG_BIG = -1e30

LANES = 128
SUBLANES = 8
SUBLANES_BF16 = 16
VMEM_LIMIT_CAP = 56 << 20

ROW_TILE = 2048
PROJ_TN = 1024
LN_TILE = 512
MERGE_TILE = 512
FFN_TILE = 1024
RET_CHUNK = 256
RET_BLOCK = 1024
DIFF_TQ = 256
DIFF_BLOCKS = 8
DIFF_FINISH_SLACK = 2
DIFF_TILE = 512
DIFF_VT_ROWS = DIFF_DV + SUBLANES_BF16
DIFF_SOFTMAX_ROWS = 64
NA_UNIT_ROWS = 4
NA_UNION_ROWS = 12
NA_UNITS = 16
NA_SOFTMAX_ROWS = 64

NT_DIMS = (((1,), (1,)), ((), ()))
TN_DIMS = (((0,), (0,)), ((), ()))


def _params(n_axes, vmem_mb):
    return pltpu.CompilerParams(dimension_semantics=("arbitrary",) * n_axes,
                                vmem_limit_bytes=min(vmem_mb << 20, VMEM_LIMIT_CAP))


def _sigmoid(x):
    return 1.0 / (1.0 + jnp.exp(-x))


def _layer_norm_rows(x):
    mu = jnp.mean(x, axis=-1, keepdims=True)
    xc = x - mu
    var = jnp.mean(xc * xc, axis=-1, keepdims=True)
    return xc * lax.rsqrt(var + LN_EPS)


def _mod_kernel(c_ref, w_ref, b_ref, o_ref):
    c = c_ref[...]
    a = (c * _sigmoid(c)).astype(BF16)
    o_ref[0] = jnp.dot(a, w_ref[0].astype(BF16), preferred_element_type=F32) + b_ref[0]


def _modulation(cvec, w_mod, b_mod):
    depth = w_mod.shape[0]
    d = D_MODEL
    return pl.pallas_call(
        _mod_kernel,
        grid=(depth, 6),
        in_specs=[pl.BlockSpec((SUBLANES, d), lambda l, j: (0, 0)),
                  pl.BlockSpec((1, d, d), lambda l, j: (l, 0, j)),
                  pl.BlockSpec((1, 1, d), lambda l, j: (l, 0, j))],
        out_specs=pl.BlockSpec((1, SUBLANES, d), lambda l, j: (l, 0, j)),
        out_shape=jax.ShapeDtypeStruct((depth, SUBLANES, 6 * d), F32),
        compiler_params=_params(2, 32),
        name="modulation",
    )(cvec, w_mod, b_mod.reshape(depth, 1, 6 * d))


def _ln_mod_kernel(x_ref, sh_ref, sc_ref, o_ref):
    y = _layer_norm_rows(x_ref[...])
    o_ref[...] = (y * (1.0 + sc_ref[0]) + sh_ref[0]).astype(o_ref.dtype)


def _ln_mod(x, shift, scale, mod_row):
    r, d = x.shape
    tm = min(LN_TILE, r)
    return pl.pallas_call(
        _ln_mod_kernel,
        grid=(r // tm,),
        in_specs=[pl.BlockSpec((tm, d), lambda i: (i, 0)),
                  pl.BlockSpec((1, 1, d), lambda i: (mod_row(i, tm), 0, 0)),
                  pl.BlockSpec((1, 1, d), lambda i: (mod_row(i, tm), 0, 0))],
        out_specs=pl.BlockSpec((tm, d), lambda i: (i, 0)),
        out_shape=jax.ShapeDtypeStruct((r, d), BF16),
        compiler_params=_params(1, 32),
        name="ln_modulate",
    )(x, shift, scale)


def _proj_kernel(h_ref, w_ref, cs_ref, *rest, mode, transposed, tn, tk_out):
    o_ref = rest[-1]
    acc = jnp.dot(h_ref[...], w_ref[0].astype(BF16), preferred_element_type=F32) * cs_ref[...]
    if mode == "rope_ret":
        cos = rest[0][...]
        sin = rest[1][...]
        half = RET_DK // 2
        for hh in range(tn // RET_DK):
            a = hh * RET_DK
            x1 = acc[:, a:a + half]
            x2 = acc[:, a + half:a + RET_DK]
            o_ref[:, a:a + half] = (x1 * cos - x2 * sin).astype(o_ref.dtype)
            o_ref[:, a + half:a + RET_DK] = (x1 * sin + x2 * cos).astype(o_ref.dtype)
    elif mode == "rope_diff":
        cos = rest[0][...]
        sin_lo = rest[1][...]
        sin_hi = rest[2][...]
        q16 = DIFF_HD // 4
        for g in range(tn // LANES):
            x = acc[:, g * LANES:(g + 1) * LANES]
            y = x * cos + pltpu.roll(x, LANES - q16, 1) * sin_lo + pltpu.roll(x, q16, 1) * sin_hi
            o_ref[:, g * LANES:(g + 1) * LANES] = y.astype(o_ref.dtype)
    elif transposed:
        acc_t = acc.T
        ones = jnp.ones((DIFF_VT_ROWS - DIFF_DV, tk_out), o_ref.dtype)
        for a in range(o_ref.shape[0]):
            for hh in range(tn // DIFF_DV):
                r0 = hh * DIFF_VT_ROWS
                o_ref[a, r0:r0 + DIFF_DV, :] = acc_t[hh * DIFF_DV:(hh + 1) * DIFF_DV,
                                                     a * tk_out:(a + 1) * tk_out].astype(o_ref.dtype)
                o_ref[a, r0 + DIFF_DV:r0 + DIFF_VT_ROWS, :] = ones
    else:
        o_ref[...] = acc.astype(o_ref.dtype)


def _proj(h, w, colscale, tm, first, rest0, ntiles, out_dtype, mode="plain", tables=(), tiles_per_batch=1,
          tk_out=None):
    w_all, layer = w
    r, d = h.shape
    tn = PROJ_TN
    ncols = ntiles * tn
    assert r % tm == 0
    transposed = tk_out is not None
    wcol = lambda j: jnp.where(j == 0, first, rest0 + j - 1)
    in_specs = [pl.BlockSpec((tm, d), lambda i, j: (i, 0)),
                pl.BlockSpec((1, d, tn), lambda i, j: (layer, 0, wcol(j))),
                pl.BlockSpec((1, tn), lambda i, j: (0, wcol(j)))]
    for _ in tables:
        in_specs.append(pl.BlockSpec((tm, LANES), lambda i, j: (i % tiles_per_batch, 0)))
    if transposed:
        assert tm % tk_out == 0
        na = tm // tk_out
        rows_out = tn // DIFF_DV * DIFF_VT_ROWS
        out_specs = pl.BlockSpec((na, rows_out, tk_out), lambda i, j: (i, j, 0))
        out_shape = jax.ShapeDtypeStruct((r // tk_out, ncols // DIFF_DV * DIFF_VT_ROWS, tk_out), out_dtype)
    else:
        out_specs = pl.BlockSpec((tm, tn), lambda i, j: (i, j))
        out_shape = jax.ShapeDtypeStruct((r, ncols), out_dtype)
    return pl.pallas_call(
        functools.partial(_proj_kernel, mode=mode, transposed=transposed, tn=tn, tk_out=tk_out),
        grid=(r // tm, ntiles),
        in_specs=in_specs,
        out_specs=out_specs,
        out_shape=out_shape,
        compiler_params=_params(2, 52),
        name="in_proj_" + mode + ("_t" if transposed else ""),
    )(h, w_all, colscale, *tables)


def _ret_tables_kernel(dec_ref, intra_ref, qd_ref, kd_ref, cd_ref, *, chunk):
    direction = pl.program_id(0)
    head = pl.program_id(1)
    cf = float(chunk)
    dv = jnp.full((1, 1), dec_ref[direction, head], F32)
    lg = jnp.minimum(dv, 0.0) - jnp.log(1.0 + jnp.exp(-jnp.abs(dv)))
    rev = direction == 1
    ii = lax.broadcasted_iota(jnp.int32, (chunk, chunk), 0).astype(F32)
    jj = lax.broadcasted_iota(jnp.int32, (chunk, chunk), 1).astype(F32)
    dist = jnp.where(rev, jj - ii, ii - jj)
    intra_ref[0, 0] = jnp.where(dist >= 0.0, jnp.exp(lg * jnp.maximum(dist, 0.0)), 0.0)
    i1 = lax.broadcasted_iota(jnp.int32, (chunk, RET_DK), 0).astype(F32)
    qd_ref[0, 0] = jnp.exp(lg * jnp.where(rev, cf - i1, i1 + 1.0))
    kd_ref[0, 0] = jnp.exp(lg * jnp.where(rev, i1, cf - 1.0 - i1))
    cd_ref[0, 0] = jnp.exp(jnp.broadcast_to(lg, (SUBLANES, RET_DK)) * cf)


def _retention_decays(dec_f, dec_b, chunk):
    hh, dk = RET_HEADS, RET_DK
    blk = lambda *shape: pl.BlockSpec((1, 1) + shape, lambda d, h: (d, h, 0, 0))
    return pl.pallas_call(
        functools.partial(_ret_tables_kernel, chunk=chunk),
        grid=(2, hh),
        in_specs=[pl.BlockSpec(memory_space=pltpu.SMEM)],
        out_specs=[blk(chunk, chunk), blk(chunk, dk), blk(chunk, dk), blk(SUBLANES, dk)],
        out_shape=[jax.ShapeDtypeStruct((2, hh, chunk, chunk), F32),
                   jax.ShapeDtypeStruct((2, hh, chunk, dk), F32),
                   jax.ShapeDtypeStruct((2, hh, chunk, dk), F32),
                   jax.ShapeDtypeStruct((2, hh, SUBLANES, dk), F32)],
        compiler_params=_params(2, 32),
        name="retention_tables",
    )(jnp.stack([dec_f, dec_b]).astype(F32))


def _ret_kernel(intra_ref, qd_ref, kd_ref, cd_ref, qk_ref, v_ref, s0_ref, *rest, reverse, has_prev, chunk, n_chunks):
    if has_prev:
        prev_ref, o_ref, sfin_ref, s_scr = rest
    else:
        o_ref, sfin_ref, s_scr = rest
    @pl.when(pl.program_id(1) == 0)
    def _load_initial_state():
        s_scr[...] = s0_ref[0]

    dk = RET_DK
    order = range(n_chunks - 1, -1, -1) if reverse else range(n_chunks)
    for h in range(RET_HEADS):
        cols = slice(h * dk, (h + 1) * dk)
        kcols = slice((RET_HEADS + h) * dk, (RET_HEADS + h + 1) * dk)
        s = s_scr[h]
        for c in order:
            rows = slice(c * chunk, (c + 1) * chunk)
            q = qk_ref[0, rows, cols]
            k = qk_ref[0, rows, kcols]
            v = v_ref[0, rows, cols]
            att = lax.dot_general(q, k, NT_DIMS, preferred_element_type=F32) * intra_ref[0, h]
            qd = (q.astype(F32) * qd_ref[0, h]).astype(BF16)
            o = (jnp.dot(att.astype(BF16), v, preferred_element_type=F32)
                 + jnp.dot(qd, s.astype(BF16), preferred_element_type=F32))
            kd = (k.astype(F32) * kd_ref[0, h]).astype(BF16)
            s = s * cd_ref[0, h, 0:1, :] + lax.dot_general(kd, v, TN_DIMS, preferred_element_type=F32)
            if has_prev:
                o = o + prev_ref[0, rows, cols]
            o_ref[0, rows, cols] = o
        s_scr[h] = s
        sfin_ref[0, h] = s


def _retention(tables, qk, v_arr, s0, prev, reverse):
    b, length, _ = qk.shape
    hh, dk = RET_HEADS, RET_DK
    width = hh * dk
    c = tables[0].shape[-1]
    rb = min(RET_BLOCK, length)
    n = length // rb
    direction = 1 if reverse else 0
    pos = (lambda i: n - 1 - i) if reverse else (lambda i: i)
    tab = lambda *shape: pl.BlockSpec((1, hh) + shape, lambda bi, i: (direction, 0, 0, 0))
    in_specs = [tab(c, c), tab(c, dk), tab(c, dk), tab(SUBLANES, dk),
                pl.BlockSpec((1, rb, 2 * width), lambda bi, i: (bi, pos(i), 0)),
                pl.BlockSpec((1, rb, width), lambda bi, i: (bi, pos(i), 0)),
                pl.BlockSpec((1, hh, dk, dk), lambda bi, i: (bi, 0, 0, 0))]
    args = [*tables, qk, v_arr, s0]
    if prev is not None:
        in_specs.append(pl.BlockSpec((1, rb, width), lambda bi, i: (bi, pos(i), 0)))
        args.append(prev)
    return pl.pallas_call(
        functools.partial(_ret_kernel, reverse=reverse, has_prev=prev is not None, chunk=c, n_chunks=rb // c),
        grid=(b, n),
        in_specs=in_specs,
        out_specs=[pl.BlockSpec((1, rb, width), lambda bi, i: (bi, pos(i), 0)),
                   pl.BlockSpec((1, hh, dk, dk), lambda bi, i: (bi, 0, 0, 0))],
        out_shape=[jax.ShapeDtypeStruct((b, length, width), F32),
                   jax.ShapeDtypeStruct((b, hh, dk, dk), F32)],
        scratch_shapes=[pltpu.VMEM((hh, dk, dk), F32)],
        compiler_params=_params(2, 48),
        name="retention_bwd" if reverse else "retention_fwd",
    )(*args)


def _diff_kernel(zero_ref, lam_ref, g_ref, q_ref, *rest, tq, nb, n_lat, tile, lc, lam_init):
    if n_lat:
        kl_ref, vtl_ref = rest[:2]
        rest = rest[2:]
    kc_ref, vtc_ref, o_ref, qq_scr, klast_scr = rest[:5]
    s_buf, mt_buf, p_buf, al_buf = rest[5:9], rest[9:13], rest[13:15], rest[15:17]
    m_scr, acc_scr = rest[17:]
    n = max(n_lat, 1)
    last_rows = (tile if n_lat else 0) + lc
    rows_of = lambda j: last_rows if j == n - 1 else tile
    if n_lat:
        klast_scr[0:tile, :] = kl_ref[0, (n_lat - 1) * tile:n_lat * tile, :]
    klast_scr[last_rows - lc:last_rows, :] = kc_ref[0]

    def scores(g, slot):
        blk, j = divmod(g, n)
        if j == 0:
            q = q_ref[0, blk * tq:(blk + 1) * tq, :].astype(F32)
            lane = lax.broadcasted_iota(jnp.int32, q.shape, 1)
            qq_scr[blk, 0:tq, :] = jnp.where(lane < DIFF_HD, q, 0.0).astype(BF16)
            qq_scr[blk, tq:2 * tq, :] = jnp.where(lane >= DIFF_HD, q, 0.0).astype(BF16)
            m_scr[blk] = jnp.full(m_scr.shape[1:], NEG_BIG, F32)
        k = klast_scr[...] if j == n - 1 else kl_ref[0, j * tile:(j + 1) * tile, :]
        s = lax.dot_general(k, qq_scr[blk], NT_DIMS, preferred_element_type=F32)
        mt_buf[slot][...] = jnp.max(s, axis=0, keepdims=True)
        s_buf[slot][0:k.shape[0], :] = s

    def softmax(g, anchor):
        blk, j = divmod(g, n)
        s_slot, p_slot, rows = g % 4, g % 2, rows_of(j)
        m_old = m_scr[blk]
        if anchor is not None:
            m_old = jnp.maximum(m_old, anchor + NEG_BIG)
        m_new = jnp.maximum(m_old, mt_buf[s_slot][...])
        al_buf[p_slot][...] = jnp.exp2(m_old - m_new)
        m_scr[blk] = m_new
        z = pl.multiple_of(zero_ref[0], 8)
        s_buf[s_slot][pl.ds(z, 8), :] = s_buf[s_slot][pl.ds(z, 8), :]
        ch = DIFF_SOFTMAX_ROWS
        for r0 in range(0, rows, ch):
            p_buf[p_slot][r0:r0 + ch, :] = jnp.exp2(s_buf[s_slot][r0:r0 + ch, :] - m_new).astype(BF16)

    def values(g):
        blk, j = divmod(g, n)
        slot = g % 2
        pv = 0.0
        if n_lat:
            pv = jnp.dot(vtl_ref[j], p_buf[slot][0:tile, :], preferred_element_type=F32)
        if j == n - 1:
            pv = pv + jnp.dot(vtc_ref[0], p_buf[slot][last_rows - lc:last_rows, :], preferred_element_type=F32)
        acc_scr[blk] = pv if j == 0 else acc_scr[blk] * al_buf[slot][...] + pv

    def finish(blk):
        acc = acc_scr[blk]
        o = acc[0:DIFF_DV] * (1.0 / acc[DIFF_DV:DIFF_DV + 1])
        lam4 = lam_ref[...]
        s1 = jnp.sum(lam4[0:1] * lam4[1:2], axis=-1, keepdims=True)
        s2 = jnp.sum(lam4[2:3] * lam4[3:4], axis=-1, keepdims=True)
        lam = jnp.exp(s1) - jnp.exp(s2) + lam_init
        y = (o[:, :tq] - lam * o[:, tq:]).T
        ms = jnp.mean(y * y, axis=-1, keepdims=True)
        y = y * lax.rsqrt(ms + SUBLN_EPS) * g_ref[...] * (1.0 - lam_init)
        o_ref[0, blk * tq:(blk + 1) * tq, :] = y.astype(o_ref.dtype)
        d = jnp.concatenate([y[0:1, :]] * (2 * tq // LANES), axis=1) * 0.0
        return jnp.where(jnp.isnan(d), 0.0, d)

    total = nb * n
    scores(0, 0)
    if total > 1:
        scores(1, 1)
    anchor, anchor_at = None, -1
    for g in range(total):
        if g >= 1:
            values(g - 1)
            if g % n == 0:
                anchor, anchor_at = finish(g // n - 1), min(g + DIFF_FINISH_SLACK, total - 1)
        if g + 2 < total:
            scores(g + 2, (g + 2) % 4)
        softmax(g, anchor if g == anchor_at else None)
    values(total - 1)
    finish(nb - 1)


def _diff_attention(lam4, subln_g, q_arr, lat, ctx, lam_init):
    b, lq, _ = q_arr.shape
    hh = DIFF_HEADS
    tq = min(DIFF_TQ, lq)
    qk_c, vt_c = ctx
    lc = qk_c.shape[1]
    assert vt_c.shape == (b, hh * DIFF_VT_ROWS, lc)
    nb = math.gcd(DIFF_BLOCKS, lq // tq)
    in_specs = [pl.BlockSpec(memory_space=pltpu.SMEM),
                pl.BlockSpec((4, DIFF_HD), lambda bi, h, i: (0, 0)),
                pl.BlockSpec((1, DIFF_DV), lambda bi, h, i: (0, 0)),
                pl.BlockSpec((1, nb * tq, LANES), lambda bi, h, i: (bi, i, h))]
    args = [jnp.zeros((1,), jnp.int32), lam4, subln_g, q_arr]
    n_lat, tile = 0, lc
    if lat is not None:
        qk_l, vt_l = lat
        length = qk_l.shape[1]
        tile = vt_l.shape[2]
        n_lat = length // tile
        assert lc <= tile
        in_specs += [pl.BlockSpec((1, length, LANES), lambda bi, h, i: (bi, 0, hh + h)),
                     pl.BlockSpec((n_lat, DIFF_VT_ROWS, tile), lambda bi, h, i: (bi, h, 0))]
        args += [qk_l, vt_l]
    in_specs += [pl.BlockSpec((1, lc, LANES), lambda bi, h, i: (bi, 0, hh + h)),
                 pl.BlockSpec((1, DIFF_VT_ROWS, lc), lambda bi, h, i: (bi, h, 0))]
    args += [qk_c, vt_c]
    last_rows = (tile if n_lat else 0) + lc
    return pl.pallas_call(
        functools.partial(_diff_kernel, tq=tq, nb=nb, n_lat=n_lat, tile=tile, lc=lc, lam_init=lam_init),
        grid=(b, hh, lq // (nb * tq)),
        in_specs=in_specs,
        out_specs=pl.BlockSpec((1, nb * tq, LANES), lambda bi, h, i: (bi, i, h)),
        out_shape=jax.ShapeDtypeStruct((b, lq, hh * DIFF_DV), BF16),
        scratch_shapes=[pltpu.VMEM((nb, 2 * tq, LANES), BF16),
                        pltpu.VMEM((last_rows, LANES), BF16),
                        *[pltpu.VMEM((last_rows, 2 * tq), F32)] * 4,
                        *[pltpu.VMEM((1, 2 * tq), F32)] * 4,
                        *[pltpu.VMEM((last_rows, 2 * tq), BF16)] * 2,
                        *[pltpu.VMEM((1, 2 * tq), F32)] * 2,
                        pltpu.VMEM((nb, 1, 2 * tq), F32),
                        pltpu.VMEM((nb, DIFF_VT_ROWS, 2 * tq), F32)],
        compiler_params=_params(3, 40),
        name="diff_attention" if lat is not None else "diff_attention_ctx",
    )(*args)


def _softmax_pv(s_list, v_list):
    m = s_list[0].max(axis=-1, keepdims=True)
    for s in s_list[1:]:
        m = jnp.maximum(m, s.max(axis=-1, keepdims=True))
    den = 0.0
    o = 0.0
    for s, v in zip(s_list, v_list):
        p = jnp.exp2(s - m)
        den = den + jnp.sum(p, axis=-1, keepdims=True)
        o = o + jnp.dot(p.astype(BF16), v, preferred_element_type=F32)
    return o * (1.0 / den)


def _na_bias_kernel(tab_ref, o_ref, *, rows):
    w = GRID_W
    for cfg in range(3):
        for a in range(NA_UNION_ROWS):
            for g in range(NA_UNIT_ROWS):
                r = rows[(cfg * NA_UNION_ROWS + a) * NA_UNIT_ROWS + g]
                o_ref[0, cfg, a * w:(a + 1) * w, g * LANES:(g + 1) * LANES] = tab_ref[0, r]


def _na_expand_bias(tab):
    npair = tab.shape[0]
    nk, cols = NA_UNION_ROWS * GRID_W, NA_UNIT_ROWS * LANES
    return pl.pallas_call(
        functools.partial(_na_bias_kernel, rows=_na_bias_rows()),
        grid=(npair,),
        in_specs=[pl.BlockSpec((1,) + tab.shape[1:], lambda p: (p, 0, 0, 0))],
        out_specs=pl.BlockSpec((1, 3, nk, cols), lambda p: (p, 0, 0, 0)),
        out_shape=jax.ShapeDtypeStruct((npair, 3, nk, cols), F32),
        compiler_params=_params(1, 32),
        name="na_bias_table",
    )(tab)


def _na_kernel(zero_ref, q_ref, k_ref, vtl_ref, kc_ref, vtc_ref, bias_ref, o_ref, *scr, units, n_rows, lc):
    qq_scr, keys_scr = scr[0:2]
    s_buf, mt_buf, p_buf = scr[2:5], scr[5:8], scr[8:10]
    w = GRID_W
    uq = NA_UNIT_ROWS * w
    nk = NA_UNION_ROWS * w
    step_r0 = pl.program_id(2) * (units * NA_UNIT_ROWS)
    lo_lanes = lax.broadcasted_iota(jnp.int32, (w, LANES), 1) < NA_HD

    def first_key_row(u):
        return jnp.clip(step_r0 + u * NA_UNIT_ROWS - NA_KH // 2, 0, n_rows - NA_UNION_ROWS)

    def scores(u, slot):
        start = first_key_row(u)
        cfg = (step_r0 + u * NA_UNIT_ROWS - start) // NA_UNIT_ROWS
        for g in range(NA_UNIT_ROWS):
            qg = q_ref[0, u * uq + g * w:u * uq + (g + 1) * w, :].astype(F32)
            qq_scr[g * LANES:g * LANES + w, :] = jnp.where(lo_lanes, qg, 0.0).astype(BF16)
            qq_scr[g * LANES + w:(g + 1) * LANES, :] = jnp.where(lo_lanes, 0.0, qg).astype(BF16)
        koff = pl.multiple_of(start * w, uq)
        keys_scr[0:nk, :] = k_ref[0, pl.ds(koff, nk), :]
        s = lax.dot_general(keys_scr[...], qq_scr[...], NT_DIMS, preferred_element_type=F32)
        s_nb = s[0:nk] + bias_ref[0, cfg]
        s_cx = s[nk:nk + lc]
        mt_buf[slot][...] = jnp.maximum(jnp.max(s_nb, axis=0, keepdims=True), jnp.max(s_cx, axis=0, keepdims=True))
        s_buf[slot][0:nk, :] = s_nb
        s_buf[slot][nk:nk + lc, :] = s_cx

    def softmax(s_slot, p_slot):
        m = mt_buf[s_slot][...]
        z = pl.multiple_of(zero_ref[0], 8)
        s_buf[s_slot][pl.ds(z, 8), :] = s_buf[s_slot][pl.ds(z, 8), :]
        ch = NA_SOFTMAX_ROWS
        for r0 in range(0, nk + lc, ch):
            p_buf[p_slot][r0:r0 + ch, :] = jnp.exp2(s_buf[s_slot][r0:r0 + ch, :] - m).astype(BF16)

    def values(u, p_slot):
        pc0 = first_key_row(u) // NA_UNIT_ROWS
        acc = jnp.dot(vtc_ref[0], p_buf[p_slot][nk:nk + lc, :], preferred_element_type=F32)
        for i in range(NA_UNION_ROWS // NA_UNIT_ROWS):
            acc = acc + jnp.dot(vtl_ref[pc0 + i], p_buf[p_slot][i * uq:(i + 1) * uq, :],
                                preferred_element_type=F32)
        o = acc[0:LANES] * (1.0 / acc[LANES:LANES + 1])
        for g in range(NA_UNIT_ROWS):
            x_t = o[:, g * LANES:(g + 1) * LANES].T
            y = jnp.where(lo_lanes, x_t[0:w], x_t[w:2 * w])
            o_ref[0, u * uq + g * w:u * uq + (g + 1) * w, :] = y.astype(o_ref.dtype)

    keys_scr[nk:nk + lc, :] = kc_ref[0]
    scores(0, 0)
    if units > 1:
        scores(1, 1)
    for u in range(units):
        if u >= 1:
            values(u - 1, (u - 1) % 2)
        if u + 2 < units:
            scores(u + 2, (u + 2) % 3)
        softmax(u % 3, u % 2)
    values(units - 1, (units - 1) % 2)


def _na_attention(qk_l, vt_l, qk_c, vt_c, bias):
    b, length, _ = qk_l.shape
    lc = qk_c.shape[1]
    n_rows = length // GRID_W
    uq = NA_UNIT_ROWS * GRID_W
    nk = NA_UNION_ROWS * GRID_W
    assert n_rows >= NA_UNION_ROWS and n_rows % NA_UNIT_ROWS == 0 and vt_l.shape[2] == uq
    units = min(NA_UNITS, n_rows // NA_UNIT_ROWS)
    npair = NA_HEADS // 2
    n_pieces = length // uq
    tq = units * uq
    return pl.pallas_call(
        functools.partial(_na_kernel, units=units, n_rows=n_rows, lc=lc),
        grid=(b, npair, length // tq),
        in_specs=[pl.BlockSpec(memory_space=pltpu.SMEM),
                  pl.BlockSpec((1, tq, LANES), lambda bi, p, i: (bi, i, npair + p)),
                  pl.BlockSpec((1, length, LANES), lambda bi, p, i: (bi, 0, 2 * npair + p)),
                  pl.BlockSpec((n_pieces, DIFF_VT_ROWS, uq), lambda bi, p, i: (bi, p, 0)),
                  pl.BlockSpec((1, lc, LANES), lambda bi, p, i: (bi, 0, 2 * npair + p)),
                  pl.BlockSpec((1, DIFF_VT_ROWS, lc), lambda bi, p, i: (bi, p, 0)),
                  pl.BlockSpec((1, 3, nk, 2 * uq), lambda bi, p, i: (p, 0, 0, 0))],
        out_specs=pl.BlockSpec((1, tq, LANES), lambda bi, p, i: (bi, i, p)),
        out_shape=jax.ShapeDtypeStruct((b, length, NA_HEADS * NA_HD), BF16),
        scratch_shapes=[pltpu.VMEM((2 * uq, LANES), BF16),
                        pltpu.VMEM((nk + lc, LANES), BF16),
                        *[pltpu.VMEM((nk + lc, 2 * uq), F32)] * 3,
                        *[pltpu.VMEM((1, 2 * uq), F32)] * 3,
                        *[pltpu.VMEM((nk + lc, 2 * uq), BF16)] * 2],
        compiler_params=_params(3, 48),
        name="neighbourhood_attention",
    )(jnp.zeros((1,), jnp.int32), qk_l, qk_l, vt_l, qk_c, vt_c, bias)


def _ctx_attn_kernel(q_ref, k_ref, v_ref, o_ref):
    q = q_ref[0].astype(F32)
    k = k_ref[0]
    v = v_ref[0]
    lane = lax.broadcasted_iota(jnp.int32, q.shape, 1)
    outs = []
    for hh in range(2):
        sel = (lane < NA_HD) if hh == 0 else (lane >= NA_HD)
        qm = jnp.where(sel, q, 0.0).astype(BF16)
        s = lax.dot_general(qm, k, NT_DIMS, preferred_element_type=F32)
        outs.append(_softmax_pv([s], [v]))
    o_ref[0] = jnp.where(lane < NA_HD, outs[0], outs[1]).astype(o_ref.dtype)


def _ctx_attention(qkv_c):
    b, lc, _ = qkv_c.shape
    npair = NA_HEADS // 2
    return pl.pallas_call(
        _ctx_attn_kernel,
        grid=(b, npair),
        in_specs=[pl.BlockSpec((1, lc, LANES), lambda bi, p: (bi, 0, npair + p)),
                  pl.BlockSpec((1, lc, LANES), lambda bi, p: (bi, 0, 2 * npair + p)),
                  pl.BlockSpec((1, lc, LANES), lambda bi, p: (bi, 0, 3 * npair + p))],
        out_specs=pl.BlockSpec((1, lc, LANES), lambda bi, p: (bi, 0, p)),
        out_shape=jax.ShapeDtypeStruct((b, lc, NA_HEADS * NA_HD), BF16),
        compiler_params=_params(2, 32),
        name="ctx_attention",
    )(qkv_c, qkv_c, qkv_c)


def _merge_kernel(oret_ref, gates_ref, odiff_ref, ona_ref, x_ref, g1_ref, lng_ref, lnb_ref,
                  wpa_ref, wpb_ref, wpc_ref, wo_ref, o_ref, *, alpha):
    d = D_MODEL
    ya_parts = []
    for hh in range(RET_HEADS):
        a = hh * RET_DK
        seg = _layer_norm_rows(oret_ref[:, a:a + RET_DK])
        g = gates_ref[:, a:a + RET_DK]
        ya_parts.append((g * _sigmoid(g) * seg).astype(BF16))
    ya_in = jnp.concatenate(ya_parts, axis=1)
    y_a = jnp.dot(ya_in, wpa_ref[...], preferred_element_type=F32)
    y_b = jnp.dot(odiff_ref[...], wpb_ref[...], preferred_element_type=F32)
    y_c = jnp.dot(ona_ref[...], wpc_ref[...], preferred_element_type=F32)
    m = (_sigmoid(gates_ref[:, d:2 * d]) * y_a + _sigmoid(gates_ref[:, 2 * d:3 * d]) * y_b
         + _sigmoid(gates_ref[:, 3 * d:4 * d]) * y_c)
    y = jnp.dot(m.astype(BF16), wo_ref[...], preferred_element_type=F32)
    z = alpha * x_ref[...] + g1_ref[0] * y
    o_ref[...] = _layer_norm_rows(z) * lng_ref[...] + lnb_ref[...]


def _merge(o_ret, gates, o_diff, o_na, x, g1, ln_g, ln_b, w_pa, w_pb, w_pc, w_o, mod_row, alpha):
    r, d = x.shape
    tm = min(MERGE_TILE, r)
    row = lambda i: (i, 0)
    const = lambda i: (0, 0)
    wspec = pl.BlockSpec((d, d), const, pipeline_mode=pl.Buffered(1))
    return pl.pallas_call(
        functools.partial(_merge_kernel, alpha=alpha),
        grid=(r // tm,),
        in_specs=[pl.BlockSpec((tm, d), row), pl.BlockSpec((tm, 4 * d), row), pl.BlockSpec((tm, d), row),
                  pl.BlockSpec((tm, d), row), pl.BlockSpec((tm, d), row),
                  pl.BlockSpec((1, 1, d), lambda i: (mod_row(i, tm), 0, 0)),
                  pl.BlockSpec((1, d), const), pl.BlockSpec((1, d), const),
                  wspec, wspec, wspec, wspec],
        out_specs=pl.BlockSpec((tm, d), row),
        out_shape=jax.ShapeDtypeStruct((r, d), F32),
        compiler_params=_params(1, 52),
        name="merge_out_proj",
    )(o_ret, gates, o_diff, o_na, x, g1, ln_g, ln_b, w_pa, w_pb, w_pc, w_o)


def _ffn_kernel(x_ref, sh_ref, sc_ref, g2_ref, lng_ref, lnb_ref, w1_ref, w2_ref, o_ref, *, alpha, ff_chunk):
    x = x_ref[...]
    h = (_layer_norm_rows(x) * (1.0 + sc_ref[0]) + sh_ref[0]).astype(BF16)
    acc = jnp.zeros(x.shape, F32)
    for cidx in range(D_FF // ff_chunk):
        a = cidx * ff_chunk
        u = jnp.maximum(jnp.dot(h, w1_ref[:, a:a + ff_chunk], preferred_element_type=F32), 0.0)
        acc = acc + jnp.dot((u * u).astype(BF16), w2_ref[a:a + ff_chunk, :], preferred_element_type=F32)
    z = alpha * x + g2_ref[0] * acc
    o_ref[...] = _layer_norm_rows(z) * lng_ref[...] + lnb_ref[...]


def _ffn(x, shift, scale, g2, ln_g, ln_b, w1, w2, mod_row, alpha):
    r, d = x.shape
    tm = min(FFN_TILE, r)
    row = lambda i: (i, 0)
    const = lambda i: (0, 0)
    mspec = pl.BlockSpec((1, 1, d), lambda i: (mod_row(i, tm), 0, 0))
    return pl.pallas_call(
        functools.partial(_ffn_kernel, alpha=alpha, ff_chunk=1024),
        grid=(r // tm,),
        in_specs=[pl.BlockSpec((tm, d), row), mspec, mspec, mspec,
                  pl.BlockSpec((1, d), const), pl.BlockSpec((1, d), const),
                  pl.BlockSpec((d, D_FF), const, pipeline_mode=pl.Buffered(1)),
                  pl.BlockSpec((D_FF, d), const, pipeline_mode=pl.Buffered(1))],
        out_specs=pl.BlockSpec((tm, d), row),
        out_shape=jax.ShapeDtypeStruct((r, d), F32),
        compiler_params=_params(1, 56),
        name="ffn",
    )(x, shift, scale, g2, ln_g, ln_b, w1, w2)


def _retention_tables(length):
    t = jnp.arange(length, dtype=F32)
    inv = ROPE_BASE ** (-jnp.linspace(0.0, 1.0, RET_DK // 2, dtype=F32))
    ang = t[:, None] * inv[None, :]
    return jnp.cos(ang), jnp.sin(ang)


def _axial_tables(length):
    t = jnp.arange(length)
    row = (t // GRID_W).astype(F32)
    col = (t % GRID_W).astype(F32)
    half = DIFF_HD // 2
    inv = ROPE_BASE ** (-jnp.arange(0, half, 2, dtype=F32) / half)
    ang_r = row[:, None] * inv[None, :]
    ang_c = col[:, None] * inv[None, :]
    zero = jnp.zeros_like(ang_r)
    unit_ang = jnp.concatenate([ang_r, ang_r, ang_c, ang_c], axis=-1)
    cos = jnp.cos(unit_ang)
    sin_r, sin_c = jnp.sin(ang_r), jnp.sin(ang_c)
    sin_lo = jnp.concatenate([-sin_r, zero, -sin_c, zero], axis=-1)
    sin_hi = jnp.concatenate([zero, sin_r, zero, sin_c], axis=-1)
    rep = LANES // DIFF_HD
    return tuple(jnp.tile(a, (1, rep)) for a in (cos, sin_lo, sin_hi))


def _na_bias_table(rpb):
    nh, nr, _ = rpb.shape
    w = GRID_W
    wid = 2 * w - 1
    p = jnp.pad(rpb.astype(F32) * math.log2(math.e), ((0, 0), (0, 0), (w - NA_KW, w - NA_KW)))
    y = p[..., ::-1]
    a = jnp.broadcast_to(y[:, :, None, :], (nh, nr, w, wid))
    a = jnp.pad(a, ((0, 0), (0, 0), (0, 0), (0, 1))).reshape(nh, nr, w * (wid + 1))
    a = a[:, :, :w * wid].reshape(nh, nr, w, wid)
    m = a[..., w - 1:]
    j = jnp.arange(w)
    cs = jnp.clip(j - NA_KW // 2, 0, w - NA_KW)
    col_mask = (j[:, None] >= cs[None, :]) & (j[:, None] < cs[None, :] + NA_KW)
    m = jnp.where(col_mask[None, None], m, NEG_BIG)
    m = jnp.pad(m, ((0, 0), (0, 1), (0, 0), (0, 0)), constant_values=NEG_BIG)
    m = m.reshape(nh // 2, 2, nr + 1, w, w).transpose(0, 2, 3, 1, 4)
    return m.reshape(nh // 2, nr + 1, w, 2 * w)


def _na_bias_rows():
    un, g_rows = NA_UNION_ROWS, NA_UNIT_ROWS
    rows = []
    for cfg in range(3):
        for a in range(un):
            for g in range(g_rows):
                q_row = cfg * g_rows + g
                win0 = (0, g, un - NA_KH)[cfg]
                in_win = win0 <= a < win0 + NA_KH
                rows.append(a - q_row + NA_KH - 1 if in_win else 2 * NA_KH - 1)
    return tuple(rows)


def _column_scale():
    d = D_MODEL
    cs = jnp.ones((IN_WIDTH,), F32)
    cs = cs.at[d:2 * d].set(RET_DK ** -0.5)
    cs = cs.at[4 * d:5 * d].set(DIFF_HD ** -0.5 * math.log2(math.e))
    cs = cs.at[7 * d:8 * d].set(NA_HD ** -0.5 * math.log2(math.e))
    return cs.reshape(1, IN_WIDTH)


def _project_all(h, w, colscale, tm, rope, ret_tables, ax_tables, tiles_per_batch, tk_out, na_tk_out, plain_na_v):
    if rope:
        ret_qk = _proj(h, w, colscale, tm, 0, 1, 2, BF16, "rope_ret", ret_tables, tiles_per_batch)
        diff_qk = _proj(h, w, colscale, tm, 4, 5, 2, BF16, "rope_diff", ax_tables, tiles_per_batch)
    else:
        ret_qk = _proj(h, w, colscale, tm, 0, 1, 2, BF16)
        diff_qk = _proj(h, w, colscale, tm, 4, 5, 2, BF16)
    v_na = _proj(h, w, colscale, tm, 2, 7, 4 if plain_na_v else 3, BF16)
    gates = _proj(h, w, colscale, tm, 3, 10, 4, F32)
    diff_vt = _proj(h, w, colscale, tm, 6, 7, 1, BF16, tk_out=tk_out)
    na_vt = _proj(h, w, colscale, tm, 9, 10, 1, BF16, tk_out=na_tk_out)
    return ret_qk, diff_qk, v_na, gates, diff_vt, na_vt


def kernel(x, c, ctx, c_ctx, w_mod, b_mod, w_in, ret_decay_f, ret_decay_b, diff_lq1, diff_lk1, diff_lq2, diff_lk2,
           diff_subln_g, na_rpb, w_pa, w_pb, w_pc, w_o, ln1_g, ln1_b, w_ff1, w_ff2, ln2_g, ln2_b):
    b, length, d = x.shape
    lc = ctx.shape[1]
    depth = w_mod.shape[0]
    n_cond = SUBLANES
    assert d == D_MODEL and b < n_cond and length % min(ROW_TILE, length) == 0 and length % GRID_W == 0
    assert lc % RET_CHUNK == 0 or lc < RET_CHUNK
    alpha = (2.0 * depth) ** 0.25

    cvec = jnp.zeros((n_cond, d), F32).at[:b].set(c).at[b].set(c_ctx)
    mods = _modulation(cvec, w_mod, b_mod)

    lat_row = lambda i, tm: (i * tm) // length
    ctx_row = lambda i, tm: b
    colscale = _column_scale()
    ret_tables = _retention_tables(length)
    ax_tables = _axial_tables(length)

    x_l = x.reshape(b * length, d)
    x_c = ctx.reshape(b * lc, d)
    zero_state = jnp.zeros((b, RET_HEADS, RET_DK, RET_DK), F32)
    for l in range(depth):
        with_ctx_out = l < depth - 1
        mod = [mods[l, :, k * d:(k + 1) * d].reshape(n_cond, 1, d) for k in range(6)]
        sh1, sc1, g1, sh2, sc2, g2 = mod
        w = (w_in, l)
        wpa, wpb, wpc, wo = (a[l].astype(BF16) for a in (w_pa, w_pb, w_pc, w_o))
        w1, w2 = w_ff1[l].astype(BF16), w_ff2[l].astype(BF16)
        lng1, lnb1 = ln1_g[l].reshape(1, d), ln1_b[l].reshape(1, d)
        lng2, lnb2 = ln2_g[l].reshape(1, d), ln2_b[l].reshape(1, d)
        lam_init = 0.8 - 0.6 * math.exp(-0.3 * l)
        lam4 = jnp.stack([diff_lq1[l], diff_lk1[l], diff_lq2[l], diff_lk2[l]]).astype(F32)
        subln = diff_subln_g[l].reshape(1, DIFF_DV).astype(F32)
        bias = _na_expand_bias(_na_bias_table(na_rpb[l]))

        h_l = _ln_mod(x_l, sh1, sc1, lat_row)
        h_c = _ln_mod(x_c, sh1, sc1, ctx_row)
        tm_l = min(ROW_TILE, length)
        rqk_l, dqk_l, vna_l, gates_l, dvt_l, nvt_l = _project_all(
            h_l, w, colscale, tm_l, True, ret_tables, ax_tables, length // tm_l, min(DIFF_TILE, length),
            NA_UNIT_ROWS * GRID_W, False)
        rqk_c, dqk_c, vna_c, gates_c, dvt_c, nvt_c = _project_all(
            h_c, w, colscale, min(ROW_TILE, b * lc), False, (), (), 1, lc, lc, True)
        sh3 = lambda a, n: a.reshape(b, n, a.shape[-1])

        decays = _retention_decays(ret_decay_f[l], ret_decay_b[l], min(RET_CHUNK, lc))
        rqk_l3, vna_l3, rqk_c3, vna_c3 = sh3(rqk_l, length), sh3(vna_l, length), sh3(rqk_c, lc), sh3(vna_c, lc)
        o_cf, s_cf = _retention(decays, rqk_c3, vna_c3, zero_state, None, False)
        o_rc, s_cb = _retention(decays, rqk_c3, vna_c3, zero_state, o_cf, True)
        o_lf, _ = _retention(decays, rqk_l3, vna_l3, s_cf, None, False)
        o_rl, _ = _retention(decays, rqk_l3, vna_l3, s_cb, o_lf, True)

        dqk_l3, dqk_c3 = sh3(dqk_l, length), sh3(dqk_c, lc)
        o_dl = _diff_attention(lam4, subln, dqk_l3, (dqk_l3, dvt_l), (dqk_c3, dvt_c), lam_init)
        o_nl = _na_attention(vna_l3, nvt_l, vna_c3, nvt_c, bias)

        x_l = _merge(o_rl.reshape(b * length, d), gates_l, o_dl.reshape(b * length, d), o_nl.reshape(b * length, d),
                     x_l, g1, lng1, lnb1, wpa, wpb, wpc, wo, lat_row, alpha)
        x_l = _ffn(x_l, sh2, sc2, g2, lng2, lnb2, w1, w2, lat_row, alpha)
        if with_ctx_out:
            o_dc = _diff_attention(lam4, subln, dqk_c3, None, (dqk_c3, dvt_c), lam_init)
            o_nc = _ctx_attention(vna_c3)
            x_c = _merge(o_rc.reshape(b * lc, d), gates_c, o_dc.reshape(b * lc, d), o_nc.reshape(b * lc, d),
                         x_c, g1, lng1, lnb1, wpa, wpb, wpc, wo, ctx_row, alpha)
            x_c = _ffn(x_c, sh2, sc2, g2, lng2, lnb2, w1, w2, ctx_row, alpha)
    return x_l.reshape(b, length, d)
```

```python
import functools
import math

import jax
import jax.numpy as jnp
from jax import lax
from jax.experimental import pallas as pl
from jax.experimental.pallas import tpu as pltpu

F32 = jnp.float32
BF16 = jnp.bfloat16

D_MODEL = 1024
GRID_W = 64
RET_HEADS, RET_DK = 4, 256
DIFF_HEADS, DIFF_HD, DIFF_DV = 8, 64, 128
NA_HEADS, NA_HD, NA_KH, NA_KW = 16, 64, 8, 16
D_FF = 4 * D_MODEL
ROPE_BASE = 10000.0
LN_EPS = 1e-6
SUBLN_EPS = 1e-5
IN_WIDTH = 13 * D_MODEL
NEG_BIG = -1e30

LANES = 128
SUBLANES = 8
SUBLANES_BF16 = 16
VMEM_LIMIT_CAP = 56 << 20

ROW_TILE = 2048
PROJ_TN = 1024
LN_TILE = 512
MERGE_TILE = 512
FFN_TILE = 1024
RET_CHUNK = 256
RET_BLOCK = 1024
DIFF_TQ = 256
DIFF_BLOCKS = 8
DIFF_FINISH_SLACK = 2
DIFF_TILE = 512
DIFF_VT_ROWS = DIFF_DV + SUBLANES_BF16
DIFF_SOFTMAX_ROWS = 64
NA_UNIT_ROWS = 4
NA_UNION_ROWS = 12
NA_UNITS = 16
NA_SOFTMAX_ROWS = 64

NT_DIMS = (((1,), (1,)), ((), ()))
TN_DIMS = (((0,), (0,)), ((), ()))


def _params(n_axes, vmem_mb):
    return pltpu.CompilerParams(dimension_semantics=("arbitrary",) * n_axes,
                                vmem_limit_bytes=min(vmem_mb << 20, VMEM_LIMIT_CAP))


def _sigmoid(x):
    return 1.0 / (1.0 + jnp.exp(-x))


def _layer_norm_rows(x):
    mu = jnp.mean(x, axis=-1, keepdims=True)
    xc = x - mu
    var = jnp.mean(xc * xc, axis=-1, keepdims=True)
    return xc * lax.rsqrt(var + LN_EPS)


def _mod_kernel(c_ref, w_ref, b_ref, o_ref):
    c = c_ref[...]
    a = (c * _sigmoid(c)).astype(BF16)
    o_ref[0] = jnp.dot(a, w_ref[0].astype(BF16), preferred_element_type=F32) + b_ref[0]


def _modulation(cvec, w_mod, b_mod):
    depth = w_mod.shape[0]
    d = D_MODEL
    return pl.pallas_call(
        _mod_kernel,
        grid=(depth, 6),
        in_specs=[pl.BlockSpec((SUBLANES, d), lambda l, j: (0, 0)),
                  pl.BlockSpec((1, d, d), lambda l, j: (l, 0, j)),
                  pl.BlockSpec((1, 1, d), lambda l, j: (l, 0, j))],
        out_specs=pl.BlockSpec((1, SUBLANES, d), lambda l, j: (l, 0, j)),
        out_shape=jax.ShapeDtypeStruct((depth, SUBLANES, 6 * d), F32),
        compiler_params=_params(2, 32),
        name="modulation",
    )(cvec, w_mod, b_mod.reshape(depth, 1, 6 * d))


def _ln_mod_kernel(x_ref, sh_ref, sc_ref, o_ref):
    y = _layer_norm_rows(x_ref[...])
    o_ref[...] = (y * (1.0 + sc_ref[0]) + sh_ref[0]).astype(o_ref.dtype)


def _ln_mod(x, shift, scale, mod_row):
    r, d = x.shape
    tm = min(LN_TILE, r)
    return pl.pallas_call(
        _ln_mod_kernel,
        grid=(r // tm,),
        in_specs=[pl.BlockSpec((tm, d), lambda i: (i, 0)),
                  pl.BlockSpec((1, 1, d), lambda i: (mod_row(i, tm), 0, 0)),
                  pl.BlockSpec((1, 1, d), lambda i: (mod_row(i, tm), 0, 0))],
        out_specs=pl.BlockSpec((tm, d), lambda i: (i, 0)),
        out_shape=jax.ShapeDtypeStruct((r, d), BF16),
        compiler_params=_params(1, 32),
        name="ln_modulate",
    )(x, shift, scale)


def _proj_kernel(h_ref, w_ref, cs_ref, *rest, mode, transposed, tn, tk_out):
    o_ref = rest[-1]
    acc = jnp.dot(h_ref[...], w_ref[0].astype(BF16), preferred_element_type=F32) * cs_ref[...]
    if mode == "rope_ret":
        cos = rest[0][...]
        sin = rest[1][...]
        half = RET_DK // 2
        for hh in range(tn // RET_DK):
            a = hh * RET_DK
            x1 = acc[:, a:a + half]
            x2 = acc[:, a + half:a + RET_DK]
            o_ref[:, a:a + half] = (x1 * cos - x2 * sin).astype(o_ref.dtype)
            o_ref[:, a + half:a + RET_DK] = (x1 * sin + x2 * cos).astype(o_ref.dtype)
    elif mode == "rope_diff":
        cos = rest[0][...]
        sin_lo = rest[1][...]
        sin_hi = rest[2][...]
        q16 = DIFF_HD // 4
        for g in range(tn // LANES):
            x = acc[:, g * LANES:(g + 1) * LANES]
            y = x * cos + pltpu.roll(x, LANES - q16, 1) * sin_lo + pltpu.roll(x, q16, 1) * sin_hi
            o_ref[:, g * LANES:(g + 1) * LANES] = y.astype(o_ref.dtype)
    elif transposed:
        acc_t = acc.T
        ones = jnp.ones((DIFF_VT_ROWS - DIFF_DV, tk_out), o_ref.dtype)
        for a in range(o_ref.shape[0]):
            for hh in range(tn // DIFF_DV):
                r0 = hh * DIFF_VT_ROWS
                o_ref[a, r0:r0 + DIFF_DV, :] = acc_t[hh * DIFF_DV:(hh + 1) * DIFF_DV,
                                                     a * tk_out:(a + 1) * tk_out].astype(o_ref.dtype)
                o_ref[a, r0 + DIFF_DV:r0 + DIFF_VT_ROWS, :] = ones
    else:
        o_ref[...] = acc.astype(o_ref.dtype)


def _proj(h, w, colscale, tm, first, rest0, ntiles, out_dtype, mode="plain", tables=(), tiles_per_batch=1,
          tk_out=None):
    w_all, layer = w
    r, d = h.shape
    tn = PROJ_TN
    ncols = ntiles * tn
    assert r % tm == 0
    transposed = tk_out is not None
    wcol = lambda j: jnp.where(j == 0, first, rest0 + j - 1)
    in_specs = [pl.BlockSpec((tm, d), lambda i, j: (i, 0)),
                pl.BlockSpec((1, d, tn), lambda i, j: (layer, 0, wcol(j))),
                pl.BlockSpec((1, tn), lambda i, j: (0, wcol(j)))]
    for _ in tables:
        in_specs.append(pl.BlockSpec((tm, LANES), lambda i, j: (i % tiles_per_batch, 0)))
    if transposed:
        assert tm % tk_out == 0
        na = tm // tk_out
        rows_out = tn // DIFF_DV * DIFF_VT_ROWS
        out_specs = pl.BlockSpec((na, rows_out, tk_out), lambda i, j: (i, j, 0))
        out_shape = jax.ShapeDtypeStruct((r // tk_out, ncols // DIFF_DV * DIFF_VT_ROWS, tk_out), out_dtype)
    else:
        out_specs = pl.BlockSpec((tm, tn), lambda i, j: (i, j))
        out_shape = jax.ShapeDtypeStruct((r, ncols), out_dtype)
    return pl.pallas_call(
        functools.partial(_proj_kernel, mode=mode, transposed=transposed, tn=tn, tk_out=tk_out),
        grid=(r // tm, ntiles),
        in_specs=in_specs,
        out_specs=out_specs,
        out_shape=out_shape,
        compiler_params=_params(2, 52),
        name="in_proj_" + mode + ("_t" if transposed else ""),
    )(h, w_all, colscale, *tables)


def _ret_tables_kernel(dec_ref, intra_ref, qd_ref, kd_ref, cd_ref, *, chunk):
    direction = pl.program_id(0)
    head = pl.program_id(1)
    cf = float(chunk)
    dv = jnp.full((1, 1), dec_ref[direction, head], F32)
    lg = jnp.minimum(dv, 0.0) - jnp.log(1.0 + jnp.exp(-jnp.abs(dv)))
    rev = direction == 1
    ii = lax.broadcasted_iota(jnp.int32, (chunk, chunk), 0).astype(F32)
    jj = lax.broadcasted_iota(jnp.int32, (chunk, chunk), 1).astype(F32)
    dist = jnp.where(rev, jj - ii, ii - jj)
    intra_ref[0, 0] = jnp.where(dist >= 0.0, jnp.exp(lg * jnp.maximum(dist, 0.0)), 0.0)
    i1 = lax.broadcasted_iota(jnp.int32, (chunk, RET_DK), 0).astype(F32)
    qd_ref[0, 0] = jnp.exp(lg * jnp.where(rev, cf - i1, i1 + 1.0))
    kd_ref[0, 0] = jnp.exp(lg * jnp.where(rev, i1, cf - 1.0 - i1))
    cd_ref[0, 0] = jnp.exp(jnp.broadcast_to(lg, (SUBLANES, RET_DK)) * cf)


def _retention_decays(dec_f, dec_b, chunk):
    hh, dk = RET_HEADS, RET_DK
    blk = lambda *shape: pl.BlockSpec((1, 1) + shape, lambda d, h: (d, h, 0, 0))
    return pl.pallas_call(
        functools.partial(_ret_tables_kernel, chunk=chunk),
        grid=(2, hh),
        in_specs=[pl.BlockSpec(memory_space=pltpu.SMEM)],
        out_specs=[blk(chunk, chunk), blk(chunk, dk), blk(chunk, dk), blk(SUBLANES, dk)],
        out_shape=[jax.ShapeDtypeStruct((2, hh, chunk, chunk), F32),
                   jax.ShapeDtypeStruct((2, hh, chunk, dk), F32),
                   jax.ShapeDtypeStruct((2, hh, chunk, dk), F32),
                   jax.ShapeDtypeStruct((2, hh, SUBLANES, dk), F32)],
        compiler_params=_params(2, 32),
        name="retention_tables",
    )(jnp.stack([dec_f, dec_b]).astype(F32))


def _ret_kernel(intra_ref, qd_ref, kd_ref, cd_ref, qk_ref, v_ref, s0_ref, *rest, reverse, has_prev, chunk, n_chunks):
    if has_prev:
        prev_ref, o_ref, sfin_ref, s_scr = rest
    else:
        o_ref, sfin_ref, s_scr = rest
    @pl.when(pl.program_id(1) == 0)
    def _load_initial_state():
        s_scr[...] = s0_ref[0]

    dk = RET_DK
    order = range(n_chunks - 1, -1, -1) if reverse else range(n_chunks)
    for h in range(RET_HEADS):
        cols = slice(h * dk, (h + 1) * dk)
        kcols = slice((RET_HEADS + h) * dk, (RET_HEADS + h + 1) * dk)
        s = s_scr[h]
        for c in order:
            rows = slice(c * chunk, (c + 1) * chunk)
            q = qk_ref[0, rows, cols]
            k = qk_ref[0, rows, kcols]
            v = v_ref[0, rows, cols]
            att = lax.dot_general(q, k, NT_DIMS, preferred_element_type=F32) * intra_ref[0, h]
            qd = (q.astype(F32) * qd_ref[0, h]).astype(BF16)
            o = (jnp.dot(att.astype(BF16), v, preferred_element_type=F32)
                 + jnp.dot(qd, s.astype(BF16), preferred_element_type=F32))
            kd = (k.astype(F32) * kd_ref[0, h]).astype(BF16)
            s = s * cd_ref[0, h, 0:1, :] + lax.dot_general(kd, v, TN_DIMS, preferred_element_type=F32)
            if has_prev:
                o = o + prev_ref[0, rows, cols]
            o_ref[0, rows, cols] = o
        s_scr[h] = s
        sfin_ref[0, h] = s


def _retention(tables, qk, v_arr, s0, prev, reverse):
    b, length, _ = qk.shape
    hh, dk = RET_HEADS, RET_DK
    width = hh * dk
    c = tables[0].shape[-1]
    rb = min(RET_BLOCK, length)
    n = length // rb
    direction = 1 if reverse else 0
    pos = (lambda i: n - 1 - i) if reverse else (lambda i: i)
    tab = lambda *shape: pl.BlockSpec((1, hh) + shape, lambda bi, i: (direction, 0, 0, 0))
    in_specs = [tab(c, c), tab(c, dk), tab(c, dk), tab(SUBLANES, dk),
                pl.BlockSpec((1, rb, 2 * width), lambda bi, i: (bi, pos(i), 0)),
                pl.BlockSpec((1, rb, width), lambda bi, i: (bi, pos(i), 0)),
                pl.BlockSpec((1, hh, dk, dk), lambda bi, i: (bi, 0, 0, 0))]
    args = [*tables, qk, v_arr, s0]
    if prev is not None:
        in_specs.append(pl.BlockSpec((1, rb, width), lambda bi, i: (bi, pos(i), 0)))
        args.append(prev)
    return pl.pallas_call(
        functools.partial(_ret_kernel, reverse=reverse, has_prev=prev is not None, chunk=c, n_chunks=rb // c),
        grid=(b, n),
        in_specs=in_specs,
        out_specs=[pl.BlockSpec((1, rb, width), lambda bi, i: (bi, pos(i), 0)),
                   pl.BlockSpec((1, hh, dk, dk), lambda bi, i: (bi, 0, 0, 0))],
        out_shape=[jax.ShapeDtypeStruct((b, length, width), F32),
                   jax.ShapeDtypeStruct((b, hh, dk, dk), F32)],
        scratch_shapes=[pltpu.VMEM((hh, dk, dk), F32)],
        compiler_params=_params(2, 48),
        name="retention_bwd" if reverse else "retention_fwd",
    )(*args)


def _diff_kernel(zero_ref, lam_ref, g_ref, q_ref, *rest, tq, nb, n_lat, tile, lc, lam_init):
    if n_lat:
        kl_ref, vtl_ref = rest[:2]
        rest = rest[2:]
    kc_ref, vtc_ref, o_ref, qq_scr, klast_scr = rest[:5]
    s_buf, mt_buf, p_buf, al_buf = rest[5:9], rest[9:13], rest[13:15], rest[15:17]
    m_scr, acc_scr = rest[17:]
    n = max(n_lat, 1)
    last_rows = (tile if n_lat else 0) + lc
    rows_of = lambda j: last_rows if j == n - 1 else tile
    if n_lat:
        klast_scr[0:tile, :] = kl_ref[0, (n_lat - 1) * tile:n_lat * tile, :]
    klast_scr[last_rows - lc:last_rows, :] = kc_ref[0]

    def scores(g, slot):
        blk, j = divmod(g, n)
        if j == 0:
            q = q_ref[0, blk * tq:(blk + 1) * tq, :].astype(F32)
            lane = lax.broadcasted_iota(jnp.int32, q.shape, 1)
            qq_scr[blk, 0:tq, :] = jnp.where(lane < DIFF_HD, q, 0.0).astype(BF16)
            qq_scr[blk, tq:2 * tq, :] = jnp.where(lane >= DIFF_HD, q, 0.0).astype(BF16)
            m_scr[blk] = jnp.full(m_scr.shape[1:], NEG_BIG, F32)
        k = klast_scr[...] if j == n - 1 else kl_ref[0, j * tile:(j + 1) * tile, :]
        s = lax.dot_general(k, qq_scr[blk], NT_DIMS, preferred_element_type=F32)
        mt_buf[slot][...] = jnp.max(s, axis=0, keepdims=True)
        s_buf[slot][0:k.shape[0], :] = s

    def softmax(g, anchor):
        blk, j = divmod(g, n)
        s_slot, p_slot, rows = g % 4, g % 2, rows_of(j)
        m_old = m_scr[blk]
        if anchor is not None:
            m_old = jnp.maximum(m_old, anchor + NEG_BIG)
        m_new = jnp.maximum(m_old, mt_buf[s_slot][...])
        al_buf[p_slot][...] = jnp.exp2(m_old - m_new)
        m_scr[blk] = m_new
        z = pl.multiple_of(zero_ref[0], 8)
        s_buf[s_slot][pl.ds(z, 8), :] = s_buf[s_slot][pl.ds(z, 8), :]
        ch = DIFF_SOFTMAX_ROWS
        for r0 in range(0, rows, ch):
            p_buf[p_slot][r0:r0 + ch, :] = jnp.exp2(s_buf[s_slot][r0:r0 + ch, :] - m_new).astype(BF16)

    def values(g):
        blk, j = divmod(g, n)
        slot = g % 2
        pv = 0.0
        if n_lat:
            pv = jnp.dot(vtl_ref[j], p_buf[slot][0:tile, :], preferred_element_type=F32)
        if j == n - 1:
            pv = pv + jnp.dot(vtc_ref[0], p_buf[slot][last_rows - lc:last_rows, :], preferred_element_type=F32)
        acc_scr[blk] = pv if j == 0 else acc_scr[blk] * al_buf[slot][...] + pv

    def finish(blk):
        acc = acc_scr[blk]
        o = acc[0:DIFF_DV] * (1.0 / acc[DIFF_DV:DIFF_DV + 1])
        lam4 = lam_ref[...]
        s1 = jnp.sum(lam4[0:1] * lam4[1:2], axis=-1, keepdims=True)
        s2 = jnp.sum(lam4[2:3] * lam4[3:4], axis=-1, keepdims=True)
        lam = jnp.exp(s1) - jnp.exp(s2) + lam_init
        y = (o[:, :tq] - lam * o[:, tq:]).T
        ms = jnp.mean(y * y, axis=-1, keepdims=True)
        y = y * lax.rsqrt(ms + SUBLN_EPS) * g_ref[...] * (1.0 - lam_init)
        o_ref[0, blk * tq:(blk + 1) * tq, :] = y.astype(o_ref.dtype)
        d = jnp.concatenate([y[0:1, :]] * (2 * tq // LANES), axis=1) * 0.0
        return jnp.where(jnp.isnan(d), 0.0, d)

    total = nb * n
    scores(0, 0)
    if total > 1:
        scores(1, 1)
    anchor, anchor_at = None, -1
    for g in range(total):
        if g >= 1:
            values(g - 1)
            if g % n == 0:
                anchor, anchor_at = finish(g // n - 1), min(g + DIFF_FINISH_SLACK, total - 1)
        if g + 2 < total:
            scores(g + 2, (g + 2) % 4)
        softmax(g, anchor if g == anchor_at else None)
    values(total - 1)
    finish(nb - 1)


def _diff_attention(lam4, subln_g, q_arr, lat, ctx, lam_init):
    b, lq, _ = q_arr.shape
    hh = DIFF_HEADS
    tq = min(DIFF_TQ, lq)
    qk_c, vt_c = ctx
    lc = qk_c.shape[1]
    assert vt_c.shape == (b, hh * DIFF_VT_ROWS, lc)
    nb = math.gcd(DIFF_BLOCKS, lq // tq)
    in_specs = [pl.BlockSpec(memory_space=pltpu.SMEM),
                pl.BlockSpec((4, DIFF_HD), lambda bi, h, i: (0, 0)),
                pl.BlockSpec((1, DIFF_DV), lambda bi, h, i: (0, 0)),
                pl.BlockSpec((1, nb * tq, LANES), lambda bi, h, i: (bi, i, h))]
    args = [jnp.zeros((1,), jnp.int32), lam4, subln_g, q_arr]
    n_lat, tile = 0, lc
    if lat is not None:
        qk_l, vt_l = lat
        length = qk_l.shape[1]
        tile = vt_l.shape[2]
        n_lat = length // tile
        assert lc <= tile
        in_specs += [pl.BlockSpec((1, length, LANES), lambda bi, h, i: (bi, 0, hh + h)),
                     pl.BlockSpec((n_lat, DIFF_VT_ROWS, tile), lambda bi, h, i: (bi, h, 0))]
        args += [qk_l, vt_l]
    in_specs += [pl.BlockSpec((1, lc, LANES), lambda bi, h, i: (bi, 0, hh + h)),
                 pl.BlockSpec((1, DIFF_VT_ROWS, lc), lambda bi, h, i: (bi, h, 0))]
    args += [qk_c, vt_c]
    last_rows = (tile if n_lat else 0) + lc
    return pl.pallas_call(
        functools.partial(_diff_kernel, tq=tq, nb=nb, n_lat=n_lat, tile=tile, lc=lc, lam_init=lam_init),
        grid=(b, hh, lq // (nb * tq)),
        in_specs=in_specs,
        out_specs=pl.BlockSpec((1, nb * tq, LANES), lambda bi, h, i: (bi, i, h)),
        out_shape=jax.ShapeDtypeStruct((b, lq, hh * DIFF_DV), BF16),
        scratch_shapes=[pltpu.VMEM((nb, 2 * tq, LANES), BF16),
                        pltpu.VMEM((last_rows, LANES), BF16),
                        *[pltpu.VMEM((last_rows, 2 * tq), F32)] * 4,
                        *[pltpu.VMEM((1, 2 * tq), F32)] * 4,
                        *[pltpu.VMEM((last_rows, 2 * tq), BF16)] * 2,
                        *[pltpu.VMEM((1, 2 * tq), F32)] * 2,
                        pltpu.VMEM((nb, 1, 2 * tq), F32),
                        pltpu.VMEM((nb, DIFF_VT_ROWS, 2 * tq), F32)],
        compiler_params=_params(3, 40),
        name="diff_attention" if lat is not None else "diff_attention_ctx",
    )(*args)


def _softmax_pv(s_list, v_list):
    m = s_list[0].max(axis=-1, keepdims=True)
    for s in s_list[1:]:
        m = jnp.maximum(m, s.max(axis=-1, keepdims=True))
    den = 0.0
    o = 0.0
    for s, v in zip(s_list, v_list):
        p = jnp.exp2(s - m)
        den = den + jnp.sum(p, axis=-1, keepdims=True)
        o = o + jnp.dot(p.astype(BF16), v, preferred_element_type=F32)
    return o * (1.0 / den)


def _na_bias_kernel(tab_ref, o_ref, *, rows):
    w = GRID_W
    for cfg in range(3):
        for a in range(NA_UNION_ROWS):
            for hh in range(2):
                for g in range(0, NA_UNIT_ROWS, 2):
                    r0, r1 = (rows[(cfg * NA_UNION_ROWS + a) * NA_UNIT_ROWS + g + i] for i in range(2))
                    c0 = (hh * NA_UNIT_ROWS + g) * w
                    o_ref[0, cfg, a * w:(a + 1) * w, c0:c0 + LANES] = jnp.concatenate(
                        [tab_ref[0, hh, r0], tab_ref[0, hh, r1]], axis=1)


def _na_expand_bias(tab):
    npair = tab.shape[0]
    nk, cols = NA_UNION_ROWS * GRID_W, NA_UNIT_ROWS * LANES
    return pl.pallas_call(
        functools.partial(_na_bias_kernel, rows=_na_bias_rows()),
        grid=(npair,),
        in_specs=[pl.BlockSpec((1,) + tab.shape[1:], lambda p: (p, 0, 0, 0, 0))],
        out_specs=pl.BlockSpec((1, 3, nk, cols), lambda p: (p, 0, 0, 0)),
        out_shape=jax.ShapeDtypeStruct((npair, 3, nk, cols), F32),
        compiler_params=_params(1, 32),
        name="na_bias_table",
    )(tab)


def _na_kernel(zero_ref, q_ref, k_ref, vtl_ref, kc_ref, vtc_ref, bias_ref, o_ref, *scr, units, n_rows, lc):
    keys_scr = scr[0]
    s_buf, mt_buf, p_buf = scr[1:4], scr[4:7], scr[7:9]
    w = GRID_W
    uq = NA_UNIT_ROWS * w
    nk = NA_UNION_ROWS * w
    step_r0 = pl.program_id(2) * (units * NA_UNIT_ROWS)

    def first_key_row(u):
        return jnp.clip(step_r0 + u * NA_UNIT_ROWS - NA_KH // 2, 0, n_rows - NA_UNION_ROWS)

    def scores(u, slot):
        start = first_key_row(u)
        cfg = (step_r0 + u * NA_UNIT_ROWS - start) // NA_UNIT_ROWS
        q_t = q_ref[0, u * uq:(u + 1) * uq, :].astype(F32).T
        row = lax.broadcasted_iota(jnp.int32, q_t.shape, 0)
        qq = jnp.concatenate([jnp.where(row < NA_HD, q_t, 0.0), jnp.where(row >= NA_HD, q_t, 0.0)],
                             axis=1).astype(BF16)
        koff = pl.multiple_of(start * w, uq)
        keys_scr[0:nk, :] = k_ref[0, pl.ds(koff, nk), :]
        s = jnp.dot(keys_scr[...], qq, preferred_element_type=F32)
        s_nb = s[0:nk] + bias_ref[0, cfg]
        s_cx = s[nk:nk + lc]
        mt_buf[slot][...] = jnp.maximum(jnp.max(s_nb, axis=0, keepdims=True), jnp.max(s_cx, axis=0, keepdims=True))
        s_buf[slot][0:nk, :] = s_nb
        s_buf[slot][nk:nk + lc, :] = s_cx

    def softmax(s_slot, p_slot):
        m = mt_buf[s_slot][...]
        z = pl.multiple_of(zero_ref[0], 8)
        s_buf[s_slot][pl.ds(z, 8), :] = s_buf[s_slot][pl.ds(z, 8), :]
        ch = NA_SOFTMAX_ROWS
        for r0 in range(0, nk + lc, ch):
            p_buf[p_slot][r0:r0 + ch, :] = jnp.exp2(s_buf[s_slot][r0:r0 + ch, :] - m).astype(BF16)

    def values(u, p_slot):
        pc0 = first_key_row(u) // NA_UNIT_ROWS
        acc = jnp.dot(vtc_ref[0], p_buf[p_slot][nk:nk + lc, :], preferred_element_type=F32)
        for i in range(NA_UNION_ROWS // NA_UNIT_ROWS):
            acc = acc + jnp.dot(vtl_ref[pc0 + i], p_buf[p_slot][i * uq:(i + 1) * uq, :],
                                preferred_element_type=F32)
        inv = 1.0 / acc[LANES:LANES + 1]
        y = jnp.concatenate([acc[0:NA_HD, 0:uq] * inv[:, 0:uq],
                             acc[NA_HD:LANES, uq:2 * uq] * inv[:, uq:2 * uq]], axis=0)
        o_ref[0, u * uq:(u + 1) * uq, :] = y.T.astype(o_ref.dtype)

    keys_scr[nk:nk + lc, :] = kc_ref[0]
    scores(0, 0)
    if units > 1:
        scores(1, 1)
    for u in range(units):
        if u >= 1:
            values(u - 1, (u - 1) % 2)
        if u + 2 < units:
            scores(u + 2, (u + 2) % 3)
        softmax(u % 3, u % 2)
    values(units - 1, (units - 1) % 2)


def _na_attention(qk_l, vt_l, qk_c, vt_c, bias):
    b, length, _ = qk_l.shape
    lc = qk_c.shape[1]
    n_rows = length // GRID_W
    uq = NA_UNIT_ROWS * GRID_W
    nk = NA_UNION_ROWS * GRID_W
    assert n_rows >= NA_UNION_ROWS and n_rows % NA_UNIT_ROWS == 0 and vt_l.shape[2] == uq
    units = min(NA_UNITS, n_rows // NA_UNIT_ROWS)
    npair = NA_HEADS // 2
    n_pieces = length // uq
    tq = units * uq
    return pl.pallas_call(
        functools.partial(_na_kernel, units=units, n_rows=n_rows, lc=lc),
        grid=(b, npair, length // tq),
        in_specs=[pl.BlockSpec(memory_space=pltpu.SMEM),
                  pl.BlockSpec((1, tq, LANES), lambda bi, p, i: (bi, i, npair + p)),
                  pl.BlockSpec((1, length, LANES), lambda bi, p, i: (bi, 0, 2 * npair + p)),
                  pl.BlockSpec((n_pieces, DIFF_VT_ROWS, uq), lambda bi, p, i: (bi, p, 0)),
                  pl.BlockSpec((1, lc, LANES), lambda bi, p, i: (bi, 0, 2 * npair + p)),
                  pl.BlockSpec((1, DIFF_VT_ROWS, lc), lambda bi, p, i: (bi, p, 0)),
                  pl.BlockSpec((1, 3, nk, 2 * uq), lambda bi, p, i: (p, 0, 0, 0))],
        out_specs=pl.BlockSpec((1, tq, LANES), lambda bi, p, i: (bi, i, p)),
        out_shape=jax.ShapeDtypeStruct((b, length, NA_HEADS * NA_HD), BF16),
        scratch_shapes=[pltpu.VMEM((nk + lc, LANES), BF16),
                        *[pltpu.VMEM((nk + lc, 2 * uq), F32)] * 3,
                        *[pltpu.VMEM((1, 2 * uq), F32)] * 3,
                        *[pltpu.VMEM((nk + lc, 2 * uq), BF16)] * 2],
        compiler_params=_params(3, 48),
        name="neighbourhood_attention",
    )(jnp.zeros((1,), jnp.int32), qk_l, qk_l, vt_l, qk_c, vt_c, bias)


def _ctx_attn_kernel(q_ref, k_ref, v_ref, o_ref):
    q = q_ref[0].astype(F32)
    k = k_ref[0]
    v = v_ref[0]
    lane = lax.broadcasted_iota(jnp.int32, q.shape, 1)
    outs = []
    for hh in range(2):
        sel = (lane < NA_HD) if hh == 0 else (lane >= NA_HD)
        qm = jnp.where(sel, q, 0.0).astype(BF16)
        s = lax.dot_general(qm, k, NT_DIMS, preferred_element_type=F32)
        outs.append(_softmax_pv([s], [v]))
    o_ref[0] = jnp.where(lane < NA_HD, outs[0], outs[1]).astype(o_ref.dtype)


def _ctx_attention(qkv_c):
    b, lc, _ = qkv_c.shape
    npair = NA_HEADS // 2
    return pl.pallas_call(
        _ctx_attn_kernel,
        grid=(b, npair),
        in_specs=[pl.BlockSpec((1, lc, LANES), lambda bi, p: (bi, 0, npair + p)),
                  pl.BlockSpec((1, lc, LANES), lambda bi, p: (bi, 0, 2 * npair + p)),
                  pl.BlockSpec((1, lc, LANES), lambda bi, p: (bi, 0, 3 * npair + p))],
        out_specs=pl.BlockSpec((1, lc, LANES), lambda bi, p: (bi, 0, p)),
        out_shape=jax.ShapeDtypeStruct((b, lc, NA_HEADS * NA_HD), BF16),
        compiler_params=_params(2, 32),
        name="ctx_attention",
    )(qkv_c, qkv_c, qkv_c)


def _merge_kernel(oret_ref, gates_ref, odiff_ref, ona_ref, x_ref, g1_ref, lng_ref, lnb_ref,
                  wpa_ref, wpb_ref, wpc_ref, wo_ref, o_ref, *, alpha):
    d = D_MODEL
    ya_parts = []
    for hh in range(RET_HEADS):
        a = hh * RET_DK
        seg = _layer_norm_rows(oret_ref[:, a:a + RET_DK])
        g = gates_ref[:, a:a + RET_DK]
        ya_parts.append((g * _sigmoid(g) * seg).astype(BF16))
    ya_in = jnp.concatenate(ya_parts, axis=1)
    y_a = jnp.dot(ya_in, wpa_ref[...], preferred_element_type=F32)
    y_b = jnp.dot(odiff_ref[...], wpb_ref[...], preferred_element_type=F32)
    y_c = jnp.dot(ona_ref[...], wpc_ref[...], preferred_element_type=F32)
    m = (_sigmoid(gates_ref[:, d:2 * d]) * y_a + _sigmoid(gates_ref[:, 2 * d:3 * d]) * y_b
         + _sigmoid(gates_ref[:, 3 * d:4 * d]) * y_c)
    y = jnp.dot(m.astype(BF16), wo_ref[...], preferred_element_type=F32)
    z = alpha * x_ref[...] + g1_ref[0] * y
    o_ref[...] = _layer_norm_rows(z) * lng_ref[...] + lnb_ref[...]


def _merge(o_ret, gates, o_diff, o_na, x, g1, ln_g, ln_b, w_pa, w_pb, w_pc, w_o, mod_row, alpha):
    r, d = x.shape
    tm = min(MERGE_TILE, r)
    row = lambda i: (i, 0)
    const = lambda i: (0, 0)
    wspec = pl.BlockSpec((d, d), const, pipeline_mode=pl.Buffered(1))
    return pl.pallas_call(
        functools.partial(_merge_kernel, alpha=alpha),
        grid=(r // tm,),
        in_specs=[pl.BlockSpec((tm, d), row), pl.BlockSpec((tm, 4 * d), row), pl.BlockSpec((tm, d), row),
                  pl.BlockSpec((tm, d), row), pl.BlockSpec((tm, d), row),
                  pl.BlockSpec((1, 1, d), lambda i: (mod_row(i, tm), 0, 0)),
                  pl.BlockSpec((1, d), const), pl.BlockSpec((1, d), const),
                  wspec, wspec, wspec, wspec],
        out_specs=pl.BlockSpec((tm, d), row),
        out_shape=jax.ShapeDtypeStruct((r, d), F32),
        compiler_params=_params(1, 52),
        name="merge_out_proj",
    )(o_ret, gates, o_diff, o_na, x, g1, ln_g, ln_b, w_pa, w_pb, w_pc, w_o)


def _ffn_kernel(x_ref, sh_ref, sc_ref, g2_ref, lng_ref, lnb_ref, w1_ref, w2_ref, o_ref, *, alpha, ff_chunk):
    x = x_ref[...]
    h = (_layer_norm_rows(x) * (1.0 + sc_ref[0]) + sh_ref[0]).astype(BF16)
    acc = jnp.zeros(x.shape, F32)
    for cidx in range(D_FF // ff_chunk):
        a = cidx * ff_chunk
        u = jnp.maximum(jnp.dot(h, w1_ref[:, a:a + ff_chunk], preferred_element_type=F32), 0.0)
        acc = acc + jnp.dot((u * u).astype(BF16), w2_ref[a:a + ff_chunk, :], preferred_element_type=F32)
    z = alpha * x + g2_ref[0] * acc
    o_ref[...] = _layer_norm_rows(z) * lng_ref[...] + lnb_ref[...]


def _ffn(x, shift, scale, g2, ln_g, ln_b, w1, w2, mod_row, alpha):
    r, d = x.shape
    tm = min(FFN_TILE, r)
    row = lambda i: (i, 0)
    const = lambda i: (0, 0)
    mspec = pl.BlockSpec((1, 1, d), lambda i: (mod_row(i, tm), 0, 0))
    return pl.pallas_call(
        functools.partial(_ffn_kernel, alpha=alpha, ff_chunk=1024),
        grid=(r // tm,),
        in_specs=[pl.BlockSpec((tm, d), row), mspec, mspec, mspec,
                  pl.BlockSpec((1, d), const), pl.BlockSpec((1, d), const),
                  pl.BlockSpec((d, D_FF), const, pipeline_mode=pl.Buffered(1)),
                  pl.BlockSpec((D_FF, d), const, pipeline_mode=pl.Buffered(1))],
        out_specs=pl.BlockSpec((tm, d), row),
        out_shape=jax.ShapeDtypeStruct((r, d), F32),
        compiler_params=_params(1, 56),
        name="ffn",
    )(x, shift, scale, g2, ln_g, ln_b, w1, w2)


def _retention_tables(length):
    t = jnp.arange(length, dtype=F32)
    inv = ROPE_BASE ** (-jnp.linspace(0.0, 1.0, RET_DK // 2, dtype=F32))
    ang = t[:, None] * inv[None, :]
    return jnp.cos(ang), jnp.sin(ang)


def _axial_tables(length):
    t = jnp.arange(length)
    row = (t // GRID_W).astype(F32)
    col = (t % GRID_W).astype(F32)
    half = DIFF_HD // 2
    inv = ROPE_BASE ** (-jnp.arange(0, half, 2, dtype=F32) / half)
    ang_r = row[:, None] * inv[None, :]
    ang_c = col[:, None] * inv[None, :]
    zero = jnp.zeros_like(ang_r)
    unit_ang = jnp.concatenate([ang_r, ang_r, ang_c, ang_c], axis=-1)
    cos = jnp.cos(unit_ang)
    sin_r, sin_c = jnp.sin(ang_r), jnp.sin(ang_c)
    sin_lo = jnp.concatenate([-sin_r, zero, -sin_c, zero], axis=-1)
    sin_hi = jnp.concatenate([zero, sin_r, zero, sin_c], axis=-1)
    rep = LANES // DIFF_HD
    return tuple(jnp.tile(a, (1, rep)) for a in (cos, sin_lo, sin_hi))


def _na_bias_table(rpb):
    nh, nr, _ = rpb.shape
    w = GRID_W
    wid = 2 * w - 1
    p = jnp.pad(rpb.astype(F32) * math.log2(math.e), ((0, 0), (0, 0), (w - NA_KW, w - NA_KW)))
    y = p[..., ::-1]
    a = jnp.broadcast_to(y[:, :, None, :], (nh, nr, w, wid))
    a = jnp.pad(a, ((0, 0), (0, 0), (0, 0), (0, 1))).reshape(nh, nr, w * (wid + 1))
    a = a[:, :, :w * wid].reshape(nh, nr, w, wid)
    m = a[..., w - 1:]
    j = jnp.arange(w)
    cs = jnp.clip(j - NA_KW // 2, 0, w - NA_KW)
    col_mask = (j[:, None] >= cs[None, :]) & (j[:, None] < cs[None, :] + NA_KW)
    m = jnp.where(col_mask[None, None], m, NEG_BIG)
    m = jnp.pad(m, ((0, 0), (0, 1), (0, 0), (0, 0)), constant_values=NEG_BIG)
    return m.reshape(nh // 2, 2, nr + 1, w, w)


def _na_bias_rows():
    un, g_rows = NA_UNION_ROWS, NA_UNIT_ROWS
    rows = []
    for cfg in range(3):
        for a in range(un):
            for g in range(g_rows):
                q_row = cfg * g_rows + g
                win0 = (0, g, un - NA_KH)[cfg]
                in_win = win0 <= a < win0 + NA_KH
                rows.append(a - q_row + NA_KH - 1 if in_win else 2 * NA_KH - 1)
    return tuple(rows)


def _column_scale():
    d = D_MODEL
    cs = jnp.ones((IN_WIDTH,), F32)
    cs = cs.at[d:2 * d].set(RET_DK ** -0.5)
    cs = cs.at[4 * d:5 * d].set(DIFF_HD ** -0.5 * math.log2(math.e))
    cs = cs.at[7 * d:8 * d].set(NA_HD ** -0.5 * math.log2(math.e))
    return cs.reshape(1, IN_WIDTH)


def _project_all(h, w, colscale, tm, rope, ret_tables, ax_tables, tiles_per_batch, tk_out, na_tk_out, plain_na_v):
    if rope:
        ret_qk = _proj(h, w, colscale, tm, 0, 1, 2, BF16, "rope_ret", ret_tables, tiles_per_batch)
        diff_qk = _proj(h, w, colscale, tm, 4, 5, 2, BF16, "rope_diff", ax_tables, tiles_per_batch)
    else:
        ret_qk = _proj(h, w, colscale, tm, 0, 1, 2, BF16)
        diff_qk = _proj(h, w, colscale, tm, 4, 5, 2, BF16)
    v_na = _proj(h, w, colscale, tm, 2, 7, 4 if plain_na_v else 3, BF16)
    gates = _proj(h, w, colscale, tm, 3, 10, 4, F32)
    diff_vt = _proj(h, w, colscale, tm, 6, 7, 1, BF16, tk_out=tk_out)
    na_vt = _proj(h, w, colscale, tm, 9, 10, 1, BF16, tk_out=na_tk_out)
    return ret_qk, diff_qk, v_na, gates, diff_vt, na_vt


def kernel(x, c, ctx, c_ctx, w_mod, b_mod, w_in, ret_decay_f, ret_decay_b, diff_lq1, diff_lk1, diff_lq2, diff_lk2,
           diff_subln_g, na_rpb, w_pa, w_pb, w_pc, w_o, ln1_g, ln1_b, w_ff1, w_ff2, ln2_g, ln2_b):
    b, length, d = x.shape
    lc = ctx.shape[1]
    depth = w_mod.shape[0]
    n_cond = SUBLANES
    assert d == D_MODEL and b < n_cond and length % min(ROW_TILE, length) == 0 and length % GRID_W == 0
    assert lc % RET_CHUNK == 0 or lc < RET_CHUNK
    alpha = (2.0 * depth) ** 0.25

    cvec = jnp.zeros((n_cond, d), F32).at[:b].set(c).at[b].set(c_ctx)
    mods = _modulation(cvec, w_mod, b_mod)

    lat_row = lambda i, tm: (i * tm) // length
    ctx_row = lambda i, tm: b
    colscale = _column_scale()
    ret_tables = _retention_tables(length)
    ax_tables = _axial_tables(length)

    x_l = x.reshape(b * length, d)
    x_c = ctx.reshape(b * lc, d)
    zero_state = jnp.zeros((b, RET_HEADS, RET_DK, RET_DK), F32)
    for l in range(depth):
        with_ctx_out = l < depth - 1
        mod = [mods[l, :, k * d:(k + 1) * d].reshape(n_cond, 1, d) for k in range(6)]
        sh1, sc1, g1, sh2, sc2, g2 = mod
        w = (w_in, l)
        wpa, wpb, wpc, wo = (a[l].astype(BF16) for a in (w_pa, w_pb, w_pc, w_o))
        w1, w2 = w_ff1[l].astype(BF16), w_ff2[l].astype(BF16)
        lng1, lnb1 = ln1_g[l].reshape(1, d), ln1_b[l].reshape(1, d)
        lng2, lnb2 = ln2_g[l].reshape(1, d), ln2_b[l].reshape(1, d)
        lam_init = 0.8 - 0.6 * math.exp(-0.3 * l)
        lam4 = jnp.stack([diff_lq1[l], diff_lk1[l], diff_lq2[l], diff_lk2[l]]).astype(F32)
        subln = diff_subln_g[l].reshape(1, DIFF_DV).astype(F32)
        bias = _na_expand_bias(_na_bias_table(na_rpb[l]))

        h_l = _ln_mod(x_l, sh1, sc1, lat_row)
        h_c = _ln_mod(x_c, sh1, sc1, ctx_row)
        tm_l = min(ROW_TILE, length)
        rqk_l, dqk_l, vna_l, gates_l, dvt_l, nvt_l = _project_all(
            h_l, w, colscale, tm_l, True, ret_tables, ax_tables, length // tm_l, min(DIFF_TILE, length),
            NA_UNIT_ROWS * GRID_W, False)
        rqk_c, dqk_c, vna_c, gates_c, dvt_c, nvt_c = _project_all(
            h_c, w, colscale, min(ROW_TILE, b * lc), False, (), (), 1, lc, lc, True)
        sh3 = lambda a, n: a.reshape(b, n, a.shape[-1])

        decays = _retention_decays(ret_decay_f[l], ret_decay_b[l], min(RET_CHUNK, lc))
        rqk_l3, vna_l3, rqk_c3, vna_c3 = sh3(rqk_l, length), sh3(vna_l, length), sh3(rqk_c, lc), sh3(vna_c, lc)
        o_cf, s_cf = _retention(decays, rqk_c3, vna_c3, zero_state, None, False)
        o_rc, s_cb = _retention(decays, rqk_c3, vna_c3, zero_state, o_cf, True)
        o_lf, _ = _retention(decays, rqk_l3, vna_l3, s_cf, None, False)
        o_rl, _ = _retention(decays, rqk_l3, vna_l3, s_cb, o_lf, True)

        dqk_l3, dqk_c3 = sh3(dqk_l, length), sh3(dqk_c, lc)
        o_dl = _diff_attention(lam4, subln, dqk_l3, (dqk_l3, dvt_l), (dqk_c3, dvt_c), lam_init)
        o_nl = _na_attention(vna_l3, nvt_l, vna_c3, nvt_c, bias)

        x_l = _merge(o_rl.reshape(b * length, d), gates_l, o_dl.reshape(b * length, d), o_nl.reshape(b * length, d),
                     x_l, g1, lng1, lnb1, wpa, wpb, wpc, wo, lat_row, alpha)
        x_l = _ffn(x_l, sh2, sc2, g2, lng2, lnb2, w1, w2, lat_row, alpha)
        if with_ctx_out:
            o_dc = _diff_attention(lam4, subln, dqk_c3, None, (dqk_c3, dvt_c), lam_init)
            o_nc = _ctx_attention(vna_c3)
            x_c = _merge(o_rc.reshape(b * lc, d), gates_c, o_dc.reshape(b * lc, d), o_nc.reshape(b * lc, d),
                         x_c, g1, lng1, lnb1, wpa, wpb, wpc, wo, ctx_row, alpha)
            x_c = _ffn(x_c, sh2, sc2, g2, lng2, lnb2, w1, w2, ctx_row, alpha)
    return x_l.reshape(b, length, d)
```

```python
import functools
import math

import jax
import jax.numpy as jnp
from jax import lax
from jax.experimental import pallas as pl
from jax.experimental.pallas import tpu as pltpu

F32 = jnp.float32
BF16 = jnp.bfloat16

D_MODEL = 1024
GRID_W = 64
RET_HEADS, RET_DK = 4, 256
DIFF_HEADS, DIFF_HD, DIFF_DV = 8, 64, 128
NA_HEADS, NA_HD, NA_KH, NA_KW = 16, 64, 8, 16
D_FF = 4 * D_MODEL
ROPE_BASE = 10000.0
LN_EPS = 1e-6
SUBLN_EPS = 1e-5
IN_WIDTH = 13 * D_MODEL
NEG_BIG = -1e30

LANES = 128
SUBLANES = 8
SUBLANES_BF16 = 16
VMEM_LIMIT_CAP = 56 << 20

ROW_TILE = 2048
PROJ_TN = 1024
LN_TILE = 2048
MERGE_TILE = 512
FFN_TILE = 1024
RET_CHUNK = 256
RET_BLOCK = 1024
DIFF_TQ = 256
DIFF_BLOCKS = 8
DIFF_FINISH_SLACK = 2
DIFF_TILE = 512
DIFF_VT_ROWS = DIFF_DV + SUBLANES_BF16
DIFF_SOFTMAX_ROWS = 64
NA_UNIT_ROWS = 4
NA_UNION_ROWS = 12
NA_UNITS = 32
NA_SOFTMAX_ROWS = 64

NT_DIMS = (((1,), (1,)), ((), ()))
TN_DIMS = (((0,), (0,)), ((), ()))


def _params(n_axes, vmem_mb):
    return pltpu.CompilerParams(dimension_semantics=("arbitrary",) * n_axes,
                                vmem_limit_bytes=min(vmem_mb << 20, VMEM_LIMIT_CAP))


def _sigmoid(x):
    return 1.0 / (1.0 + jnp.exp(-x))


def _layer_norm_rows(x):
    mu = jnp.mean(x, axis=-1, keepdims=True)
    xc = x - mu
    var = jnp.mean(xc * xc, axis=-1, keepdims=True)
    return xc * lax.rsqrt(var + LN_EPS)


def _mod_kernel(c_ref, w_ref, b_ref, o_ref):
    c = c_ref[...]
    a = (c * _sigmoid(c)).astype(BF16)
    o_ref[0] = jnp.dot(a, w_ref[0].astype(BF16), preferred_element_type=F32) + b_ref[0]


def _modulation(cvec, w_mod, b_mod):
    depth = w_mod.shape[0]
    d = D_MODEL
    return pl.pallas_call(
        _mod_kernel,
        grid=(depth, 6),
        in_specs=[pl.BlockSpec((SUBLANES, d), lambda l, j: (0, 0)),
                  pl.BlockSpec((1, d, d), lambda l, j: (l, 0, j)),
                  pl.BlockSpec((1, 1, d), lambda l, j: (l, 0, j))],
        out_specs=pl.BlockSpec((1, SUBLANES, d), lambda l, j: (l, 0, j)),
        out_shape=jax.ShapeDtypeStruct((depth, SUBLANES, 6 * d), F32),
        compiler_params=_params(2, 32),
        name="modulation",
    )(cvec, w_mod, b_mod.reshape(depth, 1, 6 * d))


def _ln_mod_kernel(x_ref, sh_ref, sc_ref, o_ref):
    y = _layer_norm_rows(x_ref[...])
    o_ref[...] = (y * (1.0 + sc_ref[0]) + sh_ref[0]).astype(o_ref.dtype)


def _ln_mod(x, shift, scale, mod_row):
    r, d = x.shape
    tm = min(LN_TILE, r)
    return pl.pallas_call(
        _ln_mod_kernel,
        grid=(r // tm,),
        in_specs=[pl.BlockSpec((tm, d), lambda i: (i, 0)),
                  pl.BlockSpec((1, 1, d), lambda i: (mod_row(i, tm), 0, 0)),
                  pl.BlockSpec((1, 1, d), lambda i: (mod_row(i, tm), 0, 0))],
        out_specs=pl.BlockSpec((tm, d), lambda i: (i, 0)),
        out_shape=jax.ShapeDtypeStruct((r, d), BF16),
        compiler_params=_params(1, 32),
        name="ln_modulate",
    )(x, shift, scale)


def _proj_kernel(h_ref, w_ref, cs_ref, *rest, mode, transposed, tn, tk_out):
    o_ref = rest[-1]
    acc = jnp.dot(h_ref[...], w_ref[0].astype(BF16), preferred_element_type=F32) * cs_ref[...]
    if mode == "rope_ret":
        cos = rest[0][...]
        sin = rest[1][...]
        half = RET_DK // 2
        for hh in range(tn // RET_DK):
            a = hh * RET_DK
            x1 = acc[:, a:a + half]
            x2 = acc[:, a + half:a + RET_DK]
            o_ref[:, a:a + half] = (x1 * cos - x2 * sin).astype(o_ref.dtype)
            o_ref[:, a + half:a + RET_DK] = (x1 * sin + x2 * cos).astype(o_ref.dtype)
    elif mode == "rope_diff":
        cos = rest[0][...]
        sin_lo = rest[1][...]
        sin_hi = rest[2][...]
        q16 = DIFF_HD // 4
        for g in range(tn // LANES):
            x = acc[:, g * LANES:(g + 1) * LANES]
            y = x * cos + pltpu.roll(x, LANES - q16, 1) * sin_lo + pltpu.roll(x, q16, 1) * sin_hi
            o_ref[:, g * LANES:(g + 1) * LANES] = y.astype(o_ref.dtype)
    elif transposed:
        acc_t = acc.T
        ones = jnp.ones((DIFF_VT_ROWS - DIFF_DV, tk_out), o_ref.dtype)
        for a in range(o_ref.shape[0]):
            for hh in range(tn // DIFF_DV):
                r0 = hh * DIFF_VT_ROWS
                o_ref[a, r0:r0 + DIFF_DV, :] = acc_t[hh * DIFF_DV:(hh + 1) * DIFF_DV,
                                                     a * tk_out:(a + 1) * tk_out].astype(o_ref.dtype)
                o_ref[a, r0 + DIFF_DV:r0 + DIFF_VT_ROWS, :] = ones
    else:
        o_ref[...] = acc.astype(o_ref.dtype)


def _proj(h, w, colscale, tm, first, rest0, ntiles, out_dtype, mode="plain", tables=(), tiles_per_batch=1,
          tk_out=None):
    w_all, layer = w
    r, d = h.shape
    tn = PROJ_TN
    ncols = ntiles * tn
    assert r % tm == 0
    transposed = tk_out is not None
    wcol = lambda j: jnp.where(j == 0, first, rest0 + j - 1)
    in_specs = [pl.BlockSpec((tm, d), lambda i, j: (i, 0)),
                pl.BlockSpec((1, d, tn), lambda i, j: (layer, 0, wcol(j))),
                pl.BlockSpec((1, tn), lambda i, j: (0, wcol(j)))]
    for _ in tables:
        in_specs.append(pl.BlockSpec((tm, LANES), lambda i, j: (i % tiles_per_batch, 0)))
    if transposed:
        assert tm % tk_out == 0
        na = tm // tk_out
        rows_out = tn // DIFF_DV * DIFF_VT_ROWS
        out_specs = pl.BlockSpec((na, rows_out, tk_out), lambda i, j: (i, j, 0))
        out_shape = jax.ShapeDtypeStruct((r // tk_out, ncols // DIFF_DV * DIFF_VT_ROWS, tk_out), out_dtype)
    else:
        out_specs = pl.BlockSpec((tm, tn), lambda i, j: (i, j))
        out_shape = jax.ShapeDtypeStruct((r, ncols), out_dtype)
    return pl.pallas_call(
        functools.partial(_proj_kernel, mode=mode, transposed=transposed, tn=tn, tk_out=tk_out),
        grid=(r // tm, ntiles),
        in_specs=in_specs,
        out_specs=out_specs,
        out_shape=out_shape,
        compiler_params=_params(2, 52),
        name="in_proj_" + mode + ("_t" if transposed else ""),
    )(h, w_all, colscale, *tables)


def _ret_tables_kernel(dec_ref, intra_ref, qd_ref, kd_ref, cd_ref, *, chunk):
    direction = pl.program_id(0)
    head = pl.program_id(1)
    cf = float(chunk)
    dv = jnp.full((1, 1), dec_ref[direction, head], F32)
    lg = jnp.minimum(dv, 0.0) - jnp.log(1.0 + jnp.exp(-jnp.abs(dv)))
    rev = direction == 1
    ii = lax.broadcasted_iota(jnp.int32, (chunk, chunk), 0).astype(F32)
    jj = lax.broadcasted_iota(jnp.int32, (chunk, chunk), 1).astype(F32)
    dist = jnp.where(rev, jj - ii, ii - jj)
    intra_ref[0, 0] = jnp.where(dist >= 0.0, jnp.exp(lg * jnp.maximum(dist, 0.0)), 0.0)
    i1 = lax.broadcasted_iota(jnp.int32, (chunk, RET_DK), 0).astype(F32)
    qd_ref[0, 0] = jnp.exp(lg * jnp.where(rev, cf - i1, i1 + 1.0))
    kd_ref[0, 0] = jnp.exp(lg * jnp.where(rev, i1, cf - 1.0 - i1))
    cd_ref[0, 0] = jnp.exp(jnp.broadcast_to(lg, (SUBLANES, RET_DK)) * cf)


def _retention_decays(dec_f, dec_b, chunk):
    hh, dk = RET_HEADS, RET_DK
    blk = lambda *shape: pl.BlockSpec((1, 1) + shape, lambda d, h: (d, h, 0, 0))
    return pl.pallas_call(
        functools.partial(_ret_tables_kernel, chunk=chunk),
        grid=(2, hh),
        in_specs=[pl.BlockSpec(memory_space=pltpu.SMEM)],
        out_specs=[blk(chunk, chunk), blk(chunk, dk), blk(chunk, dk), blk(SUBLANES, dk)],
        out_shape=[jax.ShapeDtypeStruct((2, hh, chunk, chunk), F32),
                   jax.ShapeDtypeStruct((2, hh, chunk, dk), F32),
                   jax.ShapeDtypeStruct((2, hh, chunk, dk), F32),
                   jax.ShapeDtypeStruct((2, hh, SUBLANES, dk), F32)],
        compiler_params=_params(2, 32),
        name="retention_tables",
    )(jnp.stack([dec_f, dec_b]).astype(F32))


def _ret_kernel(intra_ref, qd_ref, kd_ref, cd_ref, qk_ref, v_ref, s0_ref, *rest, reverse, has_prev, chunk, n_chunks):
    if has_prev:
        prev_ref, o_ref, sfin_ref, s_scr = rest
    else:
        o_ref, sfin_ref, s_scr = rest
    @pl.when(pl.program_id(1) == 0)
    def _load_initial_state():
        s_scr[...] = s0_ref[0]

    dk = RET_DK
    order = range(n_chunks - 1, -1, -1) if reverse else range(n_chunks)
    for h in range(RET_HEADS):
        cols = slice(h * dk, (h + 1) * dk)
        kcols = slice((RET_HEADS + h) * dk, (RET_HEADS + h + 1) * dk)
        s = s_scr[h]
        for c in order:
            rows = slice(c * chunk, (c + 1) * chunk)
            q = qk_ref[0, rows, cols]
            k = qk_ref[0, rows, kcols]
            v = v_ref[0, rows, cols]
            att = lax.dot_general(q, k, NT_DIMS, preferred_element_type=F32) * intra_ref[0, h]
            qd = (q.astype(F32) * qd_ref[0, h]).astype(BF16)
            o = (jnp.dot(att.astype(BF16), v, preferred_element_type=F32)
                 + jnp.dot(qd, s.astype(BF16), preferred_element_type=F32))
            kd = (k.astype(F32) * kd_ref[0, h]).astype(BF16)
            s = s * cd_ref[0, h, 0:1, :] + lax.dot_general(kd, v, TN_DIMS, preferred_element_type=F32)
            if has_prev:
                o = o + prev_ref[0, rows, cols]
            o_ref[0, rows, cols] = o
        s_scr[h] = s
        sfin_ref[0, h] = s


def _retention(tables, qk, v_arr, s0, prev, reverse):
    b, length, _ = qk.shape
    hh, dk = RET_HEADS, RET_DK
    width = hh * dk
    c = tables[0].shape[-1]
    rb = min(RET_BLOCK, length)
    n = length // rb
    direction = 1 if reverse else 0
    pos = (lambda i: n - 1 - i) if reverse else (lambda i: i)
    tab = lambda *shape: pl.BlockSpec((1, hh) + shape, lambda bi, i: (direction, 0, 0, 0))
    in_specs = [tab(c, c), tab(c, dk), tab(c, dk), tab(SUBLANES, dk),
                pl.BlockSpec((1, rb, 2 * width), lambda bi, i: (bi, pos(i), 0)),
                pl.BlockSpec((1, rb, width), lambda bi, i: (bi, pos(i), 0)),
                pl.BlockSpec((1, hh, dk, dk), lambda bi, i: (bi, 0, 0, 0))]
    args = [*tables, qk, v_arr, s0]
    if prev is not None:
        in_specs.append(pl.BlockSpec((1, rb, width), lambda bi, i: (bi, pos(i), 0)))
        args.append(prev)
    return pl.pallas_call(
        functools.partial(_ret_kernel, reverse=reverse, has_prev=prev is not None, chunk=c, n_chunks=rb // c),
        grid=(b, n),
        in_specs=in_specs,
        out_specs=[pl.BlockSpec((1, rb, width), lambda bi, i: (bi, pos(i), 0)),
                   pl.BlockSpec((1, hh, dk, dk), lambda bi, i: (bi, 0, 0, 0))],
        out_shape=[jax.ShapeDtypeStruct((b, length, width), F32),
                   jax.ShapeDtypeStruct((b, hh, dk, dk), F32)],
        scratch_shapes=[pltpu.VMEM((hh, dk, dk), F32)],
        compiler_params=_params(2, 48),
        name="retention_bwd" if reverse else "retention_fwd",
    )(*args)


def _diff_kernel(zero_ref, lam_ref, g_ref, q_ref, *rest, tq, nb, n_lat, tile, lc, lam_init):
    if n_lat:
        kl_ref, vtl_ref = rest[:2]
        rest = rest[2:]
    kc_ref, vtc_ref, o_ref, qq_scr, klast_scr = rest[:5]
    s_buf, mt_buf, p_buf, al_buf = rest[5:9], rest[9:13], rest[13:15], rest[15:17]
    m_scr, acc_scr = rest[17:]
    n = max(n_lat, 1)
    last_rows = (tile if n_lat else 0) + lc
    rows_of = lambda j: last_rows if j == n - 1 else tile
    if n_lat:
        klast_scr[0:tile, :] = kl_ref[0, (n_lat - 1) * tile:n_lat * tile, :]
    klast_scr[last_rows - lc:last_rows, :] = kc_ref[0]

    def scores(g, slot):
        blk, j = divmod(g, n)
        if j == 0:
            q = q_ref[0, blk * tq:(blk + 1) * tq, :].astype(F32)
            lane = lax.broadcasted_iota(jnp.int32, q.shape, 1)
            qq_scr[blk, 0:tq, :] = jnp.where(lane < DIFF_HD, q, 0.0).astype(BF16)
            qq_scr[blk, tq:2 * tq, :] = jnp.where(lane >= DIFF_HD, q, 0.0).astype(BF16)
            m_scr[blk] = jnp.full(m_scr.shape[1:], NEG_BIG, F32)
        k = klast_scr[...] if j == n - 1 else kl_ref[0, j * tile:(j + 1) * tile, :]
        s = lax.dot_general(k, qq_scr[blk], NT_DIMS, preferred_element_type=F32)
        mt_buf[slot][...] = jnp.max(s, axis=0, keepdims=True)
        s_buf[slot][0:k.shape[0], :] = s

    def softmax(g, anchor):
        blk, j = divmod(g, n)
        s_slot, p_slot, rows = g % 4, g % 2, rows_of(j)
        m_old = m_scr[blk]
        if anchor is not None:
            m_old = jnp.maximum(m_old, anchor + NEG_BIG)
        m_new = jnp.maximum(m_old, mt_buf[s_slot][...])
        al_buf[p_slot][...] = jnp.exp2(m_old - m_new)
        m_scr[blk] = m_new
        z = pl.multiple_of(zero_ref[0], 8)
        s_buf[s_slot][pl.ds(z, 8), :] = s_buf[s_slot][pl.ds(z, 8), :]
        ch = DIFF_SOFTMAX_ROWS
        for r0 in range(0, rows, ch):
            p_buf[p_slot][r0:r0 + ch, :] = jnp.exp2(s_buf[s_slot][r0:r0 + ch, :] - m_new).astype(BF16)

    def values(g):
        blk, j = divmod(g, n)
        slot = g % 2
        pv = 0.0
        if n_lat:
            pv = jnp.dot(vtl_ref[j], p_buf[slot][0:tile, :], preferred_element_type=F32)
        if j == n - 1:
            pv = pv + jnp.dot(vtc_ref[0], p_buf[slot][last_rows - lc:last_rows, :], preferred_element_type=F32)
        acc_scr[blk] = pv if j == 0 else acc_scr[blk] * al_buf[slot][...] + pv

    def finish(blk):
        acc = acc_scr[blk]
        o = acc[0:DIFF_DV] * (1.0 / acc[DIFF_DV:DIFF_DV + 1])
        lam4 = lam_ref[...]
        s1 = jnp.sum(lam4[0:1] * lam4[1:2], axis=-1, keepdims=True)
        s2 = jnp.sum(lam4[2:3] * lam4[3:4], axis=-1, keepdims=True)
        lam = jnp.exp(s1) - jnp.exp(s2) + lam_init
        y = (o[:, :tq] - lam * o[:, tq:]).T
        ms = jnp.mean(y * y, axis=-1, keepdims=True)
        y = y * lax.rsqrt(ms + SUBLN_EPS) * g_ref[...] * (1.0 - lam_init)
        o_ref[0, blk * tq:(blk + 1) * tq, :] = y.astype(o_ref.dtype)
        d = jnp.concatenate([y[0:1, :]] * (2 * tq // LANES), axis=1) * 0.0
        return jnp.where(jnp.isnan(d), 0.0, d)

    total = nb * n
    scores(0, 0)
    if total > 1:
        scores(1, 1)
    anchor, anchor_at = None, -1
    for g in range(total):
        if g >= 1:
            values(g - 1)
            if g % n == 0:
                anchor, anchor_at = finish(g // n - 1), min(g + DIFF_FINISH_SLACK, total - 1)
        if g + 2 < total:
            scores(g + 2, (g + 2) % 4)
        softmax(g, anchor if g == anchor_at else None)
    values(total - 1)
    finish(nb - 1)


def _diff_attention(lam4, subln_g, q_arr, lat, ctx, lam_init):
    b, lq, _ = q_arr.shape
    hh = DIFF_HEADS
    tq = min(DIFF_TQ, lq)
    qk_c, vt_c = ctx
    lc = qk_c.shape[1]
    assert vt_c.shape == (b, hh * DIFF_VT_ROWS, lc)
    nb = math.gcd(DIFF_BLOCKS, lq // tq)
    in_specs = [pl.BlockSpec(memory_space=pltpu.SMEM),
                pl.BlockSpec((4, DIFF_HD), lambda bi, h, i: (0, 0)),
                pl.BlockSpec((1, DIFF_DV), lambda bi, h, i: (0, 0)),
                pl.BlockSpec((1, nb * tq, LANES), lambda bi, h, i: (bi, i, h))]
    args = [jnp.zeros((1,), jnp.int32), lam4, subln_g, q_arr]
    n_lat, tile = 0, lc
    if lat is not None:
        qk_l, vt_l = lat
        length = qk_l.shape[1]
        tile = vt_l.shape[2]
        n_lat = length // tile
        assert lc <= tile
        in_specs += [pl.BlockSpec((1, length, LANES), lambda bi, h, i: (bi, 0, hh + h)),
                     pl.BlockSpec((n_lat, DIFF_VT_ROWS, tile), lambda bi, h, i: (bi, h, 0))]
        args += [qk_l, vt_l]
    in_specs += [pl.BlockSpec((1, lc, LANES), lambda bi, h, i: (bi, 0, hh + h)),
                 pl.BlockSpec((1, DIFF_VT_ROWS, lc), lambda bi, h, i: (bi, h, 0))]
    args += [qk_c, vt_c]
    last_rows = (tile if n_lat else 0) + lc
    return pl.pallas_call(
        functools.partial(_diff_kernel, tq=tq, nb=nb, n_lat=n_lat, tile=tile, lc=lc, lam_init=lam_init),
        grid=(b, hh, lq // (nb * tq)),
        in_specs=in_specs,
        out_specs=pl.BlockSpec((1, nb * tq, LANES), lambda bi, h, i: (bi, i, h)),
        out_shape=jax.ShapeDtypeStruct((b, lq, hh * DIFF_DV), BF16),
        scratch_shapes=[pltpu.VMEM((nb, 2 * tq, LANES), BF16),
                        pltpu.VMEM((last_rows, LANES), BF16),
                        *[pltpu.VMEM((last_rows, 2 * tq), F32)] * 4,
                        *[pltpu.VMEM((1, 2 * tq), F32)] * 4,
                        *[pltpu.VMEM((last_rows, 2 * tq), BF16)] * 2,
                        *[pltpu.VMEM((1, 2 * tq), F32)] * 2,
                        pltpu.VMEM((nb, 1, 2 * tq), F32),
                        pltpu.VMEM((nb, DIFF_VT_ROWS, 2 * tq), F32)],
        compiler_params=_params(3, 40),
        name="diff_attention" if lat is not None else "diff_attention_ctx",
    )(*args)


def _softmax_pv(s_list, v_list):
    m = s_list[0].max(axis=-1, keepdims=True)
    for s in s_list[1:]:
        m = jnp.maximum(m, s.max(axis=-1, keepdims=True))
    den = 0.0
    o = 0.0
    for s, v in zip(s_list, v_list):
        p = jnp.exp2(s - m)
        den = den + jnp.sum(p, axis=-1, keepdims=True)
        o = o + jnp.dot(p.astype(BF16), v, preferred_element_type=F32)
    return o * (1.0 / den)


def _na_bias_kernel(tab_ref, o_ref, *, rows):
    w = GRID_W
    for cfg in range(3):
        for a in range(NA_UNION_ROWS):
            for hh in range(2):
                for g in range(0, NA_UNIT_ROWS, 2):
                    r0, r1 = (rows[(cfg * NA_UNION_ROWS + a) * NA_UNIT_ROWS + g + i] for i in range(2))
                    c0 = (hh * NA_UNIT_ROWS + g) * w
                    o_ref[0, cfg, a * w:(a + 1) * w, c0:c0 + LANES] = jnp.concatenate(
                        [tab_ref[0, hh, r0], tab_ref[0, hh, r1]], axis=1)


def _na_expand_bias(tab):
    npair = tab.shape[0]
    nk, cols = NA_UNION_ROWS * GRID_W, NA_UNIT_ROWS * LANES
    return pl.pallas_call(
        functools.partial(_na_bias_kernel, rows=_na_bias_rows()),
        grid=(npair,),
        in_specs=[pl.BlockSpec((1,) + tab.shape[1:], lambda p: (p, 0, 0, 0, 0))],
        out_specs=pl.BlockSpec((1, 3, nk, cols), lambda p: (p, 0, 0, 0)),
        out_shape=jax.ShapeDtypeStruct((npair, 3, nk, cols), F32),
        compiler_params=_params(1, 32),
        name="na_bias_table",
    )(tab)


def _na_kernel(zero_ref, q_ref, k_ref, vtl_ref, kc_ref, vtc_ref, bias_ref, o_ref, *scr, units, n_rows, lc):
    keys_scr = scr[0]
    s_buf, mt_buf, p_buf = scr[1:4], scr[4:7], scr[7:9]
    w = GRID_W
    uq = NA_UNIT_ROWS * w
    nk = NA_UNION_ROWS * w
    step_r0 = pl.program_id(2) * (units * NA_UNIT_ROWS)

    def first_key_row(u):
        return jnp.clip(step_r0 + u * NA_UNIT_ROWS - NA_KH // 2, 0, n_rows - NA_UNION_ROWS)

    def scores(u, slot):
        start = first_key_row(u)
        cfg = (step_r0 + u * NA_UNIT_ROWS - start) // NA_UNIT_ROWS
        q_t = q_ref[0, u * uq:(u + 1) * uq, :].astype(F32).T
        row = lax.broadcasted_iota(jnp.int32, q_t.shape, 0)
        qq = jnp.concatenate([jnp.where(row < NA_HD, q_t, 0.0), jnp.where(row >= NA_HD, q_t, 0.0)],
                             axis=1).astype(BF16)
        koff = pl.multiple_of(start * w, uq)
        keys_scr[0:nk, :] = k_ref[0, pl.ds(koff, nk), :]
        s = jnp.dot(keys_scr[...], qq, preferred_element_type=F32)
        s_nb = s[0:nk] + bias_ref[0, cfg]
        s_cx = s[nk:nk + lc]
        mt_buf[slot][...] = jnp.maximum(jnp.max(s_nb, axis=0, keepdims=True), jnp.max(s_cx, axis=0, keepdims=True))
        s_buf[slot][0:nk, :] = s_nb
        s_buf[slot][nk:nk + lc, :] = s_cx

    def softmax(s_slot, p_slot):
        m = mt_buf[s_slot][...]
        z = pl.multiple_of(zero_ref[0], 8)
        s_buf[s_slot][pl.ds(z, 8), :] = s_buf[s_slot][pl.ds(z, 8), :]
        ch = NA_SOFTMAX_ROWS
        for r0 in range(0, nk + lc, ch):
            p_buf[p_slot][r0:r0 + ch, :] = jnp.exp2(s_buf[s_slot][r0:r0 + ch, :] - m).astype(BF16)

    def values(u, p_slot):
        pc0 = first_key_row(u) // NA_UNIT_ROWS
        acc = jnp.dot(vtc_ref[0], p_buf[p_slot][nk:nk + lc, :], preferred_element_type=F32)
        for i in range(NA_UNION_ROWS // NA_UNIT_ROWS):
            acc = acc + jnp.dot(vtl_ref[pc0 + i], p_buf[p_slot][i * uq:(i + 1) * uq, :],
                                preferred_element_type=F32)
        inv = 1.0 / acc[LANES:LANES + 1]
        y = jnp.concatenate([acc[0:NA_HD, 0:uq] * inv[:, 0:uq],
                             acc[NA_HD:LANES, uq:2 * uq] * inv[:, uq:2 * uq]], axis=0)
        o_ref[0, u * uq:(u + 1) * uq, :] = y.T.astype(o_ref.dtype)

    keys_scr[nk:nk + lc, :] = kc_ref[0]
    scores(0, 0)
    if units > 1:
        scores(1, 1)
    for u in range(units):
        if u >= 1:
            values(u - 1, (u - 1) % 2)
        if u + 2 < units:
            scores(u + 2, (u + 2) % 3)
        softmax(u % 3, u % 2)
    values(units - 1, (units - 1) % 2)


def _na_attention(qk_l, vt_l, qk_c, vt_c, bias):
    b, length, _ = qk_l.shape
    lc = qk_c.shape[1]
    n_rows = length // GRID_W
    uq = NA_UNIT_ROWS * GRID_W
    nk = NA_UNION_ROWS * GRID_W
    assert n_rows >= NA_UNION_ROWS and n_rows % NA_UNIT_ROWS == 0 and vt_l.shape[2] == uq
    units = min(NA_UNITS, n_rows // NA_UNIT_ROWS)
    npair = NA_HEADS // 2
    n_pieces = length // uq
    tq = units * uq
    return pl.pallas_call(
        functools.partial(_na_kernel, units=units, n_rows=n_rows, lc=lc),
        grid=(b, npair, length // tq),
        in_specs=[pl.BlockSpec(memory_space=pltpu.SMEM),
                  pl.BlockSpec((1, tq, LANES), lambda bi, p, i: (bi, i, npair + p)),
                  pl.BlockSpec((1, length, LANES), lambda bi, p, i: (bi, 0, 2 * npair + p)),
                  pl.BlockSpec((n_pieces, DIFF_VT_ROWS, uq), lambda bi, p, i: (bi, p, 0)),
                  pl.BlockSpec((1, lc, LANES), lambda bi, p, i: (bi, 0, 2 * npair + p)),
                  pl.BlockSpec((1, DIFF_VT_ROWS, lc), lambda bi, p, i: (bi, p, 0)),
                  pl.BlockSpec((1, 3, nk, 2 * uq), lambda bi, p, i: (p, 0, 0, 0))],
        out_specs=pl.BlockSpec((1, tq, LANES), lambda bi, p, i: (bi, i, p)),
        out_shape=jax.ShapeDtypeStruct((b, length, NA_HEADS * NA_HD), BF16),
        scratch_shapes=[pltpu.VMEM((nk + lc, LANES), BF16),
                        *[pltpu.VMEM((nk + lc, 2 * uq), F32)] * 3,
                        *[pltpu.VMEM((1, 2 * uq), F32)] * 3,
                        *[pltpu.VMEM((nk + lc, 2 * uq), BF16)] * 2],
        compiler_params=_params(3, 48),
        name="neighbourhood_attention",
    )(jnp.zeros((1,), jnp.int32), qk_l, qk_l, vt_l, qk_c, vt_c, bias)


def _ctx_attn_kernel(q_ref, k_ref, v_ref, o_ref):
    q = q_ref[0].astype(F32)
    k = k_ref[0]
    v = v_ref[0]
    lane = lax.broadcasted_iota(jnp.int32, q.shape, 1)
    outs = []
    for hh in range(2):
        sel = (lane < NA_HD) if hh == 0 else (lane >= NA_HD)
        qm = jnp.where(sel, q, 0.0).astype(BF16)
        s = lax.dot_general(qm, k, NT_DIMS, preferred_element_type=F32)
        outs.append(_softmax_pv([s], [v]))
    o_ref[0] = jnp.where(lane < NA_HD, outs[0], outs[1]).astype(o_ref.dtype)


def _ctx_attention(qkv_c):
    b, lc, _ = qkv_c.shape
    npair = NA_HEADS // 2
    return pl.pallas_call(
        _ctx_attn_kernel,
        grid=(b, npair),
        in_specs=[pl.BlockSpec((1, lc, LANES), lambda bi, p: (bi, 0, npair + p)),
                  pl.BlockSpec((1, lc, LANES), lambda bi, p: (bi, 0, 2 * npair + p)),
                  pl.BlockSpec((1, lc, LANES), lambda bi, p: (bi, 0, 3 * npair + p))],
        out_specs=pl.BlockSpec((1, lc, LANES), lambda bi, p: (bi, 0, p)),
        out_shape=jax.ShapeDtypeStruct((b, lc, NA_HEADS * NA_HD), BF16),
        compiler_params=_params(2, 32),
        name="ctx_attention",
    )(qkv_c, qkv_c, qkv_c)


def _merge_kernel(oret_ref, gates_ref, odiff_ref, ona_ref, x_ref, g1_ref, lng_ref, lnb_ref,
                  wpa_ref, wpb_ref, wpc_ref, wo_ref, o_ref, *, alpha):
    d = D_MODEL
    ya_parts = []
    for hh in range(RET_HEADS):
        a = hh * RET_DK
        seg = _layer_norm_rows(oret_ref[:, a:a + RET_DK])
        g = gates_ref[:, a:a + RET_DK]
        ya_parts.append((g * _sigmoid(g) * seg).astype(BF16))
    ya_in = jnp.concatenate(ya_parts, axis=1)
    y_a = jnp.dot(ya_in, wpa_ref[...], preferred_element_type=F32)
    y_b = jnp.dot(odiff_ref[...], wpb_ref[...], preferred_element_type=F32)
    y_c = jnp.dot(ona_ref[...], wpc_ref[...], preferred_element_type=F32)
    m = (_sigmoid(gates_ref[:, d:2 * d]) * y_a + _sigmoid(gates_ref[:, 2 * d:3 * d]) * y_b
         + _sigmoid(gates_ref[:, 3 * d:4 * d]) * y_c)
    y = jnp.dot(m.astype(BF16), wo_ref[...], preferred_element_type=F32)
    z = alpha * x_ref[...] + g1_ref[0] * y
    o_ref[...] = _layer_norm_rows(z) * lng_ref[...] + lnb_ref[...]


def _merge(o_ret, gates, o_diff, o_na, x, g1, ln_g, ln_b, w_pa, w_pb, w_pc, w_o, mod_row, alpha):
    r, d = x.shape
    tm = min(MERGE_TILE, r)
    row = lambda i: (i, 0)
    const = lambda i: (0, 0)
    wspec = pl.BlockSpec((d, d), const, pipeline_mode=pl.Buffered(1))
    return pl.pallas_call(
        functools.partial(_merge_kernel, alpha=alpha),
        grid=(r // tm,),
        in_specs=[pl.BlockSpec((tm, d), row), pl.BlockSpec((tm, 4 * d), row), pl.BlockSpec((tm, d), row),
                  pl.BlockSpec((tm, d), row), pl.BlockSpec((tm, d), row),
                  pl.BlockSpec((1, 1, d), lambda i: (mod_row(i, tm), 0, 0)),
                  pl.BlockSpec((1, d), const), pl.BlockSpec((1, d), const),
                  wspec, wspec, wspec, wspec],
        out_specs=pl.BlockSpec((tm, d), row),
        out_shape=jax.ShapeDtypeStruct((r, d), F32),
        compiler_params=_params(1, 52),
        name="merge_out_proj",
    )(o_ret, gates, o_diff, o_na, x, g1, ln_g, ln_b, w_pa, w_pb, w_pc, w_o)


def _ffn_kernel(x_ref, sh_ref, sc_ref, g2_ref, lng_ref, lnb_ref, w1_ref, w2_ref, o_ref, *, alpha, ff_chunk):
    x = x_ref[...]
    h = (_layer_norm_rows(x) * (1.0 + sc_ref[0]) + sh_ref[0]).astype(BF16)
    acc = jnp.zeros(x.shape, F32)
    for cidx in range(D_FF // ff_chunk):
        a = cidx * ff_chunk
        u = jnp.maximum(jnp.dot(h, w1_ref[:, a:a + ff_chunk], preferred_element_type=F32), 0.0)
        acc = acc + jnp.dot((u * u).astype(BF16), w2_ref[a:a + ff_chunk, :], preferred_element_type=F32)
    z = alpha * x + g2_ref[0] * acc
    o_ref[...] = _layer_norm_rows(z) * lng_ref[...] + lnb_ref[...]


def _ffn(x, shift, scale, g2, ln_g, ln_b, w1, w2, mod_row, alpha):
    r, d = x.shape
    tm = min(FFN_TILE, r)
    row = lambda i: (i, 0)
    const = lambda i: (0, 0)
    mspec = pl.BlockSpec((1, 1, d), lambda i: (mod_row(i, tm), 0, 0))
    return pl.pallas_call(
        functools.partial(_ffn_kernel, alpha=alpha, ff_chunk=1024),
        grid=(r // tm,),
        in_specs=[pl.BlockSpec((tm, d), row), mspec, mspec, mspec,
                  pl.BlockSpec((1, d), const), pl.BlockSpec((1, d), const),
                  pl.BlockSpec((d, D_FF), const, pipeline_mode=pl.Buffered(1)),
                  pl.BlockSpec((D_FF, d), const, pipeline_mode=pl.Buffered(1))],
        out_specs=pl.BlockSpec((tm, d), row),
        out_shape=jax.ShapeDtypeStruct((r, d), F32),
        compiler_params=_params(1, 56),
        name="ffn",
    )(x, shift, scale, g2, ln_g, ln_b, w1, w2)


def _retention_tables(length):
    t = jnp.arange(length, dtype=F32)
    inv = ROPE_BASE ** (-jnp.linspace(0.0, 1.0, RET_DK // 2, dtype=F32))
    ang = t[:, None] * inv[None, :]
    return jnp.cos(ang), jnp.sin(ang)


def _axial_tables(length):
    t = jnp.arange(length)
    row = (t // GRID_W).astype(F32)
    col = (t % GRID_W).astype(F32)
    half = DIFF_HD // 2
    inv = ROPE_BASE ** (-jnp.arange(0, half, 2, dtype=F32) / half)
    ang_r = row[:, None] * inv[None, :]
    ang_c = col[:, None] * inv[None, :]
    zero = jnp.zeros_like(ang_r)
    unit_ang = jnp.concatenate([ang_r, ang_r, ang_c, ang_c], axis=-1)
    cos = jnp.cos(unit_ang)
    sin_r, sin_c = jnp.sin(ang_r), jnp.sin(ang_c)
    sin_lo = jnp.concatenate([-sin_r, zero, -sin_c, zero], axis=-1)
    sin_hi = jnp.concatenate([zero, sin_r, zero, sin_c], axis=-1)
    rep = LANES // DIFF_HD
    return tuple(jnp.tile(a, (1, rep)) for a in (cos, sin_lo, sin_hi))


def _na_bias_table(rpb):
    nh, nr, _ = rpb.shape
    w = GRID_W
    wid = 2 * w - 1
    p = jnp.pad(rpb.astype(F32) * math.log2(math.e), ((0, 0), (0, 0), (w - NA_KW, w - NA_KW)))
    y = p[..., ::-1]
    a = jnp.broadcast_to(y[:, :, None, :], (nh, nr, w, wid))
    a = jnp.pad(a, ((0, 0), (0, 0), (0, 0), (0, 1))).reshape(nh, nr, w * (wid + 1))
    a = a[:, :, :w * wid].reshape(nh, nr, w, wid)
    m = a[..., w - 1:]
    j = jnp.arange(w)
    cs = jnp.clip(j - NA_KW // 2, 0, w - NA_KW)
    col_mask = (j[:, None] >= cs[None, :]) & (j[:, None] < cs[None, :] + NA_KW)
    m = jnp.where(col_mask[None, None], m, NEG_BIG)
    m = jnp.pad(m, ((0, 0), (0, 1), (0, 0), (0, 0)), constant_values=NEG_BIG)
    return m.reshape(nh // 2, 2, nr + 1, w, w)


def _na_bias_rows():
    un, g_rows = NA_UNION_ROWS, NA_UNIT_ROWS
    rows = []
    for cfg in range(3):
        for a in range(un):
            for g in range(g_rows):
                q_row = cfg * g_rows + g
                win0 = (0, g, un - NA_KH)[cfg]
                in_win = win0 <= a < win0 + NA_KH
                rows.append(a - q_row + NA_KH - 1 if in_win else 2 * NA_KH - 1)
    return tuple(rows)


def _column_scale():
    d = D_MODEL
    cs = jnp.ones((IN_WIDTH,), F32)
    cs = cs.at[d:2 * d].set(RET_DK ** -0.5)
    cs = cs.at[4 * d:5 * d].set(DIFF_HD ** -0.5 * math.log2(math.e))
    cs = cs.at[7 * d:8 * d].set(NA_HD ** -0.5 * math.log2(math.e))
    return cs.reshape(1, IN_WIDTH)


def _project_all(h, w, colscale, tm, rope, ret_tables, ax_tables, tiles_per_batch, tk_out, na_tk_out, plain_na_v):
    if rope:
        ret_qk = _proj(h, w, colscale, tm, 0, 1, 2, BF16, "rope_ret", ret_tables, tiles_per_batch)
        diff_qk = _proj(h, w, colscale, tm, 4, 5, 2, BF16, "rope_diff", ax_tables, tiles_per_batch)
    else:
        ret_qk = _proj(h, w, colscale, tm, 0, 1, 2, BF16)
        diff_qk = _proj(h, w, colscale, tm, 4, 5, 2, BF16)
    v_na = _proj(h, w, colscale, tm, 2, 7, 4 if plain_na_v else 3, BF16)
    gates = _proj(h, w, colscale, tm, 3, 10, 4, F32)
    diff_vt = _proj(h, w, colscale, tm, 6, 7, 1, BF16, tk_out=tk_out)
    na_vt = _proj(h, w, colscale, tm, 9, 10, 1, BF16, tk_out=na_tk_out)
    return ret_qk, diff_qk, v_na, gates, diff_vt, na_vt


def kernel(x, c, ctx, c_ctx, w_mod, b_mod, w_in, ret_decay_f, ret_decay_b, diff_lq1, diff_lk1, diff_lq2, diff_lk2,
           diff_subln_g, na_rpb, w_pa, w_pb, w_pc, w_o, ln1_g, ln1_b, w_ff1, w_ff2, ln2_g, ln2_b):
    b, length, d = x.shape
    lc = ctx.shape[1]
    depth = w_mod.shape[0]
    n_cond = SUBLANES
    assert d == D_MODEL and b < n_cond and length % min(ROW_TILE, length) == 0 and length % GRID_W == 0
    assert lc % RET_CHUNK == 0 or lc < RET_CHUNK
    alpha = (2.0 * depth) ** 0.25

    cvec = jnp.zeros((n_cond, d), F32).at[:b].set(c).at[b].set(c_ctx)
    mods = _modulation(cvec, w_mod, b_mod)

    lat_row = lambda i, tm: (i * tm) // length
    ctx_row = lambda i, tm: b
    colscale = _column_scale()
    ret_tables = _retention_tables(length)
    ax_tables = _axial_tables(length)

    x_l = x.reshape(b * length, d)
    x_c = ctx.reshape(b * lc, d)
    zero_state = jnp.zeros((b, RET_HEADS, RET_DK, RET_DK), F32)
    for l in range(depth):
        with_ctx_out = l < depth - 1
        mod = [mods[l, :, k * d:(k + 1) * d].reshape(n_cond, 1, d) for k in range(6)]
        sh1, sc1, g1, sh2, sc2, g2 = mod
        w = (w_in, l)
        wpa, wpb, wpc, wo = (a[l].astype(BF16) for a in (w_pa, w_pb, w_pc, w_o))
        w1, w2 = w_ff1[l].astype(BF16), w_ff2[l].astype(BF16)
        lng1, lnb1 = ln1_g[l].reshape(1, d), ln1_b[l].reshape(1, d)
        lng2, lnb2 = ln2_g[l].reshape(1, d), ln2_b[l].reshape(1, d)
        lam_init = 0.8 - 0.6 * math.exp(-0.3 * l)
        lam4 = jnp.stack([diff_lq1[l], diff_lk1[l], diff_lq2[l], diff_lk2[l]]).astype(F32)
        subln = diff_subln_g[l].reshape(1, DIFF_DV).astype(F32)
        bias = _na_expand_bias(_na_bias_table(na_rpb[l]))

        h_l = _ln_mod(x_l, sh1, sc1, lat_row)
        h_c = _ln_mod(x_c, sh1, sc1, ctx_row)
        tm_l = min(ROW_TILE, length)
        rqk_l, dqk_l, vna_l, gates_l, dvt_l, nvt_l = _project_all(
            h_l, w, colscale, tm_l, True, ret_tables, ax_tables, length // tm_l, min(DIFF_TILE, length),
            NA_UNIT_ROWS * GRID_W, False)
        rqk_c, dqk_c, vna_c, gates_c, dvt_c, nvt_c = _project_all(
            h_c, w, colscale, min(ROW_TILE, b * lc), False, (), (), 1, lc, lc, True)
        sh3 = lambda a, n: a.reshape(b, n, a.shape[-1])

        decays = _retention_decays(ret_decay_f[l], ret_decay_b[l], min(RET_CHUNK, lc))
        rqk_l3, vna_l3, rqk_c3, vna_c3 = sh3(rqk_l, length), sh3(vna_l, length), sh3(rqk_c, lc), sh3(vna_c, lc)
        o_cf, s_cf = _retention(decays, rqk_c3, vna_c3, zero_state, None, False)
        o_rc, s_cb = _retention(decays, rqk_c3, vna_c3, zero_state, o_cf, True)
        o_lf, _ = _retention(decays, rqk_l3, vna_l3, s_cf, None, False)
        o_rl, _ = _retention(decays, rqk_l3, vna_l3, s_cb, o_lf, True)

        dqk_l3, dqk_c3 = sh3(dqk_l, length), sh3(dqk_c, lc)
        o_dl = _diff_attention(lam4, subln, dqk_l3, (dqk_l3, dvt_l), (dqk_c3, dvt_c), lam_init)
        o_nl = _na_attention(vna_l3, nvt_l, vna_c3, nvt_c, bias)

        x_l = _merge(o_rl.reshape(b * length, d), gates_l, o_dl.reshape(b * length, d), o_nl.reshape(b * length, d),
                     x_l, g1, lng1, lnb1, wpa, wpb, wpc, wo, lat_row, alpha)
        x_l = _ffn(x_l, sh2, sc2, g2, lng2, lnb2, w1, w2, lat_row, alpha)
        if with_ctx_out:
            o_dc = _diff_attention(lam4, subln, dqk_c3, None, (dqk_c3, dvt_c), lam_init)
            o_nc = _ctx_attention(vna_c3)
            x_c = _merge(o_rc.reshape(b * lc, d), gates_c, o_dc.reshape(b * lc, d), o_nc.reshape(b * lc, d),
                         x_c, g1, lng1, lnb1, wpa, wpb, wpc, wo, ctx_row, alpha)
            x_c = _ffn(x_c, sh2, sc2, g2, lng2, lnb2, w1, w2, ctx_row, alpha)
    return x_l.reshape(b, length, d)
```
